```python
import math
import jax, jax.numpy as jnp
from jax import lax
import numpy as np

D_MODEL = 1024
BATCH = 8
SEQ = 2048
DEPTH = 1

ROPE_THETA = 10000.0
NORM_EPS = 1e-6
NEG_INF = -1e30

MLA_HEADS = 4
MLA_NOPE_DIM = 128
MLA_ROPE_DIM = 64
MLA_V_DIM = 128
MLA_Q_RANK = 256
MLA_KV_RANK = 128
MLA_QK_DIM = MLA_NOPE_DIM + MLA_ROPE_DIM
MLA_WIDTH = MLA_HEADS * MLA_V_DIM
MLA_QUERY_BLOCK = 128

SWA_Q_HEADS = 8
SWA_KV_HEADS = 2
SWA_HEAD_DIM = 64
SWA_WINDOW = 128
SWA_BLOCK = 128
SWA_WIDTH = SWA_Q_HEADS * SWA_HEAD_DIM

MIX_WIDTH = MLA_WIDTH + SWA_WIDTH

IN_SIZES = (
    MLA_Q_RANK,
    MLA_KV_RANK,
    MLA_ROPE_DIM,
    MLA_WIDTH,
    SWA_Q_HEADS * SWA_HEAD_DIM,
    SWA_KV_HEADS * SWA_HEAD_DIM,
    SWA_KV_HEADS * SWA_HEAD_DIM,
    SWA_WIDTH,
)
IN_WIDTH = sum(IN_SIZES)
IN_SPLITS = tuple(int(v) for v in np.cumsum(IN_SIZES)[:-1])

kernel_name = "hymba_mla_swa_sink_gated_block"


def rms_norm(x, g):
    xf = x.astype(jnp.float32)
    y = xf * lax.rsqrt(jnp.mean(xf * xf, axis=-1, keepdims=True) + NORM_EPS)
    return (y * g.astype(jnp.float32)).astype(x.dtype)


def rope_tables(seq, dim):
    pos = jnp.arange(seq, dtype=jnp.float32)
    inv_freq = 1.0 / (ROPE_THETA ** (jnp.arange(0, dim, 2, dtype=jnp.float32) / dim))
    ang = pos[:, None] * inv_freq[None, :]
    return jnp.cos(ang), jnp.sin(ang)


def apply_rope(x, cos, sin):
    half = x.shape[-1] // 2
    shape = (cos.shape[0],) + (1,) * (x.ndim - 3) + (half,)
    c = cos.reshape(shape).astype(x.dtype)
    s = sin.reshape(shape).astype(x.dtype)
    x1, x2 = x[..., :half], x[..., half:]
    return jnp.concatenate([x1 * c - x2 * s, x2 * c + x1 * s], axis=-1)


def mla_attention(q_nope, q_rope, k_nope, k_rope, v):
    B, S, H, _ = q_nope.shape
    nb = S // MLA_QUERY_BLOCK
    scale = 1.0 / math.sqrt(MLA_QK_DIM)
    kpos = jnp.arange(S)

    def one_block(args):
        qn, qr, i = args
        s = (jnp.einsum('bqhd,bkhd->bhqk', qn, k_nope).astype(jnp.float32)
             + jnp.einsum('bqhr,bkr->bhqk', qr, k_rope).astype(jnp.float32)) * scale
        qpos = i * MLA_QUERY_BLOCK + jnp.arange(MLA_QUERY_BLOCK)
        causal = kpos[None, :] <= qpos[:, None]
        s = jnp.where(causal[None, None], s, NEG_INF)
        p = jax.nn.softmax(s, axis=-1).astype(v.dtype)
        return jnp.einsum('bhqk,bkhd->bqhd', p, v)

    qn_b = q_nope.reshape(B, nb, MLA_QUERY_BLOCK, H, MLA_NOPE_DIM).transpose(1, 0, 2, 3, 4)
    qr_b = q_rope.reshape(B, nb, MLA_QUERY_BLOCK, H, MLA_ROPE_DIM).transpose(1, 0, 2, 3, 4)
    out = lax.map(one_block, (qn_b, qr_b, jnp.arange(nb)))
    return out.transpose(1, 0, 2, 3, 4).reshape(B, S, H * MLA_V_DIM)


def sliding_window_sink_attention(q, k, v, sinks):
    B, S, HQ, D = q.shape
    HKV = k.shape[2]
    G = HQ // HKV
    nb = S // SWA_BLOCK
    scale = 1.0 / math.sqrt(D)

    qb = q.reshape(B, nb, SWA_BLOCK, HKV, G, D)

    def with_prev(t):
        tb = t.reshape(B, nb, SWA_BLOCK, HKV, D)
        prev = jnp.pad(tb[:, :-1], ((0, 0), (1, 0), (0, 0), (0, 0), (0, 0)))
        return jnp.concatenate([prev, tb], axis=2)

    kk = with_prev(k)
    vv = with_prev(v)
    s = jnp.einsum('bnqhgd,bnkhd->bnhgqk', qb, kk).astype(jnp.float32) * scale

    qi = jnp.arange(SWA_BLOCK)[:, None]
    ki = jnp.arange(2 * SWA_BLOCK)[None, :]
    rel = qi + SWA_BLOCK - ki
    band = (rel >= 0) & (rel < SWA_WINDOW)
    has_prev = (jnp.arange(nb)[:, None, None] > 0) | (ki >= SWA_BLOCK)[None]
    valid = band[None] & has_prev
    s = jnp.where(valid[None, :, None, None], s, NEG_INF)

    sink = jnp.broadcast_to(
        sinks.astype(jnp.float32).reshape(1, 1, HKV, G, 1, 1), s.shape[:-1] + (1,))
    p = jax.nn.softmax(jnp.concatenate([s, sink], axis=-1), axis=-1)[..., :-1]
    out = jnp.einsum('bnhgqk,bnkhd->bnqhgd', p.astype(v.dtype), vv)
    return out.reshape(B, S, HQ * D)


def hybrid_layer(x, ln_g, w_in, q_a_g, w_q_up, kv_a_g, w_kv_up, sinks, w_out,
                 cos_mla, sin_mla, cos_swa, sin_swa):
    B, S, _ = x.shape
    h = rms_norm(x, ln_g)
    proj = h @ w_in
    c_q, c_kv, k_rope, g_mla, q_s, k_s, v_s, g_swa = jnp.split(proj, IN_SPLITS, axis=-1)

    q = (rms_norm(c_q, q_a_g) @ w_q_up).reshape(B, S, MLA_HEADS, MLA_QK_DIM)
    q_nope = q[..., :MLA_NOPE_DIM]
    q_rope = apply_rope(q[..., MLA_NOPE_DIM:], cos_mla, sin_mla)
    k_rope = apply_rope(k_rope, cos_mla, sin_mla)
    kv = (rms_norm(c_kv, kv_a_g) @ w_kv_up).reshape(B, S, MLA_HEADS, MLA_NOPE_DIM + MLA_V_DIM)
    k_nope = kv[..., :MLA_NOPE_DIM]
    v_mla = kv[..., MLA_NOPE_DIM:]
    o_mla = mla_attention(q_nope, q_rope, k_nope, k_rope, v_mla)

    q_s = apply_rope(q_s.reshape(B, S, SWA_Q_HEADS, SWA_HEAD_DIM), cos_swa, sin_swa)
    k_s = apply_rope(k_s.reshape(B, S, SWA_KV_HEADS, SWA_HEAD_DIM), cos_swa, sin_swa)
    v_s = v_s.reshape(B, S, SWA_KV_HEADS, SWA_HEAD_DIM)
    o_swa = sliding_window_sink_attention(q_s, k_s, v_s, sinks)

    mixed = jnp.concatenate([o_mla * jax.nn.silu(g_mla), o_swa * jax.nn.silu(g_swa)], axis=-1)
    return x + mixed @ w_out


def setup_inputs(seed: int = 0) -> dict:
    key = jax.random.key(seed)
    ks = jax.random.split(key, 11)
    f32 = jnp.float32

    def w(k, shape, fan_in):
        return jax.random.normal(k, shape, f32) * (fan_in ** -0.5)

    def gain(k, shape):
        return 1.0 + 0.02 * jax.random.normal(k, shape, f32)

    return {
        "x": jax.random.normal(ks[0], (BATCH, SEQ, D_MODEL), f32),
        "ln_mix": gain(ks[1], (DEPTH, D_MODEL)),
        "w_in": w(ks[2], (DEPTH, D_MODEL, IN_WIDTH), D_MODEL),
        "q_a_norm": gain(ks[3], (DEPTH, MLA_Q_RANK)),
        "w_q_up": w(ks[4], (DEPTH, MLA_Q_RANK, MLA_HEADS * MLA_QK_DIM), MLA_Q_RANK),
        "kv_a_norm": gain(ks[5], (DEPTH, MLA_KV_RANK)),
        "w_kv_up": w(ks[6], (DEPTH, MLA_KV_RANK, MLA_HEADS * (MLA_NOPE_DIM + MLA_V_DIM)), MLA_KV_RANK),
        "attn_sinks": jax.random.normal(ks[7], (DEPTH, SWA_Q_HEADS), f32),
        "w_out": w(ks[8], (DEPTH, MIX_WIDTH, D_MODEL), MIX_WIDTH),
        "final_norm": gain(ks[9], (D_MODEL,)),
    }


def reference(x, ln_mix, w_in, q_a_norm, w_q_up, kv_a_norm, w_kv_up, attn_sinks, w_out, final_norm):
    S = x.shape[1]
    cos_mla, sin_mla = rope_tables(S, MLA_ROPE_DIM)
    cos_swa, sin_swa = rope_tables(S, SWA_HEAD_DIM)
    h = x
    for l in range(DEPTH):
        h = hybrid_layer(h, ln_mix[l], w_in[l], q_a_norm[l], w_q_up[l], kv_a_norm[l],
                         w_kv_up[l], attn_sinks[l], w_out[l],
                         cos_mla, sin_mla, cos_swa, sin_swa)
    return rms_norm(h, final_norm)
```

```python
import functools
import math

import jax
import jax.numpy as jnp
from jax import lax
from jax.experimental import pallas as pl
from jax.experimental.pallas import tpu as pltpu

ROPE_THETA = 10000.0
NORM_EPS = 1e-6
NEG_INF = -1e30
LOG2E = 1.4426950408889634

MLA_HEADS = 4
MLA_NOPE = 128
MLA_ROPE = 64
MLA_V = 128
MLA_Q_RANK = 256
MLA_KV_RANK = 128
MLA_QK = MLA_NOPE + MLA_ROPE
MLA_WIDTH = MLA_HEADS * MLA_V

SWA_Q_HEADS = 8
SWA_KV_HEADS = 2
SWA_D = 64
SWA_WINDOW = 128
SWA_GROUP = SWA_Q_HEADS // SWA_KV_HEADS
SWA_WIDTH = SWA_Q_HEADS * SWA_D

LANES = 128
V7X_VMEM_LIMIT_BYTES = 56 * 1024 * 1024

PROJ_ROWS = 512
MLA_BLOCK = 512


def _rope_tables(seq, dim):
    half = dim // 2
    pos = jnp.arange(seq, dtype=jnp.float32)
    inv_freq = 1.0 / (ROPE_THETA ** (jnp.arange(0, dim, 2, dtype=jnp.float32) / dim))
    ang = pos[:, None] * inv_freq[None, :]
    cos, sin = jnp.cos(ang), jnp.sin(ang)
    reps = LANES // dim
    zero = jnp.zeros_like(sin)
    c = jnp.tile(jnp.concatenate([cos, cos], axis=1), (1, reps))
    s_up = jnp.tile(jnp.concatenate([zero, sin], axis=1), (1, reps))
    s_dn = jnp.tile(jnp.concatenate([-sin, zero], axis=1), (1, reps))
    return c, s_up, s_dn


def _rope(x, c, s_up, s_dn, half):
    return (x * c + pltpu.roll(x, half, 1) * s_up
            + pltpu.roll(x, LANES - half, 1) * s_dn)


def _rms(x, g):
    return x * lax.rsqrt(jnp.mean(x * x, axis=-1, keepdims=True) + NORM_EPS) * g


def _silu(g):
    return g / (1.0 + jnp.exp(-g))


def _dot(a, b):
    return jnp.dot(a, b, preferred_element_type=jnp.float32)


def _dot_nt(a, b):
    return lax.dot_general(a, b, (((1,), (1,)), ((), ())),
                           preferred_element_type=jnp.float32)


PROJ_COLS = 2560


def _proj_kernel(x_ref, ln_ref, win_ref, qg_ref, wq_ref, kvg_ref, wkv_ref,
                 cm_ref, sum_ref, sdm_ref, cs_ref, sus_ref, sds_ref,
                 qm_ref, km_ref, vm_ref, gm_ref, qs_ref, ks_ref, vs_ref, gs_ref):
    bf = jnp.bfloat16
    x = x_ref[...]
    h = _rms(x, ln_ref[...]).astype(bf)

    cm, sum_, sdm = cm_ref[...], sum_ref[...], sdm_ref[...]
    cs, sus, sds = cs_ref[...], sus_ref[...], sds_ref[...]
    lane = lax.broadcasted_iota(jnp.int32, (x.shape[0], LANES), 1)
    lo = lane < (LANES // 2)

    lat = _dot(h, win_ref[:, 0:512])
    c_q = lat[:, 0:MLA_Q_RANK]
    c_kv = lat[:, MLA_Q_RANK:MLA_Q_RANK + MLA_KV_RANK]
    kr = _rope(lat[:, 384:512], cm, sum_, sdm, MLA_ROPE // 2)
    q = _dot(_rms(c_q, qg_ref[...]).astype(bf), wq_ref[...])
    kv = _dot(_rms(c_kv, kvg_ref[...]).astype(bf), wkv_ref[...])

    q_scale = LOG2E / math.sqrt(MLA_QK)
    kr_lo = jnp.where(lo, kr, 0.0).astype(bf)
    kr_hi = jnp.where(lo, 0.0, kr).astype(bf)
    nope_w = MLA_HEADS * MLA_NOPE
    q_rope = [
        (_rope(q[:, nope_w + j * LANES: nope_w + (j + 1) * LANES], cm, sum_, sdm,
               MLA_ROPE // 2) * q_scale).astype(bf)
        for j in range(MLA_HEADS * MLA_ROPE // LANES)
    ]
    for hd in range(MLA_HEADS):
        base = hd * 2 * LANES
        qm_ref[:, base:base + LANES] = (q[:, hd * MLA_NOPE:(hd + 1) * MLA_NOPE] * q_scale).astype(bf)
        qm_ref[:, base + LANES:base + 2 * LANES] = q_rope[hd // 2]
        km_ref[:, base:base + LANES] = kv[:, hd * MLA_NOPE:(hd + 1) * MLA_NOPE].astype(bf)
        km_ref[:, base + LANES:base + 2 * LANES] = kr_lo if hd % 2 == 0 else kr_hi
    vm_ref[...] = kv[:, nope_w:].astype(bf)

    gm_ref[...] = _silu(_dot(h, win_ref[:, 512:1024])).astype(bf)

    q_s = _dot(h, win_ref[:, 1024:1536])
    s_scale = LOG2E / math.sqrt(SWA_D)
    for j in range(SWA_WIDTH // LANES):
        sl = slice(j * LANES, (j + 1) * LANES)
        qs_ref[:, sl] = (_rope(q_s[:, sl], cs, sus, sds, SWA_D // 2) * s_scale).astype(bf)

    kv_s = _dot(h, win_ref[:, 1536:2048])
    for j in range(SWA_KV_HEADS):
        k2 = _rope(kv_s[:, j * LANES:(j + 1) * LANES], cs, sus, sds, SWA_D // 2)
        v2 = kv_s[:, 256 + j * LANES: 256 + (j + 1) * LANES]
        ks_ref[:, (2 * j) * LANES:(2 * j + 1) * LANES] = jnp.where(lo, k2, 0.0).astype(bf)
        ks_ref[:, (2 * j + 1) * LANES:(2 * j + 2) * LANES] = jnp.where(lo, 0.0, k2).astype(bf)
        vs_ref[:, (2 * j) * LANES:(2 * j + 1) * LANES] = jnp.where(lo, v2, 0.0).astype(bf)
        vs_ref[:, (2 * j + 1) * LANES:(2 * j + 2) * LANES] = jnp.where(lo, 0.0, v2).astype(bf)

    gs_ref[...] = _silu(_dot(h, win_ref[:, 2048:2560])).astype(bf)


def _proj_call(x2, ln_g, w_in_p, q_g, w_q_p, kv_g, w_kv_p, tabs_m, tabs_s, seq):
    n, d = x2.shape
    tm = PROJ_ROWS
    steps_per_seq = seq // tm
    row = lambda i: (i, 0)
    const = lambda i: (0, 0)
    tab = lambda i: (i % steps_per_seq, 0)
    bf = jnp.bfloat16
    out_widths = (1024, 1024, 512, 512, 512, 512, 512, 512)
    return pl.pallas_call(
        _proj_kernel,
        grid=(n // tm,),
        in_specs=[
            pl.BlockSpec((tm, d), row),
            pl.BlockSpec((1, d), const),
            pl.BlockSpec(w_in_p.shape, const),
            pl.BlockSpec((1, MLA_Q_RANK), const),
            pl.BlockSpec(w_q_p.shape, const),
            pl.BlockSpec((1, MLA_KV_RANK), const),
            pl.BlockSpec(w_kv_p.shape, const),
        ] + [pl.BlockSpec((tm, LANES), tab)] * 6,
        out_specs=[pl.BlockSpec((tm, w), row) for w in out_widths],
        out_shape=[jax.ShapeDtypeStruct((n, w), bf) for w in out_widths],
        compiler_params=pltpu.CompilerParams(
            dimension_semantics=("arbitrary",),
            vmem_limit_bytes=V7X_VMEM_LIMIT_BYTES),
        name="proj",
    )(x2, ln_g, w_in_p, q_g, w_q_p, kv_g, w_kv_p, *tabs_m, *tabs_s)


def _mla_kernel(q_ref, k_ref, v_ref, g_ref, o_ref):
    t = MLA_BLOCK
    qi = pl.program_id(2)
    q = q_ref[...]

    def step(j, carry, diagonal):
        m, l, acc = carry
        start = pl.multiple_of(j * t, t)
        k = k_ref[pl.ds(start, t), :]
        v = v_ref[pl.ds(start, t), :]
        s = _dot_nt(q, k)
        if diagonal:
            r = lax.broadcasted_iota(jnp.int32, (t, t), 0)
            c = lax.broadcasted_iota(jnp.int32, (t, t), 1)
            s = jnp.where(c <= r, s, NEG_INF)
        m_new = jnp.maximum(m, jnp.max(s, axis=1, keepdims=True))
        alpha = jnp.exp2(m - m_new)
        p = jnp.exp2(s - m_new)
        l = alpha * l + jnp.sum(p, axis=1, keepdims=True)
        acc = alpha * acc + _dot(p.astype(jnp.bfloat16), v)
        return m_new, l, acc

    init = (jnp.full((t, 1), NEG_INF, jnp.float32),
            jnp.zeros((t, 1), jnp.float32),
            jnp.zeros((t, MLA_V), jnp.float32))
    carry = lax.fori_loop(0, qi, lambda j, c: step(j, c, False), init)
    _, l, acc = step(qi, carry, True)
    o_ref[...] = (acc / l * g_ref[...].astype(jnp.float32)).astype(o_ref.dtype)


def _mla_call(qm, km, vm, gm, batch, seq):
    n = qm.shape[0]
    t = MLA_BLOCK
    nq = seq // t
    return pl.pallas_call(
        _mla_kernel,
        grid=(batch, MLA_HEADS, nq),
        in_specs=[
            pl.BlockSpec((t, 2 * LANES), lambda b, h, i: (b * nq + i, h)),
            pl.BlockSpec((seq, 2 * LANES), lambda b, h, i: (b, h)),
            pl.BlockSpec((seq, MLA_V), lambda b, h, i: (b, h)),
            pl.BlockSpec((t, MLA_V), lambda b, h, i: (b * nq + i, h)),
        ],
        out_specs=pl.BlockSpec((t, MLA_V), lambda b, h, i: (b * nq + i, h)),
        out_shape=jax.ShapeDtypeStruct((n, MLA_WIDTH), jnp.bfloat16),
        compiler_params=pltpu.CompilerParams(
            dimension_semantics=("arbitrary", "arbitrary", "arbitrary"),
            vmem_limit_bytes=V7X_VMEM_LIMIT_BYTES),
        name="mla",
    )(qm, km, vm, gm)


def _swa_kernel(sink_ref, q_ref, k_ref, v_ref, g_ref, o_ref):
    w = SWA_WINDOW
    hk = pl.program_id(1)
    nb = q_ref.shape[0] // w
    bf = jnp.bfloat16

    row = lax.broadcasted_iota(jnp.int32, (2 * w, 1), 0)
    sink = [sink_ref[hk * SWA_GROUP + i] * LOG2E for i in range(SWA_GROUP)]
    sink_lo = jnp.where(row < w, sink[0], sink[2])
    sink_hi = jnp.where(row < w, sink[1], sink[3])

    def block(n, kv_rows, valid):
        qrows = slice(n * w, (n + 1) * w)
        qst = jnp.concatenate([q_ref[qrows, 0:LANES], q_ref[qrows, LANES:2 * LANES]], axis=0)
        outs = []
        for half, snk in ((0, sink_lo), (1, sink_hi)):
            lanes = slice(half * LANES, (half + 1) * LANES)
            k = k_ref[kv_rows, lanes]
            v = v_ref[kv_rows, lanes]
            s = jnp.where(valid, _dot_nt(qst, k), NEG_INF)
            m = jnp.maximum(jnp.max(s, axis=1, keepdims=True), snk)
            p = jnp.exp2(s - m)
            den = jnp.sum(p, axis=1, keepdims=True) + jnp.exp2(snk - m)
            outs.append(_dot(p.astype(bf), v) / den)
        o = outs[0] + outs[1]
        g = g_ref[qrows, :].astype(jnp.float32)
        o_ref[qrows, 0:LANES] = (o[0:w] * g[:, 0:LANES]).astype(o_ref.dtype)
        o_ref[qrows, LANES:2 * LANES] = (o[w:2 * w] * g[:, LANES:2 * LANES]).astype(o_ref.dtype)

    qi0 = lax.broadcasted_iota(jnp.int32, (2 * w, w), 0) % w
    ki0 = lax.broadcasted_iota(jnp.int32, (2 * w, w), 1)
    block(0, slice(0, w), ki0 <= qi0)

    qi = lax.broadcasted_iota(jnp.int32, (2 * w, 2 * w), 0) % w
    ki = lax.broadcasted_iota(jnp.int32, (2 * w, 2 * w), 1)
    rel = qi + w - ki
    band = (rel >= 0) & (rel < SWA_WINDOW)
    for n in range(1, nb):
        block(n, slice((n - 1) * w, (n + 1) * w), band)


def _swa_call(sinks, qs, ks4, vs4, gs, batch, seq):
    n = qs.shape[0]
    blk = lambda b, h: (b, h)
    spec = pl.BlockSpec((seq, 2 * LANES), blk)
    return pl.pallas_call(
        _swa_kernel,
        grid=(batch, SWA_KV_HEADS),
        in_specs=[pl.BlockSpec(memory_space=pltpu.SMEM), spec, spec, spec, spec],
        out_specs=spec,
        out_shape=jax.ShapeDtypeStruct((n, SWA_WIDTH), jnp.bfloat16),
        compiler_params=pltpu.CompilerParams(
            dimension_semantics=("arbitrary", "arbitrary"),
            vmem_limit_bytes=V7X_VMEM_LIMIT_BYTES),
        name="swa",
    )(sinks, qs, ks4, vs4, gs)


def _out_kernel(x_ref, mm_ref, ms_ref, wa_ref, wb_ref, fg_ref, o_ref):
    y = x_ref[...] + _dot(mm_ref[...], wa_ref[...]) + _dot(ms_ref[...], wb_ref[...])
    o_ref[...] = _rms(y, fg_ref[...])


def _out_call(x2, mm, ms, w_a, w_b, fg):
    n, d = x2.shape
    tm = PROJ_ROWS
    row = lambda i: (i, 0)
    const = lambda i: (0, 0)
    return pl.pallas_call(
        _out_kernel,
        grid=(n // tm,),
        in_specs=[
            pl.BlockSpec((tm, d), row),
            pl.BlockSpec((tm, MLA_WIDTH), row),
            pl.BlockSpec((tm, SWA_WIDTH), row),
            pl.BlockSpec(w_a.shape, const),
            pl.BlockSpec(w_b.shape, const),
            pl.BlockSpec((1, d), const),
        ],
        out_specs=pl.BlockSpec((tm, d), row),
        out_shape=jax.ShapeDtypeStruct((n, d), jnp.float32),
        compiler_params=pltpu.CompilerParams(
            dimension_semantics=("arbitrary",),
            vmem_limit_bytes=V7X_VMEM_LIMIT_BYTES),
        name="out",
    )(x2, mm, ms, w_a, w_b, fg)


def _prep_weights(w_in, w_q_up, w_kv_up):
    bf = jnp.bfloat16
    o = 0
    parts = {}
    for name, width in (("c_q", MLA_Q_RANK), ("c_kv", MLA_KV_RANK), ("k_rope", MLA_ROPE),
                        ("g_mla", MLA_WIDTH), ("q_s", SWA_WIDTH),
                        ("k_s", SWA_KV_HEADS * SWA_D), ("v_s", SWA_KV_HEADS * SWA_D),
                        ("g_swa", SWA_WIDTH)):
        parts[name] = w_in[:, o:o + width]
        o += width

    def twice_each(w):
        return jnp.concatenate(
            [w[:, j * SWA_D:(j + 1) * SWA_D] for j in range(SWA_KV_HEADS) for _ in range(2)], axis=1)

    w_in_p = jnp.concatenate([
        parts["c_q"], parts["c_kv"], parts["k_rope"], parts["k_rope"], parts["g_mla"],
        parts["q_s"], twice_each(parts["k_s"]), twice_each(parts["v_s"]), parts["g_swa"],
    ], axis=1).astype(bf)
    assert w_in_p.shape[1] == PROJ_COLS

    wq = w_q_up.reshape(MLA_Q_RANK, MLA_HEADS, MLA_QK)
    w_q_p = jnp.concatenate([wq[:, :, :MLA_NOPE].reshape(MLA_Q_RANK, -1),
                             wq[:, :, MLA_NOPE:].reshape(MLA_Q_RANK, -1)], axis=1).astype(bf)
    wkv = w_kv_up.reshape(MLA_KV_RANK, MLA_HEADS, MLA_NOPE + MLA_V)
    w_kv_p = jnp.concatenate([wkv[:, :, :MLA_NOPE].reshape(MLA_KV_RANK, -1),
                              wkv[:, :, MLA_NOPE:].reshape(MLA_KV_RANK, -1)], axis=1).astype(bf)
    return w_in_p, w_q_p, w_kv_p


def _layer(x2, ln_g, w_in, q_g, w_q_up, kv_g, w_kv_up, sinks, w_out, tabs_m, tabs_s,
           batch, seq):
    w_in_p, w_q_p, w_kv_p = _prep_weights(w_in, w_q_up, w_kv_up)
    qm, km, vm, gm, qs, ks4, vs4, gs = _proj_call(
        x2, ln_g.reshape(1, -1), w_in_p, q_g.reshape(1, -1), w_q_p, kv_g.reshape(1, -1),
        w_kv_p, tabs_m, tabs_s, seq)
    mm = _mla_call(qm, km, vm, gm, batch, seq)
    ms = _swa_call(sinks, qs, ks4, vs4, gs, batch, seq)
    w_o = w_out.astype(jnp.bfloat16)
    return mm, ms, w_o[:MLA_WIDTH], w_o[MLA_WIDTH:]


def kernel(x, ln_mix, w_in, q_a_norm, w_q_up, kv_a_norm, w_kv_up, attn_sinks, w_out, final_norm):
    batch, seq, d = x.shape
    depth = ln_mix.shape[0]
    assert depth == 1, "final norm is fused into the single layer's output kernel"
    assert seq % MLA_BLOCK == 0 and seq % PROJ_ROWS == 0 and seq % SWA_WINDOW == 0
    tabs_m = _rope_tables(seq, MLA_ROPE)
    tabs_s = _rope_tables(seq, SWA_D)
    x2 = x.reshape(batch * seq, d)
    mm, ms, w_a, w_b = _layer(x2, ln_mix[0], w_in[0], q_a_norm[0], w_q_up[0], kv_a_norm[0],
                              w_kv_up[0], attn_sinks[0], w_out[0], tabs_m, tabs_s, batch, seq)
    out = _out_call(x2, mm, ms, w_a, w_b, final_norm.reshape(1, -1))
    return out.reshape(batch, seq, d)
```

```python
import functools
import math

import jax
import jax.numpy as jnp
from jax import lax
from jax.experimental import pallas as pl
from jax.experimental.pallas import tpu as pltpu

ROPE_THETA = 10000.0
NORM_EPS = 1e-6
NEG_INF = -1e30
LOG2E = 1.4426950408889634

MLA_HEADS = 4
MLA_NOPE = 128
MLA_ROPE = 64
MLA_V = 128
MLA_Q_RANK = 256
MLA_KV_RANK = 128
MLA_QK = MLA_NOPE + MLA_ROPE
MLA_WIDTH = MLA_HEADS * MLA_V

SWA_Q_HEADS = 8
SWA_KV_HEADS = 2
SWA_D = 64
SWA_WINDOW = 128
SWA_GROUP = SWA_Q_HEADS // SWA_KV_HEADS
SWA_WIDTH = SWA_Q_HEADS * SWA_D

LANES = 128
V7X_VMEM_LIMIT_BYTES = 56 * 1024 * 1024

PROJ_ROWS = 512
MLA_BLOCK = 256


def _rope_tables(seq, dim):
    half = dim // 2
    pos = jnp.arange(seq, dtype=jnp.float32)
    inv_freq = 1.0 / (ROPE_THETA ** (jnp.arange(0, dim, 2, dtype=jnp.float32) / dim))
    ang = pos[:, None] * inv_freq[None, :]
    cos, sin = jnp.cos(ang), jnp.sin(ang)
    reps = LANES // dim
    zero = jnp.zeros_like(sin)
    c = jnp.tile(jnp.concatenate([cos, cos], axis=1), (1, reps))
    s_up = jnp.tile(jnp.concatenate([zero, sin], axis=1), (1, reps))
    s_dn = jnp.tile(jnp.concatenate([-sin, zero], axis=1), (1, reps))
    return c, s_up, s_dn


def _rope(x, c, s_up, s_dn, half):
    return (x * c + pltpu.roll(x, half, 1) * s_up
            + pltpu.roll(x, LANES - half, 1) * s_dn)


def _rms(x, g):
    return x * lax.rsqrt(jnp.mean(x * x, axis=-1, keepdims=True) + NORM_EPS) * g


def _silu(g):
    return g / (1.0 + jnp.exp(-g))


def _dot(a, b):
    return jnp.dot(a, b, preferred_element_type=jnp.float32)


def _dot_nt(a, b):
    return lax.dot_general(a, b, (((1,), (1,)), ((), ())),
                           preferred_element_type=jnp.float32)


PROJ_COLS = 2560


def _proj_kernel(x_ref, ln_ref, win_ref, qg_ref, wq_ref, kvg_ref, wkv_ref,
                 cm_ref, sum_ref, sdm_ref, cs_ref, sus_ref, sds_ref,
                 qm_ref, km_ref, vm_ref, gm_ref, qs_ref, ks_ref, vs_ref, gs_ref):
    bf = jnp.bfloat16
    x = x_ref[...]
    h = _rms(x, ln_ref[...]).astype(bf)

    cm, sum_, sdm = cm_ref[...], sum_ref[...], sdm_ref[...]
    cs, sus, sds = cs_ref[...], sus_ref[...], sds_ref[...]
    lane = lax.broadcasted_iota(jnp.int32, (x.shape[0], LANES), 1)
    lo = lane < (LANES // 2)

    lat = _dot(h, win_ref[:, 0:512])
    c_q = lat[:, 0:MLA_Q_RANK]
    c_kv = lat[:, MLA_Q_RANK:MLA_Q_RANK + MLA_KV_RANK]
    kr = _rope(lat[:, 384:512], cm, sum_, sdm, MLA_ROPE // 2)
    q = _dot(_rms(c_q, qg_ref[...]).astype(bf), wq_ref[...])
    kv = _dot(_rms(c_kv, kvg_ref[...]).astype(bf), wkv_ref[...])

    q_scale = LOG2E / math.sqrt(MLA_QK)
    kr_lo = jnp.where(lo, kr, 0.0).astype(bf)
    kr_hi = jnp.where(lo, 0.0, kr).astype(bf)
    nope_w = MLA_HEADS * MLA_NOPE
    q_rope = [
        (_rope(q[:, nope_w + j * LANES: nope_w + (j + 1) * LANES], cm, sum_, sdm,
               MLA_ROPE // 2) * q_scale).astype(bf)
        for j in range(MLA_HEADS * MLA_ROPE // LANES)
    ]
    for hd in range(MLA_HEADS):
        base = hd * 2 * LANES
        qm_ref[:, base:base + LANES] = (q[:, hd * MLA_NOPE:(hd + 1) * MLA_NOPE] * q_scale).astype(bf)
        qm_ref[:, base + LANES:base + 2 * LANES] = q_rope[hd // 2]
        km_ref[:, base:base + LANES] = kv[:, hd * MLA_NOPE:(hd + 1) * MLA_NOPE].astype(bf)
        km_ref[:, base + LANES:base + 2 * LANES] = kr_lo if hd % 2 == 0 else kr_hi
    vm_ref[...] = kv[:, nope_w:].astype(bf)

    gm_ref[...] = _silu(_dot(h, win_ref[:, 512:1024])).astype(bf)

    q_s = _dot(h, win_ref[:, 1024:1536])
    s_scale = LOG2E / math.sqrt(SWA_D)
    for j in range(SWA_WIDTH // LANES):
        sl = slice(j * LANES, (j + 1) * LANES)
        qs_ref[:, sl] = (_rope(q_s[:, sl], cs, sus, sds, SWA_D // 2) * s_scale).astype(bf)

    kv_s = _dot(h, win_ref[:, 1536:2048])
    for j in range(SWA_KV_HEADS):
        k2 = _rope(kv_s[:, j * LANES:(j + 1) * LANES], cs, sus, sds, SWA_D // 2)
        v2 = kv_s[:, 256 + j * LANES: 256 + (j + 1) * LANES]
        ks_ref[:, (2 * j) * LANES:(2 * j + 1) * LANES] = jnp.where(lo, k2, 0.0).astype(bf)
        ks_ref[:, (2 * j + 1) * LANES:(2 * j + 2) * LANES] = jnp.where(lo, 0.0, k2).astype(bf)
        vs_ref[:, (2 * j) * LANES:(2 * j + 1) * LANES] = jnp.where(lo, v2, 0.0).astype(bf)
        vs_ref[:, (2 * j + 1) * LANES:(2 * j + 2) * LANES] = jnp.where(lo, 0.0, v2).astype(bf)

    gs_ref[...] = _silu(_dot(h, win_ref[:, 2048:2560])).astype(bf)


def _proj_call(x2, ln_g, w_in_p, q_g, w_q_p, kv_g, w_kv_p, tabs_m, tabs_s, seq):
    n, d = x2.shape
    tm = PROJ_ROWS
    steps_per_seq = seq // tm
    row = lambda i: (i, 0)
    const = lambda i: (0, 0)
    tab = lambda i: (i % steps_per_seq, 0)
    bf = jnp.bfloat16
    out_widths = (1024, 1024, 512, 512, 512, 512, 512, 512)
    return pl.pallas_call(
        _proj_kernel,
        grid=(n // tm,),
        in_specs=[
            pl.BlockSpec((tm, d), row),
            pl.BlockSpec((1, d), const),
            pl.BlockSpec(w_in_p.shape, const),
            pl.BlockSpec((1, MLA_Q_RANK), const),
            pl.BlockSpec(w_q_p.shape, const),
            pl.BlockSpec((1, MLA_KV_RANK), const),
            pl.BlockSpec(w_kv_p.shape, const),
        ] + [pl.BlockSpec((tm, LANES), tab)] * 6,
        out_specs=[pl.BlockSpec((tm, w), row) for w in out_widths],
        out_shape=[jax.ShapeDtypeStruct((n, w), bf) for w in out_widths],
        compiler_params=pltpu.CompilerParams(
            dimension_semantics=("arbitrary",),
            vmem_limit_bytes=V7X_VMEM_LIMIT_BYTES),
        name="proj",
    )(x2, ln_g, w_in_p, q_g, w_q_p, kv_g, w_kv_p, *tabs_m, *tabs_s)


def _mla_kernel(q_ref, k_ref, v_ref, g_ref, o_ref, s_ref, p_ref):
    t = MLA_BLOCK
    seq = q_ref.shape[0]
    r = lax.broadcasted_iota(jnp.int32, (t, t), 0)
    c = lax.broadcasted_iota(jnp.int32, (t, t), 1)
    causal = c <= r
    for i in range(seq // t):
        slot = i % 2
        rows = slice(i * t, (i + 1) * t)
        q = q_ref[rows, :]
        m_l = None
        for j in range(i + 1):
            cols = slice(j * t, (j + 1) * t)
            s = _dot_nt(q, k_ref[cols, :])
            if j == i:
                s = jnp.where(causal, s, NEG_INF)
            s_ref[slot, :, cols] = s
            for u in range(t // LANES):
                su = s[:, u * LANES:(u + 1) * LANES]
                m_l = su if m_l is None else jnp.maximum(m_l, su)
        m = jnp.max(m_l, axis=1, keepdims=True)
        l_l = jnp.zeros((t, LANES), jnp.float32)
        for j in range(i + 1):
            cols = slice(j * t, (j + 1) * t)
            p = jnp.exp2(s_ref[slot, :, cols] - m)
            for u in range(t // LANES):
                l_l = l_l + p[:, u * LANES:(u + 1) * LANES]
            p_ref[slot, :, cols] = p.astype(jnp.bfloat16)
        l = jnp.sum(l_l, axis=1, keepdims=True)
        kv = (i + 1) * t
        acc = _dot(p_ref[slot, :, 0:kv], v_ref[0:kv, :])
        o_ref[rows, :] = (acc / l * g_ref[rows, :].astype(jnp.float32)).astype(o_ref.dtype)


def _mla_call(qm, km, vm, gm, batch, seq):
    n = qm.shape[0]
    t = MLA_BLOCK
    return pl.pallas_call(
        _mla_kernel,
        grid=(batch, MLA_HEADS),
        in_specs=[
            pl.BlockSpec((seq, 2 * LANES), lambda b, h: (b, h)),
            pl.BlockSpec((seq, 2 * LANES), lambda b, h: (b, h)),
            pl.BlockSpec((seq, MLA_V), lambda b, h: (b, h)),
            pl.BlockSpec((seq, MLA_V), lambda b, h: (b, h)),
        ],
        out_specs=pl.BlockSpec((seq, MLA_V), lambda b, h: (b, h)),
        out_shape=jax.ShapeDtypeStruct((n, MLA_WIDTH), jnp.bfloat16),
        scratch_shapes=[pltpu.VMEM((2, t, seq), jnp.float32),
                        pltpu.VMEM((2, t, seq), jnp.bfloat16)],
        compiler_params=pltpu.CompilerParams(
            dimension_semantics=("arbitrary", "arbitrary"),
            vmem_limit_bytes=V7X_VMEM_LIMIT_BYTES),
        name="mla",
    )(qm, km, vm, gm)


def _swa_kernel(sink_ref, q_ref, k_ref, v_ref, g_ref, o_ref):
    w = SWA_WINDOW
    hk = pl.program_id(1)
    nb = q_ref.shape[0] // w
    bf = jnp.bfloat16

    row = lax.broadcasted_iota(jnp.int32, (2 * w, 1), 0)
    sink = [sink_ref[hk * SWA_GROUP + i] * LOG2E for i in range(SWA_GROUP)]
    sink_lo = jnp.where(row < w, sink[0], sink[2])
    sink_hi = jnp.where(row < w, sink[1], sink[3])

    def block(n, kv_rows, valid):
        qrows = slice(n * w, (n + 1) * w)
        qst = jnp.concatenate([q_ref[qrows, 0:LANES], q_ref[qrows, LANES:2 * LANES]], axis=0)
        outs = []
        for half, snk in ((0, sink_lo), (1, sink_hi)):
            lanes = slice(half * LANES, (half + 1) * LANES)
            k = k_ref[kv_rows, lanes]
            v = v_ref[kv_rows, lanes]
            s = jnp.where(valid, _dot_nt(qst, k), NEG_INF)
            m = jnp.maximum(jnp.max(s, axis=1, keepdims=True), snk)
            p = jnp.exp2(s - m)
            den = jnp.sum(p, axis=1, keepdims=True) + jnp.exp2(snk - m)
            outs.append(_dot(p.astype(bf), v) / den)
        o = outs[0] + outs[1]
        g = g_ref[qrows, :].astype(jnp.float32)
        o_ref[qrows, 0:LANES] = (o[0:w] * g[:, 0:LANES]).astype(o_ref.dtype)
        o_ref[qrows, LANES:2 * LANES] = (o[w:2 * w] * g[:, LANES:2 * LANES]).astype(o_ref.dtype)

    qi0 = lax.broadcasted_iota(jnp.int32, (2 * w, w), 0) % w
    ki0 = lax.broadcasted_iota(jnp.int32, (2 * w, w), 1)
    block(0, slice(0, w), ki0 <= qi0)

    qi = lax.broadcasted_iota(jnp.int32, (2 * w, 2 * w), 0) % w
    ki = lax.broadcasted_iota(jnp.int32, (2 * w, 2 * w), 1)
    rel = qi + w - ki
    band = (rel >= 0) & (rel < SWA_WINDOW)
    for n in range(1, nb):
        block(n, slice((n - 1) * w, (n + 1) * w), band)


def _swa_call(sinks, qs, ks4, vs4, gs, batch, seq):
    n = qs.shape[0]
    blk = lambda b, h: (b, h)
    spec = pl.BlockSpec((seq, 2 * LANES), blk)
    return pl.pallas_call(
        _swa_kernel,
        grid=(batch, SWA_KV_HEADS),
        in_specs=[pl.BlockSpec(memory_space=pltpu.SMEM), spec, spec, spec, spec],
        out_specs=spec,
        out_shape=jax.ShapeDtypeStruct((n, SWA_WIDTH), jnp.bfloat16),
        compiler_params=pltpu.CompilerParams(
            dimension_semantics=("arbitrary", "arbitrary"),
            vmem_limit_bytes=V7X_VMEM_LIMIT_BYTES),
        name="swa",
    )(sinks, qs, ks4, vs4, gs)


def _out_kernel(x_ref, mm_ref, ms_ref, wa_ref, wb_ref, fg_ref, o_ref):
    y = x_ref[...] + _dot(mm_ref[...], wa_ref[...]) + _dot(ms_ref[...], wb_ref[...])
    o_ref[...] = _rms(y, fg_ref[...])


def _out_call(x2, mm, ms, w_a, w_b, fg):
    n, d = x2.shape
    tm = PROJ_ROWS
    row = lambda i: (i, 0)
    const = lambda i: (0, 0)
    return pl.pallas_call(
        _out_kernel,
        grid=(n // tm,),
        in_specs=[
            pl.BlockSpec((tm, d), row),
            pl.BlockSpec((tm, MLA_WIDTH), row),
            pl.BlockSpec((tm, SWA_WIDTH), row),
            pl.BlockSpec(w_a.shape, const),
            pl.BlockSpec(w_b.shape, const),
            pl.BlockSpec((1, d), const),
        ],
        out_specs=pl.BlockSpec((tm, d), row),
        out_shape=jax.ShapeDtypeStruct((n, d), jnp.float32),
        compiler_params=pltpu.CompilerParams(
            dimension_semantics=("arbitrary",),
            vmem_limit_bytes=V7X_VMEM_LIMIT_BYTES),
        name="out",
    )(x2, mm, ms, w_a, w_b, fg)


def _prep_weights(w_in, w_q_up, w_kv_up):
    bf = jnp.bfloat16
    o = 0
    parts = {}
    for name, width in (("c_q", MLA_Q_RANK), ("c_kv", MLA_KV_RANK), ("k_rope", MLA_ROPE),
                        ("g_mla", MLA_WIDTH), ("q_s", SWA_WIDTH),
                        ("k_s", SWA_KV_HEADS * SWA_D), ("v_s", SWA_KV_HEADS * SWA_D),
                        ("g_swa", SWA_WIDTH)):
        parts[name] = w_in[:, o:o + width]
        o += width

    def twice_each(w):
        return jnp.concatenate(
            [w[:, j * SWA_D:(j + 1) * SWA_D] for j in range(SWA_KV_HEADS) for _ in range(2)], axis=1)

    w_in_p = jnp.concatenate([
        parts["c_q"], parts["c_kv"], parts["k_rope"], parts["k_rope"], parts["g_mla"],
        parts["q_s"], twice_each(parts["k_s"]), twice_each(parts["v_s"]), parts["g_swa"],
    ], axis=1).astype(bf)
    assert w_in_p.shape[1] == PROJ_COLS

    wq = w_q_up.reshape(MLA_Q_RANK, MLA_HEADS, MLA_QK)
    w_q_p = jnp.concatenate([wq[:, :, :MLA_NOPE].reshape(MLA_Q_RANK, -1),
                             wq[:, :, MLA_NOPE:].reshape(MLA_Q_RANK, -1)], axis=1).astype(bf)
    wkv = w_kv_up.reshape(MLA_KV_RANK, MLA_HEADS, MLA_NOPE + MLA_V)
    w_kv_p = jnp.concatenate([wkv[:, :, :MLA_NOPE].reshape(MLA_KV_RANK, -1),
                              wkv[:, :, MLA_NOPE:].reshape(MLA_KV_RANK, -1)], axis=1).astype(bf)
    return w_in_p, w_q_p, w_kv_p


def _layer(x2, ln_g, w_in, q_g, w_q_up, kv_g, w_kv_up, sinks, w_out, tabs_m, tabs_s,
           batch, seq):
    w_in_p, w_q_p, w_kv_p = _prep_weights(w_in, w_q_up, w_kv_up)
    qm, km, vm, gm, qs, ks4, vs4, gs = _proj_call(
        x2, ln_g.reshape(1, -1), w_in_p, q_g.reshape(1, -1), w_q_p, kv_g.reshape(1, -1),
        w_kv_p, tabs_m, tabs_s, seq)
    mm = _mla_call(qm, km, vm, gm, batch, seq)
    ms = _swa_call(sinks, qs, ks4, vs4, gs, batch, seq)
    w_o = w_out.astype(jnp.bfloat16)
    return mm, ms, w_o[:MLA_WIDTH], w_o[MLA_WIDTH:]


def kernel(x, ln_mix, w_in, q_a_norm, w_q_up, kv_a_norm, w_kv_up, attn_sinks, w_out, final_norm):
    batch, seq, d = x.shape
    depth = ln_mix.shape[0]
    assert depth == 1, "final norm is fused into the single layer's output kernel"
    assert seq % MLA_BLOCK == 0 and seq % PROJ_ROWS == 0 and seq % SWA_WINDOW == 0
    tabs_m = _rope_tables(seq, MLA_ROPE)
    tabs_s = _rope_tables(seq, SWA_D)
    x2 = x.reshape(batch * seq, d)
    mm, ms, w_a, w_b = _layer(x2, ln_mix[0], w_in[0], q_a_norm[0], w_q_up[0], kv_a_norm[0],
                              w_kv_up[0], attn_sinks[0], w_out[0], tabs_m, tabs_s, batch, seq)
    out = _out_call(x2, mm, ms, w_a, w_b, final_norm.reshape(1, -1))
    return out.reshape(batch, seq, d)
```

```python
import functools
import math

import jax
import jax.numpy as jnp
from jax import lax
from jax.experimental import pallas as pl
from jax.experimental.pallas import tpu as pltpu

ROPE_THETA = 10000.0
NORM_EPS = 1e-6
NEG_INF = -1e30
LOG2E = 1.4426950408889634

MLA_HEADS = 4
MLA_NOPE = 128
MLA_ROPE = 64
MLA_V = 128
MLA_Q_RANK = 256
MLA_KV_RANK = 128
MLA_QK = MLA_NOPE + MLA_ROPE
MLA_WIDTH = MLA_HEADS * MLA_V

SWA_Q_HEADS = 8
SWA_KV_HEADS = 2
SWA_D = 64
SWA_WINDOW = 128
SWA_GROUP = SWA_Q_HEADS // SWA_KV_HEADS
SWA_WIDTH = SWA_Q_HEADS * SWA_D

LANES = 128
V7X_VMEM_LIMIT_BYTES = 56 * 1024 * 1024

PROJ_ROWS = 512
MLA_BLOCK = 256


def _rope_tables(seq, dim):
    half = dim // 2
    pos = jnp.arange(seq, dtype=jnp.float32)
    inv_freq = 1.0 / (ROPE_THETA ** (jnp.arange(0, dim, 2, dtype=jnp.float32) / dim))
    ang = pos[:, None] * inv_freq[None, :]
    cos, sin = jnp.cos(ang), jnp.sin(ang)
    reps = LANES // dim
    zero = jnp.zeros_like(sin)
    c = jnp.tile(jnp.concatenate([cos, cos], axis=1), (1, reps))
    s_up = jnp.tile(jnp.concatenate([zero, sin], axis=1), (1, reps))
    s_dn = jnp.tile(jnp.concatenate([-sin, zero], axis=1), (1, reps))
    return c, s_up, s_dn


def _rope(x, c, s_up, s_dn, half):
    return (x * c + pltpu.roll(x, half, 1) * s_up
            + pltpu.roll(x, LANES - half, 1) * s_dn)


def _rms(x, g):
    return x * lax.rsqrt(jnp.mean(x * x, axis=-1, keepdims=True) + NORM_EPS) * g


def _silu(g):
    return g / (1.0 + jnp.exp(-g))


def _dot(a, b):
    return jnp.dot(a, b, preferred_element_type=jnp.float32)


def _dot_nt(a, b):
    return lax.dot_general(a, b, (((1,), (1,)), ((), ())),
                           preferred_element_type=jnp.float32)


PROJ_COLS = 2560


def _proj_kernel(x_ref, ln_ref, win_ref, qg_ref, wq_ref, kvg_ref, wkv_ref,
                 cm_ref, sum_ref, sdm_ref, cs_ref, sus_ref, sds_ref,
                 qm_ref, km_ref, vm_ref, gm_ref, qs_ref, ks_ref, vs_ref, gs_ref):
    bf = jnp.bfloat16
    x = x_ref[...]
    h = _rms(x, ln_ref[...]).astype(bf)

    cm, sum_, sdm = cm_ref[...], sum_ref[...], sdm_ref[...]
    cs, sus, sds = cs_ref[...], sus_ref[...], sds_ref[...]
    lane = lax.broadcasted_iota(jnp.int32, (x.shape[0], LANES), 1)
    lo = lane < (LANES // 2)

    lat = _dot(h, win_ref[:, 0:512])
    c_q = lat[:, 0:MLA_Q_RANK]
    c_kv = lat[:, MLA_Q_RANK:MLA_Q_RANK + MLA_KV_RANK]
    kr = _rope(lat[:, 384:512], cm, sum_, sdm, MLA_ROPE // 2)
    q = _dot(_rms(c_q, qg_ref[...]).astype(bf), wq_ref[...])
    kv = _dot(_rms(c_kv, kvg_ref[...]).astype(bf), wkv_ref[...])

    q_scale = LOG2E / math.sqrt(MLA_QK)
    kr_lo = jnp.where(lo, kr, 0.0).astype(bf)
    kr_hi = jnp.where(lo, 0.0, kr).astype(bf)
    nope_w = MLA_HEADS * MLA_NOPE
    q_rope = [
        (_rope(q[:, nope_w + j * LANES: nope_w + (j + 1) * LANES], cm, sum_, sdm,
               MLA_ROPE // 2) * q_scale).astype(bf)
        for j in range(MLA_HEADS * MLA_ROPE // LANES)
    ]
    for hd in range(MLA_HEADS):
        base = hd * 2 * LANES
        qm_ref[:, base:base + LANES] = (q[:, hd * MLA_NOPE:(hd + 1) * MLA_NOPE] * q_scale).astype(bf)
        qm_ref[:, base + LANES:base + 2 * LANES] = q_rope[hd // 2]
        km_ref[:, base:base + LANES] = kv[:, hd * MLA_NOPE:(hd + 1) * MLA_NOPE].astype(bf)
        km_ref[:, base + LANES:base + 2 * LANES] = kr_lo if hd % 2 == 0 else kr_hi
    vm_ref[...] = kv[:, nope_w:].astype(bf)

    gm_ref[...] = _silu(_dot(h, win_ref[:, 512:1024])).astype(bf)

    q_s = _dot(h, win_ref[:, 1024:1536])
    s_scale = LOG2E / math.sqrt(SWA_D)
    for j in range(SWA_WIDTH // LANES):
        sl = slice(j * LANES, (j + 1) * LANES)
        qs_ref[:, sl] = (_rope(q_s[:, sl], cs, sus, sds, SWA_D // 2) * s_scale).astype(bf)

    kv_s = _dot(h, win_ref[:, 1536:2048])
    for j in range(SWA_KV_HEADS):
        k2 = _rope(kv_s[:, j * LANES:(j + 1) * LANES], cs, sus, sds, SWA_D // 2)
        v2 = kv_s[:, 256 + j * LANES: 256 + (j + 1) * LANES]
        ks_ref[:, (2 * j) * LANES:(2 * j + 1) * LANES] = jnp.where(lo, k2, 0.0).astype(bf)
        ks_ref[:, (2 * j + 1) * LANES:(2 * j + 2) * LANES] = jnp.where(lo, 0.0, k2).astype(bf)
        vs_ref[:, (2 * j) * LANES:(2 * j + 1) * LANES] = jnp.where(lo, v2, 0.0).astype(bf)
        vs_ref[:, (2 * j + 1) * LANES:(2 * j + 2) * LANES] = jnp.where(lo, 0.0, v2).astype(bf)

    gs_ref[...] = _silu(_dot(h, win_ref[:, 2048:2560])).astype(bf)


def _proj_call(x2, ln_g, w_in_p, q_g, w_q_p, kv_g, w_kv_p, tabs_m, tabs_s, seq):
    n, d = x2.shape
    tm = PROJ_ROWS
    steps_per_seq = seq // tm
    row = lambda i: (i, 0)
    const = lambda i: (0, 0)
    tab = lambda i: (i % steps_per_seq, 0)
    bf = jnp.bfloat16
    out_widths = (1024, 1024, 512, 512, 512, 512, 512, 512)
    return pl.pallas_call(
        _proj_kernel,
        grid=(n // tm,),
        in_specs=[
            pl.BlockSpec((tm, d), row),
            pl.BlockSpec((1, d), const),
            pl.BlockSpec(w_in_p.shape, const),
            pl.BlockSpec((1, MLA_Q_RANK), const),
            pl.BlockSpec(w_q_p.shape, const),
            pl.BlockSpec((1, MLA_KV_RANK), const),
            pl.BlockSpec(w_kv_p.shape, const),
        ] + [pl.BlockSpec((tm, LANES), tab)] * 6,
        out_specs=[pl.BlockSpec((tm, w), row) for w in out_widths],
        out_shape=[jax.ShapeDtypeStruct((n, w), bf) for w in out_widths],
        compiler_params=pltpu.CompilerParams(
            dimension_semantics=("arbitrary",),
            vmem_limit_bytes=V7X_VMEM_LIMIT_BYTES),
        name="proj",
    )(x2, ln_g, w_in_p, q_g, w_q_p, kv_g, w_kv_p, *tabs_m, *tabs_s)


def _mla_kernel(q_ref, k_ref, v_ref, g_ref, o_ref, s_ref, p_ref):
    t = MLA_BLOCK
    seq = q_ref.shape[0]
    r = lax.broadcasted_iota(jnp.int32, (t, t), 0)
    c = lax.broadcasted_iota(jnp.int32, (t, t), 1)
    causal = c <= r
    nblk = seq // t
    state = [dict(m_l=None, l_l=None) for _ in range(nblk)]

    def score_tile(i, j):
        st = state[i]
        cols = slice(j * t, (j + 1) * t)
        s = _dot_nt(q_ref[i * t:(i + 1) * t, :], k_ref[cols, :])
        if j == i:
            s = jnp.where(causal, s, NEG_INF)
        s_ref[i % 2, :, cols] = s
        for u in range(t // LANES):
            su = s[:, u * LANES:(u + 1) * LANES]
            st["m_l"] = su if st["m_l"] is None else jnp.maximum(st["m_l"], su)

    def prob_tile(i, j):
        st = state[i]
        if "m" not in st:
            st["m"] = jnp.max(st["m_l"], axis=1, keepdims=True)
        cols = slice(j * t, (j + 1) * t)
        p = jnp.exp2(s_ref[i % 2, :, cols] - st["m"])
        for u in range(t // LANES):
            pu = p[:, u * LANES:(u + 1) * LANES]
            st["l_l"] = pu if st["l_l"] is None else st["l_l"] + pu
        p_ref[i % 2, :, cols] = p.astype(jnp.bfloat16)

    def finish(i):
        rows = slice(i * t, (i + 1) * t)
        l = jnp.sum(state[i]["l_l"], axis=1, keepdims=True)
        kv = (i + 1) * t
        acc = _dot(p_ref[i % 2, :, 0:kv], v_ref[0:kv, :])
        o_ref[rows, :] = (acc / l * g_ref[rows, :].astype(jnp.float32)).astype(o_ref.dtype)

    score_tile(0, 0)
    for i in range(nblk):
        nxt = [functools.partial(score_tile, i + 1, j) for j in range(i + 2)] if i + 1 < nblk else []
        cur = [functools.partial(prob_tile, i, j) for j in range(i + 1)]
        while nxt or cur:
            if nxt:
                nxt.pop(0)()
            if cur:
                cur.pop(0)()
        finish(i)


def _mla_call(qm, km, vm, gm, batch, seq):
    n = qm.shape[0]
    t = MLA_BLOCK
    return pl.pallas_call(
        _mla_kernel,
        grid=(batch, MLA_HEADS),
        in_specs=[
            pl.BlockSpec((seq, 2 * LANES), lambda b, h: (b, h)),
            pl.BlockSpec((seq, 2 * LANES), lambda b, h: (b, h)),
            pl.BlockSpec((seq, MLA_V), lambda b, h: (b, h)),
            pl.BlockSpec((seq, MLA_V), lambda b, h: (b, h)),
        ],
        out_specs=pl.BlockSpec((seq, MLA_V), lambda b, h: (b, h)),
        out_shape=jax.ShapeDtypeStruct((n, MLA_WIDTH), jnp.bfloat16),
        scratch_shapes=[pltpu.VMEM((2, t, seq), jnp.float32),
                        pltpu.VMEM((2, t, seq), jnp.bfloat16)],
        compiler_params=pltpu.CompilerParams(
            dimension_semantics=("arbitrary", "arbitrary"),
            vmem_limit_bytes=V7X_VMEM_LIMIT_BYTES),
        name="mla",
    )(qm, km, vm, gm)


def _swa_kernel(sink_ref, q_ref, k_ref, v_ref, g_ref, o_ref):
    w = SWA_WINDOW
    hk = pl.program_id(1)
    nb = q_ref.shape[0] // w
    bf = jnp.bfloat16

    row = lax.broadcasted_iota(jnp.int32, (2 * w, 1), 0)
    sink = [sink_ref[hk * SWA_GROUP + i] * LOG2E for i in range(SWA_GROUP)]
    sink_lo = jnp.where(row < w, sink[0], sink[2])
    sink_hi = jnp.where(row < w, sink[1], sink[3])

    def block(n, kv_rows, valid):
        qrows = slice(n * w, (n + 1) * w)
        qst = jnp.concatenate([q_ref[qrows, 0:LANES], q_ref[qrows, LANES:2 * LANES]], axis=0)
        outs = []
        for half, snk in ((0, sink_lo), (1, sink_hi)):
            lanes = slice(half * LANES, (half + 1) * LANES)
            k = k_ref[kv_rows, lanes]
            v = v_ref[kv_rows, lanes]
            s = jnp.where(valid, _dot_nt(qst, k), NEG_INF)
            m = jnp.maximum(jnp.max(s, axis=1, keepdims=True), snk)
            p = jnp.exp2(s - m)
            den = jnp.sum(p, axis=1, keepdims=True) + jnp.exp2(snk - m)
            outs.append(_dot(p.astype(bf), v) / den)
        o = outs[0] + outs[1]
        g = g_ref[qrows, :].astype(jnp.float32)
        o_ref[qrows, 0:LANES] = (o[0:w] * g[:, 0:LANES]).astype(o_ref.dtype)
        o_ref[qrows, LANES:2 * LANES] = (o[w:2 * w] * g[:, LANES:2 * LANES]).astype(o_ref.dtype)

    qi0 = lax.broadcasted_iota(jnp.int32, (2 * w, w), 0) % w
    ki0 = lax.broadcasted_iota(jnp.int32, (2 * w, w), 1)
    block(0, slice(0, w), ki0 <= qi0)

    qi = lax.broadcasted_iota(jnp.int32, (2 * w, 2 * w), 0) % w
    ki = lax.broadcasted_iota(jnp.int32, (2 * w, 2 * w), 1)
    rel = qi + w - ki
    band = (rel >= 0) & (rel < SWA_WINDOW)
    for n in range(1, nb):
        block(n, slice((n - 1) * w, (n + 1) * w), band)


def _swa_call(sinks, qs, ks4, vs4, gs, batch, seq):
    n = qs.shape[0]
    blk = lambda b, h: (b, h)
    spec = pl.BlockSpec((seq, 2 * LANES), blk)
    return pl.pallas_call(
        _swa_kernel,
        grid=(batch, SWA_KV_HEADS),
        in_specs=[pl.BlockSpec(memory_space=pltpu.SMEM), spec, spec, spec, spec],
        out_specs=spec,
        out_shape=jax.ShapeDtypeStruct((n, SWA_WIDTH), jnp.bfloat16),
        compiler_params=pltpu.CompilerParams(
            dimension_semantics=("arbitrary", "arbitrary"),
            vmem_limit_bytes=V7X_VMEM_LIMIT_BYTES),
        name="swa",
    )(sinks, qs, ks4, vs4, gs)


def _out_kernel(x_ref, mm_ref, ms_ref, wa_ref, wb_ref, fg_ref, o_ref):
    y = x_ref[...] + _dot(mm_ref[...], wa_ref[...]) + _dot(ms_ref[...], wb_ref[...])
    o_ref[...] = _rms(y, fg_ref[...])


def _out_call(x2, mm, ms, w_a, w_b, fg):
    n, d = x2.shape
    tm = PROJ_ROWS
    row = lambda i: (i, 0)
    const = lambda i: (0, 0)
    return pl.pallas_call(
        _out_kernel,
        grid=(n // tm,),
        in_specs=[
            pl.BlockSpec((tm, d), row),
            pl.BlockSpec((tm, MLA_WIDTH), row),
            pl.BlockSpec((tm, SWA_WIDTH), row),
            pl.BlockSpec(w_a.shape, const),
            pl.BlockSpec(w_b.shape, const),
            pl.BlockSpec((1, d), const),
        ],
        out_specs=pl.BlockSpec((tm, d), row),
        out_shape=jax.ShapeDtypeStruct((n, d), jnp.float32),
        compiler_params=pltpu.CompilerParams(
            dimension_semantics=("arbitrary",),
            vmem_limit_bytes=V7X_VMEM_LIMIT_BYTES),
        name="out",
    )(x2, mm, ms, w_a, w_b, fg)


def _prep_weights(w_in, w_q_up, w_kv_up):
    bf = jnp.bfloat16
    o = 0
    parts = {}
    for name, width in (("c_q", MLA_Q_RANK), ("c_kv", MLA_KV_RANK), ("k_rope", MLA_ROPE),
                        ("g_mla", MLA_WIDTH), ("q_s", SWA_WIDTH),
                        ("k_s", SWA_KV_HEADS * SWA_D), ("v_s", SWA_KV_HEADS * SWA_D),
                        ("g_swa", SWA_WIDTH)):
        parts[name] = w_in[:, o:o + width]
        o += width

    def twice_each(w):
        return jnp.concatenate(
            [w[:, j * SWA_D:(j + 1) * SWA_D] for j in range(SWA_KV_HEADS) for _ in range(2)], axis=1)

    w_in_p = jnp.concatenate([
        parts["c_q"], parts["c_kv"], parts["k_rope"], parts["k_rope"], parts["g_mla"],
        parts["q_s"], twice_each(parts["k_s"]), twice_each(parts["v_s"]), parts["g_swa"],
    ], axis=1).astype(bf)
    assert w_in_p.shape[1] == PROJ_COLS

    wq = w_q_up.reshape(MLA_Q_RANK, MLA_HEADS, MLA_QK)
    w_q_p = jnp.concatenate([wq[:, :, :MLA_NOPE].reshape(MLA_Q_RANK, -1),
                             wq[:, :, MLA_NOPE:].reshape(MLA_Q_RANK, -1)], axis=1).astype(bf)
    wkv = w_kv_up.reshape(MLA_KV_RANK, MLA_HEADS, MLA_NOPE + MLA_V)
    w_kv_p = jnp.concatenate([wkv[:, :, :MLA_NOPE].reshape(MLA_KV_RANK, -1),
                              wkv[:, :, MLA_NOPE:].reshape(MLA_KV_RANK, -1)], axis=1).astype(bf)
    return w_in_p, w_q_p, w_kv_p


def _layer(x2, ln_g, w_in, q_g, w_q_up, kv_g, w_kv_up, sinks, w_out, tabs_m, tabs_s,
           batch, seq):
    w_in_p, w_q_p, w_kv_p = _prep_weights(w_in, w_q_up, w_kv_up)
    qm, km, vm, gm, qs, ks4, vs4, gs = _proj_call(
        x2, ln_g.reshape(1, -1), w_in_p, q_g.reshape(1, -1), w_q_p, kv_g.reshape(1, -1),
        w_kv_p, tabs_m, tabs_s, seq)
    mm = _mla_call(qm, km, vm, gm, batch, seq)
    ms = _swa_call(sinks, qs, ks4, vs4, gs, batch, seq)
    w_o = w_out.astype(jnp.bfloat16)
    return mm, ms, w_o[:MLA_WIDTH], w_o[MLA_WIDTH:]


def kernel(x, ln_mix, w_in, q_a_norm, w_q_up, kv_a_norm, w_kv_up, attn_sinks, w_out, final_norm):
    batch, seq, d = x.shape
    depth = ln_mix.shape[0]
    assert depth == 1, "final norm is fused into the single layer's output kernel"
    assert seq % MLA_BLOCK == 0 and seq % PROJ_ROWS == 0 and seq % SWA_WINDOW == 0
    tabs_m = _rope_tables(seq, MLA_ROPE)
    tabs_s = _rope_tables(seq, SWA_D)
    x2 = x.reshape(batch * seq, d)
    mm, ms, w_a, w_b = _layer(x2, ln_mix[0], w_in[0], q_a_norm[0], w_q_up[0], kv_a_norm[0],
                              w_kv_up[0], attn_sinks[0], w_out[0], tabs_m, tabs_s, batch, seq)
    out = _out_call(x2, mm, ms, w_a, w_b, final_norm.reshape(1, -1))
    return out.reshape(batch, seq, d)
```

```python
import functools
import math

import jax
import jax.numpy as jnp
from jax import lax
from jax.experimental import pallas as pl
from jax.experimental.pallas import tpu as pltpu

ROPE_THETA = 10000.0
NORM_EPS = 1e-6
NEG_INF = -1e30
LOG2E = 1.4426950408889634

MLA_HEADS = 4
MLA_NOPE = 128
MLA_ROPE = 64
MLA_V = 128
MLA_Q_RANK = 256
MLA_KV_RANK = 128
MLA_QK = MLA_NOPE + MLA_ROPE
MLA_WIDTH = MLA_HEADS * MLA_V

SWA_Q_HEADS = 8
SWA_KV_HEADS = 2
SWA_D = 64
SWA_WINDOW = 128
SWA_GROUP = SWA_Q_HEADS // SWA_KV_HEADS
SWA_WIDTH = SWA_Q_HEADS * SWA_D

LANES = 128
V7X_VMEM_LIMIT_BYTES = 56 * 1024 * 1024

PROJ_ROWS = 512
MLA_BLOCK = 256
SWA_OUT_ROWS = 1024
SWA_OUT_SUBTILE = 256


def _rope_tables(seq, dim):
    half = dim // 2
    pos = jnp.arange(seq, dtype=jnp.float32)
    inv_freq = 1.0 / (ROPE_THETA ** (jnp.arange(0, dim, 2, dtype=jnp.float32) / dim))
    ang = pos[:, None] * inv_freq[None, :]
    cos, sin = jnp.cos(ang), jnp.sin(ang)
    reps = LANES // dim
    zero = jnp.zeros_like(sin)
    c = jnp.tile(jnp.concatenate([cos, cos], axis=1), (1, reps))
    s_up = jnp.tile(jnp.concatenate([zero, sin], axis=1), (1, reps))
    s_dn = jnp.tile(jnp.concatenate([-sin, zero], axis=1), (1, reps))
    return c, s_up, s_dn


def _rope(x, c, s_up, s_dn, half):
    return (x * c + pltpu.roll(x, half, 1) * s_up
            + pltpu.roll(x, LANES - half, 1) * s_dn)


def _rms(x, g):
    return x * lax.rsqrt(jnp.mean(x * x, axis=-1, keepdims=True) + NORM_EPS) * g


def _silu(g):
    return g / (1.0 + jnp.exp(-g))


def _dot(a, b):
    return jnp.dot(a, b, preferred_element_type=jnp.float32)


def _dot_nt(a, b):
    return lax.dot_general(a, b, (((1,), (1,)), ((), ())),
                           preferred_element_type=jnp.float32)


PROJ_COLS = 2560


def _proj_kernel(x_ref, ln_ref, win_ref, qg_ref, wq_ref, kvg_ref, wkv_ref,
                 cm_ref, sum_ref, sdm_ref, cs_ref, sus_ref, sds_ref,
                 qm_ref, km_ref, vm_ref, gm_ref, qs_ref, ks_ref, vs_ref, gs_ref):
    bf = jnp.bfloat16
    x = x_ref[...]
    h = _rms(x, ln_ref[...]).astype(bf)

    cm, sum_, sdm = cm_ref[...], sum_ref[...], sdm_ref[...]
    cs, sus, sds = cs_ref[...], sus_ref[...], sds_ref[...]
    lane = lax.broadcasted_iota(jnp.int32, (x.shape[0], LANES), 1)
    lo = lane < (LANES // 2)

    lat = _dot(h, win_ref[:, 0:512])
    c_q = lat[:, 0:MLA_Q_RANK]
    c_kv = lat[:, MLA_Q_RANK:MLA_Q_RANK + MLA_KV_RANK]
    kr = _rope(lat[:, 384:512], cm, sum_, sdm, MLA_ROPE // 2)
    q = _dot(_rms(c_q, qg_ref[...]).astype(bf), wq_ref[...])
    kv = _dot(_rms(c_kv, kvg_ref[...]).astype(bf), wkv_ref[...])

    q_scale = LOG2E / math.sqrt(MLA_QK)
    kr_lo = jnp.where(lo, kr, 0.0).astype(bf)
    kr_hi = jnp.where(lo, 0.0, kr).astype(bf)
    nope_w = MLA_HEADS * MLA_NOPE
    q_rope = [
        (_rope(q[:, nope_w + j * LANES: nope_w + (j + 1) * LANES], cm, sum_, sdm,
               MLA_ROPE // 2) * q_scale).astype(bf)
        for j in range(MLA_HEADS * MLA_ROPE // LANES)
    ]
    for hd in range(MLA_HEADS):
        base = hd * 2 * LANES
        qm_ref[:, base:base + LANES] = (q[:, hd * MLA_NOPE:(hd + 1) * MLA_NOPE] * q_scale).astype(bf)
        qm_ref[:, base + LANES:base + 2 * LANES] = q_rope[hd // 2]
        km_ref[:, base:base + LANES] = kv[:, hd * MLA_NOPE:(hd + 1) * MLA_NOPE].astype(bf)
        km_ref[:, base + LANES:base + 2 * LANES] = kr_lo if hd % 2 == 0 else kr_hi
    vm_ref[...] = kv[:, nope_w:].astype(bf)

    gm_ref[...] = _silu(_dot(h, win_ref[:, 512:1024])).astype(bf)

    q_s = _dot(h, win_ref[:, 1024:1536])
    s_scale = LOG2E / math.sqrt(SWA_D)
    for j in range(SWA_WIDTH // LANES):
        sl = slice(j * LANES, (j + 1) * LANES)
        qs_ref[:, sl] = (_rope(q_s[:, sl], cs, sus, sds, SWA_D // 2) * s_scale).astype(bf)

    kv_s = _dot(h, win_ref[:, 1536:2048])
    for j in range(SWA_KV_HEADS):
        k2 = _rope(kv_s[:, j * LANES:(j + 1) * LANES], cs, sus, sds, SWA_D // 2)
        v2 = kv_s[:, 256 + j * LANES: 256 + (j + 1) * LANES]
        ks_ref[:, (2 * j) * LANES:(2 * j + 1) * LANES] = jnp.where(lo, k2, 0.0).astype(bf)
        ks_ref[:, (2 * j + 1) * LANES:(2 * j + 2) * LANES] = jnp.where(lo, 0.0, k2).astype(bf)
        vs_ref[:, (2 * j) * LANES:(2 * j + 1) * LANES] = jnp.where(lo, v2, 0.0).astype(bf)
        vs_ref[:, (2 * j + 1) * LANES:(2 * j + 2) * LANES] = jnp.where(lo, 0.0, v2).astype(bf)

    gs_ref[...] = _silu(_dot(h, win_ref[:, 2048:2560])).astype(bf)


def _proj_call(x2, ln_g, w_in_p, q_g, w_q_p, kv_g, w_kv_p, tabs_m, tabs_s, seq):
    n, d = x2.shape
    tm = PROJ_ROWS
    steps_per_seq = seq // tm
    row = lambda i: (i, 0)
    const = lambda i: (0, 0)
    tab = lambda i: (i % steps_per_seq, 0)
    bf = jnp.bfloat16
    out_widths = (1024, 1024, 512, 512, 512, 512, 512, 512)
    return pl.pallas_call(
        _proj_kernel,
        grid=(n // tm,),
        in_specs=[
            pl.BlockSpec((tm, d), row),
            pl.BlockSpec((1, d), const),
            pl.BlockSpec(w_in_p.shape, const),
            pl.BlockSpec((1, MLA_Q_RANK), const),
            pl.BlockSpec(w_q_p.shape, const),
            pl.BlockSpec((1, MLA_KV_RANK), const),
            pl.BlockSpec(w_kv_p.shape, const),
        ] + [pl.BlockSpec((tm, LANES), tab)] * 6,
        out_specs=[pl.BlockSpec((tm, w), row) for w in out_widths],
        out_shape=[jax.ShapeDtypeStruct((n, w), bf) for w in out_widths],
        compiler_params=pltpu.CompilerParams(
            dimension_semantics=("arbitrary",),
            vmem_limit_bytes=V7X_VMEM_LIMIT_BYTES),
        name="proj",
    )(x2, ln_g, w_in_p, q_g, w_q_p, kv_g, w_kv_p, *tabs_m, *tabs_s)


def _mla_kernel(q_ref, k_ref, v_ref, g_ref, o_ref, s_ref, p_ref):
    t = MLA_BLOCK
    seq = q_ref.shape[0]
    r = lax.broadcasted_iota(jnp.int32, (t, t), 0)
    c = lax.broadcasted_iota(jnp.int32, (t, t), 1)
    causal = c <= r
    nblk = seq // t
    state = [dict(m_l=None, l_l=None) for _ in range(nblk)]

    def score_tile(i, j):
        st = state[i]
        cols = slice(j * t, (j + 1) * t)
        s = _dot_nt(q_ref[i * t:(i + 1) * t, :], k_ref[cols, :])
        if j == i:
            s = jnp.where(causal, s, NEG_INF)
        s_ref[i % 2, :, cols] = s
        for u in range(t // LANES):
            su = s[:, u * LANES:(u + 1) * LANES]
            st["m_l"] = su if st["m_l"] is None else jnp.maximum(st["m_l"], su)

    def prob_tile(i, j):
        st = state[i]
        if "m" not in st:
            st["m"] = jnp.max(st["m_l"], axis=1, keepdims=True)
        cols = slice(j * t, (j + 1) * t)
        p = jnp.exp2(s_ref[i % 2, :, cols] - st["m"])
        for u in range(t // LANES):
            pu = p[:, u * LANES:(u + 1) * LANES]
            st["l_l"] = pu if st["l_l"] is None else st["l_l"] + pu
        p_ref[i % 2, :, cols] = p.astype(jnp.bfloat16)

    def finish(i):
        rows = slice(i * t, (i + 1) * t)
        l = jnp.sum(state[i]["l_l"], axis=1, keepdims=True)
        kv = (i + 1) * t
        acc = _dot(p_ref[i % 2, :, 0:kv], v_ref[0:kv, :])
        o_ref[rows, :] = (acc / l * g_ref[rows, :].astype(jnp.float32)).astype(o_ref.dtype)

    score_tile(0, 0)
    for i in range(nblk):
        nxt = [functools.partial(score_tile, i + 1, j) for j in range(i + 2)] if i + 1 < nblk else []
        cur = [functools.partial(prob_tile, i, j) for j in range(i + 1)]
        while nxt or cur:
            if nxt:
                nxt.pop(0)()
            if cur:
                cur.pop(0)()
        finish(i)


def _mla_call(qm, km, vm, gm, batch, seq):
    n = qm.shape[0]
    t = MLA_BLOCK
    return pl.pallas_call(
        _mla_kernel,
        grid=(batch, MLA_HEADS),
        in_specs=[
            pl.BlockSpec((seq, 2 * LANES), lambda b, h: (b, h)),
            pl.BlockSpec((seq, 2 * LANES), lambda b, h: (b, h)),
            pl.BlockSpec((seq, MLA_V), lambda b, h: (b, h)),
            pl.BlockSpec((seq, MLA_V), lambda b, h: (b, h)),
        ],
        out_specs=pl.BlockSpec((seq, MLA_V), lambda b, h: (b, h)),
        out_shape=jax.ShapeDtypeStruct((n, MLA_WIDTH), jnp.bfloat16),
        scratch_shapes=[pltpu.VMEM((2, t, seq), jnp.float32),
                        pltpu.VMEM((2, t, seq), jnp.bfloat16)],
        compiler_params=pltpu.CompilerParams(
            dimension_semantics=("arbitrary", "arbitrary"),
            vmem_limit_bytes=V7X_VMEM_LIMIT_BYTES),
        name="mla",
    )(qm, km, vm, gm)


def _swa_out_kernel(sink_ref, q_ref, k_ref, v_ref, kh_ref, vh_ref, g_ref, mm_ref, x_ref,
                    w_ref, fg_ref, o_ref, ms_ref):
    w = SWA_WINDOW
    chunk = pl.program_id(1)
    rows_total = q_ref.shape[0]
    bf = jnp.bfloat16

    row = lax.broadcasted_iota(jnp.int32, (2 * w, 1), 0)
    qi = lax.broadcasted_iota(jnp.int32, (2 * w, 2 * w), 0) % w
    ki = lax.broadcasted_iota(jnp.int32, (2 * w, 2 * w), 1)
    rel = qi + w - ki
    band = (rel >= 0) & (rel < SWA_WINDOW)
    band_first = band & ((ki >= w) | (chunk > 0))

    def swa_block(hk, n):
        sink = [sink_ref[hk * SWA_GROUP + i] * LOG2E for i in range(SWA_GROUP)]
        sinks = (jnp.where(row < w, sink[0], sink[2]), jnp.where(row < w, sink[1], sink[3]))
        base = hk * 2 * LANES
        qrows = slice(n * w, (n + 1) * w)
        qst = jnp.concatenate([q_ref[qrows, base:base + LANES],
                               q_ref[qrows, base + LANES:base + 2 * LANES]], axis=0)
        outs = []
        for half in range(2):
            lanes = slice(base + half * LANES, base + (half + 1) * LANES)
            if n == 0:
                k = jnp.concatenate([kh_ref[:, lanes], k_ref[0:w, lanes]], axis=0)
                v = jnp.concatenate([vh_ref[:, lanes], v_ref[0:w, lanes]], axis=0)
                valid = band_first
            else:
                k = k_ref[(n - 1) * w:(n + 1) * w, lanes]
                v = v_ref[(n - 1) * w:(n + 1) * w, lanes]
                valid = band
            s = jnp.where(valid, _dot_nt(qst, k), NEG_INF)
            m = jnp.maximum(jnp.max(s, axis=1, keepdims=True), sinks[half])
            p = jnp.exp2(s - m)
            den = jnp.sum(p, axis=1, keepdims=True) + jnp.exp2(sinks[half] - m)
            outs.append(_dot(p.astype(bf), v) / den)
        o = outs[0] + outs[1]
        g = g_ref[qrows, base:base + 2 * LANES].astype(jnp.float32)
        ms_ref[qrows, base:base + LANES] = (o[0:w] * g[:, 0:LANES]).astype(bf)
        ms_ref[qrows, base + LANES:base + 2 * LANES] = (o[w:2 * w] * g[:, LANES:2 * LANES]).astype(bf)

    t = SWA_OUT_SUBTILE

    def out_tile(i):
        rows = slice(i * t, (i + 1) * t)
        y = (x_ref[rows, :] + _dot(mm_ref[rows, :], w_ref[0:MLA_WIDTH, :])
             + _dot(ms_ref[rows, :], w_ref[MLA_WIDTH:, :]))
        o_ref[rows, :] = _rms(y, fg_ref[...])

    def swa_tasks(i):
        return [functools.partial(swa_block, hk, n)
                for n in range(i * t // w, (i + 1) * t // w) for hk in range(SWA_KV_HEADS)]

    n_sub = rows_total // t
    for task in swa_tasks(0):
        task()
    for i in range(n_sub):
        nxt = swa_tasks(i + 1) if i + 1 < n_sub else []
        if nxt:
            nxt.pop(0)()
        out_tile(i)
        for task in nxt:
            task()


def _swa_out_call(sinks, qs, ks4, vs4, gs, mm, x2, w_o, fg, batch, seq):
    n, d = x2.shape
    r = SWA_OUT_ROWS
    cps = seq // r
    bpc = r // SWA_WINDOW
    bps = seq // SWA_WINDOW
    row = lambda b, c: (b * cps + c, 0)
    halo = lambda b, c: (b * bps + jnp.maximum(c * bpc - 1, 0), 0)
    const = lambda b, c: (0, 0)
    wide = pl.BlockSpec((r, SWA_WIDTH), row)
    return pl.pallas_call(
        _swa_out_kernel,
        grid=(batch, cps),
        in_specs=[
            pl.BlockSpec(memory_space=pltpu.SMEM),
            wide, wide, wide,
            pl.BlockSpec((SWA_WINDOW, SWA_WIDTH), halo),
            pl.BlockSpec((SWA_WINDOW, SWA_WIDTH), halo),
            wide,
            pl.BlockSpec((r, MLA_WIDTH), row),
            pl.BlockSpec((r, d), row),
            pl.BlockSpec(w_o.shape, const),
            pl.BlockSpec((1, d), const),
        ],
        out_specs=pl.BlockSpec((r, d), row),
        out_shape=jax.ShapeDtypeStruct((n, d), jnp.float32),
        scratch_shapes=[pltpu.VMEM((r, SWA_WIDTH), jnp.bfloat16)],
        compiler_params=pltpu.CompilerParams(
            dimension_semantics=("arbitrary", "arbitrary"),
            vmem_limit_bytes=V7X_VMEM_LIMIT_BYTES),
        name="swa_out",
    )(sinks, qs, ks4, vs4, ks4, vs4, gs, mm, x2, w_o, fg)


def _prep_weights(w_in, w_q_up, w_kv_up):
    bf = jnp.bfloat16
    o = 0
    parts = {}
    for name, width in (("c_q", MLA_Q_RANK), ("c_kv", MLA_KV_RANK), ("k_rope", MLA_ROPE),
                        ("g_mla", MLA_WIDTH), ("q_s", SWA_WIDTH),
                        ("k_s", SWA_KV_HEADS * SWA_D), ("v_s", SWA_KV_HEADS * SWA_D),
                        ("g_swa", SWA_WIDTH)):
        parts[name] = w_in[:, o:o + width]
        o += width

    def twice_each(w):
        return jnp.concatenate(
            [w[:, j * SWA_D:(j + 1) * SWA_D] for j in range(SWA_KV_HEADS) for _ in range(2)], axis=1)

    w_in_p = jnp.concatenate([
        parts["c_q"], parts["c_kv"], parts["k_rope"], parts["k_rope"], parts["g_mla"],
        parts["q_s"], twice_each(parts["k_s"]), twice_each(parts["v_s"]), parts["g_swa"],
    ], axis=1).astype(bf)
    assert w_in_p.shape[1] == PROJ_COLS

    wq = w_q_up.reshape(MLA_Q_RANK, MLA_HEADS, MLA_QK)
    w_q_p = jnp.concatenate([wq[:, :, :MLA_NOPE].reshape(MLA_Q_RANK, -1),
                             wq[:, :, MLA_NOPE:].reshape(MLA_Q_RANK, -1)], axis=1).astype(bf)
    wkv = w_kv_up.reshape(MLA_KV_RANK, MLA_HEADS, MLA_NOPE + MLA_V)
    w_kv_p = jnp.concatenate([wkv[:, :, :MLA_NOPE].reshape(MLA_KV_RANK, -1),
                              wkv[:, :, MLA_NOPE:].reshape(MLA_KV_RANK, -1)], axis=1).astype(bf)
    return w_in_p, w_q_p, w_kv_p


def kernel(x, ln_mix, w_in, q_a_norm, w_q_up, kv_a_norm, w_kv_up, attn_sinks, w_out, final_norm):
    batch, seq, d = x.shape
    depth = ln_mix.shape[0]
    assert depth == 1, "final norm is fused into the single layer's output kernel"
    assert seq % MLA_BLOCK == 0 and seq % PROJ_ROWS == 0 and seq % SWA_OUT_ROWS == 0
    tabs_m = _rope_tables(seq, MLA_ROPE)
    tabs_s = _rope_tables(seq, SWA_D)
    x2 = x.reshape(batch * seq, d)
    w_in_p, w_q_p, w_kv_p = _prep_weights(w_in[0], w_q_up[0], w_kv_up[0])
    qm, km, vm, gm, qs, ks4, vs4, gs = _proj_call(
        x2, ln_mix[0].reshape(1, -1), w_in_p, q_a_norm[0].reshape(1, -1), w_q_p,
        kv_a_norm[0].reshape(1, -1), w_kv_p, tabs_m, tabs_s, seq)
    mm = _mla_call(qm, km, vm, gm, batch, seq)
    out = _swa_out_call(attn_sinks[0], qs, ks4, vs4, gs, mm, x2, w_out[0].astype(jnp.bfloat16),
                        final_norm.reshape(1, -1), batch, seq)
    return out.reshape(batch, seq, d)
```

```python
import functools
import math

import jax
import jax.numpy as jnp
from jax import lax
from jax.experimental import pallas as pl
from jax.experimental.pallas import tpu as pltpu

ROPE_THETA = 10000.0
NORM_EPS = 1e-6
NEG_INF = -1e30
LOG2E = 1.4426950408889634

MLA_HEADS = 4
MLA_NOPE = 128
MLA_ROPE = 64
MLA_V = 128
MLA_Q_RANK = 256
MLA_KV_RANK = 128
MLA_QK = MLA_NOPE + MLA_ROPE
MLA_WIDTH = MLA_HEADS * MLA_V

SWA_Q_HEADS = 8
SWA_KV_HEADS = 2
SWA_D = 64
SWA_WINDOW = 128
SWA_GROUP = SWA_Q_HEADS // SWA_KV_HEADS
SWA_WIDTH = SWA_Q_HEADS * SWA_D

LANES = 128
V7X_VMEM_LIMIT_BYTES = 56 * 1024 * 1024

PROJ_ROWS = 512
MLA_BLOCK = 256
SWA_OUT_ROWS = 1024
SWA_OUT_SUBTILE = 256


def _rope_tables(seq, dim):
    assert 2 * dim == LANES
    pos = jnp.arange(seq, dtype=jnp.float32)
    inv_freq = 1.0 / (ROPE_THETA ** (jnp.arange(0, dim, 2, dtype=jnp.float32) / dim))
    ang = pos[:, None] * inv_freq[None, :]
    cos, sin = jnp.cos(ang), jnp.sin(ang)
    return (jnp.concatenate([cos, cos, cos, cos], axis=1),
            jnp.concatenate([-sin, -sin, sin, sin], axis=1))


def _rope(x, c, s):
    return x * c + pltpu.roll(x, LANES // 2, 1) * s


def _pair_rope_layout(w, n_heads, dim):
    half = dim // 2
    w = w.reshape(w.shape[0], n_heads // 2, 2, 2, half)
    return w.transpose(0, 1, 3, 2, 4).reshape(w.shape[0], n_heads * dim)


def _rms(x, g):
    return x * lax.rsqrt(jnp.mean(x * x, axis=-1, keepdims=True) + NORM_EPS) * g


def _silu(g):
    return g / (1.0 + jnp.exp(-g))


def _dot(a, b):
    return jnp.dot(a, b, preferred_element_type=jnp.float32)


def _dot_nt(a, b):
    return lax.dot_general(a, b, (((1,), (1,)), ((), ())),
                           preferred_element_type=jnp.float32)


PROJ_COLS = 2304


def _proj_kernel(x_ref, ln_ref, win_ref, qg_ref, wq_ref, kvg_ref, wkv_ref,
                 cm_ref, sm_ref, cs_ref, ss_ref,
                 qm_ref, km_ref, vm_ref, gm_ref, qs_ref, ks_ref, vs_ref, gs_ref):
    bf = jnp.bfloat16
    x = x_ref[...]
    h = _rms(x, ln_ref[...]).astype(bf)

    cm, sm, cs, ss = cm_ref[...], sm_ref[...], cs_ref[...], ss_ref[...]
    lane = lax.broadcasted_iota(jnp.int32, (x.shape[0], LANES), 1)
    lo = lane < (LANES // 2)
    first = (lane % (LANES // 2)) < (LANES // 4)
    grp = 2 * LANES

    def cols(a):
        return win_ref[:, a:a + grp]

    c_q = _dot(h, cols(0))
    lat_b = _dot(h, cols(256))
    cqn = _rms(c_q, qg_ref[...]).astype(bf)
    ckvn = _rms(lat_b[:, 0:MLA_KV_RANK], kvg_ref[...]).astype(bf)
    kr = _rope(lat_b[:, LANES:2 * LANES], cm, sm)
    kr_first = jnp.where(first, kr, 0.0).astype(bf)
    kr_second = jnp.where(first, 0.0, kr).astype(bf)

    s_scale = LOG2E / math.sqrt(SWA_D)
    for g2 in range(SWA_WIDTH // grp):
        q_s = _dot(h, cols(1024 + g2 * grp))
        for j in range(2):
            dst = slice(g2 * grp + j * LANES, g2 * grp + (j + 1) * LANES)
            qs_ref[:, dst] = (_rope(q_s[:, j * LANES:(j + 1) * LANES], cs, ss) * s_scale).astype(bf)

    q_scale = LOG2E / math.sqrt(MLA_QK)
    nope_w = MLA_HEADS * MLA_NOPE
    for g2 in range(nope_w // grp):
        qn = _dot(cqn, wq_ref[:, g2 * grp:(g2 + 1) * grp])
        for j in range(2):
            hd = 2 * g2 + j
            qm_ref[:, hd * grp:hd * grp + LANES] = (qn[:, j * LANES:(j + 1) * LANES] * q_scale).astype(bf)
    qr = _dot(cqn, wq_ref[:, nope_w:nope_w + grp])
    for j in range(2):
        chunk = (_rope(qr[:, j * LANES:(j + 1) * LANES], cm, sm) * q_scale).astype(bf)
        for hd in (2 * j, 2 * j + 1):
            qm_ref[:, hd * grp + LANES:(hd + 1) * grp] = chunk
    for g2 in range(nope_w // grp):
        kn = _dot(ckvn, wkv_ref[:, g2 * grp:(g2 + 1) * grp])
        for j in range(2):
            hd = 2 * g2 + j
            km_ref[:, hd * grp:hd * grp + LANES] = kn[:, j * LANES:(j + 1) * LANES].astype(bf)
            km_ref[:, hd * grp + LANES:(hd + 1) * grp] = kr_first if j == 0 else kr_second
    for g2 in range(MLA_WIDTH // grp):
        gm_ref[:, g2 * grp:(g2 + 1) * grp] = _silu(_dot(h, cols(512 + g2 * grp))).astype(bf)

    kv_s = _dot(h, cols(1536))
    k01 = _rope(kv_s[:, 0:LANES], cs, ss)
    ks_ref[:, 0 * LANES:1 * LANES] = jnp.where(first, k01, 0.0).astype(bf)
    ks_ref[:, 1 * LANES:2 * LANES] = jnp.where(first, 0.0, pltpu.roll(k01, LANES // 4, 1)).astype(bf)
    ks_ref[:, 2 * LANES:3 * LANES] = jnp.where(first, pltpu.roll(k01, 3 * LANES // 4, 1), 0.0).astype(bf)
    ks_ref[:, 3 * LANES:4 * LANES] = jnp.where(first, 0.0, k01).astype(bf)
    v01 = kv_s[:, LANES:2 * LANES]
    v10 = pltpu.roll(v01, LANES // 2, 1)
    vs_ref[:, 0 * LANES:1 * LANES] = jnp.where(lo, v01, 0.0).astype(bf)
    vs_ref[:, 1 * LANES:2 * LANES] = jnp.where(lo, 0.0, v10).astype(bf)
    vs_ref[:, 2 * LANES:3 * LANES] = jnp.where(lo, v10, 0.0).astype(bf)
    vs_ref[:, 3 * LANES:4 * LANES] = jnp.where(lo, 0.0, v01).astype(bf)

    for g2 in range(SWA_WIDTH // grp):
        gs_ref[:, g2 * grp:(g2 + 1) * grp] = _silu(_dot(h, cols(1792 + g2 * grp))).astype(bf)

    for g2 in range(MLA_WIDTH // grp):
        vm_ref[:, g2 * grp:(g2 + 1) * grp] = _dot(
            ckvn, wkv_ref[:, nope_w + g2 * grp:nope_w + (g2 + 1) * grp]).astype(bf)


def _proj_call(x2, ln_g, w_in_p, q_g, w_q_p, kv_g, w_kv_p, tabs_m, tabs_s, seq):
    n, d = x2.shape
    tm = PROJ_ROWS
    steps_per_seq = seq // tm
    row = lambda i: (i, 0)
    const = lambda i: (0, 0)
    tab = lambda i: (i % steps_per_seq, 0)
    bf = jnp.bfloat16
    out_widths = (1024, 1024, 512, 512, 512, 512, 512, 512)
    return pl.pallas_call(
        _proj_kernel,
        grid=(n // tm,),
        in_specs=[
            pl.BlockSpec((tm, d), row),
            pl.BlockSpec((1, d), const),
            pl.BlockSpec(w_in_p.shape, const),
            pl.BlockSpec((1, MLA_Q_RANK), const),
            pl.BlockSpec(w_q_p.shape, const),
            pl.BlockSpec((1, MLA_KV_RANK), const),
            pl.BlockSpec(w_kv_p.shape, const),
        ] + [pl.BlockSpec((tm, LANES), tab)] * 4,
        out_specs=[pl.BlockSpec((tm, w), row) for w in out_widths],
        out_shape=[jax.ShapeDtypeStruct((n, w), bf) for w in out_widths],
        compiler_params=pltpu.CompilerParams(
            dimension_semantics=("arbitrary",),
            vmem_limit_bytes=V7X_VMEM_LIMIT_BYTES),
        name="proj",
    )(x2, ln_g, w_in_p, q_g, w_q_p, kv_g, w_kv_p, *tabs_m, *tabs_s)


def _mla_kernel(q_ref, k_ref, v_ref, g_ref, o_ref, s_ref, p_ref):
    t = MLA_BLOCK
    seq = q_ref.shape[0]
    r = lax.broadcasted_iota(jnp.int32, (t, t), 0)
    c = lax.broadcasted_iota(jnp.int32, (t, t), 1)
    causal = c <= r
    nblk = seq // t
    state = [dict(m_l=None, l_l=None) for _ in range(nblk)]

    def score_tile(i, j):
        st = state[i]
        cols = slice(j * t, (j + 1) * t)
        s = _dot_nt(q_ref[i * t:(i + 1) * t, :], k_ref[cols, :])
        if j == i:
            s = jnp.where(causal, s, NEG_INF)
        s_ref[i % 2, :, cols] = s
        for u in range(t // LANES):
            su = s[:, u * LANES:(u + 1) * LANES]
            st["m_l"] = su if st["m_l"] is None else jnp.maximum(st["m_l"], su)

    def prob_tile(i, j):
        st = state[i]
        if "m" not in st:
            st["m"] = jnp.max(st["m_l"], axis=1, keepdims=True)
        cols = slice(j * t, (j + 1) * t)
        p = jnp.exp2(s_ref[i % 2, :, cols] - st["m"])
        for u in range(t // LANES):
            pu = p[:, u * LANES:(u + 1) * LANES]
            st["l_l"] = pu if st["l_l"] is None else st["l_l"] + pu
        p_ref[i % 2, :, cols] = p.astype(jnp.bfloat16)

    def finish(i):
        rows = slice(i * t, (i + 1) * t)
        l = jnp.sum(state[i]["l_l"], axis=1, keepdims=True)
        kv = (i + 1) * t
        acc = _dot(p_ref[i % 2, :, 0:kv], v_ref[0:kv, :])
        o_ref[rows, :] = (acc / l * g_ref[rows, :].astype(jnp.float32)).astype(o_ref.dtype)

    score_tile(0, 0)
    for i in range(nblk):
        nxt = [functools.partial(score_tile, i + 1, j) for j in range(i + 2)] if i + 1 < nblk else []
        cur = [functools.partial(prob_tile, i, j) for j in range(i + 1)]
        while nxt or cur:
            if nxt:
                nxt.pop(0)()
            if cur:
                cur.pop(0)()
        finish(i)


def _mla_call(qm, km, vm, gm, batch, seq):
    n = qm.shape[0]
    t = MLA_BLOCK
    return pl.pallas_call(
        _mla_kernel,
        grid=(batch, MLA_HEADS),
        in_specs=[
            pl.BlockSpec((seq, 2 * LANES), lambda b, h: (b, h)),
            pl.BlockSpec((seq, 2 * LANES), lambda b, h: (b, h)),
            pl.BlockSpec((seq, MLA_V), lambda b, h: (b, h)),
            pl.BlockSpec((seq, MLA_V), lambda b, h: (b, h)),
        ],
        out_specs=pl.BlockSpec((seq, MLA_V), lambda b, h: (b, h)),
        out_shape=jax.ShapeDtypeStruct((n, MLA_WIDTH), jnp.bfloat16),
        scratch_shapes=[pltpu.VMEM((2, t, seq), jnp.float32),
                        pltpu.VMEM((2, t, seq), jnp.bfloat16)],
        compiler_params=pltpu.CompilerParams(
            dimension_semantics=("arbitrary", "arbitrary"),
            vmem_limit_bytes=V7X_VMEM_LIMIT_BYTES),
        name="mla",
    )(qm, km, vm, gm)


def _swa_out_kernel(sink_ref, q_ref, k_ref, v_ref, kh_ref, vh_ref, g_ref, mm_ref, x_ref,
                    w_ref, fg_ref, o_ref, ms_ref):
    w = SWA_WINDOW
    chunk = pl.program_id(1)
    rows_total = q_ref.shape[0]
    bf = jnp.bfloat16

    row = lax.broadcasted_iota(jnp.int32, (2 * w, 1), 0)
    qi = lax.broadcasted_iota(jnp.int32, (2 * w, 2 * w), 0) % w
    ki = lax.broadcasted_iota(jnp.int32, (2 * w, 2 * w), 1)
    rel = qi + w - ki
    band = (rel >= 0) & (rel < SWA_WINDOW)
    band_first = band & ((ki >= w) | (chunk > 0))

    def swa_block(hk, n):
        sink = [sink_ref[hk * SWA_GROUP + i] * LOG2E for i in range(SWA_GROUP)]
        sinks = (jnp.where(row < w, sink[0], sink[2]), jnp.where(row < w, sink[1], sink[3]))
        base = hk * 2 * LANES
        qrows = slice(n * w, (n + 1) * w)
        qst = jnp.concatenate([q_ref[qrows, base:base + LANES],
                               q_ref[qrows, base + LANES:base + 2 * LANES]], axis=0)
        outs = []
        for half in range(2):
            lanes = slice(base + half * LANES, base + (half + 1) * LANES)
            if n == 0:
                k = jnp.concatenate([kh_ref[:, lanes], k_ref[0:w, lanes]], axis=0)
                v = jnp.concatenate([vh_ref[:, lanes], v_ref[0:w, lanes]], axis=0)
                valid = band_first
            else:
                k = k_ref[(n - 1) * w:(n + 1) * w, lanes]
                v = v_ref[(n - 1) * w:(n + 1) * w, lanes]
                valid = band
            s = jnp.where(valid, _dot_nt(qst, k), NEG_INF)
            m = jnp.maximum(jnp.max(s, axis=1, keepdims=True), sinks[half])
            p = jnp.exp2(s - m)
            den = jnp.sum(p, axis=1, keepdims=True) + jnp.exp2(sinks[half] - m)
            outs.append(_dot(p.astype(bf), v) / den)
        o = outs[0] + outs[1]
        g = g_ref[qrows, base:base + 2 * LANES].astype(jnp.float32)
        ms_ref[qrows, base:base + LANES] = (o[0:w] * g[:, 0:LANES]).astype(bf)
        ms_ref[qrows, base + LANES:base + 2 * LANES] = (o[w:2 * w] * g[:, LANES:2 * LANES]).astype(bf)

    t = SWA_OUT_SUBTILE

    def out_tile(i):
        rows = slice(i * t, (i + 1) * t)
        y = (x_ref[rows, :] + _dot(mm_ref[rows, :], w_ref[0:MLA_WIDTH, :])
             + _dot(ms_ref[rows, :], w_ref[MLA_WIDTH:, :]))
        o_ref[rows, :] = _rms(y, fg_ref[...])

    def swa_tasks(i):
        return [functools.partial(swa_block, hk, n)
                for n in range(i * t // w, (i + 1) * t // w) for hk in range(SWA_KV_HEADS)]

    n_sub = rows_total // t
    for task in swa_tasks(0):
        task()
    for i in range(n_sub):
        nxt = swa_tasks(i + 1) if i + 1 < n_sub else []
        if nxt:
            nxt.pop(0)()
        out_tile(i)
        for task in nxt:
            task()


def _swa_out_call(sinks, qs, ks4, vs4, gs, mm, x2, w_o, fg, batch, seq):
    n, d = x2.shape
    r = SWA_OUT_ROWS
    cps = seq // r
    bpc = r // SWA_WINDOW
    bps = seq // SWA_WINDOW
    row = lambda b, c: (b * cps + c, 0)
    halo = lambda b, c: (b * bps + jnp.maximum(c * bpc - 1, 0), 0)
    const = lambda b, c: (0, 0)
    wide = pl.BlockSpec((r, SWA_WIDTH), row)
    return pl.pallas_call(
        _swa_out_kernel,
        grid=(batch, cps),
        in_specs=[
            pl.BlockSpec(memory_space=pltpu.SMEM),
            wide, wide, wide,
            pl.BlockSpec((SWA_WINDOW, SWA_WIDTH), halo),
            pl.BlockSpec((SWA_WINDOW, SWA_WIDTH), halo),
            wide,
            pl.BlockSpec((r, MLA_WIDTH), row),
            pl.BlockSpec((r, d), row),
            pl.BlockSpec(w_o.shape, const),
            pl.BlockSpec((1, d), const),
        ],
        out_specs=pl.BlockSpec((r, d), row),
        out_shape=jax.ShapeDtypeStruct((n, d), jnp.float32),
        scratch_shapes=[pltpu.VMEM((r, SWA_WIDTH), jnp.bfloat16)],
        compiler_params=pltpu.CompilerParams(
            dimension_semantics=("arbitrary", "arbitrary"),
            vmem_limit_bytes=V7X_VMEM_LIMIT_BYTES),
        name="swa_out",
    )(sinks, qs, ks4, vs4, ks4, vs4, gs, mm, x2, w_o, fg)


def _prep_weights(w_in, w_q_up, w_kv_up):
    bf = jnp.bfloat16
    o = 0
    parts = {}
    for name, width in (("c_q", MLA_Q_RANK), ("c_kv", MLA_KV_RANK), ("k_rope", MLA_ROPE),
                        ("g_mla", MLA_WIDTH), ("q_s", SWA_WIDTH),
                        ("k_s", SWA_KV_HEADS * SWA_D), ("v_s", SWA_KV_HEADS * SWA_D),
                        ("g_swa", SWA_WIDTH)):
        parts[name] = w_in[:, o:o + width]
        o += width

    k_rope2 = jnp.concatenate([parts["k_rope"], parts["k_rope"]], axis=1)
    w_in_p = jnp.concatenate([
        parts["c_q"], parts["c_kv"], _pair_rope_layout(k_rope2, 2, MLA_ROPE), parts["g_mla"],
        _pair_rope_layout(parts["q_s"], SWA_Q_HEADS, SWA_D),
        _pair_rope_layout(parts["k_s"], SWA_KV_HEADS, SWA_D), parts["v_s"], parts["g_swa"],
    ], axis=1).astype(bf)
    assert w_in_p.shape[1] == PROJ_COLS

    wq = w_q_up.reshape(MLA_Q_RANK, MLA_HEADS, MLA_QK)
    w_q_p = jnp.concatenate([
        wq[:, :, :MLA_NOPE].reshape(MLA_Q_RANK, -1),
        _pair_rope_layout(wq[:, :, MLA_NOPE:].reshape(MLA_Q_RANK, -1), MLA_HEADS, MLA_ROPE),
    ], axis=1).astype(bf)
    wkv = w_kv_up.reshape(MLA_KV_RANK, MLA_HEADS, MLA_NOPE + MLA_V)
    w_kv_p = jnp.concatenate([wkv[:, :, :MLA_NOPE].reshape(MLA_KV_RANK, -1),
                              wkv[:, :, MLA_NOPE:].reshape(MLA_KV_RANK, -1)], axis=1).astype(bf)
    return w_in_p, w_q_p, w_kv_p


def kernel(x, ln_mix, w_in, q_a_norm, w_q_up, kv_a_norm, w_kv_up, attn_sinks, w_out, final_norm):
    batch, seq, d = x.shape
    depth = ln_mix.shape[0]
    assert depth == 1, "final norm is fused into the single layer's output kernel"
    assert seq % MLA_BLOCK == 0 and seq % PROJ_ROWS == 0 and seq % SWA_OUT_ROWS == 0
    tabs_m = _rope_tables(seq, MLA_ROPE)
    tabs_s = _rope_tables(seq, SWA_D)
    x2 = x.reshape(batch * seq, d)
    w_in_p, w_q_p, w_kv_p = _prep_weights(w_in[0], w_q_up[0], w_kv_up[0])
    qm, km, vm, gm, qs, ks4, vs4, gs = _proj_call(
        x2, ln_mix[0].reshape(1, -1), w_in_p, q_a_norm[0].reshape(1, -1), w_q_p,
        kv_a_norm[0].reshape(1, -1), w_kv_p, tabs_m, tabs_s, seq)
    mm = _mla_call(qm, km, vm, gm, batch, seq)
    out = _swa_out_call(attn_sinks[0], qs, ks4, vs4, gs, mm, x2, w_out[0].astype(jnp.bfloat16),
                        final_norm.reshape(1, -1), batch, seq)
    return out.reshape(batch, seq, d)
```

```python
import functools
import math

import jax
import jax.numpy as jnp
from jax import lax
from jax.experimental import pallas as pl
from jax.experimental.pallas import tpu as pltpu

ROPE_THETA = 10000.0
NORM_EPS = 1e-6
NEG_INF = -1e30
LOG2E = 1.4426950408889634

MLA_HEADS = 4
MLA_NOPE = 128
MLA_ROPE = 64
MLA_V = 128
MLA_Q_RANK = 256
MLA_KV_RANK = 128
MLA_QK = MLA_NOPE + MLA_ROPE
MLA_WIDTH = MLA_HEADS * MLA_V

SWA_Q_HEADS = 8
SWA_KV_HEADS = 2
SWA_D = 64
SWA_WINDOW = 128
SWA_GROUP = SWA_Q_HEADS // SWA_KV_HEADS
SWA_WIDTH = SWA_Q_HEADS * SWA_D

LANES = 128
V7X_VMEM_LIMIT_BYTES = 56 * 1024 * 1024

PROJ_ROWS = 512
MLA_BLOCK = 256
SWA_OUT_ROWS = 1024
SWA_OUT_SUBTILE = 256


def _rope_tables(seq, dim):
    assert 2 * dim == LANES
    pos = jnp.arange(seq, dtype=jnp.float32)
    inv_freq = 1.0 / (ROPE_THETA ** (jnp.arange(0, dim, 2, dtype=jnp.float32) / dim))
    ang = pos[:, None] * inv_freq[None, :]
    cos, sin = jnp.cos(ang), jnp.sin(ang)
    return (jnp.concatenate([cos, cos, cos, cos], axis=1),
            jnp.concatenate([-sin, -sin, sin, sin], axis=1))


def _rope_tables_t(seq, dim):
    pos = jnp.arange(seq, dtype=jnp.float32)
    inv_freq = 1.0 / (ROPE_THETA ** (jnp.arange(0, dim, 2, dtype=jnp.float32) / dim))
    ang = pos[:, None] * inv_freq[None, :]
    return jnp.cos(ang).T, jnp.sin(ang).T


def _rope(x, c, s):
    return x * c + pltpu.roll(x, LANES // 2, 1) * s


def _pair_rope_layout(w, n_heads, dim):
    half = dim // 2
    w = w.reshape(w.shape[0], n_heads // 2, 2, 2, half)
    return w.transpose(0, 1, 3, 2, 4).reshape(w.shape[0], n_heads * dim)


def _rms(x, g):
    return x * lax.rsqrt(jnp.mean(x * x, axis=-1, keepdims=True) + NORM_EPS) * g


def _silu(g):
    return g / (1.0 + jnp.exp(-g))


def _dot(a, b):
    return jnp.dot(a, b, preferred_element_type=jnp.float32)


def _dot_nt(a, b):
    return lax.dot_general(a, b, (((1,), (1,)), ((), ())),
                           preferred_element_type=jnp.float32)


PROJ_COLS = 2304


def _proj_kernel(x_ref, ln_ref, win_ref, qg_ref, wqt_ref, kvg_ref, wkv_ref, wvt_ref,
                 cm_ref, sm_ref, cs_ref, ss_ref, ct_ref, st_ref,
                 qt_ref, km_ref, vt_ref, gm_ref, qs_ref, ks_ref, vs_ref, gs_ref):
    bf = jnp.bfloat16
    x = x_ref[...]
    h = _rms(x, ln_ref[...]).astype(bf)

    cm, sm, cs, ss = cm_ref[...], sm_ref[...], cs_ref[...], ss_ref[...]
    lane = lax.broadcasted_iota(jnp.int32, (x.shape[0], LANES), 1)
    lo = lane < (LANES // 2)
    first = (lane % (LANES // 2)) < (LANES // 4)
    grp = 2 * LANES

    def cols(a):
        return win_ref[:, a:a + grp]

    c_q = _dot(h, cols(0))
    lat_b = _dot(h, cols(256))
    cqn = _rms(c_q, qg_ref[...]).astype(bf)
    ckvn = _rms(lat_b[:, 0:MLA_KV_RANK], kvg_ref[...]).astype(bf)
    kr = _rope(lat_b[:, LANES:2 * LANES], cm, sm)
    kr_first = jnp.where(first, kr, 0.0).astype(bf)

    s_scale = LOG2E / math.sqrt(SWA_D)
    for g2 in range(SWA_WIDTH // grp):
        q_s = _dot(h, cols(1024 + g2 * grp))
        for j in range(2):
            dst = slice(g2 * grp + j * LANES, g2 * grp + (j + 1) * LANES)
            qs_ref[:, dst] = (_rope(q_s[:, j * LANES:(j + 1) * LANES], cs, ss) * s_scale).astype(bf)

    q_scale = LOG2E / math.sqrt(MLA_QK)
    nope_w = MLA_HEADS * MLA_NOPE
    half = MLA_ROPE // 2
    qt = _dot_nt(wqt_ref[...], cqn)
    ct, st = ct_ref[...], st_ref[...]
    zeros = jnp.zeros((half, x.shape[0]), bf)
    for hd in range(MLA_HEADS):
        qt_ref[hd * grp:hd * grp + LANES, :] = (qt[hd * MLA_NOPE:(hd + 1) * MLA_NOPE, :] * q_scale).astype(bf)
        x1 = qt[nope_w + hd * MLA_ROPE:nope_w + hd * MLA_ROPE + half, :]
        x2 = qt[nope_w + hd * MLA_ROPE + half:nope_w + (hd + 1) * MLA_ROPE, :]
        base = hd * grp + LANES
        qt_ref[base:base + half, :] = ((x1 * ct - x2 * st) * q_scale).astype(bf)
        qt_ref[base + half:base + 2 * half, :] = zeros
        qt_ref[base + 2 * half:base + 3 * half, :] = ((x2 * ct + x1 * st) * q_scale).astype(bf)
        qt_ref[base + 3 * half:base + 4 * half, :] = zeros
    for g2 in range(nope_w // grp):
        kn = _dot(ckvn, wkv_ref[:, g2 * grp:(g2 + 1) * grp])
        for j in range(2):
            hd = 2 * g2 + j
            km_ref[:, hd * grp:hd * grp + LANES] = kn[:, j * LANES:(j + 1) * LANES].astype(bf)
            km_ref[:, hd * grp + LANES:(hd + 1) * grp] = kr_first
    for g2 in range(MLA_WIDTH // grp):
        gm_ref[:, g2 * grp:(g2 + 1) * grp] = _silu(_dot(h, cols(512 + g2 * grp))).astype(bf)

    kv_s = _dot(h, cols(1536))
    k01 = _rope(kv_s[:, 0:LANES], cs, ss)
    ks_ref[:, 0 * LANES:1 * LANES] = jnp.where(first, k01, 0.0).astype(bf)
    ks_ref[:, 1 * LANES:2 * LANES] = jnp.where(first, 0.0, pltpu.roll(k01, LANES // 4, 1)).astype(bf)
    ks_ref[:, 2 * LANES:3 * LANES] = jnp.where(first, pltpu.roll(k01, 3 * LANES // 4, 1), 0.0).astype(bf)
    ks_ref[:, 3 * LANES:4 * LANES] = jnp.where(first, 0.0, k01).astype(bf)
    v01 = kv_s[:, LANES:2 * LANES]
    v10 = pltpu.roll(v01, LANES // 2, 1)
    vs_ref[:, 0 * LANES:1 * LANES] = jnp.where(lo, v01, 0.0).astype(bf)
    vs_ref[:, 1 * LANES:2 * LANES] = jnp.where(lo, 0.0, v10).astype(bf)
    vs_ref[:, 2 * LANES:3 * LANES] = jnp.where(lo, v10, 0.0).astype(bf)
    vs_ref[:, 3 * LANES:4 * LANES] = jnp.where(lo, 0.0, v01).astype(bf)

    for g2 in range(SWA_WIDTH // grp):
        gs_ref[:, g2 * grp:(g2 + 1) * grp] = _silu(_dot(h, cols(1792 + g2 * grp))).astype(bf)

    vt_ref[...] = _dot_nt(wvt_ref[...], ckvn).astype(bf)


def _proj_call(x2, ln_g, w_in_p, q_g, w_qt, kv_g, w_k_p, w_vt, tabs_m, tabs_s, tabs_t, seq):
    n, d = x2.shape
    tm = PROJ_ROWS
    steps_per_seq = seq // tm
    batch = n // seq
    row = lambda i: (i, 0)
    const = lambda i: (0, 0)
    tab = lambda i: (i % steps_per_seq, 0)
    tab_t = lambda i: (0, i % steps_per_seq)
    col = lambda i: (i // steps_per_seq, i % steps_per_seq)
    bf = jnp.bfloat16
    qt_rows = MLA_HEADS * 2 * LANES
    row_out = lambda w: (pl.BlockSpec((tm, w), row), jax.ShapeDtypeStruct((n, w), bf))
    col_out = lambda r: (pl.BlockSpec((r, tm), col), jax.ShapeDtypeStruct((batch * r, seq), bf))
    outs = [col_out(qt_rows), row_out(qt_rows), col_out(MLA_WIDTH), row_out(MLA_WIDTH),
            row_out(SWA_WIDTH), row_out(SWA_WIDTH), row_out(SWA_WIDTH), row_out(SWA_WIDTH)]
    return pl.pallas_call(
        _proj_kernel,
        grid=(n // tm,),
        in_specs=[
            pl.BlockSpec((tm, d), row),
            pl.BlockSpec((1, d), const),
            pl.BlockSpec(w_in_p.shape, const),
            pl.BlockSpec((1, MLA_Q_RANK), const),
            pl.BlockSpec(w_qt.shape, const),
            pl.BlockSpec((1, MLA_KV_RANK), const),
            pl.BlockSpec(w_k_p.shape, const),
            pl.BlockSpec(w_vt.shape, const),
        ] + [pl.BlockSpec((tm, LANES), tab)] * 4 + [pl.BlockSpec((MLA_ROPE // 2, tm), tab_t)] * 2,
        out_specs=[o[0] for o in outs],
        out_shape=[o[1] for o in outs],
        compiler_params=pltpu.CompilerParams(
            dimension_semantics=("arbitrary",),
            vmem_limit_bytes=V7X_VMEM_LIMIT_BYTES),
        name="proj",
    )(x2, ln_g, w_in_p, q_g, w_qt, kv_g, w_k_p, w_vt, *tabs_m, *tabs_s, *tabs_t)


def _tree(op, xs):
    xs = list(xs)
    while len(xs) > 1:
        xs = [op(xs[a], xs[a + 1]) if a + 1 < len(xs) else xs[a] for a in range(0, len(xs), 2)]
    return xs[0]


def _mla_kernel(qt_ref, k_ref, vt_ref, g_ref, o_ref, s_ref, p_ref, vx_ref):
    t = MLA_BLOCK
    seq = k_ref.shape[0]
    sub = 8
    r = lax.broadcasted_iota(jnp.int32, (t, t), 0)
    c = lax.broadcasted_iota(jnp.int32, (t, t), 1)
    causal = r <= c
    nblk = seq // t
    state = [dict(m8=None) for _ in range(nblk)]

    vx_ref[0:MLA_V, :] = vt_ref[...]
    vx_ref[MLA_V:, :] = jnp.ones((vx_ref.shape[0] - MLA_V, seq), vx_ref.dtype)

    def score_tile(i, j):
        st = state[i]
        keys = slice(j * t, (j + 1) * t)
        s = _dot(k_ref[keys, :], qt_ref[:, i * t:(i + 1) * t])
        if j == i:
            s = jnp.where(causal, s, NEG_INF)
        s_ref[i % 2, keys, :] = s
        m8 = _tree(jnp.maximum, [s[a * sub:(a + 1) * sub, :] for a in range(t // sub)])
        st["m8"] = m8 if st["m8"] is None else jnp.maximum(st["m8"], m8)

    def prob_tile(i, j):
        st = state[i]
        if "m" not in st:
            st["m"] = jnp.max(st["m8"], axis=0, keepdims=True)
        keys = slice(j * t, (j + 1) * t)
        p_ref[i % 3, keys, :] = jnp.exp2(s_ref[i % 2, keys, :] - st["m"]).astype(jnp.bfloat16)

    def finish(i):
        rows = slice(i * t, (i + 1) * t)
        kv = (i + 1) * t
        acc = _dot(vx_ref[:, 0:kv], p_ref[i % 3, 0:kv, :])
        out_t = acc[0:MLA_V, :] / acc[MLA_V:MLA_V + 1, :]
        o_ref[rows, :] = (out_t.T * g_ref[rows, :].astype(jnp.float32)).astype(o_ref.dtype)

    order = list(range(nblk - 1, -1, -1))
    for j in range(order[0] + 1):
        score_tile(order[0], j)
    for pos, i in enumerate(order):
        nxt = ([functools.partial(score_tile, order[pos + 1], j) for j in range(order[pos + 1] + 1)]
               if pos + 1 < nblk else [])
        cur = [functools.partial(prob_tile, i, j) for j in range(i + 1)]
        prev = [functools.partial(finish, order[pos - 1])] if pos > 0 else []
        while nxt or cur or prev:
            if nxt:
                nxt.pop(0)()
            if cur:
                cur.pop(0)()
            if prev:
                prev.pop(0)()
    finish(order[-1])


def _mla_call(qt, km, vt, gm, batch, seq):
    n = km.shape[0]
    t = MLA_BLOCK
    ones_rows = 16
    return pl.pallas_call(
        _mla_kernel,
        grid=(batch, MLA_HEADS),
        in_specs=[
            pl.BlockSpec((2 * LANES, seq), lambda b, h: (b * MLA_HEADS + h, 0)),
            pl.BlockSpec((seq, 2 * LANES), lambda b, h: (b, h)),
            pl.BlockSpec((MLA_V, seq), lambda b, h: (b * MLA_HEADS + h, 0)),
            pl.BlockSpec((seq, MLA_V), lambda b, h: (b, h)),
        ],
        out_specs=pl.BlockSpec((seq, MLA_V), lambda b, h: (b, h)),
        out_shape=jax.ShapeDtypeStruct((n, MLA_WIDTH), jnp.bfloat16),
        scratch_shapes=[pltpu.VMEM((2, seq, t), jnp.float32),
                        pltpu.VMEM((3, seq, t), jnp.bfloat16),
                        pltpu.VMEM((MLA_V + ones_rows, seq), jnp.bfloat16)],
        compiler_params=pltpu.CompilerParams(
            dimension_semantics=("arbitrary", "arbitrary"),
            vmem_limit_bytes=V7X_VMEM_LIMIT_BYTES),
        name="mla",
    )(qt, km, vt, gm)


def _swa_out_kernel(sink_ref, q_ref, k_ref, v_ref, kh_ref, vh_ref, g_ref, mm_ref, x_ref,
                    w_ref, fg_ref, o_ref, ms_ref):
    w = SWA_WINDOW
    chunk = pl.program_id(1)
    rows_total = q_ref.shape[0]
    bf = jnp.bfloat16

    row = lax.broadcasted_iota(jnp.int32, (2 * w, 1), 0)
    qi = lax.broadcasted_iota(jnp.int32, (2 * w, 2 * w), 0) % w
    ki = lax.broadcasted_iota(jnp.int32, (2 * w, 2 * w), 1)
    rel = qi + w - ki
    band = (rel >= 0) & (rel < SWA_WINDOW)
    band_first = band & ((ki >= w) | (chunk > 0))

    def swa_block(hk, n):
        sink = [sink_ref[hk * SWA_GROUP + i] * LOG2E for i in range(SWA_GROUP)]
        sinks = (jnp.where(row < w, sink[0], sink[2]), jnp.where(row < w, sink[1], sink[3]))
        base = hk * 2 * LANES
        qrows = slice(n * w, (n + 1) * w)
        qst = jnp.concatenate([q_ref[qrows, base:base + LANES],
                               q_ref[qrows, base + LANES:base + 2 * LANES]], axis=0)
        outs = []
        for half in range(2):
            lanes = slice(base + half * LANES, base + (half + 1) * LANES)
            if n == 0:
                k = jnp.concatenate([kh_ref[:, lanes], k_ref[0:w, lanes]], axis=0)
                v = jnp.concatenate([vh_ref[:, lanes], v_ref[0:w, lanes]], axis=0)
                valid = band_first
            else:
                k = k_ref[(n - 1) * w:(n + 1) * w, lanes]
                v = v_ref[(n - 1) * w:(n + 1) * w, lanes]
                valid = band
            s = jnp.where(valid, _dot_nt(qst, k), NEG_INF)
            m = jnp.maximum(jnp.max(s, axis=1, keepdims=True), sinks[half])
            p = jnp.exp2(s - m)
            den = jnp.sum(p, axis=1, keepdims=True) + jnp.exp2(sinks[half] - m)
            outs.append(_dot(p.astype(bf), v) / den)
        o = outs[0] + outs[1]
        g = g_ref[qrows, base:base + 2 * LANES].astype(jnp.float32)
        ms_ref[qrows, base:base + LANES] = (o[0:w] * g[:, 0:LANES]).astype(bf)
        ms_ref[qrows, base + LANES:base + 2 * LANES] = (o[w:2 * w] * g[:, LANES:2 * LANES]).astype(bf)

    t = SWA_OUT_SUBTILE

    def out_tile(i):
        rows = slice(i * t, (i + 1) * t)
        y = (x_ref[rows, :] + _dot(mm_ref[rows, :], w_ref[0:MLA_WIDTH, :])
             + _dot(ms_ref[rows, :], w_ref[MLA_WIDTH:, :]))
        o_ref[rows, :] = _rms(y, fg_ref[...])

    def swa_tasks(i):
        return [functools.partial(swa_block, hk, n)
                for n in range(i * t // w, (i + 1) * t // w) for hk in range(SWA_KV_HEADS)]

    n_sub = rows_total // t
    for task in swa_tasks(0):
        task()
    for i in range(n_sub):
        nxt = swa_tasks(i + 1) if i + 1 < n_sub else []
        if nxt:
            nxt.pop(0)()
        out_tile(i)
        for task in nxt:
            task()


def _swa_out_call(sinks, qs, ks4, vs4, gs, mm, x2, w_o, fg, batch, seq):
    n, d = x2.shape
    r = SWA_OUT_ROWS
    cps = seq // r
    bpc = r // SWA_WINDOW
    bps = seq // SWA_WINDOW
    row = lambda b, c: (b * cps + c, 0)
    halo = lambda b, c: (b * bps + jnp.maximum(c * bpc - 1, 0), 0)
    const = lambda b, c: (0, 0)
    wide = pl.BlockSpec((r, SWA_WIDTH), row)
    return pl.pallas_call(
        _swa_out_kernel,
        grid=(batch, cps),
        in_specs=[
            pl.BlockSpec(memory_space=pltpu.SMEM),
            wide, wide, wide,
            pl.BlockSpec((SWA_WINDOW, SWA_WIDTH), halo),
            pl.BlockSpec((SWA_WINDOW, SWA_WIDTH), halo),
            wide,
            pl.BlockSpec((r, MLA_WIDTH), row),
            pl.BlockSpec((r, d), row),
            pl.BlockSpec(w_o.shape, const),
            pl.BlockSpec((1, d), const),
        ],
        out_specs=pl.BlockSpec((r, d), row),
        out_shape=jax.ShapeDtypeStruct((n, d), jnp.float32),
        scratch_shapes=[pltpu.VMEM((r, SWA_WIDTH), jnp.bfloat16)],
        compiler_params=pltpu.CompilerParams(
            dimension_semantics=("arbitrary", "arbitrary"),
            vmem_limit_bytes=V7X_VMEM_LIMIT_BYTES),
        name="swa_out",
    )(sinks, qs, ks4, vs4, ks4, vs4, gs, mm, x2, w_o, fg)


def _prep_weights(w_in, w_q_up, w_kv_up):
    bf = jnp.bfloat16
    o = 0
    parts = {}
    for name, width in (("c_q", MLA_Q_RANK), ("c_kv", MLA_KV_RANK), ("k_rope", MLA_ROPE),
                        ("g_mla", MLA_WIDTH), ("q_s", SWA_WIDTH),
                        ("k_s", SWA_KV_HEADS * SWA_D), ("v_s", SWA_KV_HEADS * SWA_D),
                        ("g_swa", SWA_WIDTH)):
        parts[name] = w_in[:, o:o + width]
        o += width

    k_rope2 = jnp.concatenate([parts["k_rope"], parts["k_rope"]], axis=1)
    w_in_p = jnp.concatenate([
        parts["c_q"], parts["c_kv"], _pair_rope_layout(k_rope2, 2, MLA_ROPE), parts["g_mla"],
        _pair_rope_layout(parts["q_s"], SWA_Q_HEADS, SWA_D),
        _pair_rope_layout(parts["k_s"], SWA_KV_HEADS, SWA_D), parts["v_s"], parts["g_swa"],
    ], axis=1).astype(bf)
    assert w_in_p.shape[1] == PROJ_COLS

    wq = w_q_up.reshape(MLA_Q_RANK, MLA_HEADS, MLA_QK)
    w_qt = jnp.concatenate([wq[:, :, :MLA_NOPE].reshape(MLA_Q_RANK, -1),
                            wq[:, :, MLA_NOPE:].reshape(MLA_Q_RANK, -1)], axis=1).T.astype(bf)
    wkv = w_kv_up.reshape(MLA_KV_RANK, MLA_HEADS, MLA_NOPE + MLA_V)
    w_k_p = wkv[:, :, :MLA_NOPE].reshape(MLA_KV_RANK, -1).astype(bf)
    w_vt = wkv[:, :, MLA_NOPE:].reshape(MLA_KV_RANK, -1).T.astype(bf)
    return w_in_p, w_qt, w_k_p, w_vt


def kernel(x, ln_mix, w_in, q_a_norm, w_q_up, kv_a_norm, w_kv_up, attn_sinks, w_out, final_norm):
    batch, seq, d = x.shape
    depth = ln_mix.shape[0]
    assert depth == 1, "final norm is fused into the single layer's output kernel"
    assert seq % MLA_BLOCK == 0 and seq % PROJ_ROWS == 0 and seq % SWA_OUT_ROWS == 0
    tabs_m = _rope_tables(seq, MLA_ROPE)
    tabs_s = _rope_tables(seq, SWA_D)
    x2 = x.reshape(batch * seq, d)
    tabs_t = _rope_tables_t(seq, MLA_ROPE)
    w_in_p, w_qt, w_k_p, w_vt = _prep_weights(w_in[0], w_q_up[0], w_kv_up[0])
    qt, km, vt, gm, qs, ks4, vs4, gs = _proj_call(
        x2, ln_mix[0].reshape(1, -1), w_in_p, q_a_norm[0].reshape(1, -1), w_qt,
        kv_a_norm[0].reshape(1, -1), w_k_p, w_vt, tabs_m, tabs_s, tabs_t, seq)
    mm = _mla_call(qt, km, vt, gm, batch, seq)
    out = _swa_out_call(attn_sinks[0], qs, ks4, vs4, gs, mm, x2, w_out[0].astype(jnp.bfloat16),
                        final_norm.reshape(1, -1), batch, seq)
    return out.reshape(batch, seq, d)
```

```python
import functools
import math

import jax
import jax.numpy as jnp
from jax import lax
from jax.experimental import pallas as pl
from jax.experimental.pallas import tpu as pltpu

ROPE_THETA = 10000.0
NORM_EPS = 1e-6
NEG_INF = -1e30
LOG2E = 1.4426950408889634

MLA_HEADS = 4
MLA_NOPE = 128
MLA_ROPE = 64
MLA_V = 128
MLA_Q_RANK = 256
MLA_KV_RANK = 128
MLA_QK = MLA_NOPE + MLA_ROPE
MLA_WIDTH = MLA_HEADS * MLA_V

SWA_Q_HEADS = 8
SWA_KV_HEADS = 2
SWA_D = 64
SWA_WINDOW = 128
SWA_GROUP = SWA_Q_HEADS // SWA_KV_HEADS
SWA_WIDTH = SWA_Q_HEADS * SWA_D

LANES = 128
V7X_VMEM_LIMIT_BYTES = 56 * 1024 * 1024

PROJ_ROWS = 512
MLA_BLOCK = 256
SWA_OUT_ROWS = 1024
SWA_OUT_SUBTILE = 256
PREP_ROWS = 256


def _rope_tables(seq, dim):
    assert 2 * dim == LANES
    pos = jnp.arange(seq, dtype=jnp.float32)
    inv_freq = 1.0 / (ROPE_THETA ** (jnp.arange(0, dim, 2, dtype=jnp.float32) / dim))
    ang = pos[:, None] * inv_freq[None, :]
    cos, sin = jnp.cos(ang), jnp.sin(ang)
    return (jnp.concatenate([cos, cos, cos, cos], axis=1),
            jnp.concatenate([-sin, -sin, sin, sin], axis=1))


def _rope_tables_t(seq, dim):
    pos = jnp.arange(seq, dtype=jnp.float32)
    inv_freq = 1.0 / (ROPE_THETA ** (jnp.arange(0, dim, 2, dtype=jnp.float32) / dim))
    ang = pos[:, None] * inv_freq[None, :]
    return jnp.cos(ang).T, jnp.sin(ang).T


def _rope(x, c, s):
    return x * c + pltpu.roll(x, LANES // 2, 1) * s


def _pair_rope_layout(w, n_heads, dim):
    half = dim // 2
    w = w.reshape(w.shape[0], n_heads // 2, 2, 2, half)
    return w.transpose(0, 1, 3, 2, 4).reshape(w.shape[0], n_heads * dim)


def _rms(x, g):
    return x * lax.rsqrt(jnp.mean(x * x, axis=-1, keepdims=True) + NORM_EPS) * g


def _silu(g):
    return g / (1.0 + jnp.exp(-g))


def _dot(a, b):
    return jnp.dot(a, b, preferred_element_type=jnp.float32)


def _dot_nt(a, b):
    return lax.dot_general(a, b, (((1,), (1,)), ((), ())),
                           preferred_element_type=jnp.float32)


PROJ_COLS = 2304


def _proj_kernel(x_ref, ln_ref, win_ref, qg_ref, wqt_ref, kvg_ref, wkv_ref, wvt_ref,
                 cm_ref, sm_ref, cs_ref, ss_ref, ct_ref, st_ref,
                 qt_ref, km_ref, vt_ref, gm_ref, qs_ref, ks_ref, vs_ref, gs_ref):
    bf = jnp.bfloat16
    x = x_ref[...]
    h = _rms(x, ln_ref[...]).astype(bf)

    cm, sm, cs, ss = cm_ref[...], sm_ref[...], cs_ref[...], ss_ref[...]
    lane = lax.broadcasted_iota(jnp.int32, (x.shape[0], LANES), 1)
    lo = lane < (LANES // 2)
    first = (lane % (LANES // 2)) < (LANES // 4)
    grp = 2 * LANES

    def cols(a):
        return win_ref[:, a:a + grp]

    c_q = _dot(h, cols(0))
    lat_b = _dot(h, cols(256))
    cqn = _rms(c_q, qg_ref[...]).astype(bf)
    ckvn = _rms(lat_b[:, 0:MLA_KV_RANK], kvg_ref[...]).astype(bf)
    kr = _rope(lat_b[:, LANES:2 * LANES], cm, sm)
    kr_first = jnp.where(first, kr, 0.0).astype(bf)

    s_scale = LOG2E / math.sqrt(SWA_D)
    for g2 in range(SWA_WIDTH // grp):
        q_s = _dot(h, cols(1024 + g2 * grp))
        for j in range(2):
            dst = slice(g2 * grp + j * LANES, g2 * grp + (j + 1) * LANES)
            qs_ref[:, dst] = (_rope(q_s[:, j * LANES:(j + 1) * LANES], cs, ss) * s_scale).astype(bf)

    q_scale = LOG2E / math.sqrt(MLA_QK)
    nope_w = MLA_HEADS * MLA_NOPE
    half = MLA_ROPE // 2
    qt = _dot_nt(wqt_ref[...], cqn)
    ct, st = ct_ref[...], st_ref[...]
    zeros = jnp.zeros((half, x.shape[0]), bf)
    for hd in range(MLA_HEADS):
        qt_ref[hd * grp:hd * grp + LANES, :] = (qt[hd * MLA_NOPE:(hd + 1) * MLA_NOPE, :] * q_scale).astype(bf)
        x1 = qt[nope_w + hd * MLA_ROPE:nope_w + hd * MLA_ROPE + half, :]
        x2 = qt[nope_w + hd * MLA_ROPE + half:nope_w + (hd + 1) * MLA_ROPE, :]
        base = hd * grp + LANES
        qt_ref[base:base + half, :] = ((x1 * ct - x2 * st) * q_scale).astype(bf)
        qt_ref[base + half:base + 2 * half, :] = zeros
        qt_ref[base + 2 * half:base + 3 * half, :] = ((x2 * ct + x1 * st) * q_scale).astype(bf)
        qt_ref[base + 3 * half:base + 4 * half, :] = zeros
    for g2 in range(nope_w // grp):
        kn = _dot(ckvn, wkv_ref[:, g2 * grp:(g2 + 1) * grp])
        for j in range(2):
            hd = 2 * g2 + j
            km_ref[:, hd * grp:hd * grp + LANES] = kn[:, j * LANES:(j + 1) * LANES].astype(bf)
            km_ref[:, hd * grp + LANES:(hd + 1) * grp] = kr_first
    for g2 in range(MLA_WIDTH // grp):
        gm_ref[:, g2 * grp:(g2 + 1) * grp] = _silu(_dot(h, cols(512 + g2 * grp))).astype(bf)

    kv_s = _dot(h, cols(1536))
    k01 = _rope(kv_s[:, 0:LANES], cs, ss)
    ks_ref[:, 0 * LANES:1 * LANES] = jnp.where(first, k01, 0.0).astype(bf)
    ks_ref[:, 1 * LANES:2 * LANES] = jnp.where(first, 0.0, pltpu.roll(k01, LANES // 4, 1)).astype(bf)
    ks_ref[:, 2 * LANES:3 * LANES] = jnp.where(first, pltpu.roll(k01, 3 * LANES // 4, 1), 0.0).astype(bf)
    ks_ref[:, 3 * LANES:4 * LANES] = jnp.where(first, 0.0, k01).astype(bf)
    v01 = kv_s[:, LANES:2 * LANES]
    v10 = pltpu.roll(v01, LANES // 2, 1)
    vs_ref[:, 0 * LANES:1 * LANES] = jnp.where(lo, v01, 0.0).astype(bf)
    vs_ref[:, 1 * LANES:2 * LANES] = jnp.where(lo, 0.0, v10).astype(bf)
    vs_ref[:, 2 * LANES:3 * LANES] = jnp.where(lo, v10, 0.0).astype(bf)
    vs_ref[:, 3 * LANES:4 * LANES] = jnp.where(lo, 0.0, v01).astype(bf)

    for g2 in range(SWA_WIDTH // grp):
        gs_ref[:, g2 * grp:(g2 + 1) * grp] = _silu(_dot(h, cols(1792 + g2 * grp))).astype(bf)

    vt_ref[...] = _dot_nt(wvt_ref[...], ckvn).astype(bf)


def _proj_call(x2, ln_g, w_in_p, q_g, w_qt, kv_g, w_k_p, w_vt, tabs_m, tabs_s, tabs_t, seq):
    n, d = x2.shape
    tm = PROJ_ROWS
    steps_per_seq = seq // tm
    batch = n // seq
    row = lambda i: (i, 0)
    const = lambda i: (0, 0)
    tab = lambda i: (i % steps_per_seq, 0)
    tab_t = lambda i: (0, i % steps_per_seq)
    col = lambda i: (i // steps_per_seq, i % steps_per_seq)
    bf = jnp.bfloat16
    qt_rows = MLA_HEADS * 2 * LANES
    row_out = lambda w: (pl.BlockSpec((tm, w), row), jax.ShapeDtypeStruct((n, w), bf))
    col_out = lambda r: (pl.BlockSpec((r, tm), col), jax.ShapeDtypeStruct((batch * r, seq), bf))
    outs = [col_out(qt_rows), row_out(qt_rows), col_out(MLA_WIDTH), row_out(MLA_WIDTH),
            row_out(SWA_WIDTH), row_out(SWA_WIDTH), row_out(SWA_WIDTH), row_out(SWA_WIDTH)]
    return pl.pallas_call(
        _proj_kernel,
        grid=(n // tm,),
        in_specs=[
            pl.BlockSpec((tm, d), row),
            pl.BlockSpec((1, d), const),
            pl.BlockSpec(w_in_p.shape, const),
            pl.BlockSpec((1, MLA_Q_RANK), const),
            pl.BlockSpec(w_qt.shape, const),
            pl.BlockSpec((1, MLA_KV_RANK), const),
            pl.BlockSpec(w_k_p.shape, const),
            pl.BlockSpec(w_vt.shape, const),
        ] + [pl.BlockSpec((tm, LANES), tab)] * 4 + [pl.BlockSpec((MLA_ROPE // 2, tm), tab_t)] * 2,
        out_specs=[o[0] for o in outs],
        out_shape=[o[1] for o in outs],
        compiler_params=pltpu.CompilerParams(
            dimension_semantics=("arbitrary",),
            vmem_limit_bytes=V7X_VMEM_LIMIT_BYTES),
        name="proj",
    )(x2, ln_g, w_in_p, q_g, w_qt, kv_g, w_k_p, w_vt, *tabs_m, *tabs_s, *tabs_t)


def _tree(op, xs):
    xs = list(xs)
    while len(xs) > 1:
        xs = [op(xs[a], xs[a + 1]) if a + 1 < len(xs) else xs[a] for a in range(0, len(xs), 2)]
    return xs[0]


def _mla_kernel(qt_ref, k_ref, vt_ref, g_ref, o_ref, s_ref, p_ref, vx_ref):
    t = MLA_BLOCK
    seq = k_ref.shape[0]
    sub = 8
    r = lax.broadcasted_iota(jnp.int32, (t, t), 0)
    c = lax.broadcasted_iota(jnp.int32, (t, t), 1)
    causal = r <= c
    nblk = seq // t
    state = [dict(m8=None) for _ in range(nblk)]

    vx_ref[0:MLA_V, :] = vt_ref[...]
    vx_ref[MLA_V:, :] = jnp.ones((vx_ref.shape[0] - MLA_V, seq), vx_ref.dtype)

    def score_tile(i, j):
        st = state[i]
        keys = slice(j * t, (j + 1) * t)
        s = _dot(k_ref[keys, :], qt_ref[:, i * t:(i + 1) * t])
        if j == i:
            s = jnp.where(causal, s, NEG_INF)
        s_ref[i % 2, keys, :] = s
        m8 = _tree(jnp.maximum, [s[a * sub:(a + 1) * sub, :] for a in range(t // sub)])
        st["m8"] = m8 if st["m8"] is None else jnp.maximum(st["m8"], m8)

    def prob_tile(i, j):
        st = state[i]
        if "m" not in st:
            st["m"] = jnp.max(st["m8"], axis=0, keepdims=True)
        keys = slice(j * t, (j + 1) * t)
        p_ref[i % 3, keys, :] = jnp.exp2(s_ref[i % 2, keys, :] - st["m"]).astype(jnp.bfloat16)

    def finish(i):
        rows = slice(i * t, (i + 1) * t)
        kv = (i + 1) * t
        acc = _dot(vx_ref[:, 0:kv], p_ref[i % 3, 0:kv, :])
        out_t = acc[0:MLA_V, :] / acc[MLA_V:MLA_V + 1, :]
        o_ref[rows, :] = (out_t.T * g_ref[rows, :].astype(jnp.float32)).astype(o_ref.dtype)

    order = list(range(nblk - 1, -1, -1))
    for j in range(order[0] + 1):
        score_tile(order[0], j)
    for pos, i in enumerate(order):
        nxt = ([functools.partial(score_tile, order[pos + 1], j) for j in range(order[pos + 1] + 1)]
               if pos + 1 < nblk else [])
        cur = [functools.partial(prob_tile, i, j) for j in range(i + 1)]
        prev = [functools.partial(finish, order[pos - 1])] if pos > 0 else []
        while nxt or cur or prev:
            if nxt:
                nxt.pop(0)()
            if cur:
                cur.pop(0)()
            if prev:
                prev.pop(0)()
    finish(order[-1])


def _mla_call(qt, km, vt, gm, batch, seq):
    n = km.shape[0]
    t = MLA_BLOCK
    ones_rows = 16
    return pl.pallas_call(
        _mla_kernel,
        grid=(batch, MLA_HEADS),
        in_specs=[
            pl.BlockSpec((2 * LANES, seq), lambda b, h: (b * MLA_HEADS + h, 0)),
            pl.BlockSpec((seq, 2 * LANES), lambda b, h: (b, h)),
            pl.BlockSpec((MLA_V, seq), lambda b, h: (b * MLA_HEADS + h, 0)),
            pl.BlockSpec((seq, MLA_V), lambda b, h: (b, h)),
        ],
        out_specs=pl.BlockSpec((seq, MLA_V), lambda b, h: (b, h)),
        out_shape=jax.ShapeDtypeStruct((n, MLA_WIDTH), jnp.bfloat16),
        scratch_shapes=[pltpu.VMEM((2, seq, t), jnp.float32),
                        pltpu.VMEM((3, seq, t), jnp.bfloat16),
                        pltpu.VMEM((MLA_V + ones_rows, seq), jnp.bfloat16)],
        compiler_params=pltpu.CompilerParams(
            dimension_semantics=("arbitrary", "arbitrary"),
            vmem_limit_bytes=V7X_VMEM_LIMIT_BYTES),
        name="mla",
    )(qt, km, vt, gm)


def _swa_out_kernel(sink_ref, q_ref, k_ref, v_ref, kh_ref, vh_ref, g_ref, mm_ref, x_ref,
                    w_ref, fg_ref, o_ref, ms_ref):
    w = SWA_WINDOW
    chunk = pl.program_id(1)
    rows_total = q_ref.shape[0]
    bf = jnp.bfloat16

    row = lax.broadcasted_iota(jnp.int32, (2 * w, 1), 0)
    qi = lax.broadcasted_iota(jnp.int32, (2 * w, 2 * w), 0) % w
    ki = lax.broadcasted_iota(jnp.int32, (2 * w, 2 * w), 1)
    rel = qi + w - ki
    band = (rel >= 0) & (rel < SWA_WINDOW)
    band_first = band & ((ki >= w) | (chunk > 0))

    def swa_block(hk, n):
        sink = [sink_ref[hk * SWA_GROUP + i] * LOG2E for i in range(SWA_GROUP)]
        sinks = (jnp.where(row < w, sink[0], sink[2]), jnp.where(row < w, sink[1], sink[3]))
        base = hk * 2 * LANES
        qrows = slice(n * w, (n + 1) * w)
        qst = jnp.concatenate([q_ref[qrows, base:base + LANES],
                               q_ref[qrows, base + LANES:base + 2 * LANES]], axis=0)
        outs = []
        for half in range(2):
            lanes = slice(base + half * LANES, base + (half + 1) * LANES)
            if n == 0:
                k = jnp.concatenate([kh_ref[:, lanes], k_ref[0:w, lanes]], axis=0)
                v = jnp.concatenate([vh_ref[:, lanes], v_ref[0:w, lanes]], axis=0)
                valid = band_first
            else:
                k = k_ref[(n - 1) * w:(n + 1) * w, lanes]
                v = v_ref[(n - 1) * w:(n + 1) * w, lanes]
                valid = band
            s = jnp.where(valid, _dot_nt(qst, k), NEG_INF)
            m = jnp.maximum(jnp.max(s, axis=1, keepdims=True), sinks[half])
            p = jnp.exp2(s - m)
            den = jnp.sum(p, axis=1, keepdims=True) + jnp.exp2(sinks[half] - m)
            outs.append(_dot(p.astype(bf), v) / den)
        o = outs[0] + outs[1]
        g = g_ref[qrows, base:base + 2 * LANES].astype(jnp.float32)
        ms_ref[qrows, base:base + LANES] = (o[0:w] * g[:, 0:LANES]).astype(bf)
        ms_ref[qrows, base + LANES:base + 2 * LANES] = (o[w:2 * w] * g[:, LANES:2 * LANES]).astype(bf)

    t = SWA_OUT_SUBTILE

    def out_tile(i):
        rows = slice(i * t, (i + 1) * t)
        y = (x_ref[rows, :] + _dot(mm_ref[rows, :], w_ref[0:MLA_WIDTH, :])
             + _dot(ms_ref[rows, :], w_ref[MLA_WIDTH:, :]))
        o_ref[rows, :] = _rms(y, fg_ref[...])

    def swa_tasks(i):
        return [functools.partial(swa_block, hk, n)
                for n in range(i * t // w, (i + 1) * t // w) for hk in range(SWA_KV_HEADS)]

    n_sub = rows_total // t
    for task in swa_tasks(0):
        task()
    for i in range(n_sub):
        nxt = swa_tasks(i + 1) if i + 1 < n_sub else []
        if nxt:
            nxt.pop(0)()
        out_tile(i)
        for task in nxt:
            task()


def _swa_out_call(sinks, qs, ks4, vs4, gs, mm, x2, w_o, fg, batch, seq):
    n, d = x2.shape
    r = SWA_OUT_ROWS
    cps = seq // r
    bpc = r // SWA_WINDOW
    bps = seq // SWA_WINDOW
    row = lambda b, c: (b * cps + c, 0)
    halo = lambda b, c: (b * bps + jnp.maximum(c * bpc - 1, 0), 0)
    const = lambda b, c: (0, 0)
    wide = pl.BlockSpec((r, SWA_WIDTH), row)
    return pl.pallas_call(
        _swa_out_kernel,
        grid=(batch, cps),
        in_specs=[
            pl.BlockSpec(memory_space=pltpu.SMEM),
            wide, wide, wide,
            pl.BlockSpec((SWA_WINDOW, SWA_WIDTH), halo),
            pl.BlockSpec((SWA_WINDOW, SWA_WIDTH), halo),
            wide,
            pl.BlockSpec((r, MLA_WIDTH), row),
            pl.BlockSpec((r, d), row),
            pl.BlockSpec(w_o.shape, const),
            pl.BlockSpec((1, d), const),
        ],
        out_specs=pl.BlockSpec((r, d), row),
        out_shape=jax.ShapeDtypeStruct((n, d), jnp.float32),
        scratch_shapes=[pltpu.VMEM((r, SWA_WIDTH), jnp.bfloat16)],
        compiler_params=pltpu.CompilerParams(
            dimension_semantics=("arbitrary", "arbitrary"),
            vmem_limit_bytes=V7X_VMEM_LIMIT_BYTES),
        name="swa_out",
    )(sinks, qs, ks4, vs4, ks4, vs4, gs, mm, x2, w_o, fg)


def _pair_rope_pieces(src, dst, n_heads, dim):
    half = dim // 2
    out = []
    for pair in range(n_heads // 2):
        for which in range(2):
            head = src + (2 * pair + which) * dim
            chunk = dst + pair * 2 * dim
            out.append((head, chunk + which * half, half))
            out.append((head + half, chunk + 2 * half + which * half, half))
    return out


def _w_in_pieces():
    pieces, src, dst = [], 0, 0
    for name, width in (("c_q", MLA_Q_RANK), ("c_kv", MLA_KV_RANK), ("k_rope", MLA_ROPE),
                        ("g_mla", MLA_WIDTH), ("q_s", SWA_WIDTH),
                        ("k_s", SWA_KV_HEADS * SWA_D), ("v_s", SWA_KV_HEADS * SWA_D),
                        ("g_swa", SWA_WIDTH)):
        if name == "k_rope":
            half = MLA_ROPE // 2
            pieces += [(src, dst, half), (src, dst + half, half),
                       (src + half, dst + 2 * half, half), (src + half, dst + 3 * half, half)]
            dst += 2 * MLA_ROPE
        elif name == "q_s":
            pieces += _pair_rope_pieces(src, dst, SWA_Q_HEADS, SWA_D)
            dst += width
        elif name == "k_s":
            pieces += _pair_rope_pieces(src, dst, SWA_KV_HEADS, SWA_D)
            dst += width
        else:
            pieces.append((src, dst, width))
            dst += width
        src += width
    assert dst == PROJ_COLS
    return pieces


def _gather_cols(src_ref, pieces, n_dst_cols):
    rows, n_src = src_ref.shape
    lane = lax.broadcasted_iota(jnp.int32, (rows, LANES), 1)
    loaded, rolled = {}, {}

    def load(k):
        if k not in loaded:
            width = min(LANES, n_src - k * LANES)
            v = src_ref[:, k * LANES:k * LANES + width]
            if width < LANES:
                v = jnp.concatenate([v, jnp.zeros((rows, LANES - width), v.dtype)], axis=1)
            loaded[k] = v
        return loaded[k]

    def shifted(k, shift):
        if shift == 0:
            return load(k)
        if (k, shift) not in rolled:
            rolled[(k, shift)] = pltpu.roll(load(k), shift, 1)
        return rolled[(k, shift)]

    chunks = []
    for d in range(n_dst_cols // LANES):
        acc = None
        for s0, d0, w in pieces:
            lo, hi = max(d0, d * LANES), min(d0 + w, (d + 1) * LANES)
            while lo < hi:
                s_abs = s0 + lo - d0
                k, a = divmod(s_abs, LANES)
                n = min(hi - lo, LANES - a)
                a_dst = lo - d * LANES
                val = shifted(k, (a_dst - a) % LANES)
                if n == LANES:
                    acc = val
                else:
                    mask = (lane >= a_dst) & (lane < a_dst + n)
                    acc = jnp.where(mask, val, 0.0 if acc is None else acc)
                lo += n
        chunks.append(acc)
    return chunks


def _prep_big_kernel(win_ref, wout_ref, winp_ref, wo_ref):
    for d, chunk in enumerate(_gather_cols(win_ref, _w_in_pieces(), PROJ_COLS)):
        winp_ref[:, d * LANES:(d + 1) * LANES] = chunk.astype(winp_ref.dtype)
    wo_ref[...] = wout_ref[...].astype(wo_ref.dtype)


def _prep_small_kernel(wq_ref, wkv_ref, wqt_ref, wk_ref, wvt_ref):
    q_pieces = ([(hd * MLA_QK, hd * MLA_NOPE, MLA_NOPE) for hd in range(MLA_HEADS)]
                + [(hd * MLA_QK + MLA_NOPE, MLA_HEADS * MLA_NOPE + hd * MLA_ROPE, MLA_ROPE)
                   for hd in range(MLA_HEADS)])
    wq = jnp.concatenate(_gather_cols(wq_ref, q_pieces, MLA_HEADS * MLA_QK), axis=1)
    wqt_ref[...] = wq.T.astype(wqt_ref.dtype)
    per_head = MLA_NOPE + MLA_V
    for hd in range(MLA_HEADS):
        wk_ref[:, hd * MLA_NOPE:(hd + 1) * MLA_NOPE] = (
            wkv_ref[:, hd * per_head:hd * per_head + MLA_NOPE].astype(wk_ref.dtype))
        wvt_ref[hd * MLA_V:(hd + 1) * MLA_V, :] = (
            wkv_ref[:, hd * per_head + MLA_NOPE:(hd + 1) * per_head].T.astype(wvt_ref.dtype))


def _prep_weights(w_in, w_q_up, w_kv_up, w_out):
    bf = jnp.bfloat16
    d, n_in = w_in.shape
    rb = PREP_ROWS
    params = pltpu.CompilerParams(dimension_semantics=("arbitrary",),
                                  vmem_limit_bytes=V7X_VMEM_LIMIT_BYTES)
    w_in_p, w_o = pl.pallas_call(
        _prep_big_kernel,
        grid=(d // rb,),
        in_specs=[pl.BlockSpec((rb, n_in), lambda i: (i, 0)),
                  pl.BlockSpec((rb, w_out.shape[1]), lambda i: (i, 0))],
        out_specs=[pl.BlockSpec((rb, PROJ_COLS), lambda i: (i, 0)),
                   pl.BlockSpec((rb, w_out.shape[1]), lambda i: (i, 0))],
        out_shape=[jax.ShapeDtypeStruct((d, PROJ_COLS), bf),
                   jax.ShapeDtypeStruct(w_out.shape, bf)],
        compiler_params=params,
        name="prep_big",
    )(w_in, w_out)
    w_qt, w_k_p, w_vt = pl.pallas_call(
        _prep_small_kernel,
        out_shape=[jax.ShapeDtypeStruct((MLA_HEADS * MLA_QK, MLA_Q_RANK), bf),
                   jax.ShapeDtypeStruct((MLA_KV_RANK, MLA_HEADS * MLA_NOPE), bf),
                   jax.ShapeDtypeStruct((MLA_WIDTH, MLA_KV_RANK), bf)],
        name="prep_small",
    )(w_q_up, w_kv_up)
    return w_in_p, w_qt, w_k_p, w_vt, w_o


def kernel(x, ln_mix, w_in, q_a_norm, w_q_up, kv_a_norm, w_kv_up, attn_sinks, w_out, final_norm):
    batch, seq, d = x.shape
    depth = ln_mix.shape[0]
    assert depth == 1, "final norm is fused into the single layer's output kernel"
    assert seq % MLA_BLOCK == 0 and seq % PROJ_ROWS == 0 and seq % SWA_OUT_ROWS == 0
    tabs_m = _rope_tables(seq, MLA_ROPE)
    tabs_s = _rope_tables(seq, SWA_D)
    x2 = x.reshape(batch * seq, d)
    tabs_t = _rope_tables_t(seq, MLA_ROPE)
    w_in_p, w_qt, w_k_p, w_vt, w_o = _prep_weights(w_in[0], w_q_up[0], w_kv_up[0], w_out[0])
    qt, km, vt, gm, qs, ks4, vs4, gs = _proj_call(
        x2, ln_mix[0].reshape(1, -1), w_in_p, q_a_norm[0].reshape(1, -1), w_qt,
        kv_a_norm[0].reshape(1, -1), w_k_p, w_vt, tabs_m, tabs_s, tabs_t, seq)
    mm = _mla_call(qt, km, vt, gm, batch, seq)
    out = _swa_out_call(attn_sinks[0], qs, ks4, vs4, gs, mm, x2, w_o,
                        final_norm.reshape(1, -1), batch, seq)
    return out.reshape(batch, seq, d)
```

```python
import functools
import math

import jax
import jax.numpy as jnp
from jax import lax
from jax.experimental import pallas as pl
from jax.experimental.pallas import tpu as pltpu

ROPE_THETA = 10000.0
NORM_EPS = 1e-6
NEG_INF = -1e30
LOG2E = 1.4426950408889634

MLA_HEADS = 4
MLA_NOPE = 128
MLA_ROPE = 64
MLA_V = 128
MLA_Q_RANK = 256
MLA_KV_RANK = 128
MLA_QK = MLA_NOPE + MLA_ROPE
MLA_WIDTH = MLA_HEADS * MLA_V

SWA_Q_HEADS = 8
SWA_KV_HEADS = 2
SWA_D = 64
SWA_WINDOW = 128
SWA_GROUP = SWA_Q_HEADS // SWA_KV_HEADS
SWA_WIDTH = SWA_Q_HEADS * SWA_D

LANES = 128
V7X_VMEM_LIMIT_BYTES = 56 * 1024 * 1024

PROJ_ROWS = 512
MLA_BLOCK = 256
SWA_OUT_ROWS = 1024
SWA_OUT_SUBTILE = 256
PREP_ROWS = 256


def _rope_tables(seq, dim):
    assert 2 * dim == LANES
    pos = jnp.arange(seq, dtype=jnp.float32)
    inv_freq = 1.0 / (ROPE_THETA ** (jnp.arange(0, dim, 2, dtype=jnp.float32) / dim))
    ang = pos[:, None] * inv_freq[None, :]
    cos, sin = jnp.cos(ang), jnp.sin(ang)
    return (jnp.concatenate([cos, cos, cos, cos], axis=1),
            jnp.concatenate([-sin, -sin, sin, sin], axis=1))


def _rope_tables_t(seq, dim):
    pos = jnp.arange(seq, dtype=jnp.float32)
    inv_freq = 1.0 / (ROPE_THETA ** (jnp.arange(0, dim, 2, dtype=jnp.float32) / dim))
    ang = pos[:, None] * inv_freq[None, :]
    return jnp.cos(ang).T, jnp.sin(ang).T


def _rope(x, c, s):
    return x * c + pltpu.roll(x, LANES // 2, 1) * s


def _pair_rope_layout(w, n_heads, dim):
    half = dim // 2
    w = w.reshape(w.shape[0], n_heads // 2, 2, 2, half)
    return w.transpose(0, 1, 3, 2, 4).reshape(w.shape[0], n_heads * dim)


def _rms(x, g):
    return x * lax.rsqrt(jnp.mean(x * x, axis=-1, keepdims=True) + NORM_EPS) * g


def _silu(g):
    return g / (1.0 + jnp.exp(-g))


def _dot(a, b):
    return jnp.dot(a, b, preferred_element_type=jnp.float32)


def _dot_nt(a, b):
    return lax.dot_general(a, b, (((1,), (1,)), ((), ())),
                           preferred_element_type=jnp.float32)


PROJ_COLS = 2304


def _proj_kernel(x_ref, ln_ref, win_ref, qg_ref, wqt_ref, kvg_ref, wkv_ref, wvt_ref,
                 cm_ref, sm_ref, cs_ref, ss_ref, ct_ref, st_ref,
                 qt_ref, km_ref, vt_ref, gm_ref, qs_ref, ks_ref, vs_ref, gs_ref):
    bf = jnp.bfloat16
    x = x_ref[...]
    h = _rms(x, ln_ref[...]).astype(bf)

    cm, sm, cs, ss = cm_ref[...], sm_ref[...], cs_ref[...], ss_ref[...]
    lane = lax.broadcasted_iota(jnp.int32, (x.shape[0], LANES), 1)
    lo = lane < (LANES // 2)
    first = (lane % (LANES // 2)) < (LANES // 4)
    grp = 2 * LANES

    def cols(a):
        return win_ref[:, a:a + grp]

    c_q = _dot(h, cols(0))
    lat_b = _dot(h, cols(256))
    cqn = _rms(c_q, qg_ref[...]).astype(bf)
    ckvn = _rms(lat_b[:, 0:MLA_KV_RANK], kvg_ref[...]).astype(bf)
    kr = _rope(lat_b[:, LANES:2 * LANES], cm, sm)
    kr_first = jnp.where(first, kr, 0.0).astype(bf)

    s_scale = LOG2E / math.sqrt(SWA_D)
    for g2 in range(SWA_WIDTH // grp):
        q_s = _dot(h, cols(1024 + g2 * grp))
        for j in range(2):
            dst = slice(g2 * grp + j * LANES, g2 * grp + (j + 1) * LANES)
            qs_ref[:, dst] = (_rope(q_s[:, j * LANES:(j + 1) * LANES], cs, ss) * s_scale).astype(bf)

    q_scale = LOG2E / math.sqrt(MLA_QK)
    nope_w = MLA_HEADS * MLA_NOPE
    half = MLA_ROPE // 2
    qt = _dot_nt(wqt_ref[...], cqn)
    ct, st = ct_ref[...], st_ref[...]
    zeros = jnp.zeros((half, x.shape[0]), bf)
    for hd in range(MLA_HEADS):
        qt_ref[hd * grp:hd * grp + LANES, :] = (qt[hd * MLA_NOPE:(hd + 1) * MLA_NOPE, :] * q_scale).astype(bf)
        x1 = qt[nope_w + hd * MLA_ROPE:nope_w + hd * MLA_ROPE + half, :]
        x2 = qt[nope_w + hd * MLA_ROPE + half:nope_w + (hd + 1) * MLA_ROPE, :]
        base = hd * grp + LANES
        qt_ref[base:base + half, :] = ((x1 * ct - x2 * st) * q_scale).astype(bf)
        qt_ref[base + half:base + 2 * half, :] = zeros
        qt_ref[base + 2 * half:base + 3 * half, :] = ((x2 * ct + x1 * st) * q_scale).astype(bf)
        qt_ref[base + 3 * half:base + 4 * half, :] = zeros
    for g2 in range(nope_w // grp):
        kn = _dot(ckvn, wkv_ref[:, g2 * grp:(g2 + 1) * grp])
        for j in range(2):
            hd = 2 * g2 + j
            km_ref[:, hd * grp:hd * grp + LANES] = kn[:, j * LANES:(j + 1) * LANES].astype(bf)
            km_ref[:, hd * grp + LANES:(hd + 1) * grp] = kr_first
    for g2 in range(MLA_WIDTH // grp):
        gm_ref[:, g2 * grp:(g2 + 1) * grp] = _silu(_dot(h, cols(512 + g2 * grp))).astype(bf)

    kv_s = _dot(h, cols(1536))
    k01 = _rope(kv_s[:, 0:LANES], cs, ss)
    ks_ref[:, 0 * LANES:1 * LANES] = jnp.where(first, k01, 0.0).astype(bf)
    ks_ref[:, 1 * LANES:2 * LANES] = jnp.where(first, 0.0, pltpu.roll(k01, LANES // 4, 1)).astype(bf)
    ks_ref[:, 2 * LANES:3 * LANES] = jnp.where(first, pltpu.roll(k01, 3 * LANES // 4, 1), 0.0).astype(bf)
    ks_ref[:, 3 * LANES:4 * LANES] = jnp.where(first, 0.0, k01).astype(bf)
    v01 = kv_s[:, LANES:2 * LANES]
    v10 = pltpu.roll(v01, LANES // 2, 1)
    vs_ref[:, 0 * LANES:1 * LANES] = jnp.where(lo, v01, 0.0).astype(bf)
    vs_ref[:, 1 * LANES:2 * LANES] = jnp.where(lo, 0.0, v10).astype(bf)
    vs_ref[:, 2 * LANES:3 * LANES] = jnp.where(lo, v10, 0.0).astype(bf)
    vs_ref[:, 3 * LANES:4 * LANES] = jnp.where(lo, 0.0, v01).astype(bf)

    for g2 in range(SWA_WIDTH // grp):
        gs_ref[:, g2 * grp:(g2 + 1) * grp] = _silu(_dot(h, cols(1792 + g2 * grp))).astype(bf)

    vt_ref[...] = _dot_nt(wvt_ref[...], ckvn).astype(bf)


def _proj_call(x2, ln_g, w_in_p, q_g, w_qt, kv_g, w_k_p, w_vt, tabs_m, tabs_s, tabs_t, seq):
    n, d = x2.shape
    tm = PROJ_ROWS
    steps_per_seq = seq // tm
    batch = n // seq
    row = lambda i: (i, 0)
    const = lambda i: (0, 0)
    tab = lambda i: (i % steps_per_seq, 0)
    tab_t = lambda i: (0, i % steps_per_seq)
    col = lambda i: (i // steps_per_seq, i % steps_per_seq)
    bf = jnp.bfloat16
    qt_rows = MLA_HEADS * 2 * LANES
    row_out = lambda w: (pl.BlockSpec((tm, w), row), jax.ShapeDtypeStruct((n, w), bf))
    col_out = lambda r: (pl.BlockSpec((r, tm), col), jax.ShapeDtypeStruct((batch * r, seq), bf))
    outs = [col_out(qt_rows), row_out(qt_rows), col_out(MLA_WIDTH), row_out(MLA_WIDTH),
            row_out(SWA_WIDTH), row_out(SWA_WIDTH), row_out(SWA_WIDTH), row_out(SWA_WIDTH)]
    return pl.pallas_call(
        _proj_kernel,
        grid=(n // tm,),
        in_specs=[
            pl.BlockSpec((tm, d), row),
            pl.BlockSpec((1, d), const),
            pl.BlockSpec(w_in_p.shape, const),
            pl.BlockSpec((1, MLA_Q_RANK), const),
            pl.BlockSpec(w_qt.shape, const),
            pl.BlockSpec((1, MLA_KV_RANK), const),
            pl.BlockSpec(w_k_p.shape, const),
            pl.BlockSpec(w_vt.shape, const),
        ] + [pl.BlockSpec((tm, LANES), tab)] * 4 + [pl.BlockSpec((MLA_ROPE // 2, tm), tab_t)] * 2,
        out_specs=[o[0] for o in outs],
        out_shape=[o[1] for o in outs],
        compiler_params=pltpu.CompilerParams(
            dimension_semantics=("arbitrary",),
            vmem_limit_bytes=V7X_VMEM_LIMIT_BYTES),
        name="proj",
    )(x2, ln_g, w_in_p, q_g, w_qt, kv_g, w_k_p, w_vt, *tabs_m, *tabs_s, *tabs_t)


def _tree(op, xs):
    xs = list(xs)
    while len(xs) > 1:
        xs = [op(xs[a], xs[a + 1]) if a + 1 < len(xs) else xs[a] for a in range(0, len(xs), 2)]
    return xs[0]


def _mla_kernel(qt_ref, k_ref, vt_ref, g_ref, o_ref, s_ref, p_ref, vx_ref):
    t = MLA_BLOCK
    seq = k_ref.shape[0]
    sub = 8
    r = lax.broadcasted_iota(jnp.int32, (t, t), 0)
    c = lax.broadcasted_iota(jnp.int32, (t, t), 1)
    causal = r <= c
    nblk = seq // t
    state = [dict(m8=None) for _ in range(nblk)]

    vx_ref[0:MLA_V, :] = vt_ref[...]
    vx_ref[MLA_V:, :] = jnp.ones((vx_ref.shape[0] - MLA_V, seq), vx_ref.dtype)

    def score_tile(i, j):
        st = state[i]
        keys = slice(j * t, (j + 1) * t)
        s = _dot(k_ref[keys, :], qt_ref[:, i * t:(i + 1) * t])
        if j == i:
            s = jnp.where(causal, s, NEG_INF)
        s_ref[i % 2, keys, :] = s
        m8 = _tree(jnp.maximum, [s[a * sub:(a + 1) * sub, :] for a in range(t // sub)])
        st["m8"] = m8 if st["m8"] is None else jnp.maximum(st["m8"], m8)

    def prob_tile(i, j):
        st = state[i]
        if "m" not in st:
            st["m"] = jnp.max(st["m8"], axis=0, keepdims=True)
        keys = slice(j * t, (j + 1) * t)
        p_ref[i % 3, keys, :] = jnp.exp2(s_ref[i % 2, keys, :] - st["m"]).astype(jnp.bfloat16)

    def finish(i):
        rows = slice(i * t, (i + 1) * t)
        kv = (i + 1) * t
        acc = _dot(vx_ref[:, 0:kv], p_ref[i % 3, 0:kv, :])
        out_t = acc[0:MLA_V, :] / acc[MLA_V:MLA_V + 1, :]
        o_ref[rows, :] = (out_t.T * g_ref[rows, :].astype(jnp.float32)).astype(o_ref.dtype)

    order = list(range(nblk - 1, -1, -1))
    for j in range(order[0] + 1):
        score_tile(order[0], j)
    for pos, i in enumerate(order):
        nxt = ([functools.partial(score_tile, order[pos + 1], j) for j in range(order[pos + 1] + 1)]
               if pos + 1 < nblk else [])
        cur = [functools.partial(prob_tile, i, j) for j in range(i + 1)]
        prev = [functools.partial(finish, order[pos - 1])] if pos > 0 else []
        while nxt or cur or prev:
            if nxt:
                nxt.pop(0)()
            if cur:
                cur.pop(0)()
            if prev:
                prev.pop(0)()
    finish(order[-1])


def _mla_call(qt, km, vt, gm, batch, seq):
    n = km.shape[0]
    t = MLA_BLOCK
    ones_rows = 16
    return pl.pallas_call(
        _mla_kernel,
        grid=(batch, MLA_HEADS),
        in_specs=[
            pl.BlockSpec((2 * LANES, seq), lambda b, h: (b * MLA_HEADS + h, 0)),
            pl.BlockSpec((seq, 2 * LANES), lambda b, h: (b, h)),
            pl.BlockSpec((MLA_V, seq), lambda b, h: (b * MLA_HEADS + h, 0)),
            pl.BlockSpec((seq, MLA_V), lambda b, h: (b, h)),
        ],
        out_specs=pl.BlockSpec((seq, MLA_V), lambda b, h: (b, h)),
        out_shape=jax.ShapeDtypeStruct((n, MLA_WIDTH), jnp.bfloat16),
        scratch_shapes=[pltpu.VMEM((2, seq, t), jnp.float32),
                        pltpu.VMEM((3, seq, t), jnp.bfloat16),
                        pltpu.VMEM((MLA_V + ones_rows, seq), jnp.bfloat16)],
        compiler_params=pltpu.CompilerParams(
            dimension_semantics=("arbitrary", "arbitrary"),
            vmem_limit_bytes=V7X_VMEM_LIMIT_BYTES),
        name="mla",
    )(qt, km, vt, gm)


def _swa_out_kernel(sink_ref, q_ref, k_ref, v_ref, kh_ref, vh_ref, g_ref, mm_ref, x_ref,
                    w_ref, fg_ref, o_ref, ms_ref):
    w = SWA_WINDOW
    chunk = pl.program_id(1)
    rows_total = q_ref.shape[0]
    bf = jnp.bfloat16

    row = lax.broadcasted_iota(jnp.int32, (2 * w, 1), 0)
    qi = lax.broadcasted_iota(jnp.int32, (2 * w, 2 * w), 0) % w
    ki = lax.broadcasted_iota(jnp.int32, (2 * w, 2 * w), 1)
    rel = qi + w - ki
    band = (rel >= 0) & (rel < SWA_WINDOW)
    band_first = band & ((ki >= w) | (chunk > 0))

    def swa_block(hk, n):
        sink = [sink_ref[hk * SWA_GROUP + i] * LOG2E for i in range(SWA_GROUP)]
        sinks = (jnp.where(row < w, sink[0], sink[2]), jnp.where(row < w, sink[1], sink[3]))
        base = hk * 2 * LANES
        qrows = slice(n * w, (n + 1) * w)
        qst = jnp.concatenate([q_ref[qrows, base:base + LANES],
                               q_ref[qrows, base + LANES:base + 2 * LANES]], axis=0)
        outs = []
        for half in range(2):
            lanes = slice(base + half * LANES, base + (half + 1) * LANES)
            if n == 0:
                k = jnp.concatenate([kh_ref[:, lanes], k_ref[0:w, lanes]], axis=0)
                v = jnp.concatenate([vh_ref[:, lanes], v_ref[0:w, lanes]], axis=0)
                valid = band_first
            else:
                k = k_ref[(n - 1) * w:(n + 1) * w, lanes]
                v = v_ref[(n - 1) * w:(n + 1) * w, lanes]
                valid = band
            s = jnp.where(valid, _dot_nt(qst, k), NEG_INF)
            m = jnp.maximum(jnp.max(s, axis=1, keepdims=True), sinks[half])
            p = jnp.exp2(s - m)
            den = jnp.sum(p, axis=1, keepdims=True) + jnp.exp2(sinks[half] - m)
            outs.append(_dot(p.astype(bf), v) / den)
        o = outs[0] + outs[1]
        g = g_ref[qrows, base:base + 2 * LANES].astype(jnp.float32)
        ms_ref[qrows, base:base + LANES] = (o[0:w] * g[:, 0:LANES]).astype(bf)
        ms_ref[qrows, base + LANES:base + 2 * LANES] = (o[w:2 * w] * g[:, LANES:2 * LANES]).astype(bf)

    t = SWA_OUT_SUBTILE

    def out_tile(i):
        rows = slice(i * t, (i + 1) * t)
        y = (x_ref[rows, :] + _dot(mm_ref[rows, :], w_ref[0:MLA_WIDTH, :])
             + _dot(ms_ref[rows, :], w_ref[MLA_WIDTH:, :]))
        o_ref[rows, :] = _rms(y, fg_ref[...])

    def swa_tasks(i):
        return [functools.partial(swa_block, hk, n)
                for n in range(i * t // w, (i + 1) * t // w) for hk in range(SWA_KV_HEADS)]

    n_sub = rows_total // t
    for task in swa_tasks(0):
        task()
    for i in range(n_sub):
        nxt = swa_tasks(i + 1) if i + 1 < n_sub else []
        if nxt:
            nxt.pop(0)()
        out_tile(i)
        for task in nxt:
            task()


def _swa_out_call(sinks, qs, ks4, vs4, gs, mm, x2, w_o, fg, batch, seq):
    n, d = x2.shape
    r = SWA_OUT_ROWS
    cps = seq // r
    bpc = r // SWA_WINDOW
    bps = seq // SWA_WINDOW
    row = lambda b, c: (b * cps + c, 0)
    halo = lambda b, c: (b * bps + jnp.maximum(c * bpc - 1, 0), 0)
    const = lambda b, c: (0, 0)
    wide = pl.BlockSpec((r, SWA_WIDTH), row)
    return pl.pallas_call(
        _swa_out_kernel,
        grid=(batch, cps),
        in_specs=[
            pl.BlockSpec(memory_space=pltpu.SMEM),
            wide, wide, wide,
            pl.BlockSpec((SWA_WINDOW, SWA_WIDTH), halo),
            pl.BlockSpec((SWA_WINDOW, SWA_WIDTH), halo),
            wide,
            pl.BlockSpec((r, MLA_WIDTH), row),
            pl.BlockSpec((r, d), row),
            pl.BlockSpec(w_o.shape, const),
            pl.BlockSpec((1, d), const),
        ],
        out_specs=pl.BlockSpec((r, d), row),
        out_shape=jax.ShapeDtypeStruct((n, d), jnp.float32),
        scratch_shapes=[pltpu.VMEM((r, SWA_WIDTH), jnp.bfloat16)],
        compiler_params=pltpu.CompilerParams(
            dimension_semantics=("arbitrary", "arbitrary"),
            vmem_limit_bytes=V7X_VMEM_LIMIT_BYTES),
        name="swa_out",
    )(sinks, qs, ks4, vs4, ks4, vs4, gs, mm, x2, w_o, fg)


def _pair_rope_pieces(src, dst, n_heads, dim):
    half = dim // 2
    out = []
    for pair in range(n_heads // 2):
        for which in range(2):
            head = src + (2 * pair + which) * dim
            chunk = dst + pair * 2 * dim
            out.append((head, chunk + which * half, half))
            out.append((head + half, chunk + 2 * half + which * half, half))
    return out


def _w_in_pieces():
    pieces, src, dst = [], 0, 0
    for name, width in (("c_q", MLA_Q_RANK), ("c_kv", MLA_KV_RANK), ("k_rope", MLA_ROPE),
                        ("g_mla", MLA_WIDTH), ("q_s", SWA_WIDTH),
                        ("k_s", SWA_KV_HEADS * SWA_D), ("v_s", SWA_KV_HEADS * SWA_D),
                        ("g_swa", SWA_WIDTH)):
        if name == "k_rope":
            half = MLA_ROPE // 2
            pieces += [(src, dst, half), (src, dst + half, half),
                       (src + half, dst + 2 * half, half), (src + half, dst + 3 * half, half)]
            dst += 2 * MLA_ROPE
        elif name == "q_s":
            pieces += _pair_rope_pieces(src, dst, SWA_Q_HEADS, SWA_D)
            dst += width
        elif name == "k_s":
            pieces += _pair_rope_pieces(src, dst, SWA_KV_HEADS, SWA_D)
            dst += width
        else:
            pieces.append((src, dst, width))
            dst += width
        src += width
    assert dst == PROJ_COLS
    return pieces


def _gather_cols(src_ref, pieces, n_dst_cols):
    rows, n_src = src_ref.shape
    lane = lax.broadcasted_iota(jnp.int32, (rows, LANES), 1)
    loaded, rolled = {}, {}

    def load(k):
        if k not in loaded:
            width = min(LANES, n_src - k * LANES)
            v = src_ref[:, k * LANES:k * LANES + width]
            if width < LANES:
                v = jnp.concatenate([v, jnp.zeros((rows, LANES - width), v.dtype)], axis=1)
            loaded[k] = v
        return loaded[k]

    def shifted(k, shift):
        if shift == 0:
            return load(k)
        if (k, shift) not in rolled:
            rolled[(k, shift)] = pltpu.roll(load(k), shift, 1)
        return rolled[(k, shift)]

    chunks = []
    for d in range(n_dst_cols // LANES):
        acc = None
        for s0, d0, w in pieces:
            lo, hi = max(d0, d * LANES), min(d0 + w, (d + 1) * LANES)
            while lo < hi:
                s_abs = s0 + lo - d0
                k, a = divmod(s_abs, LANES)
                n = min(hi - lo, LANES - a)
                a_dst = lo - d * LANES
                val = shifted(k, (a_dst - a) % LANES)
                if n == LANES:
                    acc = val
                else:
                    mask = (lane >= a_dst) & (lane < a_dst + n)
                    acc = jnp.where(mask, val, 0.0 if acc is None else acc)
                lo += n
        chunks.append(acc)
    return chunks


def _prep_big_kernel(wint_ref, wout_ref, winp_ref, wo_ref):
    pieces = _w_in_pieces()
    for d in range(PROJ_COLS // LANES):
        parts = []
        for s0, d0, w in pieces:
            lo, hi = max(d0, d * LANES), min(d0 + w, (d + 1) * LANES)
            if lo < hi:
                parts.append((lo, wint_ref[s0 + lo - d0:s0 + hi - d0, :]))
        parts.sort(key=lambda t: t[0])
        assert sum(p.shape[0] for _, p in parts) == LANES
        blk = jnp.concatenate([p for _, p in parts], axis=0) if len(parts) > 1 else parts[0][1]
        winp_ref[:, d * LANES:(d + 1) * LANES] = blk.T.astype(winp_ref.dtype)
    wo_ref[...] = wout_ref[...].astype(wo_ref.dtype)


def _prep_small_kernel(wq_ref, wkv_ref, wqt_ref, wk_ref, wvt_ref):
    q_pieces = ([(hd * MLA_QK, hd * MLA_NOPE, MLA_NOPE) for hd in range(MLA_HEADS)]
                + [(hd * MLA_QK + MLA_NOPE, MLA_HEADS * MLA_NOPE + hd * MLA_ROPE, MLA_ROPE)
                   for hd in range(MLA_HEADS)])
    wq = jnp.concatenate(_gather_cols(wq_ref, q_pieces, MLA_HEADS * MLA_QK), axis=1)
    wqt_ref[...] = wq.T.astype(wqt_ref.dtype)
    per_head = MLA_NOPE + MLA_V
    for hd in range(MLA_HEADS):
        wk_ref[:, hd * MLA_NOPE:(hd + 1) * MLA_NOPE] = (
            wkv_ref[:, hd * per_head:hd * per_head + MLA_NOPE].astype(wk_ref.dtype))
        wvt_ref[hd * MLA_V:(hd + 1) * MLA_V, :] = (
            wkv_ref[:, hd * per_head + MLA_NOPE:(hd + 1) * per_head].T.astype(wvt_ref.dtype))


def _prep_weights(w_in, w_q_up, w_kv_up, w_out):
    bf = jnp.bfloat16
    d, n_in = w_in.shape
    rb = PREP_ROWS
    params = pltpu.CompilerParams(dimension_semantics=("arbitrary",),
                                  vmem_limit_bytes=V7X_VMEM_LIMIT_BYTES)
    w_in_p, w_o = pl.pallas_call(
        _prep_big_kernel,
        grid=(d // rb,),
        in_specs=[pl.BlockSpec((n_in, rb), lambda i: (0, i)),
                  pl.BlockSpec((rb, w_out.shape[1]), lambda i: (i, 0))],
        out_specs=[pl.BlockSpec((rb, PROJ_COLS), lambda i: (i, 0)),
                   pl.BlockSpec((rb, w_out.shape[1]), lambda i: (i, 0))],
        out_shape=[jax.ShapeDtypeStruct((d, PROJ_COLS), bf),
                   jax.ShapeDtypeStruct(w_out.shape, bf)],
        compiler_params=params,
        name="prep_big",
    )(w_in.T, w_out)
    w_qt, w_k_p, w_vt = pl.pallas_call(
        _prep_small_kernel,
        out_shape=[jax.ShapeDtypeStruct((MLA_HEADS * MLA_QK, MLA_Q_RANK), bf),
                   jax.ShapeDtypeStruct((MLA_KV_RANK, MLA_HEADS * MLA_NOPE), bf),
                   jax.ShapeDtypeStruct((MLA_WIDTH, MLA_KV_RANK), bf)],
        name="prep_small",
    )(w_q_up, w_kv_up)
    return w_in_p, w_qt, w_k_p, w_vt, w_o


def kernel(x, ln_mix, w_in, q_a_norm, w_q_up, kv_a_norm, w_kv_up, attn_sinks, w_out, final_norm):
    batch, seq, d = x.shape
    depth = ln_mix.shape[0]
    assert depth == 1, "final norm is fused into the single layer's output kernel"
    assert seq % MLA_BLOCK == 0 and seq % PROJ_ROWS == 0 and seq % SWA_OUT_ROWS == 0
    tabs_m = _rope_tables(seq, MLA_ROPE)
    tabs_s = _rope_tables(seq, SWA_D)
    x2 = x.reshape(batch * seq, d)
    tabs_t = _rope_tables_t(seq, MLA_ROPE)
    w_in_p, w_qt, w_k_p, w_vt, w_o = _prep_weights(w_in[0], w_q_up[0], w_kv_up[0], w_out[0])
    qt, km, vt, gm, qs, ks4, vs4, gs = _proj_call(
        x2, ln_mix[0].reshape(1, -1), w_in_p, q_a_norm[0].reshape(1, -1), w_qt,
        kv_a_norm[0].reshape(1, -1), w_k_p, w_vt, tabs_m, tabs_s, tabs_t, seq)
    mm = _mla_call(qt, km, vt, gm, batch, seq)
    out = _swa_out_call(attn_sinks[0], qs, ks4, vs4, gs, mm, x2, w_o,
                        final_norm.reshape(1, -1), batch, seq)
    return out.reshape(batch, seq, d)
```

```python
import functools
import math

import jax
import jax.numpy as jnp
from jax import lax
from jax.experimental import pallas as pl
from jax.experimental.pallas import tpu as pltpu

ROPE_THETA = 10000.0
NORM_EPS = 1e-6
NEG_INF = -1e30
LOG2E = 1.4426950408889634

MLA_HEADS = 4
MLA_NOPE = 128
MLA_ROPE = 64
MLA_V = 128
MLA_Q_RANK = 256
MLA_KV_RANK = 128
MLA_QK = MLA_NOPE + MLA_ROPE
MLA_WIDTH = MLA_HEADS * MLA_V

SWA_Q_HEADS = 8
SWA_KV_HEADS = 2
SWA_D = 64
SWA_WINDOW = 128
SWA_GROUP = SWA_Q_HEADS // SWA_KV_HEADS
SWA_WIDTH = SWA_Q_HEADS * SWA_D

LANES = 128
V7X_VMEM_LIMIT_BYTES = 56 * 1024 * 1024

PROJ_ROWS = 512
MLA_BLOCK = 256
MLA_HEADS_PER_STEP = 2
SWA_OUT_ROWS = 1024
SWA_OUT_SUBTILE = 256
PREP_ROWS = 256


def _rope_tables(seq, dim):
    assert 2 * dim == LANES
    pos = jnp.arange(seq, dtype=jnp.float32)
    inv_freq = 1.0 / (ROPE_THETA ** (jnp.arange(0, dim, 2, dtype=jnp.float32) / dim))
    ang = pos[:, None] * inv_freq[None, :]
    cos, sin = jnp.cos(ang), jnp.sin(ang)
    return (jnp.concatenate([cos, cos, cos, cos], axis=1),
            jnp.concatenate([-sin, -sin, sin, sin], axis=1))


def _rope_tables_t(seq, dim):
    pos = jnp.arange(seq, dtype=jnp.float32)
    inv_freq = 1.0 / (ROPE_THETA ** (jnp.arange(0, dim, 2, dtype=jnp.float32) / dim))
    ang = pos[:, None] * inv_freq[None, :]
    return jnp.cos(ang).T, jnp.sin(ang).T


def _rope(x, c, s):
    return x * c + pltpu.roll(x, LANES // 2, 1) * s


def _pair_rope_layout(w, n_heads, dim):
    half = dim // 2
    w = w.reshape(w.shape[0], n_heads // 2, 2, 2, half)
    return w.transpose(0, 1, 3, 2, 4).reshape(w.shape[0], n_heads * dim)


def _rms(x, g):
    return x * lax.rsqrt(jnp.mean(x * x, axis=-1, keepdims=True) + NORM_EPS) * g


def _silu(g):
    return g / (1.0 + jnp.exp(-g))


def _dot(a, b):
    return jnp.dot(a, b, preferred_element_type=jnp.float32)


def _dot_nt(a, b):
    return lax.dot_general(a, b, (((1,), (1,)), ((), ())),
                           preferred_element_type=jnp.float32)


PROJ_COLS = 2304


def _proj_kernel(x_ref, ln_ref, win_ref, qg_ref, wqt_ref, kvg_ref, wkv_ref, wvt_ref,
                 cm_ref, sm_ref, cs_ref, ss_ref, ct_ref, st_ref,
                 qt_ref, km_ref, vt_ref, gm_ref, qs_ref, ks_ref, vs_ref, gs_ref):
    bf = jnp.bfloat16
    x = x_ref[...]
    h = _rms(x, ln_ref[...]).astype(bf)

    cm, sm, cs, ss = cm_ref[...], sm_ref[...], cs_ref[...], ss_ref[...]
    lane = lax.broadcasted_iota(jnp.int32, (x.shape[0], LANES), 1)
    lo = lane < (LANES // 2)
    first = (lane % (LANES // 2)) < (LANES // 4)
    grp = 2 * LANES

    def cols(a):
        return win_ref[:, a:a + grp]

    c_q = _dot(h, cols(0))
    lat_b = _dot(h, cols(256))
    cqn = _rms(c_q, qg_ref[...]).astype(bf)
    ckvn = _rms(lat_b[:, 0:MLA_KV_RANK], kvg_ref[...]).astype(bf)
    kr = _rope(lat_b[:, LANES:2 * LANES], cm, sm)
    kr_first = jnp.where(first, kr, 0.0).astype(bf)

    s_scale = LOG2E / math.sqrt(SWA_D)
    for g2 in range(SWA_WIDTH // grp):
        q_s = _dot(h, cols(1024 + g2 * grp))
        for j in range(2):
            dst = slice(g2 * grp + j * LANES, g2 * grp + (j + 1) * LANES)
            qs_ref[:, dst] = (_rope(q_s[:, j * LANES:(j + 1) * LANES], cs, ss) * s_scale).astype(bf)

    q_scale = LOG2E / math.sqrt(MLA_QK)
    nope_w = MLA_HEADS * MLA_NOPE
    half = MLA_ROPE // 2
    qt = _dot_nt(wqt_ref[...], cqn)
    ct, st = ct_ref[...], st_ref[...]
    zeros = jnp.zeros((half, x.shape[0]), bf)
    for hd in range(MLA_HEADS):
        qt_ref[hd * grp:hd * grp + LANES, :] = (qt[hd * MLA_NOPE:(hd + 1) * MLA_NOPE, :] * q_scale).astype(bf)
        x1 = qt[nope_w + hd * MLA_ROPE:nope_w + hd * MLA_ROPE + half, :]
        x2 = qt[nope_w + hd * MLA_ROPE + half:nope_w + (hd + 1) * MLA_ROPE, :]
        base = hd * grp + LANES
        qt_ref[base:base + half, :] = ((x1 * ct - x2 * st) * q_scale).astype(bf)
        qt_ref[base + half:base + 2 * half, :] = zeros
        qt_ref[base + 2 * half:base + 3 * half, :] = ((x2 * ct + x1 * st) * q_scale).astype(bf)
        qt_ref[base + 3 * half:base + 4 * half, :] = zeros
    for g2 in range(nope_w // grp):
        kn = _dot(ckvn, wkv_ref[:, g2 * grp:(g2 + 1) * grp])
        for j in range(2):
            hd = 2 * g2 + j
            km_ref[:, hd * grp:hd * grp + LANES] = kn[:, j * LANES:(j + 1) * LANES].astype(bf)
            km_ref[:, hd * grp + LANES:(hd + 1) * grp] = kr_first
    for g2 in range(MLA_WIDTH // grp):
        gm_ref[:, g2 * grp:(g2 + 1) * grp] = _silu(_dot(h, cols(512 + g2 * grp))).astype(bf)

    kv_s = _dot(h, cols(1536))
    k01 = _rope(kv_s[:, 0:LANES], cs, ss)
    ks_ref[:, 0 * LANES:1 * LANES] = jnp.where(first, k01, 0.0).astype(bf)
    ks_ref[:, 1 * LANES:2 * LANES] = jnp.where(first, 0.0, pltpu.roll(k01, LANES // 4, 1)).astype(bf)
    ks_ref[:, 2 * LANES:3 * LANES] = jnp.where(first, pltpu.roll(k01, 3 * LANES // 4, 1), 0.0).astype(bf)
    ks_ref[:, 3 * LANES:4 * LANES] = jnp.where(first, 0.0, k01).astype(bf)
    v01 = kv_s[:, LANES:2 * LANES]
    v10 = pltpu.roll(v01, LANES // 2, 1)
    vs_ref[:, 0 * LANES:1 * LANES] = jnp.where(lo, v01, 0.0).astype(bf)
    vs_ref[:, 1 * LANES:2 * LANES] = jnp.where(lo, 0.0, v10).astype(bf)
    vs_ref[:, 2 * LANES:3 * LANES] = jnp.where(lo, v10, 0.0).astype(bf)
    vs_ref[:, 3 * LANES:4 * LANES] = jnp.where(lo, 0.0, v01).astype(bf)

    for g2 in range(SWA_WIDTH // grp):
        gs_ref[:, g2 * grp:(g2 + 1) * grp] = _silu(_dot(h, cols(1792 + g2 * grp))).astype(bf)

    vt_ref[...] = _dot_nt(wvt_ref[...], ckvn).astype(bf)


def _proj_call(x2, ln_g, w_in_p, q_g, w_qt, kv_g, w_k_p, w_vt, tabs_m, tabs_s, tabs_t, seq):
    n, d = x2.shape
    tm = PROJ_ROWS
    steps_per_seq = seq // tm
    batch = n // seq
    row = lambda i: (i, 0)
    const = lambda i: (0, 0)
    tab = lambda i: (i % steps_per_seq, 0)
    tab_t = lambda i: (0, i % steps_per_seq)
    col = lambda i: (i // steps_per_seq, i % steps_per_seq)
    bf = jnp.bfloat16
    qt_rows = MLA_HEADS * 2 * LANES
    row_out = lambda w: (pl.BlockSpec((tm, w), row), jax.ShapeDtypeStruct((n, w), bf))
    col_out = lambda r: (pl.BlockSpec((r, tm), col), jax.ShapeDtypeStruct((batch * r, seq), bf))
    outs = [col_out(qt_rows), row_out(qt_rows), col_out(MLA_WIDTH), row_out(MLA_WIDTH),
            row_out(SWA_WIDTH), row_out(SWA_WIDTH), row_out(SWA_WIDTH), row_out(SWA_WIDTH)]
    return pl.pallas_call(
        _proj_kernel,
        grid=(n // tm,),
        in_specs=[
            pl.BlockSpec((tm, d), row),
            pl.BlockSpec((1, d), const),
            pl.BlockSpec(w_in_p.shape, const),
            pl.BlockSpec((1, MLA_Q_RANK), const),
            pl.BlockSpec(w_qt.shape, const),
            pl.BlockSpec((1, MLA_KV_RANK), const),
            pl.BlockSpec(w_k_p.shape, const),
            pl.BlockSpec(w_vt.shape, const),
        ] + [pl.BlockSpec((tm, LANES), tab)] * 4 + [pl.BlockSpec((MLA_ROPE // 2, tm), tab_t)] * 2,
        out_specs=[o[0] for o in outs],
        out_shape=[o[1] for o in outs],
        compiler_params=pltpu.CompilerParams(
            dimension_semantics=("arbitrary",),
            vmem_limit_bytes=V7X_VMEM_LIMIT_BYTES),
        name="proj",
    )(x2, ln_g, w_in_p, q_g, w_qt, kv_g, w_k_p, w_vt, *tabs_m, *tabs_s, *tabs_t)


def _tree(op, xs):
    xs = list(xs)
    while len(xs) > 1:
        xs = [op(xs[a], xs[a + 1]) if a + 1 < len(xs) else xs[a] for a in range(0, len(xs), 2)]
    return xs[0]


def _mla_kernel(qt_ref, k_ref, vt_ref, g_ref, o_ref, s_ref, p_ref, vx_ref):
    t = MLA_BLOCK
    seq = k_ref.shape[0]
    sub = 8
    qk = 2 * LANES
    r = lax.broadcasted_iota(jnp.int32, (t, t), 0)
    c = lax.broadcasted_iota(jnp.int32, (t, t), 1)
    causal = r <= c
    nblk = seq // t
    units = [(hd, i) for hd in range(MLA_HEADS_PER_STEP) for i in range(nblk - 1, -1, -1)]
    state = [dict(m8=None) for _ in units]

    for hd in range(MLA_HEADS_PER_STEP):
        vx_ref[hd, 0:MLA_V, :] = vt_ref[hd * MLA_V:(hd + 1) * MLA_V, :]
        vx_ref[hd, MLA_V:, :] = jnp.ones((vx_ref.shape[1] - MLA_V, seq), vx_ref.dtype)

    def score_tile(u, j):
        hd, i = units[u]
        st = state[u]
        keys = slice(j * t, (j + 1) * t)
        s = _dot(k_ref[keys, hd * qk:(hd + 1) * qk],
                 qt_ref[hd * qk:(hd + 1) * qk, i * t:(i + 1) * t])
        if j == i:
            s = jnp.where(causal, s, NEG_INF)
        s_ref[u % 2, keys, :] = s
        m8 = _tree(jnp.maximum, [s[a * sub:(a + 1) * sub, :] for a in range(t // sub)])
        st["m8"] = m8 if st["m8"] is None else jnp.maximum(st["m8"], m8)

    def prob_tile(u, j):
        st = state[u]
        if "m" not in st:
            st["m"] = jnp.max(st["m8"], axis=0, keepdims=True)
        keys = slice(j * t, (j + 1) * t)
        p_ref[u % 3, keys, :] = jnp.exp2(s_ref[u % 2, keys, :] - st["m"]).astype(jnp.bfloat16)

    def finish(u):
        hd, i = units[u]
        rows = slice(i * t, (i + 1) * t)
        kv = (i + 1) * t
        acc = _dot(vx_ref[hd, :, 0:kv], p_ref[u % 3, 0:kv, :])
        out_t = acc[0:MLA_V, :] / acc[MLA_V:MLA_V + 1, :]
        gate = g_ref[rows, hd * MLA_V:(hd + 1) * MLA_V].astype(jnp.float32)
        o_ref[rows, hd * MLA_V:(hd + 1) * MLA_V] = (out_t.T * gate).astype(o_ref.dtype)

    for j in range(units[0][1] + 1):
        score_tile(0, j)
    for u, (_, i) in enumerate(units):
        nxt = ([functools.partial(score_tile, u + 1, j) for j in range(units[u + 1][1] + 1)]
               if u + 1 < len(units) else [])
        cur = [functools.partial(prob_tile, u, j) for j in range(i + 1)]
        prev = [functools.partial(finish, u - 1)] if u > 0 else []
        while nxt or cur or prev:
            if nxt:
                nxt.pop(0)()
            if cur:
                cur.pop(0)()
            if prev:
                prev.pop(0)()
    finish(len(units) - 1)


def _mla_call(qt, km, vt, gm, batch, seq):
    n = km.shape[0]
    t = MLA_BLOCK
    hps = MLA_HEADS_PER_STEP
    groups = MLA_HEADS // hps
    ones_rows = 16
    feat = lambda b, h: (b * groups + h, 0)
    tok = lambda b, h: (b, h)
    return pl.pallas_call(
        _mla_kernel,
        grid=(batch, groups),
        in_specs=[
            pl.BlockSpec((hps * 2 * LANES, seq), feat),
            pl.BlockSpec((seq, hps * 2 * LANES), tok),
            pl.BlockSpec((hps * MLA_V, seq), feat),
            pl.BlockSpec((seq, hps * MLA_V), tok),
        ],
        out_specs=pl.BlockSpec((seq, hps * MLA_V), tok),
        out_shape=jax.ShapeDtypeStruct((n, MLA_WIDTH), jnp.bfloat16),
        scratch_shapes=[pltpu.VMEM((2, seq, t), jnp.float32),
                        pltpu.VMEM((3, seq, t), jnp.bfloat16),
                        pltpu.VMEM((hps, MLA_V + ones_rows, seq), jnp.bfloat16)],
        compiler_params=pltpu.CompilerParams(
            dimension_semantics=("arbitrary", "arbitrary"),
            vmem_limit_bytes=V7X_VMEM_LIMIT_BYTES),
        name="mla",
    )(qt, km, vt, gm)


def _swa_out_kernel(sink_ref, q_ref, k_ref, v_ref, kh_ref, vh_ref, g_ref, mm_ref, x_ref,
                    w_ref, fg_ref, o_ref, ms_ref):
    w = SWA_WINDOW
    chunk = pl.program_id(1)
    rows_total = q_ref.shape[0]
    bf = jnp.bfloat16

    row = lax.broadcasted_iota(jnp.int32, (2 * w, 1), 0)
    qi = lax.broadcasted_iota(jnp.int32, (2 * w, 2 * w), 0) % w
    ki = lax.broadcasted_iota(jnp.int32, (2 * w, 2 * w), 1)
    rel = qi + w - ki
    band = (rel >= 0) & (rel < SWA_WINDOW)
    band_first = band & ((ki >= w) | (chunk > 0))

    def swa_block(hk, n):
        sink = [sink_ref[hk * SWA_GROUP + i] * LOG2E for i in range(SWA_GROUP)]
        sinks = (jnp.where(row < w, sink[0], sink[2]), jnp.where(row < w, sink[1], sink[3]))
        base = hk * 2 * LANES
        qrows = slice(n * w, (n + 1) * w)
        qst = jnp.concatenate([q_ref[qrows, base:base + LANES],
                               q_ref[qrows, base + LANES:base + 2 * LANES]], axis=0)
        outs = []
        for half in range(2):
            lanes = slice(base + half * LANES, base + (half + 1) * LANES)
            if n == 0:
                k = jnp.concatenate([kh_ref[:, lanes], k_ref[0:w, lanes]], axis=0)
                v = jnp.concatenate([vh_ref[:, lanes], v_ref[0:w, lanes]], axis=0)
                valid = band_first
            else:
                k = k_ref[(n - 1) * w:(n + 1) * w, lanes]
                v = v_ref[(n - 1) * w:(n + 1) * w, lanes]
                valid = band
            s = jnp.where(valid, _dot_nt(qst, k), NEG_INF)
            m = jnp.maximum(jnp.max(s, axis=1, keepdims=True), sinks[half])
            p = jnp.exp2(s - m)
            den = jnp.sum(p, axis=1, keepdims=True) + jnp.exp2(sinks[half] - m)
            outs.append(_dot(p.astype(bf), v) / den)
        o = outs[0] + outs[1]
        g = g_ref[qrows, base:base + 2 * LANES].astype(jnp.float32)
        ms_ref[qrows, base:base + LANES] = (o[0:w] * g[:, 0:LANES]).astype(bf)
        ms_ref[qrows, base + LANES:base + 2 * LANES] = (o[w:2 * w] * g[:, LANES:2 * LANES]).astype(bf)

    t = SWA_OUT_SUBTILE

    def out_tile(i):
        rows = slice(i * t, (i + 1) * t)
        y = (x_ref[rows, :] + _dot(mm_ref[rows, :], w_ref[0:MLA_WIDTH, :])
             + _dot(ms_ref[rows, :], w_ref[MLA_WIDTH:, :]))
        o_ref[rows, :] = _rms(y, fg_ref[...])

    def swa_tasks(i):
        return [functools.partial(swa_block, hk, n)
                for n in range(i * t // w, (i + 1) * t // w) for hk in range(SWA_KV_HEADS)]

    n_sub = rows_total // t
    for task in swa_tasks(0):
        task()
    for i in range(n_sub):
        nxt = swa_tasks(i + 1) if i + 1 < n_sub else []
        if nxt:
            nxt.pop(0)()
        out_tile(i)
        for task in nxt:
            task()


def _swa_out_call(sinks, qs, ks4, vs4, gs, mm, x2, w_o, fg, batch, seq):
    n, d = x2.shape
    r = SWA_OUT_ROWS
    cps = seq // r
    bpc = r // SWA_WINDOW
    bps = seq // SWA_WINDOW
    row = lambda b, c: (b * cps + c, 0)
    halo = lambda b, c: (b * bps + jnp.maximum(c * bpc - 1, 0), 0)
    const = lambda b, c: (0, 0)
    wide = pl.BlockSpec((r, SWA_WIDTH), row)
    return pl.pallas_call(
        _swa_out_kernel,
        grid=(batch, cps),
        in_specs=[
            pl.BlockSpec(memory_space=pltpu.SMEM),
            wide, wide, wide,
            pl.BlockSpec((SWA_WINDOW, SWA_WIDTH), halo),
            pl.BlockSpec((SWA_WINDOW, SWA_WIDTH), halo),
            wide,
            pl.BlockSpec((r, MLA_WIDTH), row),
            pl.BlockSpec((r, d), row),
            pl.BlockSpec(w_o.shape, const),
            pl.BlockSpec((1, d), const),
        ],
        out_specs=pl.BlockSpec((r, d), row),
        out_shape=jax.ShapeDtypeStruct((n, d), jnp.float32),
        scratch_shapes=[pltpu.VMEM((r, SWA_WIDTH), jnp.bfloat16)],
        compiler_params=pltpu.CompilerParams(
            dimension_semantics=("arbitrary", "arbitrary"),
            vmem_limit_bytes=V7X_VMEM_LIMIT_BYTES),
        name="swa_out",
    )(sinks, qs, ks4, vs4, ks4, vs4, gs, mm, x2, w_o, fg)


def _pair_rope_pieces(src, dst, n_heads, dim):
    half = dim // 2
    out = []
    for pair in range(n_heads // 2):
        for which in range(2):
            head = src + (2 * pair + which) * dim
            chunk = dst + pair * 2 * dim
            out.append((head, chunk + which * half, half))
            out.append((head + half, chunk + 2 * half + which * half, half))
    return out


def _w_in_pieces():
    pieces, src, dst = [], 0, 0
    for name, width in (("c_q", MLA_Q_RANK), ("c_kv", MLA_KV_RANK), ("k_rope", MLA_ROPE),
                        ("g_mla", MLA_WIDTH), ("q_s", SWA_WIDTH),
                        ("k_s", SWA_KV_HEADS * SWA_D), ("v_s", SWA_KV_HEADS * SWA_D),
                        ("g_swa", SWA_WIDTH)):
        if name == "k_rope":
            half = MLA_ROPE // 2
            pieces += [(src, dst, half), (src, dst + half, half),
                       (src + half, dst + 2 * half, half), (src + half, dst + 3 * half, half)]
            dst += 2 * MLA_ROPE
        elif name == "q_s":
            pieces += _pair_rope_pieces(src, dst, SWA_Q_HEADS, SWA_D)
            dst += width
        elif name == "k_s":
            pieces += _pair_rope_pieces(src, dst, SWA_KV_HEADS, SWA_D)
            dst += width
        else:
            pieces.append((src, dst, width))
            dst += width
        src += width
    assert dst == PROJ_COLS
    return pieces


def _gather_cols(src_ref, pieces, n_dst_cols):
    rows, n_src = src_ref.shape
    lane = lax.broadcasted_iota(jnp.int32, (rows, LANES), 1)
    loaded, rolled = {}, {}

    def load(k):
        if k not in loaded:
            width = min(LANES, n_src - k * LANES)
            v = src_ref[:, k * LANES:k * LANES + width]
            if width < LANES:
                v = jnp.concatenate([v, jnp.zeros((rows, LANES - width), v.dtype)], axis=1)
            loaded[k] = v
        return loaded[k]

    def shifted(k, shift):
        if shift == 0:
            return load(k)
        if (k, shift) not in rolled:
            rolled[(k, shift)] = pltpu.roll(load(k), shift, 1)
        return rolled[(k, shift)]

    chunks = []
    for d in range(n_dst_cols // LANES):
        acc = None
        for s0, d0, w in pieces:
            lo, hi = max(d0, d * LANES), min(d0 + w, (d + 1) * LANES)
            while lo < hi:
                s_abs = s0 + lo - d0
                k, a = divmod(s_abs, LANES)
                n = min(hi - lo, LANES - a)
                a_dst = lo - d * LANES
                val = shifted(k, (a_dst - a) % LANES)
                if n == LANES:
                    acc = val
                else:
                    mask = (lane >= a_dst) & (lane < a_dst + n)
                    acc = jnp.where(mask, val, 0.0 if acc is None else acc)
                lo += n
        chunks.append(acc)
    return chunks


def _prep_big_kernel(wint_ref, wout_ref, winp_ref, wo_ref):
    pieces = _w_in_pieces()
    for d in range(PROJ_COLS // LANES):
        parts = []
        for s0, d0, w in pieces:
            lo, hi = max(d0, d * LANES), min(d0 + w, (d + 1) * LANES)
            if lo < hi:
                parts.append((lo, wint_ref[s0 + lo - d0:s0 + hi - d0, :]))
        parts.sort(key=lambda t: t[0])
        assert sum(p.shape[0] for _, p in parts) == LANES
        blk = jnp.concatenate([p for _, p in parts], axis=0) if len(parts) > 1 else parts[0][1]
        winp_ref[:, d * LANES:(d + 1) * LANES] = blk.T.astype(winp_ref.dtype)
    wo_ref[...] = wout_ref[...].astype(wo_ref.dtype)


def _prep_small_kernel(wq_ref, wkv_ref, wqt_ref, wk_ref, wvt_ref):
    q_pieces = ([(hd * MLA_QK, hd * MLA_NOPE, MLA_NOPE) for hd in range(MLA_HEADS)]
                + [(hd * MLA_QK + MLA_NOPE, MLA_HEADS * MLA_NOPE + hd * MLA_ROPE, MLA_ROPE)
                   for hd in range(MLA_HEADS)])
    wq = jnp.concatenate(_gather_cols(wq_ref, q_pieces, MLA_HEADS * MLA_QK), axis=1)
    wqt_ref[...] = wq.T.astype(wqt_ref.dtype)
    per_head = MLA_NOPE + MLA_V
    for hd in range(MLA_HEADS):
        wk_ref[:, hd * MLA_NOPE:(hd + 1) * MLA_NOPE] = (
            wkv_ref[:, hd * per_head:hd * per_head + MLA_NOPE].astype(wk_ref.dtype))
        wvt_ref[hd * MLA_V:(hd + 1) * MLA_V, :] = (
            wkv_ref[:, hd * per_head + MLA_NOPE:(hd + 1) * per_head].T.astype(wvt_ref.dtype))


def _prep_weights(w_in, w_q_up, w_kv_up, w_out):
    bf = jnp.bfloat16
    d, n_in = w_in.shape
    rb = PREP_ROWS
    params = pltpu.CompilerParams(dimension_semantics=("arbitrary",),
                                  vmem_limit_bytes=V7X_VMEM_LIMIT_BYTES)
    w_in_p, w_o = pl.pallas_call(
        _prep_big_kernel,
        grid=(d // rb,),
        in_specs=[pl.BlockSpec((n_in, rb), lambda i: (0, i)),
                  pl.BlockSpec((rb, w_out.shape[1]), lambda i: (i, 0))],
        out_specs=[pl.BlockSpec((rb, PROJ_COLS), lambda i: (i, 0)),
                   pl.BlockSpec((rb, w_out.shape[1]), lambda i: (i, 0))],
        out_shape=[jax.ShapeDtypeStruct((d, PROJ_COLS), bf),
                   jax.ShapeDtypeStruct(w_out.shape, bf)],
        compiler_params=params,
        name="prep_big",
    )(w_in.T, w_out)
    w_qt, w_k_p, w_vt = pl.pallas_call(
        _prep_small_kernel,
        out_shape=[jax.ShapeDtypeStruct((MLA_HEADS * MLA_QK, MLA_Q_RANK), bf),
                   jax.ShapeDtypeStruct((MLA_KV_RANK, MLA_HEADS * MLA_NOPE), bf),
                   jax.ShapeDtypeStruct((MLA_WIDTH, MLA_KV_RANK), bf)],
        name="prep_small",
    )(w_q_up, w_kv_up)
    return w_in_p, w_qt, w_k_p, w_vt, w_o


def kernel(x, ln_mix, w_in, q_a_norm, w_q_up, kv_a_norm, w_kv_up, attn_sinks, w_out, final_norm):
    batch, seq, d = x.shape
    depth = ln_mix.shape[0]
    assert depth == 1, "final norm is fused into the single layer's output kernel"
    assert seq % MLA_BLOCK == 0 and seq % PROJ_ROWS == 0 and seq % SWA_OUT_ROWS == 0
    tabs_m = _rope_tables(seq, MLA_ROPE)
    tabs_s = _rope_tables(seq, SWA_D)
    x2 = x.reshape(batch * seq, d)
    tabs_t = _rope_tables_t(seq, MLA_ROPE)
    w_in_p, w_qt, w_k_p, w_vt, w_o = _prep_weights(w_in[0], w_q_up[0], w_kv_up[0], w_out[0])
    qt, km, vt, gm, qs, ks4, vs4, gs = _proj_call(
        x2, ln_mix[0].reshape(1, -1), w_in_p, q_a_norm[0].reshape(1, -1), w_qt,
        kv_a_norm[0].reshape(1, -1), w_k_p, w_vt, tabs_m, tabs_s, tabs_t, seq)
    mm = _mla_call(qt, km, vt, gm, batch, seq)
    out = _swa_out_call(attn_sinks[0], qs, ks4, vs4, gs, mm, x2, w_o,
                        final_norm.reshape(1, -1), batch, seq)
    return out.reshape(batch, seq, d)
```

```python
import functools
import math

import jax
import jax.numpy as jnp
from jax import lax
from jax.experimental import pallas as pl
from jax.experimental.pallas import tpu as pltpu

ROPE_THETA = 10000.0
NORM_EPS = 1e-6
NEG_INF = -1e30
LOG2E = 1.4426950408889634

MLA_HEADS = 4
MLA_NOPE = 128
MLA_ROPE = 64
MLA_V = 128
MLA_Q_RANK = 256
MLA_KV_RANK = 128
MLA_QK = MLA_NOPE + MLA_ROPE
MLA_WIDTH = MLA_HEADS * MLA_V

SWA_Q_HEADS = 8
SWA_KV_HEADS = 2
SWA_D = 64
SWA_WINDOW = 128
SWA_GROUP = SWA_Q_HEADS // SWA_KV_HEADS
SWA_WIDTH = SWA_Q_HEADS * SWA_D

LANES = 128
V7X_VMEM_LIMIT_BYTES = 56 * 1024 * 1024

PROJ_ROWS = 1024
MLA_BLOCK = 256
MLA_HEADS_PER_STEP = 2
SWA_OUT_ROWS = 1024
SWA_OUT_SUBTILE = 256
PREP_ROWS = 256


def _rope_tables(seq, dim):
    assert 2 * dim == LANES
    pos = jnp.arange(seq, dtype=jnp.float32)
    inv_freq = 1.0 / (ROPE_THETA ** (jnp.arange(0, dim, 2, dtype=jnp.float32) / dim))
    ang = pos[:, None] * inv_freq[None, :]
    cos, sin = jnp.cos(ang), jnp.sin(ang)
    return (jnp.concatenate([cos, cos, cos, cos], axis=1),
            jnp.concatenate([-sin, -sin, sin, sin], axis=1))


def _rope_tables_t(seq, dim):
    pos = jnp.arange(seq, dtype=jnp.float32)
    inv_freq = 1.0 / (ROPE_THETA ** (jnp.arange(0, dim, 2, dtype=jnp.float32) / dim))
    ang = pos[:, None] * inv_freq[None, :]
    return jnp.cos(ang).T, jnp.sin(ang).T


def _rope(x, c, s):
    return x * c + pltpu.roll(x, LANES // 2, 1) * s


def _pair_rope_layout(w, n_heads, dim):
    half = dim // 2
    w = w.reshape(w.shape[0], n_heads // 2, 2, 2, half)
    return w.transpose(0, 1, 3, 2, 4).reshape(w.shape[0], n_heads * dim)


def _rms(x, g):
    return x * lax.rsqrt(jnp.mean(x * x, axis=-1, keepdims=True) + NORM_EPS) * g


def _silu(g):
    return g / (1.0 + jnp.exp(-g))


def _dot(a, b):
    return jnp.dot(a, b, preferred_element_type=jnp.float32)


def _dot_nt(a, b):
    return lax.dot_general(a, b, (((1,), (1,)), ((), ())),
                           preferred_element_type=jnp.float32)


PROJ_COLS = 1792


def _proj_kernel(x_ref, ln_ref, win_ref, wqst_ref, qg_ref, wqt_ref, kvg_ref, wkv_ref, wvt_ref,
                 cm_ref, sm_ref, cs_ref, ss_ref, ct_ref, st_ref, cst_ref, sst_ref,
                 qt_ref, km_ref, vt_ref, gm_ref, qst_ref, ks_ref, vst_ref, gs_ref):
    bf = jnp.bfloat16
    x = x_ref[...]
    h = _rms(x, ln_ref[...]).astype(bf)

    cm, sm, cs, ss = cm_ref[...], sm_ref[...], cs_ref[...], ss_ref[...]
    lane = lax.broadcasted_iota(jnp.int32, (x.shape[0], LANES), 1)
    first = (lane % (LANES // 2)) < (LANES // 4)
    grp = 2 * LANES

    def cols(a):
        return win_ref[:, a:a + grp]

    c_q = _dot(h, cols(0))
    lat_b = _dot(h, cols(256))
    cqn = _rms(c_q, qg_ref[...]).astype(bf)
    ckvn = _rms(lat_b[:, 0:MLA_KV_RANK], kvg_ref[...]).astype(bf)
    kr = _rope(lat_b[:, LANES:2 * LANES], cm, sm)
    kr_first = jnp.where(first, kr, 0.0).astype(bf)

    s_scale = LOG2E / math.sqrt(SWA_D)
    qst = _dot_nt(wqst_ref[...], h)
    cst2 = jnp.concatenate([cst_ref[...], cst_ref[...]], axis=0)
    sst2 = jnp.concatenate([sst_ref[...], sst_ref[...]], axis=0)
    for ch in range(SWA_WIDTH // LANES):
        x1 = qst[ch * LANES:ch * LANES + SWA_D, :]
        x2 = qst[ch * LANES + SWA_D:(ch + 1) * LANES, :]
        qst_ref[ch * LANES:ch * LANES + SWA_D, :] = ((x1 * cst2 - x2 * sst2) * s_scale).astype(bf)
        qst_ref[ch * LANES + SWA_D:(ch + 1) * LANES, :] = ((x2 * cst2 + x1 * sst2) * s_scale).astype(bf)

    q_scale = LOG2E / math.sqrt(MLA_QK)
    nope_w = MLA_HEADS * MLA_NOPE
    half = MLA_ROPE // 2
    qt = _dot_nt(wqt_ref[...], cqn)
    ct, st = ct_ref[...], st_ref[...]
    zeros = jnp.zeros((half, x.shape[0]), bf)
    for hd in range(MLA_HEADS):
        qt_ref[hd * grp:hd * grp + LANES, :] = (qt[hd * MLA_NOPE:(hd + 1) * MLA_NOPE, :] * q_scale).astype(bf)
        x1 = qt[nope_w + hd * MLA_ROPE:nope_w + hd * MLA_ROPE + half, :]
        x2 = qt[nope_w + hd * MLA_ROPE + half:nope_w + (hd + 1) * MLA_ROPE, :]
        base = hd * grp + LANES
        qt_ref[base:base + half, :] = ((x1 * ct - x2 * st) * q_scale).astype(bf)
        qt_ref[base + half:base + 2 * half, :] = zeros
        qt_ref[base + 2 * half:base + 3 * half, :] = ((x2 * ct + x1 * st) * q_scale).astype(bf)
        qt_ref[base + 3 * half:base + 4 * half, :] = zeros
    for g2 in range(nope_w // grp):
        kn = _dot(ckvn, wkv_ref[:, g2 * grp:(g2 + 1) * grp])
        for j in range(2):
            hd = 2 * g2 + j
            km_ref[:, hd * grp:hd * grp + LANES] = kn[:, j * LANES:(j + 1) * LANES].astype(bf)
            km_ref[:, hd * grp + LANES:(hd + 1) * grp] = kr_first
    for g2 in range(MLA_WIDTH // grp):
        gm_ref[:, g2 * grp:(g2 + 1) * grp] = _silu(_dot(h, cols(512 + g2 * grp))).astype(bf)

    kv_s = _dot(h, cols(1024))
    k01 = _rope(kv_s[:, 0:LANES], cs, ss)
    ks_ref[:, 0 * LANES:1 * LANES] = jnp.where(first, k01, 0.0).astype(bf)
    ks_ref[:, 1 * LANES:2 * LANES] = jnp.where(first, 0.0, pltpu.roll(k01, LANES // 4, 1)).astype(bf)
    ks_ref[:, 2 * LANES:3 * LANES] = jnp.where(first, pltpu.roll(k01, 3 * LANES // 4, 1), 0.0).astype(bf)
    ks_ref[:, 3 * LANES:4 * LANES] = jnp.where(first, 0.0, k01).astype(bf)
    vst_ref[...] = kv_s[:, LANES:2 * LANES].T.astype(bf)

    for g2 in range(SWA_WIDTH // grp):
        gs_ref[:, g2 * grp:(g2 + 1) * grp] = _silu(_dot(h, cols(1280 + g2 * grp))).astype(bf)

    vt_ref[...] = _dot_nt(wvt_ref[...], ckvn).astype(bf)


def _proj_call(x2, ln_g, w_in_p, w_qst, q_g, w_qt, kv_g, w_k_p, w_vt, tabs_m, tabs_s, tabs_t,
               tabs_st, seq):
    n, d = x2.shape
    tm = PROJ_ROWS
    steps_per_seq = seq // tm
    batch = n // seq
    row = lambda i: (i, 0)
    const = lambda i: (0, 0)
    tab = lambda i: (i % steps_per_seq, 0)
    tab_t = lambda i: (0, i % steps_per_seq)
    col = lambda i: (i // steps_per_seq, i % steps_per_seq)
    bf = jnp.bfloat16
    qt_rows = MLA_HEADS * 2 * LANES
    row_out = lambda w: (pl.BlockSpec((tm, w), row), jax.ShapeDtypeStruct((n, w), bf))
    col_out = lambda r: (pl.BlockSpec((r, tm), col), jax.ShapeDtypeStruct((batch * r, seq), bf))
    outs = [col_out(qt_rows), row_out(qt_rows), col_out(MLA_WIDTH), row_out(MLA_WIDTH),
            col_out(SWA_WIDTH), row_out(SWA_WIDTH), col_out(SWA_KV_HEADS * SWA_D), row_out(SWA_WIDTH)]
    return pl.pallas_call(
        _proj_kernel,
        grid=(n // tm,),
        in_specs=[
            pl.BlockSpec((tm, d), row),
            pl.BlockSpec((1, d), const),
            pl.BlockSpec(w_in_p.shape, const),
            pl.BlockSpec(w_qst.shape, const),
            pl.BlockSpec((1, MLA_Q_RANK), const),
            pl.BlockSpec(w_qt.shape, const),
            pl.BlockSpec((1, MLA_KV_RANK), const),
            pl.BlockSpec(w_k_p.shape, const),
            pl.BlockSpec(w_vt.shape, const),
        ] + [pl.BlockSpec((tm, LANES), tab)] * 4
          + [pl.BlockSpec((MLA_ROPE // 2, tm), tab_t)] * 2
          + [pl.BlockSpec((SWA_D // 2, tm), tab_t)] * 2,
        out_specs=[o[0] for o in outs],
        out_shape=[o[1] for o in outs],
        compiler_params=pltpu.CompilerParams(
            dimension_semantics=("arbitrary",),
            vmem_limit_bytes=V7X_VMEM_LIMIT_BYTES),
        name="proj",
    )(x2, ln_g, w_in_p, w_qst, q_g, w_qt, kv_g, w_k_p, w_vt, *tabs_m, *tabs_s, *tabs_t, *tabs_st)


def _tree(op, xs):
    xs = list(xs)
    while len(xs) > 1:
        xs = [op(xs[a], xs[a + 1]) if a + 1 < len(xs) else xs[a] for a in range(0, len(xs), 2)]
    return xs[0]


def _mla_kernel(qt_ref, k_ref, vt_ref, g_ref, o_ref, s_ref, p_ref, vx_ref):
    t = MLA_BLOCK
    seq = k_ref.shape[0]
    sub = 8
    qk = 2 * LANES
    r = lax.broadcasted_iota(jnp.int32, (t, t), 0)
    c = lax.broadcasted_iota(jnp.int32, (t, t), 1)
    causal = r <= c
    nblk = seq // t
    units = [(hd, i) for hd in range(MLA_HEADS_PER_STEP) for i in range(nblk - 1, -1, -1)]
    state = [dict(m8=None) for _ in units]

    for hd in range(MLA_HEADS_PER_STEP):
        vx_ref[hd, 0:MLA_V, :] = vt_ref[hd * MLA_V:(hd + 1) * MLA_V, :]
        vx_ref[hd, MLA_V:, :] = jnp.ones((vx_ref.shape[1] - MLA_V, seq), vx_ref.dtype)

    def score_tile(u, j):
        hd, i = units[u]
        st = state[u]
        keys = slice(j * t, (j + 1) * t)
        s = _dot(k_ref[keys, hd * qk:(hd + 1) * qk],
                 qt_ref[hd * qk:(hd + 1) * qk, i * t:(i + 1) * t])
        if j == i:
            s = jnp.where(causal, s, NEG_INF)
        s_ref[u % 2, keys, :] = s
        m8 = _tree(jnp.maximum, [s[a * sub:(a + 1) * sub, :] for a in range(t // sub)])
        st["m8"] = m8 if st["m8"] is None else jnp.maximum(st["m8"], m8)

    def prob_tile(u, j):
        st = state[u]
        if "m" not in st:
            st["m"] = jnp.max(st["m8"], axis=0, keepdims=True)
        keys = slice(j * t, (j + 1) * t)
        p_ref[u % 3, keys, :] = jnp.exp2(s_ref[u % 2, keys, :] - st["m"]).astype(jnp.bfloat16)

    def finish(u):
        hd, i = units[u]
        rows = slice(i * t, (i + 1) * t)
        kv = (i + 1) * t
        acc = _dot(vx_ref[hd, :, 0:kv], p_ref[u % 3, 0:kv, :])
        out_t = acc[0:MLA_V, :] / acc[MLA_V:MLA_V + 1, :]
        gate = g_ref[rows, hd * MLA_V:(hd + 1) * MLA_V].astype(jnp.float32)
        o_ref[rows, hd * MLA_V:(hd + 1) * MLA_V] = (out_t.T * gate).astype(o_ref.dtype)

    for j in range(units[0][1] + 1):
        score_tile(0, j)
    for u, (_, i) in enumerate(units):
        nxt = ([functools.partial(score_tile, u + 1, j) for j in range(units[u + 1][1] + 1)]
               if u + 1 < len(units) else [])
        cur = [functools.partial(prob_tile, u, j) for j in range(i + 1)]
        prev = [functools.partial(finish, u - 1)] if u > 0 else []
        while nxt or cur or prev:
            if nxt:
                nxt.pop(0)()
            if cur:
                cur.pop(0)()
            if prev:
                prev.pop(0)()
    finish(len(units) - 1)


def _mla_call(qt, km, vt, gm, batch, seq):
    n = km.shape[0]
    t = MLA_BLOCK
    hps = MLA_HEADS_PER_STEP
    groups = MLA_HEADS // hps
    ones_rows = 16
    feat = lambda b, h: (b * groups + h, 0)
    tok = lambda b, h: (b, h)
    return pl.pallas_call(
        _mla_kernel,
        grid=(batch, groups),
        in_specs=[
            pl.BlockSpec((hps * 2 * LANES, seq), feat),
            pl.BlockSpec((seq, hps * 2 * LANES), tok),
            pl.BlockSpec((hps * MLA_V, seq), feat),
            pl.BlockSpec((seq, hps * MLA_V), tok),
        ],
        out_specs=pl.BlockSpec((seq, hps * MLA_V), tok),
        out_shape=jax.ShapeDtypeStruct((n, MLA_WIDTH), jnp.bfloat16),
        scratch_shapes=[pltpu.VMEM((2, seq, t), jnp.float32),
                        pltpu.VMEM((3, seq, t), jnp.bfloat16),
                        pltpu.VMEM((hps, MLA_V + ones_rows, seq), jnp.bfloat16)],
        compiler_params=pltpu.CompilerParams(
            dimension_semantics=("arbitrary", "arbitrary"),
            vmem_limit_bytes=V7X_VMEM_LIMIT_BYTES),
        name="mla",
    )(qt, km, vt, gm)


def _swa_out_kernel(sink_ref, qt_ref, k_ref, kh_ref, vt_ref, vth_ref, g_ref, mm_ref, x_ref,
                    w_ref, fg_ref, o_ref, ms_ref, kx_ref, vx_ref):
    w = SWA_WINDOW
    chunk = pl.program_id(1)
    rows_total = k_ref.shape[0]
    bf = jnp.bfloat16
    sub = 8

    kx_ref[0:w, :] = kh_ref[...]
    kx_ref[w:, :] = k_ref[...]
    for hk in range(SWA_KV_HEADS):
        vx_ref[hk, 0:SWA_D, 0:w] = vth_ref[hk * SWA_D:(hk + 1) * SWA_D, :]
        vx_ref[hk, 0:SWA_D, w:] = vt_ref[hk * SWA_D:(hk + 1) * SWA_D, :]
        vx_ref[hk, SWA_D:, :] = jnp.ones((vx_ref.shape[1] - SWA_D, vx_ref.shape[2]), bf)

    ki = lax.broadcasted_iota(jnp.int32, (2 * w, 2 * w), 0)
    qi = lax.broadcasted_iota(jnp.int32, (2 * w, 2 * w), 1) % w
    rel = qi + w - ki
    band = (rel >= 0) & (rel < SWA_WINDOW)
    band_first = band & ((ki >= w) | (chunk > 0))
    lane = lax.broadcasted_iota(jnp.int32, (1, 2 * w), 1)

    units = [(n, hk) for n in range(rows_total // w) for hk in range(SWA_KV_HEADS)]
    state = [dict() for _ in units]

    def sink_rows(hk):
        sink = [sink_ref[hk * SWA_GROUP + i] * LOG2E for i in range(SWA_GROUP)]
        return (jnp.where(lane < w, sink[0], sink[2]), jnp.where(lane < w, sink[1], sink[3]))

    def stage_scores(u):
        n, hk = units[u]
        base = hk * 2 * LANES
        qcols = slice(n * w, (n + 1) * w)
        keys = slice(n * w, (n + 2) * w)
        qt = jnp.concatenate([qt_ref[base:base + LANES, qcols],
                              qt_ref[base + LANES:base + 2 * LANES, qcols]], axis=1)
        valid = band_first if n == 0 else band
        sinks = sink_rows(hk)
        st = state[u]
        st["s"], st["m"] = [], []
        for half in range(2):
            lanes = slice(base + half * LANES, base + (half + 1) * LANES)
            s = jnp.where(valid, _dot(kx_ref[keys, lanes], qt), NEG_INF)
            m8 = _tree(jnp.maximum, [s[a * sub:(a + 1) * sub, :] for a in range(2 * w // sub)])
            st["s"].append(s)
            st["m"].append(jnp.maximum(jnp.max(m8, axis=0, keepdims=True), sinks[half]))

    def stage_values(u):
        n, hk = units[u]
        keys = slice(n * w, (n + 2) * w)
        st = state[u]
        st["acc"] = []
        for half in range(2):
            p = jnp.exp2(st["s"][half] - st["m"][half]).astype(bf)
            st["acc"].append(_dot(vx_ref[hk, :, keys], p))
        del st["s"]

    def stage_store(u):
        n, hk = units[u]
        base = hk * 2 * LANES
        qcols = slice(n * w, (n + 1) * w)
        sinks = sink_rows(hk)
        st = state[u]
        outs = []
        for half in range(2):
            acc = st["acc"][half]
            den = acc[SWA_D:SWA_D + 1, :] + jnp.exp2(sinks[half] - st["m"][half])
            outs.append(acc[0:SWA_D, :] / den)
        g = g_ref[qcols, base:base + 2 * LANES].astype(jnp.float32)
        for ch in range(2):
            o_t = jnp.concatenate([outs[0][:, ch * w:(ch + 1) * w],
                                   outs[1][:, ch * w:(ch + 1) * w]], axis=0)
            ms_ref[qcols, base + ch * LANES:base + (ch + 1) * LANES] = (
                o_t.T * g[:, ch * LANES:(ch + 1) * LANES]).astype(bf)
        state[u] = None

    t = SWA_OUT_SUBTILE
    units_per_tile = (t // w) * SWA_KV_HEADS

    d_model = x_ref.shape[1]
    grp = 2 * LANES
    out_state = {}

    def out_group(i, gc):
        rows = slice(i * t, (i + 1) * t)
        cols = slice(gc * grp, (gc + 1) * grp)
        y = (x_ref[rows, cols] + _dot(mm_ref[rows, :], w_ref[0:MLA_WIDTH, cols])
             + _dot(ms_ref[rows, :], w_ref[MLA_WIDTH:, cols]))
        out_state.setdefault(i, []).append(y)

    def out_norm(i):
        rows = slice(i * t, (i + 1) * t)
        ys = out_state.pop(i)
        ssq = _tree(jnp.add, [jnp.sum(y * y, axis=-1, keepdims=True) for y in ys])
        scale = lax.rsqrt(ssq / d_model + NORM_EPS)
        for gc, y in enumerate(ys):
            cols = slice(gc * grp, (gc + 1) * grp)
            o_ref[rows, cols] = y * scale * fg_ref[:, cols]

    nu = len(units)
    pending = []
    for step in range(nu + 2):
        if step < nu:
            stage_scores(step)
        if pending:
            pending.pop(0)()
        if 0 <= step - 1 < nu:
            stage_values(step - 1)
        if pending:
            pending.pop(0)()
        if 0 <= step - 2 < nu:
            stage_store(step - 2)
            if (step - 2 + 1) % units_per_tile == 0:
                i = (step - 2) // units_per_tile
                pending += [functools.partial(out_group, i, gc) for gc in range(d_model // grp)]
                pending.append(functools.partial(out_norm, i))
    for task in pending:
        task()


def _swa_out_call(sinks, qst, ks4, vst, gs, mm, x2, w_o, fg, batch, seq):
    n, d = x2.shape
    r = SWA_OUT_ROWS
    w = SWA_WINDOW
    cps = seq // r
    bpc = r // w
    bps = seq // w
    v_rows = SWA_KV_HEADS * SWA_D
    ones_rows = 16
    row = lambda b, c: (b * cps + c, 0)
    halo = lambda b, c: (b * bps + jnp.maximum(c * bpc - 1, 0), 0)
    feat = lambda b, c: (b, c)
    feat_halo = lambda b, c: (b, jnp.maximum(c * bpc - 1, 0))
    const = lambda b, c: (0, 0)
    wide = pl.BlockSpec((r, SWA_WIDTH), row)
    return pl.pallas_call(
        _swa_out_kernel,
        grid=(batch, cps),
        in_specs=[
            pl.BlockSpec(memory_space=pltpu.SMEM),
            pl.BlockSpec((SWA_WIDTH, r), feat),
            wide,
            pl.BlockSpec((w, SWA_WIDTH), halo),
            pl.BlockSpec((v_rows, r), feat),
            pl.BlockSpec((v_rows, w), feat_halo),
            wide,
            pl.BlockSpec((r, MLA_WIDTH), row),
            pl.BlockSpec((r, d), row),
            pl.BlockSpec(w_o.shape, const),
            pl.BlockSpec((1, d), const),
        ],
        out_specs=pl.BlockSpec((r, d), row),
        out_shape=jax.ShapeDtypeStruct((n, d), jnp.float32),
        scratch_shapes=[pltpu.VMEM((r, SWA_WIDTH), jnp.bfloat16),
                        pltpu.VMEM((r + w, SWA_WIDTH), jnp.bfloat16),
                        pltpu.VMEM((SWA_KV_HEADS, SWA_D + ones_rows, r + w), jnp.bfloat16)],
        compiler_params=pltpu.CompilerParams(
            dimension_semantics=("arbitrary", "arbitrary"),
            vmem_limit_bytes=V7X_VMEM_LIMIT_BYTES),
        name="swa_out",
    )(sinks, qst, ks4, ks4, vst, vst, gs, mm, x2, w_o, fg)


def _pair_rope_pieces(src, dst, n_heads, dim):
    half = dim // 2
    out = []
    for pair in range(n_heads // 2):
        for which in range(2):
            head = src + (2 * pair + which) * dim
            chunk = dst + pair * 2 * dim
            out.append((head, chunk + which * half, half))
            out.append((head + half, chunk + 2 * half + which * half, half))
    return out


def _w_in_pieces():
    pieces, qs_pieces, src, dst = [], [], 0, 0
    for name, width in (("c_q", MLA_Q_RANK), ("c_kv", MLA_KV_RANK), ("k_rope", MLA_ROPE),
                        ("g_mla", MLA_WIDTH), ("q_s", SWA_WIDTH),
                        ("k_s", SWA_KV_HEADS * SWA_D), ("v_s", SWA_KV_HEADS * SWA_D),
                        ("g_swa", SWA_WIDTH)):
        if name == "k_rope":
            half = MLA_ROPE // 2
            pieces += [(src, dst, half), (src, dst + half, half),
                       (src + half, dst + 2 * half, half), (src + half, dst + 3 * half, half)]
            dst += 2 * MLA_ROPE
        elif name == "q_s":
            qs_pieces += _pair_rope_pieces(src, 0, SWA_Q_HEADS, SWA_D)
        elif name == "k_s":
            pieces += _pair_rope_pieces(src, dst, SWA_KV_HEADS, SWA_D)
            dst += width
        else:
            pieces.append((src, dst, width))
            dst += width
        src += width
    assert dst == PROJ_COLS
    return pieces, qs_pieces


def _gather_cols(src_ref, pieces, n_dst_cols):
    rows, n_src = src_ref.shape
    lane = lax.broadcasted_iota(jnp.int32, (rows, LANES), 1)
    loaded, rolled = {}, {}

    def load(k):
        if k not in loaded:
            width = min(LANES, n_src - k * LANES)
            v = src_ref[:, k * LANES:k * LANES + width]
            if width < LANES:
                v = jnp.concatenate([v, jnp.zeros((rows, LANES - width), v.dtype)], axis=1)
            loaded[k] = v
        return loaded[k]

    def shifted(k, shift):
        if shift == 0:
            return load(k)
        if (k, shift) not in rolled:
            rolled[(k, shift)] = pltpu.roll(load(k), shift, 1)
        return rolled[(k, shift)]

    chunks = []
    for d in range(n_dst_cols // LANES):
        acc = None
        for s0, d0, w in pieces:
            lo, hi = max(d0, d * LANES), min(d0 + w, (d + 1) * LANES)
            while lo < hi:
                s_abs = s0 + lo - d0
                k, a = divmod(s_abs, LANES)
                n = min(hi - lo, LANES - a)
                a_dst = lo - d * LANES
                val = shifted(k, (a_dst - a) % LANES)
                if n == LANES:
                    acc = val
                else:
                    mask = (lane >= a_dst) & (lane < a_dst + n)
                    acc = jnp.where(mask, val, 0.0 if acc is None else acc)
                lo += n
        chunks.append(acc)
    return chunks


def _gather_rows(src_ref, pieces, d):
    parts = []
    for s0, d0, w in pieces:
        lo, hi = max(d0, d * LANES), min(d0 + w, (d + 1) * LANES)
        if lo < hi:
            parts.append((lo, src_ref[s0 + lo - d0:s0 + hi - d0, :]))
    parts.sort(key=lambda t: t[0])
    assert sum(p.shape[0] for _, p in parts) == LANES
    return jnp.concatenate([p for _, p in parts], axis=0) if len(parts) > 1 else parts[0][1]


def _prep_big_kernel(wint_ref, wout_ref, winp_ref, wqst_ref, wo_ref):
    pieces, qs_pieces = _w_in_pieces()
    for d in range(PROJ_COLS // LANES):
        winp_ref[:, d * LANES:(d + 1) * LANES] = _gather_rows(wint_ref, pieces, d).T.astype(winp_ref.dtype)
    for d in range(SWA_WIDTH // LANES):
        wqst_ref[d * LANES:(d + 1) * LANES, :] = _gather_rows(wint_ref, qs_pieces, d).astype(wqst_ref.dtype)
    wo_ref[...] = wout_ref[...].astype(wo_ref.dtype)


def _prep_small_kernel(wq_ref, wkv_ref, wqt_ref, wk_ref, wvt_ref):
    q_pieces = ([(hd * MLA_QK, hd * MLA_NOPE, MLA_NOPE) for hd in range(MLA_HEADS)]
                + [(hd * MLA_QK + MLA_NOPE, MLA_HEADS * MLA_NOPE + hd * MLA_ROPE, MLA_ROPE)
                   for hd in range(MLA_HEADS)])
    wq = jnp.concatenate(_gather_cols(wq_ref, q_pieces, MLA_HEADS * MLA_QK), axis=1)
    wqt_ref[...] = wq.T.astype(wqt_ref.dtype)
    per_head = MLA_NOPE + MLA_V
    for hd in range(MLA_HEADS):
        wk_ref[:, hd * MLA_NOPE:(hd + 1) * MLA_NOPE] = (
            wkv_ref[:, hd * per_head:hd * per_head + MLA_NOPE].astype(wk_ref.dtype))
        wvt_ref[hd * MLA_V:(hd + 1) * MLA_V, :] = (
            wkv_ref[:, hd * per_head + MLA_NOPE:(hd + 1) * per_head].T.astype(wvt_ref.dtype))


def _prep_weights(w_in, w_q_up, w_kv_up, w_out):
    bf = jnp.bfloat16
    d, n_in = w_in.shape
    rb = PREP_ROWS
    params = pltpu.CompilerParams(dimension_semantics=("arbitrary",),
                                  vmem_limit_bytes=V7X_VMEM_LIMIT_BYTES)
    w_in_p, w_qst, w_o = pl.pallas_call(
        _prep_big_kernel,
        grid=(d // rb,),
        in_specs=[pl.BlockSpec((n_in, rb), lambda i: (0, i)),
                  pl.BlockSpec((rb, w_out.shape[1]), lambda i: (i, 0))],
        out_specs=[pl.BlockSpec((rb, PROJ_COLS), lambda i: (i, 0)),
                   pl.BlockSpec((SWA_WIDTH, rb), lambda i: (0, i)),
                   pl.BlockSpec((rb, w_out.shape[1]), lambda i: (i, 0))],
        out_shape=[jax.ShapeDtypeStruct((d, PROJ_COLS), bf),
                   jax.ShapeDtypeStruct((SWA_WIDTH, d), bf),
                   jax.ShapeDtypeStruct(w_out.shape, bf)],
        compiler_params=params,
        name="prep_big",
    )(w_in.T, w_out)
    w_qt, w_k_p, w_vt = pl.pallas_call(
        _prep_small_kernel,
        out_shape=[jax.ShapeDtypeStruct((MLA_HEADS * MLA_QK, MLA_Q_RANK), bf),
                   jax.ShapeDtypeStruct((MLA_KV_RANK, MLA_HEADS * MLA_NOPE), bf),
                   jax.ShapeDtypeStruct((MLA_WIDTH, MLA_KV_RANK), bf)],
        name="prep_small",
    )(w_q_up, w_kv_up)
    return w_in_p, w_qst, w_qt, w_k_p, w_vt, w_o


def kernel(x, ln_mix, w_in, q_a_norm, w_q_up, kv_a_norm, w_kv_up, attn_sinks, w_out, final_norm):
    batch, seq, d = x.shape
    depth = ln_mix.shape[0]
    assert depth == 1, "final norm is fused into the single layer's output kernel"
    assert seq % MLA_BLOCK == 0 and seq % PROJ_ROWS == 0 and seq % SWA_OUT_ROWS == 0
    tabs_m = _rope_tables(seq, MLA_ROPE)
    tabs_s = _rope_tables(seq, SWA_D)
    x2 = x.reshape(batch * seq, d)
    tabs_t = _rope_tables_t(seq, MLA_ROPE)
    tabs_st = _rope_tables_t(seq, SWA_D)
    w_in_p, w_qst, w_qt, w_k_p, w_vt, w_o = _prep_weights(w_in[0], w_q_up[0], w_kv_up[0], w_out[0])
    qt, km, vt, gm, qst, ks4, vst, gs = _proj_call(
        x2, ln_mix[0].reshape(1, -1), w_in_p, w_qst, q_a_norm[0].reshape(1, -1), w_qt,
        kv_a_norm[0].reshape(1, -1), w_k_p, w_vt, tabs_m, tabs_s, tabs_t, tabs_st, seq)
    mm = _mla_call(qt, km, vt, gm, batch, seq)
    out = _swa_out_call(attn_sinks[0], qst, ks4, vst, gs, mm, x2, w_o,
                        final_norm.reshape(1, -1), batch, seq)
    return out.reshape(batch, seq, d)
```

```python
import functools
import math

import jax
import jax.numpy as jnp
from jax import lax
from jax.experimental import pallas as pl
from jax.experimental.pallas import tpu as pltpu

ROPE_THETA = 10000.0
NORM_EPS = 1e-6
NEG_INF = -1e30
LOG2E = 1.4426950408889634

MLA_HEADS = 4
MLA_NOPE = 128
MLA_ROPE = 64
MLA_V = 128
MLA_Q_RANK = 256
MLA_KV_RANK = 128
MLA_QK = MLA_NOPE + MLA_ROPE
MLA_WIDTH = MLA_HEADS * MLA_V

SWA_Q_HEADS = 8
SWA_KV_HEADS = 2
SWA_D = 64
SWA_WINDOW = 128
SWA_GROUP = SWA_Q_HEADS // SWA_KV_HEADS
SWA_WIDTH = SWA_Q_HEADS * SWA_D

LANES = 128
V7X_VMEM_LIMIT_BYTES = 56 * 1024 * 1024

PROJ_ROWS = 1024
MLA_BLOCK = 256
MLA_HEADS_PER_STEP = 2
SWA_OUT_ROWS = 1024
SWA_OUT_SUBTILE = 256
PREP_ROWS = 256


def _rope_tables(seq, dim):
    assert 2 * dim == LANES
    half = dim // 2
    inv_freq = 1.0 / (ROPE_THETA ** (jnp.arange(0, dim, 2, dtype=jnp.float32) / dim))
    f32 = jnp.float32
    ct, st, c_lane, s_lane = pl.pallas_call(
        _rope_table_kernel,
        out_shape=[jax.ShapeDtypeStruct((half, seq), f32), jax.ShapeDtypeStruct((half, seq), f32),
                   jax.ShapeDtypeStruct((seq, LANES), f32), jax.ShapeDtypeStruct((seq, LANES), f32)],
        name="rope_tables",
    )(inv_freq.reshape(half, 1))
    return (ct, st), (c_lane, s_lane)


def _rope_table_kernel(inv_ref, ct_ref, st_ref, cl_ref, sl_ref):
    half, seq = ct_ref.shape
    pos = lax.broadcasted_iota(jnp.int32, (half, seq), 1).astype(jnp.float32)
    ang = pos * inv_ref[...]
    c, s = jnp.cos(ang), jnp.sin(ang)
    ct_ref[...] = c
    st_ref[...] = s
    cl_ref[...] = jnp.concatenate([c, c, c, c], axis=0).T
    sl_ref[...] = jnp.concatenate([-s, -s, s, s], axis=0).T


def _rope(x, c, s):
    return x * c + pltpu.roll(x, LANES // 2, 1) * s


def _rms(x, g):
    return x * lax.rsqrt(jnp.mean(x * x, axis=-1, keepdims=True) + NORM_EPS) * g


def _silu(g):
    return g / (1.0 + jnp.exp(-g))


def _dot(a, b):
    return jnp.dot(a, b, preferred_element_type=jnp.float32)


def _dot_nt(a, b):
    return lax.dot_general(a, b, (((1,), (1,)), ((), ())),
                           preferred_element_type=jnp.float32)


PROJ_COLS = 1792


def _proj_kernel(x_ref, ln_ref, win_ref, wqst_ref, qg_ref, wqt_ref, kvg_ref, wkv_ref, wvt_ref,
                 cm_ref, sm_ref, cs_ref, ss_ref, ct_ref, st_ref, cst_ref, sst_ref,
                 qt_ref, km_ref, vt_ref, gm_ref, qst_ref, ks_ref, vst_ref, gs_ref):
    bf = jnp.bfloat16
    x = x_ref[...]
    h = _rms(x, ln_ref[...]).astype(bf)

    cm, sm, cs, ss = cm_ref[...], sm_ref[...], cs_ref[...], ss_ref[...]
    lane = lax.broadcasted_iota(jnp.int32, (x.shape[0], LANES), 1)
    first = (lane % (LANES // 2)) < (LANES // 4)
    grp = 2 * LANES

    def cols(a):
        return win_ref[:, a:a + grp]

    c_q = _dot(h, cols(0))
    lat_b = _dot(h, cols(256))
    cqn = _rms(c_q, qg_ref[...]).astype(bf)
    ckvn = _rms(lat_b[:, 0:MLA_KV_RANK], kvg_ref[...]).astype(bf)
    kr = _rope(lat_b[:, LANES:2 * LANES], cm, sm)
    kr_first = jnp.where(first, kr, 0.0).astype(bf)

    s_scale = LOG2E / math.sqrt(SWA_D)
    qst = _dot_nt(wqst_ref[...], h)
    cst2 = jnp.concatenate([cst_ref[...], cst_ref[...]], axis=0)
    sst2 = jnp.concatenate([sst_ref[...], sst_ref[...]], axis=0)
    for ch in range(SWA_WIDTH // LANES):
        x1 = qst[ch * LANES:ch * LANES + SWA_D, :]
        x2 = qst[ch * LANES + SWA_D:(ch + 1) * LANES, :]
        qst_ref[ch * LANES:ch * LANES + SWA_D, :] = ((x1 * cst2 - x2 * sst2) * s_scale).astype(bf)
        qst_ref[ch * LANES + SWA_D:(ch + 1) * LANES, :] = ((x2 * cst2 + x1 * sst2) * s_scale).astype(bf)

    q_scale = LOG2E / math.sqrt(MLA_QK)
    nope_w = MLA_HEADS * MLA_NOPE
    half = MLA_ROPE // 2
    qt = _dot_nt(wqt_ref[...], cqn)
    ct, st = ct_ref[...], st_ref[...]
    zeros = jnp.zeros((half, x.shape[0]), bf)
    for hd in range(MLA_HEADS):
        qt_ref[hd * grp:hd * grp + LANES, :] = (qt[hd * MLA_NOPE:(hd + 1) * MLA_NOPE, :] * q_scale).astype(bf)
        x1 = qt[nope_w + hd * MLA_ROPE:nope_w + hd * MLA_ROPE + half, :]
        x2 = qt[nope_w + hd * MLA_ROPE + half:nope_w + (hd + 1) * MLA_ROPE, :]
        base = hd * grp + LANES
        qt_ref[base:base + half, :] = ((x1 * ct - x2 * st) * q_scale).astype(bf)
        qt_ref[base + half:base + 2 * half, :] = zeros
        qt_ref[base + 2 * half:base + 3 * half, :] = ((x2 * ct + x1 * st) * q_scale).astype(bf)
        qt_ref[base + 3 * half:base + 4 * half, :] = zeros
    for g2 in range(nope_w // grp):
        kn = _dot(ckvn, wkv_ref[:, g2 * grp:(g2 + 1) * grp])
        for j in range(2):
            hd = 2 * g2 + j
            km_ref[:, hd * grp:hd * grp + LANES] = kn[:, j * LANES:(j + 1) * LANES].astype(bf)
            km_ref[:, hd * grp + LANES:(hd + 1) * grp] = kr_first
    for g2 in range(MLA_WIDTH // grp):
        gm_ref[:, g2 * grp:(g2 + 1) * grp] = _silu(_dot(h, cols(512 + g2 * grp))).astype(bf)

    kv_s = _dot(h, cols(1024))
    k01 = _rope(kv_s[:, 0:LANES], cs, ss)
    ks_ref[:, 0 * LANES:1 * LANES] = jnp.where(first, k01, 0.0).astype(bf)
    ks_ref[:, 1 * LANES:2 * LANES] = jnp.where(first, 0.0, pltpu.roll(k01, LANES // 4, 1)).astype(bf)
    ks_ref[:, 2 * LANES:3 * LANES] = jnp.where(first, pltpu.roll(k01, 3 * LANES // 4, 1), 0.0).astype(bf)
    ks_ref[:, 3 * LANES:4 * LANES] = jnp.where(first, 0.0, k01).astype(bf)
    vst_ref[...] = kv_s[:, LANES:2 * LANES].T.astype(bf)

    for g2 in range(SWA_WIDTH // grp):
        gs_ref[:, g2 * grp:(g2 + 1) * grp] = _silu(_dot(h, cols(1280 + g2 * grp))).astype(bf)

    vt_ref[...] = _dot_nt(wvt_ref[...], ckvn).astype(bf)


def _proj_call(x2, ln_g, w_in_p, w_qst, q_g, w_qt, kv_g, w_k_p, w_vt, tabs_m, tabs_s, tabs_t,
               tabs_st, seq):
    n, d = x2.shape
    tm = PROJ_ROWS
    steps_per_seq = seq // tm
    batch = n // seq
    row = lambda i: (i, 0)
    const = lambda i: (0, 0)
    tab = lambda i: (i % steps_per_seq, 0)
    tab_t = lambda i: (0, i % steps_per_seq)
    col = lambda i: (i // steps_per_seq, i % steps_per_seq)
    bf = jnp.bfloat16
    qt_rows = MLA_HEADS * 2 * LANES
    row_out = lambda w: (pl.BlockSpec((tm, w), row), jax.ShapeDtypeStruct((n, w), bf))
    col_out = lambda r: (pl.BlockSpec((r, tm), col), jax.ShapeDtypeStruct((batch * r, seq), bf))
    outs = [col_out(qt_rows), row_out(qt_rows), col_out(MLA_WIDTH), row_out(MLA_WIDTH),
            col_out(SWA_WIDTH), row_out(SWA_WIDTH), col_out(SWA_KV_HEADS * SWA_D), row_out(SWA_WIDTH)]
    return pl.pallas_call(
        _proj_kernel,
        grid=(n // tm,),
        in_specs=[
            pl.BlockSpec((tm, d), row),
            pl.BlockSpec((1, d), const),
            pl.BlockSpec(w_in_p.shape, const),
            pl.BlockSpec(w_qst.shape, const),
            pl.BlockSpec((1, MLA_Q_RANK), const),
            pl.BlockSpec(w_qt.shape, const),
            pl.BlockSpec((1, MLA_KV_RANK), const),
            pl.BlockSpec(w_k_p.shape, const),
            pl.BlockSpec(w_vt.shape, const),
        ] + [pl.BlockSpec((tm, LANES), tab)] * 4
          + [pl.BlockSpec((MLA_ROPE // 2, tm), tab_t)] * 2
          + [pl.BlockSpec((SWA_D // 2, tm), tab_t)] * 2,
        out_specs=[o[0] for o in outs],
        out_shape=[o[1] for o in outs],
        compiler_params=pltpu.CompilerParams(
            dimension_semantics=("arbitrary",),
            vmem_limit_bytes=V7X_VMEM_LIMIT_BYTES),
        name="proj",
    )(x2, ln_g, w_in_p, w_qst, q_g, w_qt, kv_g, w_k_p, w_vt, *tabs_m, *tabs_s, *tabs_t, *tabs_st)


def _tree(op, xs):
    xs = list(xs)
    while len(xs) > 1:
        xs = [op(xs[a], xs[a + 1]) if a + 1 < len(xs) else xs[a] for a in range(0, len(xs), 2)]
    return xs[0]


def _mla_kernel(qt_ref, k_ref, vt_ref, g_ref, o_ref, s_ref, p_ref, vx_ref):
    t = MLA_BLOCK
    seq = k_ref.shape[0]
    sub = 8
    qk = 2 * LANES
    r = lax.broadcasted_iota(jnp.int32, (t, t), 0)
    c = lax.broadcasted_iota(jnp.int32, (t, t), 1)
    causal = r <= c
    nblk = seq // t
    units = [(hd, i) for hd in range(MLA_HEADS_PER_STEP) for i in range(nblk - 1, -1, -1)]
    state = [dict(m8=None) for _ in units]

    for hd in range(MLA_HEADS_PER_STEP):
        vx_ref[hd, 0:MLA_V, :] = vt_ref[hd * MLA_V:(hd + 1) * MLA_V, :]
        vx_ref[hd, MLA_V:, :] = jnp.ones((vx_ref.shape[1] - MLA_V, seq), vx_ref.dtype)

    def score_tile(u, j):
        hd, i = units[u]
        st = state[u]
        keys = slice(j * t, (j + 1) * t)
        s = _dot(k_ref[keys, hd * qk:(hd + 1) * qk],
                 qt_ref[hd * qk:(hd + 1) * qk, i * t:(i + 1) * t])
        if j == i:
            s = jnp.where(causal, s, NEG_INF)
        s_ref[u % 2, keys, :] = s
        m8 = _tree(jnp.maximum, [s[a * sub:(a + 1) * sub, :] for a in range(t // sub)])
        st["m8"] = m8 if st["m8"] is None else jnp.maximum(st["m8"], m8)

    def prob_tile(u, j):
        st = state[u]
        if "m" not in st:
            st["m"] = jnp.max(st["m8"], axis=0, keepdims=True)
        keys = slice(j * t, (j + 1) * t)
        p_ref[u % 3, keys, :] = jnp.exp2(s_ref[u % 2, keys, :] - st["m"]).astype(jnp.bfloat16)

    def finish(u):
        hd, i = units[u]
        rows = slice(i * t, (i + 1) * t)
        kv = (i + 1) * t
        acc = _dot(vx_ref[hd, :, 0:kv], p_ref[u % 3, 0:kv, :])
        out_t = acc[0:MLA_V, :] / acc[MLA_V:MLA_V + 1, :]
        gate = g_ref[rows, hd * MLA_V:(hd + 1) * MLA_V].astype(jnp.float32)
        o_ref[rows, hd * MLA_V:(hd + 1) * MLA_V] = (out_t.T * gate).astype(o_ref.dtype)

    for j in range(units[0][1] + 1):
        score_tile(0, j)
    for u, (_, i) in enumerate(units):
        nxt = ([functools.partial(score_tile, u + 1, j) for j in range(units[u + 1][1] + 1)]
               if u + 1 < len(units) else [])
        cur = [functools.partial(prob_tile, u, j) for j in range(i + 1)]
        prev = [functools.partial(finish, u - 1)] if u > 0 else []
        while nxt or cur or prev:
            if nxt:
                nxt.pop(0)()
            if cur:
                cur.pop(0)()
            if prev:
                prev.pop(0)()
    finish(len(units) - 1)


def _mla_call(qt, km, vt, gm, batch, seq):
    n = km.shape[0]
    t = MLA_BLOCK
    hps = MLA_HEADS_PER_STEP
    groups = MLA_HEADS // hps
    ones_rows = 16
    feat = lambda b, h: (b * groups + h, 0)
    tok = lambda b, h: (b, h)
    return pl.pallas_call(
        _mla_kernel,
        grid=(batch, groups),
        in_specs=[
            pl.BlockSpec((hps * 2 * LANES, seq), feat),
            pl.BlockSpec((seq, hps * 2 * LANES), tok),
            pl.BlockSpec((hps * MLA_V, seq), feat),
            pl.BlockSpec((seq, hps * MLA_V), tok),
        ],
        out_specs=pl.BlockSpec((seq, hps * MLA_V), tok),
        out_shape=jax.ShapeDtypeStruct((n, MLA_WIDTH), jnp.bfloat16),
        scratch_shapes=[pltpu.VMEM((2, seq, t), jnp.float32),
                        pltpu.VMEM((3, seq, t), jnp.bfloat16),
                        pltpu.VMEM((hps, MLA_V + ones_rows, seq), jnp.bfloat16)],
        compiler_params=pltpu.CompilerParams(
            dimension_semantics=("arbitrary", "arbitrary"),
            vmem_limit_bytes=V7X_VMEM_LIMIT_BYTES),
        name="mla",
    )(qt, km, vt, gm)


def _swa_out_kernel(sink_ref, qt_ref, k_ref, kh_ref, vt_ref, vth_ref, g_ref, mm_ref, x_ref,
                    w_ref, fg_ref, o_ref, ms_ref, kx_ref, vx_ref):
    w = SWA_WINDOW
    chunk = pl.program_id(1)
    rows_total = k_ref.shape[0]
    bf = jnp.bfloat16
    sub = 8

    kx_ref[0:w, :] = kh_ref[...]
    kx_ref[w:, :] = k_ref[...]
    for hk in range(SWA_KV_HEADS):
        vx_ref[hk, 0:SWA_D, 0:w] = vth_ref[hk * SWA_D:(hk + 1) * SWA_D, :]
        vx_ref[hk, 0:SWA_D, w:] = vt_ref[hk * SWA_D:(hk + 1) * SWA_D, :]
        vx_ref[hk, SWA_D:, :] = jnp.ones((vx_ref.shape[1] - SWA_D, vx_ref.shape[2]), bf)

    ki = lax.broadcasted_iota(jnp.int32, (2 * w, 2 * w), 0)
    qi = lax.broadcasted_iota(jnp.int32, (2 * w, 2 * w), 1) % w
    rel = qi + w - ki
    band = (rel >= 0) & (rel < SWA_WINDOW)
    band_first = band & ((ki >= w) | (chunk > 0))
    lane = lax.broadcasted_iota(jnp.int32, (1, 2 * w), 1)

    units = [(n, hk) for n in range(rows_total // w) for hk in range(SWA_KV_HEADS)]
    state = [dict() for _ in units]

    def sink_rows(hk):
        sink = [sink_ref[hk * SWA_GROUP + i] * LOG2E for i in range(SWA_GROUP)]
        return (jnp.where(lane < w, sink[0], sink[2]), jnp.where(lane < w, sink[1], sink[3]))

    def stage_scores(u):
        n, hk = units[u]
        base = hk * 2 * LANES
        qcols = slice(n * w, (n + 1) * w)
        keys = slice(n * w, (n + 2) * w)
        qt = jnp.concatenate([qt_ref[base:base + LANES, qcols],
                              qt_ref[base + LANES:base + 2 * LANES, qcols]], axis=1)
        valid = band_first if n == 0 else band
        sinks = sink_rows(hk)
        st = state[u]
        st["s"], st["m"] = [], []
        for half in range(2):
            lanes = slice(base + half * LANES, base + (half + 1) * LANES)
            s = jnp.where(valid, _dot(kx_ref[keys, lanes], qt), NEG_INF)
            m8 = _tree(jnp.maximum, [s[a * sub:(a + 1) * sub, :] for a in range(2 * w // sub)])
            st["s"].append(s)
            st["m"].append(jnp.maximum(jnp.max(m8, axis=0, keepdims=True), sinks[half]))

    def stage_values(u):
        n, hk = units[u]
        keys = slice(n * w, (n + 2) * w)
        st = state[u]
        st["acc"] = []
        for half in range(2):
            p = jnp.exp2(st["s"][half] - st["m"][half]).astype(bf)
            st["acc"].append(_dot(vx_ref[hk, :, keys], p))
        del st["s"]

    def stage_store(u):
        n, hk = units[u]
        base = hk * 2 * LANES
        qcols = slice(n * w, (n + 1) * w)
        sinks = sink_rows(hk)
        st = state[u]
        outs = []
        for half in range(2):
            acc = st["acc"][half]
            den = acc[SWA_D:SWA_D + 1, :] + jnp.exp2(sinks[half] - st["m"][half])
            outs.append(acc[0:SWA_D, :] / den)
        g = g_ref[qcols, base:base + 2 * LANES].astype(jnp.float32)
        for ch in range(2):
            o_t = jnp.concatenate([outs[0][:, ch * w:(ch + 1) * w],
                                   outs[1][:, ch * w:(ch + 1) * w]], axis=0)
            ms_ref[qcols, base + ch * LANES:base + (ch + 1) * LANES] = (
                o_t.T * g[:, ch * LANES:(ch + 1) * LANES]).astype(bf)
        state[u] = None

    t = SWA_OUT_SUBTILE
    units_per_tile = (t // w) * SWA_KV_HEADS

    d_model = x_ref.shape[1]
    grp = 2 * LANES
    out_state = {}

    def out_group(i, gc):
        rows = slice(i * t, (i + 1) * t)
        cols = slice(gc * grp, (gc + 1) * grp)
        y = (x_ref[rows, cols] + _dot(mm_ref[rows, :], w_ref[0:MLA_WIDTH, cols])
             + _dot(ms_ref[rows, :], w_ref[MLA_WIDTH:, cols]))
        out_state.setdefault(i, []).append(y)

    def out_norm(i):
        rows = slice(i * t, (i + 1) * t)
        ys = out_state.pop(i)
        ssq = _tree(jnp.add, [jnp.sum(y * y, axis=-1, keepdims=True) for y in ys])
        scale = lax.rsqrt(ssq / d_model + NORM_EPS)
        for gc, y in enumerate(ys):
            cols = slice(gc * grp, (gc + 1) * grp)
            o_ref[rows, cols] = y * scale * fg_ref[:, cols]

    nu = len(units)
    pending = []
    for step in range(nu + 2):
        if step < nu:
            stage_scores(step)
        if pending:
            pending.pop(0)()
        if 0 <= step - 1 < nu:
            stage_values(step - 1)
        if pending:
            pending.pop(0)()
        if 0 <= step - 2 < nu:
            stage_store(step - 2)
            if (step - 2 + 1) % units_per_tile == 0:
                i = (step - 2) // units_per_tile
                pending += [functools.partial(out_group, i, gc) for gc in range(d_model // grp)]
                pending.append(functools.partial(out_norm, i))
    for task in pending:
        task()


def _swa_out_call(sinks, qst, ks4, vst, gs, mm, x2, w_o, fg, batch, seq):
    n, d = x2.shape
    r = SWA_OUT_ROWS
    w = SWA_WINDOW
    cps = seq // r
    bpc = r // w
    bps = seq // w
    v_rows = SWA_KV_HEADS * SWA_D
    ones_rows = 16
    row = lambda b, c: (b * cps + c, 0)
    halo = lambda b, c: (b * bps + jnp.maximum(c * bpc - 1, 0), 0)
    feat = lambda b, c: (b, c)
    feat_halo = lambda b, c: (b, jnp.maximum(c * bpc - 1, 0))
    const = lambda b, c: (0, 0)
    wide = pl.BlockSpec((r, SWA_WIDTH), row)
    return pl.pallas_call(
        _swa_out_kernel,
        grid=(batch, cps),
        in_specs=[
            pl.BlockSpec(memory_space=pltpu.SMEM),
            pl.BlockSpec((SWA_WIDTH, r), feat),
            wide,
            pl.BlockSpec((w, SWA_WIDTH), halo),
            pl.BlockSpec((v_rows, r), feat),
            pl.BlockSpec((v_rows, w), feat_halo),
            wide,
            pl.BlockSpec((r, MLA_WIDTH), row),
            pl.BlockSpec((r, d), row),
            pl.BlockSpec(w_o.shape, const),
            pl.BlockSpec((1, d), const),
        ],
        out_specs=pl.BlockSpec((r, d), row),
        out_shape=jax.ShapeDtypeStruct((n, d), jnp.float32),
        scratch_shapes=[pltpu.VMEM((r, SWA_WIDTH), jnp.bfloat16),
                        pltpu.VMEM((r + w, SWA_WIDTH), jnp.bfloat16),
                        pltpu.VMEM((SWA_KV_HEADS, SWA_D + ones_rows, r + w), jnp.bfloat16)],
        compiler_params=pltpu.CompilerParams(
            dimension_semantics=("arbitrary", "arbitrary"),
            vmem_limit_bytes=V7X_VMEM_LIMIT_BYTES),
        name="swa_out",
    )(sinks, qst, ks4, ks4, vst, vst, gs, mm, x2, w_o, fg)


def _pair_rope_pieces(src, dst, n_heads, dim):
    half = dim // 2
    out = []
    for pair in range(n_heads // 2):
        for which in range(2):
            head = src + (2 * pair + which) * dim
            chunk = dst + pair * 2 * dim
            out.append((head, chunk + which * half, half))
            out.append((head + half, chunk + 2 * half + which * half, half))
    return out


def _w_in_pieces():
    pieces, qs_pieces, src, dst = [], [], 0, 0
    for name, width in (("c_q", MLA_Q_RANK), ("c_kv", MLA_KV_RANK), ("k_rope", MLA_ROPE),
                        ("g_mla", MLA_WIDTH), ("q_s", SWA_WIDTH),
                        ("k_s", SWA_KV_HEADS * SWA_D), ("v_s", SWA_KV_HEADS * SWA_D),
                        ("g_swa", SWA_WIDTH)):
        if name == "k_rope":
            half = MLA_ROPE // 2
            pieces += [(src, dst, half), (src, dst + half, half),
                       (src + half, dst + 2 * half, half), (src + half, dst + 3 * half, half)]
            dst += 2 * MLA_ROPE
        elif name == "q_s":
            qs_pieces += _pair_rope_pieces(src, 0, SWA_Q_HEADS, SWA_D)
        elif name == "k_s":
            pieces += _pair_rope_pieces(src, dst, SWA_KV_HEADS, SWA_D)
            dst += width
        else:
            pieces.append((src, dst, width))
            dst += width
        src += width
    assert dst == PROJ_COLS
    return pieces, qs_pieces


def _gather_cols(src_ref, pieces, n_dst_cols):
    rows, n_src = src_ref.shape
    lane = lax.broadcasted_iota(jnp.int32, (rows, LANES), 1)
    loaded, rolled = {}, {}

    def load(k):
        if k not in loaded:
            width = min(LANES, n_src - k * LANES)
            v = src_ref[:, k * LANES:k * LANES + width]
            if width < LANES:
                v = jnp.concatenate([v, jnp.zeros((rows, LANES - width), v.dtype)], axis=1)
            loaded[k] = v
        return loaded[k]

    def shifted(k, shift):
        if shift == 0:
            return load(k)
        if (k, shift) not in rolled:
            rolled[(k, shift)] = pltpu.roll(load(k), shift, 1)
        return rolled[(k, shift)]

    chunks = []
    for d in range(n_dst_cols // LANES):
        acc = None
        for s0, d0, w in pieces:
            lo, hi = max(d0, d * LANES), min(d0 + w, (d + 1) * LANES)
            while lo < hi:
                s_abs = s0 + lo - d0
                k, a = divmod(s_abs, LANES)
                n = min(hi - lo, LANES - a)
                a_dst = lo - d * LANES
                val = shifted(k, (a_dst - a) % LANES)
                if n == LANES:
                    acc = val
                else:
                    mask = (lane >= a_dst) & (lane < a_dst + n)
                    acc = jnp.where(mask, val, 0.0 if acc is None else acc)
                lo += n
        chunks.append(acc)
    return chunks


def _gather_rows(src_ref, pieces, d):
    parts = []
    for s0, d0, w in pieces:
        lo, hi = max(d0, d * LANES), min(d0 + w, (d + 1) * LANES)
        if lo < hi:
            parts.append((lo, src_ref[s0 + lo - d0:s0 + hi - d0, :]))
    parts.sort(key=lambda t: t[0])
    assert sum(p.shape[0] for _, p in parts) == LANES
    return jnp.concatenate([p for _, p in parts], axis=0) if len(parts) > 1 else parts[0][1]


def _prep_big_kernel(wint_ref, wout_ref, winp_ref, wqst_ref, wo_ref):
    pieces, qs_pieces = _w_in_pieces()
    for d in range(PROJ_COLS // LANES):
        winp_ref[:, d * LANES:(d + 1) * LANES] = _gather_rows(wint_ref, pieces, d).T.astype(winp_ref.dtype)
    for d in range(SWA_WIDTH // LANES):
        wqst_ref[d * LANES:(d + 1) * LANES, :] = _gather_rows(wint_ref, qs_pieces, d).astype(wqst_ref.dtype)
    wo_ref[...] = wout_ref[...].astype(wo_ref.dtype)


def _prep_small_kernel(wq_ref, wkv_ref, wqt_ref, wk_ref, wvt_ref):
    q_pieces = ([(hd * MLA_QK, hd * MLA_NOPE, MLA_NOPE) for hd in range(MLA_HEADS)]
                + [(hd * MLA_QK + MLA_NOPE, MLA_HEADS * MLA_NOPE + hd * MLA_ROPE, MLA_ROPE)
                   for hd in range(MLA_HEADS)])
    wq = jnp.concatenate(_gather_cols(wq_ref, q_pieces, MLA_HEADS * MLA_QK), axis=1)
    wqt_ref[...] = wq.T.astype(wqt_ref.dtype)
    per_head = MLA_NOPE + MLA_V
    for hd in range(MLA_HEADS):
        wk_ref[:, hd * MLA_NOPE:(hd + 1) * MLA_NOPE] = (
            wkv_ref[:, hd * per_head:hd * per_head + MLA_NOPE].astype(wk_ref.dtype))
        wvt_ref[hd * MLA_V:(hd + 1) * MLA_V, :] = (
            wkv_ref[:, hd * per_head + MLA_NOPE:(hd + 1) * per_head].T.astype(wvt_ref.dtype))


def _prep_weights(w_in, w_q_up, w_kv_up, w_out):
    bf = jnp.bfloat16
    d, n_in = w_in.shape
    rb = PREP_ROWS
    params = pltpu.CompilerParams(dimension_semantics=("arbitrary",),
                                  vmem_limit_bytes=V7X_VMEM_LIMIT_BYTES)
    w_in_p, w_qst, w_o = pl.pallas_call(
        _prep_big_kernel,
        grid=(d // rb,),
        in_specs=[pl.BlockSpec((n_in, rb), lambda i: (0, i)),
                  pl.BlockSpec((rb, w_out.shape[1]), lambda i: (i, 0))],
        out_specs=[pl.BlockSpec((rb, PROJ_COLS), lambda i: (i, 0)),
                   pl.BlockSpec((SWA_WIDTH, rb), lambda i: (0, i)),
                   pl.BlockSpec((rb, w_out.shape[1]), lambda i: (i, 0))],
        out_shape=[jax.ShapeDtypeStruct((d, PROJ_COLS), bf),
                   jax.ShapeDtypeStruct((SWA_WIDTH, d), bf),
                   jax.ShapeDtypeStruct(w_out.shape, bf)],
        compiler_params=params,
        name="prep_big",
    )(w_in.T, w_out)
    w_qt, w_k_p, w_vt = pl.pallas_call(
        _prep_small_kernel,
        out_shape=[jax.ShapeDtypeStruct((MLA_HEADS * MLA_QK, MLA_Q_RANK), bf),
                   jax.ShapeDtypeStruct((MLA_KV_RANK, MLA_HEADS * MLA_NOPE), bf),
                   jax.ShapeDtypeStruct((MLA_WIDTH, MLA_KV_RANK), bf)],
        name="prep_small",
    )(w_q_up, w_kv_up)
    return w_in_p, w_qst, w_qt, w_k_p, w_vt, w_o


def kernel(x, ln_mix, w_in, q_a_norm, w_q_up, kv_a_norm, w_kv_up, attn_sinks, w_out, final_norm):
    batch, seq, d = x.shape
    depth = ln_mix.shape[0]
    assert depth == 1, "final norm is fused into the single layer's output kernel"
    assert seq % MLA_BLOCK == 0 and seq % PROJ_ROWS == 0 and seq % SWA_OUT_ROWS == 0
    tables = {dim: _rope_tables(seq, dim) for dim in {MLA_ROPE, SWA_D}}
    tabs_t, tabs_m = tables[MLA_ROPE]
    tabs_st, tabs_s = tables[SWA_D]
    x2 = x.reshape(batch * seq, d)
    w_in_p, w_qst, w_qt, w_k_p, w_vt, w_o = _prep_weights(w_in[0], w_q_up[0], w_kv_up[0], w_out[0])
    qt, km, vt, gm, qst, ks4, vst, gs = _proj_call(
        x2, ln_mix[0].reshape(1, -1), w_in_p, w_qst, q_a_norm[0].reshape(1, -1), w_qt,
        kv_a_norm[0].reshape(1, -1), w_k_p, w_vt, tabs_m, tabs_s, tabs_t, tabs_st, seq)
    mm = _mla_call(qt, km, vt, gm, batch, seq)
    out = _swa_out_call(attn_sinks[0], qst, ks4, vst, gs, mm, x2, w_o,
                        final_norm.reshape(1, -1), batch, seq)
    return out.reshape(batch, seq, d)
```

```python
import functools
import math

import jax
import jax.numpy as jnp
from jax import lax
from jax.experimental import pallas as pl
from jax.experimental.pallas import tpu as pltpu

ROPE_THETA = 10000.0
NORM_EPS = 1e-6
NEG_INF = -1e30
LOG2E = 1.4426950408889634

MLA_HEADS = 4
MLA_NOPE = 128
MLA_ROPE = 64
MLA_V = 128
MLA_Q_RANK = 256
MLA_KV_RANK = 128
MLA_QK = MLA_NOPE + MLA_ROPE
MLA_WIDTH = MLA_HEADS * MLA_V

SWA_Q_HEADS = 8
SWA_KV_HEADS = 2
SWA_D = 64
SWA_WINDOW = 128
SWA_GROUP = SWA_Q_HEADS // SWA_KV_HEADS
SWA_WIDTH = SWA_Q_HEADS * SWA_D

LANES = 128
V7X_VMEM_LIMIT_BYTES = 56 * 1024 * 1024

PROJ_ROWS = 1024
MLA_BLOCK = 256
MLA_HEADS_PER_STEP = 2
SWA_OUT_ROWS = 1024
SWA_OUT_SUBTILE = 256
SWA_STORE_LAG = 2
PREP_ROWS = 256


def _rope_tables(seq, dim):
    assert 2 * dim == LANES
    half = dim // 2
    inv_freq = 1.0 / (ROPE_THETA ** (jnp.arange(0, dim, 2, dtype=jnp.float32) / dim))
    f32 = jnp.float32
    ct, st, c_lane, s_lane = pl.pallas_call(
        _rope_table_kernel,
        out_shape=[jax.ShapeDtypeStruct((half, seq), f32), jax.ShapeDtypeStruct((half, seq), f32),
                   jax.ShapeDtypeStruct((seq, LANES), f32), jax.ShapeDtypeStruct((seq, LANES), f32)],
        name="rope_tables",
    )(inv_freq.reshape(half, 1))
    return (ct, st), (c_lane, s_lane)


def _rope_table_kernel(inv_ref, ct_ref, st_ref, cl_ref, sl_ref):
    half, seq = ct_ref.shape
    pos = lax.broadcasted_iota(jnp.int32, (half, seq), 1).astype(jnp.float32)
    ang = pos * inv_ref[...]
    c, s = jnp.cos(ang), jnp.sin(ang)
    ct_ref[...] = c
    st_ref[...] = s
    cl_ref[...] = jnp.concatenate([c, c, c, c], axis=0).T
    sl_ref[...] = jnp.concatenate([-s, -s, s, s], axis=0).T


def _rope(x, c, s):
    return x * c + pltpu.roll(x, LANES // 2, 1) * s


def _rms(x, g):
    return x * lax.rsqrt(jnp.mean(x * x, axis=-1, keepdims=True) + NORM_EPS) * g


def _silu(g):
    return g / (1.0 + jnp.exp(-g))


def _dot(a, b):
    return jnp.dot(a, b, preferred_element_type=jnp.float32)


def _dot_nt(a, b):
    return lax.dot_general(a, b, (((1,), (1,)), ((), ())),
                           preferred_element_type=jnp.float32)


PROJ_COLS = 1792


def _proj_kernel(x_ref, ln_ref, win_ref, wqst_ref, qg_ref, wqt_ref, kvg_ref, wkv_ref, wvt_ref,
                 cm_ref, sm_ref, cs_ref, ss_ref, ct_ref, st_ref, cst_ref, sst_ref,
                 qt_ref, km_ref, vt_ref, gm_ref, qst_ref, ks_ref, vst_ref, gs_ref):
    bf = jnp.bfloat16
    x = x_ref[...]
    h = _rms(x, ln_ref[...]).astype(bf)

    cm, sm, cs, ss = cm_ref[...], sm_ref[...], cs_ref[...], ss_ref[...]
    lane = lax.broadcasted_iota(jnp.int32, (x.shape[0], LANES), 1)
    first = (lane % (LANES // 2)) < (LANES // 4)
    grp = 2 * LANES

    def cols(a):
        return win_ref[:, a:a + grp]

    c_q = _dot(h, cols(0))
    lat_b = _dot(h, cols(256))
    cqn = _rms(c_q, qg_ref[...]).astype(bf)
    ckvn = _rms(lat_b[:, 0:MLA_KV_RANK], kvg_ref[...]).astype(bf)
    kr = _rope(lat_b[:, LANES:2 * LANES], cm, sm)
    kr_first = jnp.where(first, kr, 0.0).astype(bf)

    s_scale = LOG2E / math.sqrt(SWA_D)
    qst = _dot_nt(wqst_ref[...], h)
    cst2 = jnp.concatenate([cst_ref[...], cst_ref[...]], axis=0)
    sst2 = jnp.concatenate([sst_ref[...], sst_ref[...]], axis=0)
    for ch in range(SWA_WIDTH // LANES):
        x1 = qst[ch * LANES:ch * LANES + SWA_D, :]
        x2 = qst[ch * LANES + SWA_D:(ch + 1) * LANES, :]
        qst_ref[ch * LANES:ch * LANES + SWA_D, :] = ((x1 * cst2 - x2 * sst2) * s_scale).astype(bf)
        qst_ref[ch * LANES + SWA_D:(ch + 1) * LANES, :] = ((x2 * cst2 + x1 * sst2) * s_scale).astype(bf)

    q_scale = LOG2E / math.sqrt(MLA_QK)
    nope_w = MLA_HEADS * MLA_NOPE
    half = MLA_ROPE // 2
    qt = _dot_nt(wqt_ref[...], cqn)
    ct, st = ct_ref[...], st_ref[...]
    zeros = jnp.zeros((half, x.shape[0]), bf)
    for hd in range(MLA_HEADS):
        qt_ref[hd * grp:hd * grp + LANES, :] = (qt[hd * MLA_NOPE:(hd + 1) * MLA_NOPE, :] * q_scale).astype(bf)
        x1 = qt[nope_w + hd * MLA_ROPE:nope_w + hd * MLA_ROPE + half, :]
        x2 = qt[nope_w + hd * MLA_ROPE + half:nope_w + (hd + 1) * MLA_ROPE, :]
        base = hd * grp + LANES
        qt_ref[base:base + half, :] = ((x1 * ct - x2 * st) * q_scale).astype(bf)
        qt_ref[base + half:base + 2 * half, :] = zeros
        qt_ref[base + 2 * half:base + 3 * half, :] = ((x2 * ct + x1 * st) * q_scale).astype(bf)
        qt_ref[base + 3 * half:base + 4 * half, :] = zeros
    for g2 in range(nope_w // grp):
        kn = _dot(ckvn, wkv_ref[:, g2 * grp:(g2 + 1) * grp])
        for j in range(2):
            hd = 2 * g2 + j
            km_ref[:, hd * grp:hd * grp + LANES] = kn[:, j * LANES:(j + 1) * LANES].astype(bf)
            km_ref[:, hd * grp + LANES:(hd + 1) * grp] = kr_first
    for g2 in range(MLA_WIDTH // grp):
        gm_ref[:, g2 * grp:(g2 + 1) * grp] = _silu(_dot(h, cols(512 + g2 * grp))).astype(bf)

    kv_s = _dot(h, cols(1024))
    k01 = _rope(kv_s[:, 0:LANES], cs, ss)
    ks_ref[:, 0 * LANES:1 * LANES] = jnp.where(first, k01, 0.0).astype(bf)
    ks_ref[:, 1 * LANES:2 * LANES] = jnp.where(first, 0.0, pltpu.roll(k01, LANES // 4, 1)).astype(bf)
    ks_ref[:, 2 * LANES:3 * LANES] = jnp.where(first, pltpu.roll(k01, 3 * LANES // 4, 1), 0.0).astype(bf)
    ks_ref[:, 3 * LANES:4 * LANES] = jnp.where(first, 0.0, k01).astype(bf)
    vst_ref[...] = kv_s[:, LANES:2 * LANES].T.astype(bf)

    for g2 in range(SWA_WIDTH // grp):
        gs_ref[:, g2 * grp:(g2 + 1) * grp] = _silu(_dot(h, cols(1280 + g2 * grp))).astype(bf)

    vt_ref[...] = _dot_nt(wvt_ref[...], ckvn).astype(bf)


def _proj_call(x2, ln_g, w_in_p, w_qst, q_g, w_qt, kv_g, w_k_p, w_vt, tabs_m, tabs_s, tabs_t,
               tabs_st, seq):
    n, d = x2.shape
    tm = PROJ_ROWS
    steps_per_seq = seq // tm
    batch = n // seq
    row = lambda i: (i, 0)
    const = lambda i: (0, 0)
    tab = lambda i: (i % steps_per_seq, 0)
    tab_t = lambda i: (0, i % steps_per_seq)
    col = lambda i: (i // steps_per_seq, i % steps_per_seq)
    bf = jnp.bfloat16
    qt_rows = MLA_HEADS * 2 * LANES
    row_out = lambda w: (pl.BlockSpec((tm, w), row), jax.ShapeDtypeStruct((n, w), bf))
    col_out = lambda r: (pl.BlockSpec((r, tm), col), jax.ShapeDtypeStruct((batch * r, seq), bf))
    outs = [col_out(qt_rows), row_out(qt_rows), col_out(MLA_WIDTH), row_out(MLA_WIDTH),
            col_out(SWA_WIDTH), row_out(SWA_WIDTH), col_out(SWA_KV_HEADS * SWA_D), row_out(SWA_WIDTH)]
    return pl.pallas_call(
        _proj_kernel,
        grid=(n // tm,),
        in_specs=[
            pl.BlockSpec((tm, d), row),
            pl.BlockSpec((1, d), const),
            pl.BlockSpec(w_in_p.shape, const),
            pl.BlockSpec(w_qst.shape, const),
            pl.BlockSpec((1, MLA_Q_RANK), const),
            pl.BlockSpec(w_qt.shape, const),
            pl.BlockSpec((1, MLA_KV_RANK), const),
            pl.BlockSpec(w_k_p.shape, const),
            pl.BlockSpec(w_vt.shape, const),
        ] + [pl.BlockSpec((tm, LANES), tab)] * 4
          + [pl.BlockSpec((MLA_ROPE // 2, tm), tab_t)] * 2
          + [pl.BlockSpec((SWA_D // 2, tm), tab_t)] * 2,
        out_specs=[o[0] for o in outs],
        out_shape=[o[1] for o in outs],
        compiler_params=pltpu.CompilerParams(
            dimension_semantics=("arbitrary",),
            vmem_limit_bytes=V7X_VMEM_LIMIT_BYTES),
        name="proj",
    )(x2, ln_g, w_in_p, w_qst, q_g, w_qt, kv_g, w_k_p, w_vt, *tabs_m, *tabs_s, *tabs_t, *tabs_st)


def _tree(op, xs):
    xs = list(xs)
    while len(xs) > 1:
        xs = [op(xs[a], xs[a + 1]) if a + 1 < len(xs) else xs[a] for a in range(0, len(xs), 2)]
    return xs[0]


def _mla_kernel(qt_ref, k_ref, vt_ref, g_ref, o_ref, s_ref, p_ref, vx_ref):
    t = MLA_BLOCK
    seq = k_ref.shape[0]
    sub = 8
    qk = 2 * LANES
    r = lax.broadcasted_iota(jnp.int32, (t, t), 0)
    c = lax.broadcasted_iota(jnp.int32, (t, t), 1)
    causal = r <= c
    nblk = seq // t
    units = [(hd, i) for hd in range(MLA_HEADS_PER_STEP) for i in range(nblk - 1, -1, -1)]
    state = [dict(m8=None) for _ in units]

    for hd in range(MLA_HEADS_PER_STEP):
        vx_ref[hd, 0:MLA_V, :] = vt_ref[hd * MLA_V:(hd + 1) * MLA_V, :]
        vx_ref[hd, MLA_V:, :] = jnp.ones((vx_ref.shape[1] - MLA_V, seq), vx_ref.dtype)

    def score_tile(u, j):
        hd, i = units[u]
        st = state[u]
        keys = slice(j * t, (j + 1) * t)
        s = _dot(k_ref[keys, hd * qk:(hd + 1) * qk],
                 qt_ref[hd * qk:(hd + 1) * qk, i * t:(i + 1) * t])
        if j == i:
            s = jnp.where(causal, s, NEG_INF)
        s_ref[u % 2, keys, :] = s
        m8 = _tree(jnp.maximum, [s[a * sub:(a + 1) * sub, :] for a in range(t // sub)])
        st["m8"] = m8 if st["m8"] is None else jnp.maximum(st["m8"], m8)

    def prob_tile(u, j):
        st = state[u]
        if "m" not in st:
            st["m"] = jnp.max(st["m8"], axis=0, keepdims=True)
        keys = slice(j * t, (j + 1) * t)
        p_ref[u % 3, keys, :] = jnp.exp2(s_ref[u % 2, keys, :] - st["m"]).astype(jnp.bfloat16)

    def value_matmul(u):
        hd, i = units[u]
        kv = (i + 1) * t
        state[u]["acc"] = _dot(vx_ref[hd, :, 0:kv], p_ref[u % 3, 0:kv, :])

    def store(u):
        hd, i = units[u]
        rows = slice(i * t, (i + 1) * t)
        acc = state[u]["acc"]
        out_t = acc[0:MLA_V, :] / acc[MLA_V:MLA_V + 1, :]
        gate = g_ref[rows, hd * MLA_V:(hd + 1) * MLA_V].astype(jnp.float32)
        o_ref[rows, hd * MLA_V:(hd + 1) * MLA_V] = (out_t.T * gate).astype(o_ref.dtype)
        state[u] = None

    nu = len(units)
    for j in range(units[0][1] + 1):
        score_tile(0, j)
    for u in range(nu + 2):
        if 0 <= u - 2 < nu:
            store(u - 2)
        nxt = ([functools.partial(score_tile, u + 1, j) for j in range(units[u + 1][1] + 1)]
               if u + 1 < nu else [])
        cur = [functools.partial(prob_tile, u, j) for j in range(units[u][1] + 1)] if u < nu else []
        while nxt or cur:
            if cur:
                cur.pop(0)()
            if nxt:
                nxt.pop(0)()
        if 0 <= u - 1 < nu:
            value_matmul(u - 1)


def _mla_call(qt, km, vt, gm, batch, seq):
    n = km.shape[0]
    t = MLA_BLOCK
    hps = MLA_HEADS_PER_STEP
    groups = MLA_HEADS // hps
    ones_rows = 16
    feat = lambda b, h: (b * groups + h, 0)
    tok = lambda b, h: (b, h)
    return pl.pallas_call(
        _mla_kernel,
        grid=(batch, groups),
        in_specs=[
            pl.BlockSpec((hps * 2 * LANES, seq), feat),
            pl.BlockSpec((seq, hps * 2 * LANES), tok),
            pl.BlockSpec((hps * MLA_V, seq), feat),
            pl.BlockSpec((seq, hps * MLA_V), tok),
        ],
        out_specs=pl.BlockSpec((seq, hps * MLA_V), tok),
        out_shape=jax.ShapeDtypeStruct((n, MLA_WIDTH), jnp.bfloat16),
        scratch_shapes=[pltpu.VMEM((2, seq, t), jnp.float32),
                        pltpu.VMEM((3, seq, t), jnp.bfloat16),
                        pltpu.VMEM((hps, MLA_V + ones_rows, seq), jnp.bfloat16)],
        compiler_params=pltpu.CompilerParams(
            dimension_semantics=("arbitrary", "arbitrary"),
            vmem_limit_bytes=V7X_VMEM_LIMIT_BYTES),
        name="mla",
    )(qt, km, vt, gm)


def _swa_out_kernel(sink_ref, qt_ref, k_ref, kh_ref, vt_ref, vth_ref, g_ref, mm_ref, x_ref,
                    w_ref, fg_ref, o_ref, ms_ref, kx_ref, vx_ref):
    w = SWA_WINDOW
    chunk = pl.program_id(1)
    rows_total = k_ref.shape[0]
    bf = jnp.bfloat16
    sub = 8

    kx_ref[0:w, :] = kh_ref[...]
    kx_ref[w:, :] = k_ref[...]
    for hk in range(SWA_KV_HEADS):
        vx_ref[hk, 0:SWA_D, 0:w] = vth_ref[hk * SWA_D:(hk + 1) * SWA_D, :]
        vx_ref[hk, 0:SWA_D, w:] = vt_ref[hk * SWA_D:(hk + 1) * SWA_D, :]
        vx_ref[hk, SWA_D:, :] = jnp.ones((vx_ref.shape[1] - SWA_D, vx_ref.shape[2]), bf)

    ki = lax.broadcasted_iota(jnp.int32, (2 * w, 2 * w), 0)
    qi = lax.broadcasted_iota(jnp.int32, (2 * w, 2 * w), 1) % w
    rel = qi + w - ki
    band = (rel >= 0) & (rel < SWA_WINDOW)
    band_first = band & ((ki >= w) | (chunk > 0))
    lane = lax.broadcasted_iota(jnp.int32, (1, 2 * w), 1)

    units = [(n, hk) for n in range(rows_total // w) for hk in range(SWA_KV_HEADS)]
    state = [dict() for _ in units]

    def sink_rows(hk):
        sink = [sink_ref[hk * SWA_GROUP + i] * LOG2E for i in range(SWA_GROUP)]
        return (jnp.where(lane < w, sink[0], sink[2]), jnp.where(lane < w, sink[1], sink[3]))

    def stage_scores(u):
        n, hk = units[u]
        base = hk * 2 * LANES
        qcols = slice(n * w, (n + 1) * w)
        keys = slice(n * w, (n + 2) * w)
        qt = jnp.concatenate([qt_ref[base:base + LANES, qcols],
                              qt_ref[base + LANES:base + 2 * LANES, qcols]], axis=1)
        valid = band_first if n == 0 else band
        sinks = sink_rows(hk)
        st = state[u]
        st["s"], st["m"] = [], []
        for half in range(2):
            lanes = slice(base + half * LANES, base + (half + 1) * LANES)
            s = jnp.where(valid, _dot(kx_ref[keys, lanes], qt), NEG_INF)
            m8 = _tree(jnp.maximum, [s[a * sub:(a + 1) * sub, :] for a in range(2 * w // sub)])
            st["s"].append(s)
            st["m"].append(jnp.maximum(jnp.max(m8, axis=0, keepdims=True), sinks[half]))

    def stage_values(u):
        n, hk = units[u]
        keys = slice(n * w, (n + 2) * w)
        st = state[u]
        st["acc"] = []
        for half in range(2):
            p = jnp.exp2(st["s"][half] - st["m"][half]).astype(bf)
            st["acc"].append(_dot(vx_ref[hk, :, keys], p))
        del st["s"]

    def stage_store(u):
        n, hk = units[u]
        base = hk * 2 * LANES
        qcols = slice(n * w, (n + 1) * w)
        sinks = sink_rows(hk)
        st = state[u]
        outs = []
        for half in range(2):
            acc = st["acc"][half]
            den = acc[SWA_D:SWA_D + 1, :] + jnp.exp2(sinks[half] - st["m"][half])
            outs.append(acc[0:SWA_D, :] / den)
        g = g_ref[qcols, base:base + 2 * LANES].astype(jnp.float32)
        for ch in range(2):
            o_t = jnp.concatenate([outs[0][:, ch * w:(ch + 1) * w],
                                   outs[1][:, ch * w:(ch + 1) * w]], axis=0)
            ms_ref[qcols, base + ch * LANES:base + (ch + 1) * LANES] = (
                o_t.T * g[:, ch * LANES:(ch + 1) * LANES]).astype(bf)
        state[u] = None

    t = SWA_OUT_SUBTILE
    units_per_tile = (t // w) * SWA_KV_HEADS

    d_model = x_ref.shape[1]
    grp = 2 * LANES
    out_state = {}

    def out_group(i, gc):
        rows = slice(i * t, (i + 1) * t)
        cols = slice(gc * grp, (gc + 1) * grp)
        y = (x_ref[rows, cols] + _dot(mm_ref[rows, :], w_ref[0:MLA_WIDTH, cols])
             + _dot(ms_ref[rows, :], w_ref[MLA_WIDTH:, cols]))
        out_state.setdefault(i, []).append(y)

    def out_norm(i):
        rows = slice(i * t, (i + 1) * t)
        ys = out_state.pop(i)
        ssq = _tree(jnp.add, [jnp.sum(y * y, axis=-1, keepdims=True) for y in ys])
        scale = lax.rsqrt(ssq / d_model + NORM_EPS)
        for gc, y in enumerate(ys):
            cols = slice(gc * grp, (gc + 1) * grp)
            o_ref[rows, cols] = y * scale * fg_ref[:, cols]

    nu = len(units)
    lag = SWA_STORE_LAG
    pending = []
    for step in range(nu + lag):
        if step < nu:
            stage_scores(step)
        if pending:
            pending.pop(0)()
        if 0 <= step - 1 < nu:
            stage_values(step - 1)
        if pending:
            pending.pop(0)()
        if 0 <= step - lag < nu:
            stage_store(step - lag)
            if (step - lag + 1) % units_per_tile == 0:
                i = (step - lag) // units_per_tile
                pending += [functools.partial(out_group, i, gc) for gc in range(d_model // grp)]
                pending.append(functools.partial(out_norm, i))
    for task in pending:
        task()


def _swa_out_call(sinks, qst, ks4, vst, gs, mm, x2, w_o, fg, batch, seq):
    n, d = x2.shape
    r = SWA_OUT_ROWS
    w = SWA_WINDOW
    cps = seq // r
    bpc = r // w
    bps = seq // w
    v_rows = SWA_KV_HEADS * SWA_D
    ones_rows = 16
    row = lambda b, c: (b * cps + c, 0)
    halo = lambda b, c: (b * bps + jnp.maximum(c * bpc - 1, 0), 0)
    feat = lambda b, c: (b, c)
    feat_halo = lambda b, c: (b, jnp.maximum(c * bpc - 1, 0))
    const = lambda b, c: (0, 0)
    wide = pl.BlockSpec((r, SWA_WIDTH), row)
    return pl.pallas_call(
        _swa_out_kernel,
        grid=(batch, cps),
        in_specs=[
            pl.BlockSpec(memory_space=pltpu.SMEM),
            pl.BlockSpec((SWA_WIDTH, r), feat),
            wide,
            pl.BlockSpec((w, SWA_WIDTH), halo),
            pl.BlockSpec((v_rows, r), feat),
            pl.BlockSpec((v_rows, w), feat_halo),
            wide,
            pl.BlockSpec((r, MLA_WIDTH), row),
            pl.BlockSpec((r, d), row),
            pl.BlockSpec(w_o.shape, const),
            pl.BlockSpec((1, d), const),
        ],
        out_specs=pl.BlockSpec((r, d), row),
        out_shape=jax.ShapeDtypeStruct((n, d), jnp.float32),
        scratch_shapes=[pltpu.VMEM((r, SWA_WIDTH), jnp.bfloat16),
                        pltpu.VMEM((r + w, SWA_WIDTH), jnp.bfloat16),
                        pltpu.VMEM((SWA_KV_HEADS, SWA_D + ones_rows, r + w), jnp.bfloat16)],
        compiler_params=pltpu.CompilerParams(
            dimension_semantics=("arbitrary", "arbitrary"),
            vmem_limit_bytes=V7X_VMEM_LIMIT_BYTES),
        name="swa_out",
    )(sinks, qst, ks4, ks4, vst, vst, gs, mm, x2, w_o, fg)


def _pair_rope_pieces(src, dst, n_heads, dim):
    half = dim // 2
    out = []
    for pair in range(n_heads // 2):
        for which in range(2):
            head = src + (2 * pair + which) * dim
            chunk = dst + pair * 2 * dim
            out.append((head, chunk + which * half, half))
            out.append((head + half, chunk + 2 * half + which * half, half))
    return out


def _w_in_pieces():
    pieces, qs_pieces, src, dst = [], [], 0, 0
    for name, width in (("c_q", MLA_Q_RANK), ("c_kv", MLA_KV_RANK), ("k_rope", MLA_ROPE),
                        ("g_mla", MLA_WIDTH), ("q_s", SWA_WIDTH),
                        ("k_s", SWA_KV_HEADS * SWA_D), ("v_s", SWA_KV_HEADS * SWA_D),
                        ("g_swa", SWA_WIDTH)):
        if name == "k_rope":
            half = MLA_ROPE // 2
            pieces += [(src, dst, half), (src, dst + half, half),
                       (src + half, dst + 2 * half, half), (src + half, dst + 3 * half, half)]
            dst += 2 * MLA_ROPE
        elif name == "q_s":
            qs_pieces += _pair_rope_pieces(src, 0, SWA_Q_HEADS, SWA_D)
        elif name == "k_s":
            pieces += _pair_rope_pieces(src, dst, SWA_KV_HEADS, SWA_D)
            dst += width
        else:
            pieces.append((src, dst, width))
            dst += width
        src += width
    assert dst == PROJ_COLS
    return pieces, qs_pieces


def _gather_cols(src_ref, pieces, n_dst_cols):
    rows, n_src = src_ref.shape
    lane = lax.broadcasted_iota(jnp.int32, (rows, LANES), 1)
    loaded, rolled = {}, {}

    def load(k):
        if k not in loaded:
            width = min(LANES, n_src - k * LANES)
            v = src_ref[:, k * LANES:k * LANES + width]
            if width < LANES:
                v = jnp.concatenate([v, jnp.zeros((rows, LANES - width), v.dtype)], axis=1)
            loaded[k] = v
        return loaded[k]

    def shifted(k, shift):
        if shift == 0:
            return load(k)
        if (k, shift) not in rolled:
            rolled[(k, shift)] = pltpu.roll(load(k), shift, 1)
        return rolled[(k, shift)]

    chunks = []
    for d in range(n_dst_cols // LANES):
        acc = None
        for s0, d0, w in pieces:
            lo, hi = max(d0, d * LANES), min(d0 + w, (d + 1) * LANES)
            while lo < hi:
                s_abs = s0 + lo - d0
                k, a = divmod(s_abs, LANES)
                n = min(hi - lo, LANES - a)
                a_dst = lo - d * LANES
                val = shifted(k, (a_dst - a) % LANES)
                if n == LANES:
                    acc = val
                else:
                    mask = (lane >= a_dst) & (lane < a_dst + n)
                    acc = jnp.where(mask, val, 0.0 if acc is None else acc)
                lo += n
        chunks.append(acc)
    return chunks


def _gather_rows(src_ref, pieces, d):
    parts = []
    for s0, d0, w in pieces:
        lo, hi = max(d0, d * LANES), min(d0 + w, (d + 1) * LANES)
        if lo < hi:
            parts.append((lo, src_ref[s0 + lo - d0:s0 + hi - d0, :]))
    parts.sort(key=lambda t: t[0])
    assert sum(p.shape[0] for _, p in parts) == LANES
    return jnp.concatenate([p for _, p in parts], axis=0) if len(parts) > 1 else parts[0][1]


def _prep_big_kernel(wint_ref, wout_ref, winp_ref, wqst_ref, wo_ref):
    pieces, qs_pieces = _w_in_pieces()
    for d in range(PROJ_COLS // LANES):
        winp_ref[:, d * LANES:(d + 1) * LANES] = _gather_rows(wint_ref, pieces, d).T.astype(winp_ref.dtype)
    for d in range(SWA_WIDTH // LANES):
        wqst_ref[d * LANES:(d + 1) * LANES, :] = _gather_rows(wint_ref, qs_pieces, d).astype(wqst_ref.dtype)
    wo_ref[...] = wout_ref[...].astype(wo_ref.dtype)


def _prep_small_kernel(wq_ref, wkv_ref, wqt_ref, wk_ref, wvt_ref):
    q_pieces = ([(hd * MLA_QK, hd * MLA_NOPE, MLA_NOPE) for hd in range(MLA_HEADS)]
                + [(hd * MLA_QK + MLA_NOPE, MLA_HEADS * MLA_NOPE + hd * MLA_ROPE, MLA_ROPE)
                   for hd in range(MLA_HEADS)])
    wq = jnp.concatenate(_gather_cols(wq_ref, q_pieces, MLA_HEADS * MLA_QK), axis=1)
    wqt_ref[...] = wq.T.astype(wqt_ref.dtype)
    per_head = MLA_NOPE + MLA_V
    for hd in range(MLA_HEADS):
        wk_ref[:, hd * MLA_NOPE:(hd + 1) * MLA_NOPE] = (
            wkv_ref[:, hd * per_head:hd * per_head + MLA_NOPE].astype(wk_ref.dtype))
        wvt_ref[hd * MLA_V:(hd + 1) * MLA_V, :] = (
            wkv_ref[:, hd * per_head + MLA_NOPE:(hd + 1) * per_head].T.astype(wvt_ref.dtype))


def _prep_weights(w_in, w_q_up, w_kv_up, w_out):
    bf = jnp.bfloat16
    d, n_in = w_in.shape
    rb = PREP_ROWS
    params = pltpu.CompilerParams(dimension_semantics=("arbitrary",),
                                  vmem_limit_bytes=V7X_VMEM_LIMIT_BYTES)
    w_in_p, w_qst, w_o = pl.pallas_call(
        _prep_big_kernel,
        grid=(d // rb,),
        in_specs=[pl.BlockSpec((n_in, rb), lambda i: (0, i)),
                  pl.BlockSpec((rb, w_out.shape[1]), lambda i: (i, 0))],
        out_specs=[pl.BlockSpec((rb, PROJ_COLS), lambda i: (i, 0)),
                   pl.BlockSpec((SWA_WIDTH, rb), lambda i: (0, i)),
                   pl.BlockSpec((rb, w_out.shape[1]), lambda i: (i, 0))],
        out_shape=[jax.ShapeDtypeStruct((d, PROJ_COLS), bf),
                   jax.ShapeDtypeStruct((SWA_WIDTH, d), bf),
                   jax.ShapeDtypeStruct(w_out.shape, bf)],
        compiler_params=params,
        name="prep_big",
    )(w_in.T, w_out)
    w_qt, w_k_p, w_vt = pl.pallas_call(
        _prep_small_kernel,
        out_shape=[jax.ShapeDtypeStruct((MLA_HEADS * MLA_QK, MLA_Q_RANK), bf),
                   jax.ShapeDtypeStruct((MLA_KV_RANK, MLA_HEADS * MLA_NOPE), bf),
                   jax.ShapeDtypeStruct((MLA_WIDTH, MLA_KV_RANK), bf)],
        name="prep_small",
    )(w_q_up, w_kv_up)
    return w_in_p, w_qst, w_qt, w_k_p, w_vt, w_o


def kernel(x, ln_mix, w_in, q_a_norm, w_q_up, kv_a_norm, w_kv_up, attn_sinks, w_out, final_norm):
    batch, seq, d = x.shape
    depth = ln_mix.shape[0]
    assert depth == 1, "final norm is fused into the single layer's output kernel"
    assert seq % MLA_BLOCK == 0 and seq % PROJ_ROWS == 0 and seq % SWA_OUT_ROWS == 0
    tables = {dim: _rope_tables(seq, dim) for dim in {MLA_ROPE, SWA_D}}
    tabs_t, tabs_m = tables[MLA_ROPE]
    tabs_st, tabs_s = tables[SWA_D]
    x2 = x.reshape(batch * seq, d)
    w_in_p, w_qst, w_qt, w_k_p, w_vt, w_o = _prep_weights(w_in[0], w_q_up[0], w_kv_up[0], w_out[0])
    qt, km, vt, gm, qst, ks4, vst, gs = _proj_call(
        x2, ln_mix[0].reshape(1, -1), w_in_p, w_qst, q_a_norm[0].reshape(1, -1), w_qt,
        kv_a_norm[0].reshape(1, -1), w_k_p, w_vt, tabs_m, tabs_s, tabs_t, tabs_st, seq)
    mm = _mla_call(qt, km, vt, gm, batch, seq)
    out = _swa_out_call(attn_sinks[0], qst, ks4, vst, gs, mm, x2, w_o,
                        final_norm.reshape(1, -1), batch, seq)
    return out.reshape(batch, seq, d)
```

```python
import functools
import math

import jax
import jax.numpy as jnp
from jax import lax
from jax.experimental import pallas as pl
from jax.experimental.pallas import tpu as pltpu

ROPE_THETA = 10000.0
NORM_EPS = 1e-6
NEG_INF = -1e30
LOG2E = 1.4426950408889634

MLA_HEADS = 4
MLA_NOPE = 128
MLA_ROPE = 64
MLA_V = 128
MLA_Q_RANK = 256
MLA_KV_RANK = 128
MLA_QK = MLA_NOPE + MLA_ROPE
MLA_WIDTH = MLA_HEADS * MLA_V

SWA_Q_HEADS = 8
SWA_KV_HEADS = 2
SWA_D = 64
SWA_WINDOW = 128
SWA_GROUP = SWA_Q_HEADS // SWA_KV_HEADS
SWA_WIDTH = SWA_Q_HEADS * SWA_D

LANES = 128
SUBLANES = 8
MXU_COLS = 2 * LANES
MLA_QK_PAD = MXU_COLS
V7X_VMEM_LIMIT_BYTES = 56 * 1024 * 1024

PROJ_ROWS = 1024
MLA_BLOCK = 256
MLA_HEADS_PER_STEP = 2
SWA_OUT_ROWS = 1024
SWA_OUT_SUBTILE = 256
SWA_UNITS_PER_STEP = 4
SWA_OUT_TASKS_PER_STEP = 4
SWA_STORE_LAG = 2
PREP_ROWS = 256


def _rope_tables(seq, dim):
    assert 2 * dim == LANES
    half = dim // 2
    inv_freq = 1.0 / (ROPE_THETA ** (jnp.arange(0, dim, 2, dtype=jnp.float32) / dim))
    f32 = jnp.float32
    ct, st, c_lane, s_lane = pl.pallas_call(
        _rope_table_kernel,
        out_shape=[jax.ShapeDtypeStruct((half, seq), f32), jax.ShapeDtypeStruct((half, seq), f32),
                   jax.ShapeDtypeStruct((seq, LANES), f32), jax.ShapeDtypeStruct((seq, LANES), f32)],
        name="rope_tables",
    )(inv_freq.reshape(half, 1))
    return (ct, st), (c_lane, s_lane)


def _rope_table_kernel(inv_ref, ct_ref, st_ref, cl_ref, sl_ref):
    half, seq = ct_ref.shape
    pos = lax.broadcasted_iota(jnp.int32, (half, seq), 1).astype(jnp.float32)
    ang = pos * inv_ref[...]
    c, s = jnp.cos(ang), jnp.sin(ang)
    ct_ref[...] = c
    st_ref[...] = s
    cl_ref[...] = jnp.concatenate([c, c, c, c], axis=0).T
    sl_ref[...] = jnp.concatenate([-s, -s, s, s], axis=0).T


def _rope(x, c, s):
    return x * c + pltpu.roll(x, LANES // 2, 1) * s


def _rms(x, g):
    return x * lax.rsqrt(jnp.mean(x * x, axis=-1, keepdims=True) + NORM_EPS) * g


def _silu(g):
    return g / (1.0 + jnp.exp(-g))


def _dot(a, b):
    return jnp.dot(a, b, preferred_element_type=jnp.float32)


def _dot_nt(a, b):
    return lax.dot_general(a, b, (((1,), (1,)), ((), ())),
                           preferred_element_type=jnp.float32)


PROJ_COLS = 1792


def _proj_kernel(x_ref, ln_ref, win_ref, wqst_ref, qg_ref, wqt_ref, kvg_ref, wkv_ref, wvt_ref,
                 cm_ref, sm_ref, cs_ref, ss_ref, ct_ref, st_ref, cst_ref, sst_ref,
                 qt_ref, km_ref, vt_ref, gm_ref, qst_ref, ks_ref, vst_ref, gs_ref):
    bf = jnp.bfloat16
    x = x_ref[...]
    h = _rms(x, ln_ref[...]).astype(bf)

    cm, sm, cs, ss = cm_ref[...], sm_ref[...], cs_ref[...], ss_ref[...]
    lane = lax.broadcasted_iota(jnp.int32, (x.shape[0], LANES), 1)
    first = (lane % (LANES // 2)) < (LANES // 4)
    grp = MXU_COLS

    def cols(a):
        return win_ref[:, a:a + grp]

    c_q = _dot(h, cols(0))
    lat_b = _dot(h, cols(256))
    cqn = _rms(c_q, qg_ref[...]).astype(bf)
    ckvn = _rms(lat_b[:, 0:MLA_KV_RANK], kvg_ref[...]).astype(bf)
    kr = _rope(lat_b[:, LANES:2 * LANES], cm, sm)
    kr_first = jnp.where(first, kr, 0.0).astype(bf)

    s_scale = LOG2E / math.sqrt(SWA_D)
    qst = _dot_nt(wqst_ref[...], h)
    cst2 = jnp.concatenate([cst_ref[...], cst_ref[...]], axis=0)
    sst2 = jnp.concatenate([sst_ref[...], sst_ref[...]], axis=0)
    for ch in range(SWA_WIDTH // LANES):
        x1 = qst[ch * LANES:ch * LANES + SWA_D, :]
        x2 = qst[ch * LANES + SWA_D:(ch + 1) * LANES, :]
        qst_ref[ch * LANES:ch * LANES + SWA_D, :] = ((x1 * cst2 - x2 * sst2) * s_scale).astype(bf)
        qst_ref[ch * LANES + SWA_D:(ch + 1) * LANES, :] = ((x2 * cst2 + x1 * sst2) * s_scale).astype(bf)

    q_scale = LOG2E / math.sqrt(MLA_QK)
    nope_w = MLA_HEADS * MLA_NOPE
    half = MLA_ROPE // 2
    qt = _dot_nt(wqt_ref[...], cqn)
    ct, st = ct_ref[...], st_ref[...]
    zeros = jnp.zeros((half, x.shape[0]), bf)
    for hd in range(MLA_HEADS):
        qt_ref[hd * grp:hd * grp + LANES, :] = (qt[hd * MLA_NOPE:(hd + 1) * MLA_NOPE, :] * q_scale).astype(bf)
        x1 = qt[nope_w + hd * MLA_ROPE:nope_w + hd * MLA_ROPE + half, :]
        x2 = qt[nope_w + hd * MLA_ROPE + half:nope_w + (hd + 1) * MLA_ROPE, :]
        base = hd * grp + LANES
        qt_ref[base:base + half, :] = ((x1 * ct - x2 * st) * q_scale).astype(bf)
        qt_ref[base + half:base + 2 * half, :] = zeros
        qt_ref[base + 2 * half:base + 3 * half, :] = ((x2 * ct + x1 * st) * q_scale).astype(bf)
        qt_ref[base + 3 * half:base + 4 * half, :] = zeros
    for g2 in range(nope_w // grp):
        kn = _dot(ckvn, wkv_ref[:, g2 * grp:(g2 + 1) * grp])
        for j in range(2):
            hd = 2 * g2 + j
            km_ref[:, hd * grp:hd * grp + LANES] = kn[:, j * LANES:(j + 1) * LANES].astype(bf)
            km_ref[:, hd * grp + LANES:(hd + 1) * grp] = kr_first
    for g2 in range(MLA_WIDTH // grp):
        gm_ref[:, g2 * grp:(g2 + 1) * grp] = _silu(_dot(h, cols(512 + g2 * grp))).astype(bf)

    kv_s = _dot(h, cols(1024))
    k01 = _rope(kv_s[:, 0:LANES], cs, ss)
    ks_ref[:, 0 * LANES:1 * LANES] = jnp.where(first, k01, 0.0).astype(bf)
    ks_ref[:, 1 * LANES:2 * LANES] = jnp.where(first, 0.0, pltpu.roll(k01, LANES // 4, 1)).astype(bf)
    ks_ref[:, 2 * LANES:3 * LANES] = jnp.where(first, pltpu.roll(k01, 3 * LANES // 4, 1), 0.0).astype(bf)
    ks_ref[:, 3 * LANES:4 * LANES] = jnp.where(first, 0.0, k01).astype(bf)
    vst_ref[...] = kv_s[:, LANES:2 * LANES].T.astype(bf)

    for g2 in range(SWA_WIDTH // grp):
        gs_ref[:, g2 * grp:(g2 + 1) * grp] = _silu(_dot(h, cols(1280 + g2 * grp))).astype(bf)

    vt_ref[...] = _dot_nt(wvt_ref[...], ckvn).astype(bf)


def _proj_call(x2, ln_g, w_in_p, w_qst, q_g, w_qt, kv_g, w_k_p, w_vt, tabs_m, tabs_s, tabs_t,
               tabs_st, seq):
    n, d = x2.shape
    tm = PROJ_ROWS
    steps_per_seq = seq // tm
    batch = n // seq
    row = lambda i: (i, 0)
    const = lambda i: (0, 0)
    tab = lambda i: (i % steps_per_seq, 0)
    tab_t = lambda i: (0, i % steps_per_seq)
    col = lambda i: (i // steps_per_seq, i % steps_per_seq)
    bf = jnp.bfloat16
    qt_rows = MLA_HEADS * MLA_QK_PAD
    row_out = lambda w: (pl.BlockSpec((tm, w), row), jax.ShapeDtypeStruct((n, w), bf))
    col_out = lambda r: (pl.BlockSpec((r, tm), col), jax.ShapeDtypeStruct((batch * r, seq), bf))
    outs = [col_out(qt_rows), row_out(qt_rows), col_out(MLA_WIDTH), row_out(MLA_WIDTH),
            col_out(SWA_WIDTH), row_out(SWA_WIDTH), col_out(SWA_KV_HEADS * SWA_D), row_out(SWA_WIDTH)]
    return pl.pallas_call(
        _proj_kernel,
        grid=(n // tm,),
        in_specs=[
            pl.BlockSpec((tm, d), row),
            pl.BlockSpec((1, d), const),
            pl.BlockSpec(w_in_p.shape, const),
            pl.BlockSpec(w_qst.shape, const),
            pl.BlockSpec((1, MLA_Q_RANK), const),
            pl.BlockSpec(w_qt.shape, const),
            pl.BlockSpec((1, MLA_KV_RANK), const),
            pl.BlockSpec(w_k_p.shape, const),
            pl.BlockSpec(w_vt.shape, const),
        ] + [pl.BlockSpec((tm, LANES), tab)] * 4
          + [pl.BlockSpec((MLA_ROPE // 2, tm), tab_t)] * 2
          + [pl.BlockSpec((SWA_D // 2, tm), tab_t)] * 2,
        out_specs=[o[0] for o in outs],
        out_shape=[o[1] for o in outs],
        compiler_params=pltpu.CompilerParams(
            dimension_semantics=("arbitrary",),
            vmem_limit_bytes=V7X_VMEM_LIMIT_BYTES),
        name="proj",
    )(x2, ln_g, w_in_p, w_qst, q_g, w_qt, kv_g, w_k_p, w_vt, *tabs_m, *tabs_s, *tabs_t, *tabs_st)


def _tree(op, xs):
    xs = list(xs)
    while len(xs) > 1:
        xs = [op(xs[a], xs[a + 1]) if a + 1 < len(xs) else xs[a] for a in range(0, len(xs), 2)]
    return xs[0]


def _mla_kernel(qt_ref, k_ref, vt_ref, g_ref, o_ref, s_ref, p_ref, vx_ref):
    t = MLA_BLOCK
    seq = k_ref.shape[0]
    sub = SUBLANES
    qk = MLA_QK_PAD
    r = lax.broadcasted_iota(jnp.int32, (t, t), 0)
    c = lax.broadcasted_iota(jnp.int32, (t, t), 1)
    causal = r <= c
    nblk = seq // t
    units = [(hd, i) for hd in range(MLA_HEADS_PER_STEP) for i in range(nblk - 1, -1, -1)]
    state = [dict(m8=None) for _ in units]

    for hd in range(MLA_HEADS_PER_STEP):
        vx_ref[hd, 0:MLA_V, :] = vt_ref[hd * MLA_V:(hd + 1) * MLA_V, :]
        vx_ref[hd, MLA_V:, :] = jnp.ones((vx_ref.shape[1] - MLA_V, seq), vx_ref.dtype)

    def score_tile(u, j):
        hd, i = units[u]
        st = state[u]
        keys = slice(j * t, (j + 1) * t)
        s = _dot(k_ref[keys, hd * qk:(hd + 1) * qk],
                 qt_ref[hd * qk:(hd + 1) * qk, i * t:(i + 1) * t])
        if j == i:
            s = jnp.where(causal, s, NEG_INF)
        s_ref[u % 2, keys, :] = s
        m8 = _tree(jnp.maximum, [s[a * sub:(a + 1) * sub, :] for a in range(t // sub)])
        st["m8"] = m8 if st["m8"] is None else jnp.maximum(st["m8"], m8)

    def prob_tile(u, j):
        st = state[u]
        if "m" not in st:
            st["m"] = jnp.max(st["m8"], axis=0, keepdims=True)
        keys = slice(j * t, (j + 1) * t)
        p_ref[u % 3, keys, :] = jnp.exp2(s_ref[u % 2, keys, :] - st["m"]).astype(jnp.bfloat16)

    def value_matmul(u):
        hd, i = units[u]
        kv = (i + 1) * t
        state[u]["acc"] = _dot(vx_ref[hd, :, 0:kv], p_ref[u % 3, 0:kv, :])

    def store(u):
        hd, i = units[u]
        rows = slice(i * t, (i + 1) * t)
        acc = state[u]["acc"]
        out_t = acc[0:MLA_V, :] / acc[MLA_V:MLA_V + 1, :]
        gate = g_ref[rows, hd * MLA_V:(hd + 1) * MLA_V].astype(jnp.float32)
        o_ref[rows, hd * MLA_V:(hd + 1) * MLA_V] = (out_t.T * gate).astype(o_ref.dtype)
        state[u] = None

    nu = len(units)
    for j in range(units[0][1] + 1):
        score_tile(0, j)
    for u in range(nu + 2):
        if 0 <= u - 2 < nu:
            store(u - 2)
        nxt = ([functools.partial(score_tile, u + 1, j) for j in range(units[u + 1][1] + 1)]
               if u + 1 < nu else [])
        cur = [functools.partial(prob_tile, u, j) for j in range(units[u][1] + 1)] if u < nu else []
        while nxt or cur:
            if cur:
                cur.pop(0)()
            if nxt:
                nxt.pop(0)()
        if 0 <= u - 1 < nu:
            value_matmul(u - 1)


def _mla_call(qt, km, vt, gm, batch, seq):
    n = km.shape[0]
    t = MLA_BLOCK
    hps = MLA_HEADS_PER_STEP
    groups = MLA_HEADS // hps
    ones_rows = 16
    feat = lambda b, h: (b * groups + h, 0)
    tok = lambda b, h: (b, h)
    return pl.pallas_call(
        _mla_kernel,
        grid=(batch, groups),
        in_specs=[
            pl.BlockSpec((hps * MLA_QK_PAD, seq), feat),
            pl.BlockSpec((seq, hps * MLA_QK_PAD), tok),
            pl.BlockSpec((hps * MLA_V, seq), feat),
            pl.BlockSpec((seq, hps * MLA_V), tok),
        ],
        out_specs=pl.BlockSpec((seq, hps * MLA_V), tok),
        out_shape=jax.ShapeDtypeStruct((n, MLA_WIDTH), jnp.bfloat16),
        scratch_shapes=[pltpu.VMEM((2, seq, t), jnp.float32),
                        pltpu.VMEM((3, seq, t), jnp.bfloat16),
                        pltpu.VMEM((hps, MLA_V + ones_rows, seq), jnp.bfloat16)],
        compiler_params=pltpu.CompilerParams(
            dimension_semantics=("arbitrary", "arbitrary"),
            vmem_limit_bytes=V7X_VMEM_LIMIT_BYTES),
        name="mla",
    )(qt, km, vt, gm)


def _swa_out_kernel(sink_ref, qt_ref, k_ref, kh_ref, vt_ref, vth_ref, g_ref, mm_ref, x_ref,
                    w_ref, fg_ref, o_ref, ms_ref, kx_ref, vx_ref):
    w = SWA_WINDOW
    chunk = pl.program_id(1)
    rows_total = k_ref.shape[0]
    bf = jnp.bfloat16
    sub = SUBLANES

    kx_ref[0:w, :] = kh_ref[...]
    kx_ref[w:, :] = k_ref[...]
    for hk in range(SWA_KV_HEADS):
        vx_ref[hk, 0:SWA_D, 0:w] = vth_ref[hk * SWA_D:(hk + 1) * SWA_D, :]
        vx_ref[hk, 0:SWA_D, w:] = vt_ref[hk * SWA_D:(hk + 1) * SWA_D, :]
        vx_ref[hk, SWA_D:, :] = jnp.ones((vx_ref.shape[1] - SWA_D, vx_ref.shape[2]), bf)

    ki = lax.broadcasted_iota(jnp.int32, (2 * w, 2 * w), 0)
    qi = lax.broadcasted_iota(jnp.int32, (2 * w, 2 * w), 1) % w
    rel = qi + w - ki
    band = (rel >= 0) & (rel < SWA_WINDOW)
    band_first = band & ((ki >= w) | (chunk > 0))
    lane = lax.broadcasted_iota(jnp.int32, (1, 2 * w), 1)

    units = [(n, hk) for n in range(rows_total // w) for hk in range(SWA_KV_HEADS)]
    state = [dict() for _ in units]

    def sink_rows(hk):
        sink = [sink_ref[hk * SWA_GROUP + i] * LOG2E for i in range(SWA_GROUP)]
        return (jnp.where(lane < w, sink[0], sink[2]), jnp.where(lane < w, sink[1], sink[3]))

    def stage_scores(u, half):
        n, hk = units[u]
        base = hk * 2 * LANES
        qcols = slice(n * w, (n + 1) * w)
        keys = slice(n * w, (n + 2) * w)
        qt = jnp.concatenate([qt_ref[base:base + LANES, qcols],
                              qt_ref[base + LANES:base + 2 * LANES, qcols]], axis=1)
        valid = band_first if n == 0 else band
        st = state[u]
        lanes = slice(base + half * LANES, base + (half + 1) * LANES)
        s = jnp.where(valid, _dot(kx_ref[keys, lanes], qt), NEG_INF)
        m8 = _tree(jnp.maximum, [s[a * sub:(a + 1) * sub, :] for a in range(2 * w // sub)])
        st.setdefault("s", {})[half] = s
        st.setdefault("m", {})[half] = jnp.maximum(jnp.max(m8, axis=0, keepdims=True),
                                                   sink_rows(hk)[half])

    def stage_values(u, half):
        n, hk = units[u]
        keys = slice(n * w, (n + 2) * w)
        st = state[u]
        p = jnp.exp2(st["s"].pop(half) - st["m"][half]).astype(bf)
        st.setdefault("acc", {})[half] = _dot(vx_ref[hk, :, keys], p)

    def stage_store(u):
        n, hk = units[u]
        base = hk * 2 * LANES
        qcols = slice(n * w, (n + 1) * w)
        sinks = sink_rows(hk)
        st = state[u]
        outs = []
        for half in range(2):
            acc = st["acc"][half]
            den = acc[SWA_D:SWA_D + 1, :] + jnp.exp2(sinks[half] - st["m"][half])
            outs.append(acc[0:SWA_D, :] / den)
        g = g_ref[qcols, base:base + 2 * LANES].astype(jnp.float32)
        for ch in range(2):
            o_t = jnp.concatenate([outs[0][:, ch * w:(ch + 1) * w],
                                   outs[1][:, ch * w:(ch + 1) * w]], axis=0)
            ms_ref[qcols, base + ch * LANES:base + (ch + 1) * LANES] = (
                o_t.T * g[:, ch * LANES:(ch + 1) * LANES]).astype(bf)
        state[u] = None

    t = SWA_OUT_SUBTILE
    units_per_tile = (t // w) * SWA_KV_HEADS

    d_model = x_ref.shape[1]
    grp = MXU_COLS
    out_state = {}

    def out_group(i, gc):
        rows = slice(i * t, (i + 1) * t)
        cols = slice(gc * grp, (gc + 1) * grp)
        y = (x_ref[rows, cols] + _dot(mm_ref[rows, :], w_ref[0:MLA_WIDTH, cols])
             + _dot(ms_ref[rows, :], w_ref[MLA_WIDTH:, cols]))
        out_state.setdefault(i, []).append(y)

    def out_norm(i):
        rows = slice(i * t, (i + 1) * t)
        ys = out_state.pop(i)
        ssq = _tree(jnp.add, [jnp.sum(y * y, axis=-1, keepdims=True) for y in ys])
        scale = lax.rsqrt(ssq / d_model + NORM_EPS)
        for gc, y in enumerate(ys):
            cols = slice(gc * grp, (gc + 1) * grp)
            o_ref[rows, cols] = y * scale * fg_ref[:, cols]

    nu = len(units)
    lag = SWA_STORE_LAG
    gsz = SWA_UNITS_PER_STEP
    pending = []
    tasks_per_step = SWA_OUT_TASKS_PER_STEP

    def retire(count):
        for _ in range(min(count, len(pending))):
            pending.pop(0)()

    for step in range(nu // gsz + lag):
        for u in range(step * gsz, (step + 1) * gsz):
            if u < nu:
                stage_scores(u, 0)
                stage_scores(u, 1)
        retire(tasks_per_step // 2)
        for u in range((step - 1) * gsz, step * gsz):
            if 0 <= u < nu:
                stage_values(u, 0)
                stage_values(u, 1)
        retire(tasks_per_step - tasks_per_step // 2)
        for u in range((step - lag) * gsz, (step - lag + 1) * gsz):
            if 0 <= u < nu:
                stage_store(u)
                if (u + 1) % units_per_tile == 0:
                    i = u // units_per_tile
                    pending += [functools.partial(out_group, i, gc) for gc in range(d_model // grp)]
                    pending.append(functools.partial(out_norm, i))
    for task in pending:
        task()


def _swa_out_call(sinks, qst, ks4, vst, gs, mm, x2, w_o, fg, batch, seq):
    n, d = x2.shape
    r = SWA_OUT_ROWS
    w = SWA_WINDOW
    cps = seq // r
    bpc = r // w
    bps = seq // w
    v_rows = SWA_KV_HEADS * SWA_D
    ones_rows = 16
    row = lambda b, c: (b * cps + c, 0)
    halo = lambda b, c: (b * bps + jnp.maximum(c * bpc - 1, 0), 0)
    feat = lambda b, c: (b, c)
    feat_halo = lambda b, c: (b, jnp.maximum(c * bpc - 1, 0))
    const = lambda b, c: (0, 0)
    wide = pl.BlockSpec((r, SWA_WIDTH), row)
    return pl.pallas_call(
        _swa_out_kernel,
        grid=(batch, cps),
        in_specs=[
            pl.BlockSpec(memory_space=pltpu.SMEM),
            pl.BlockSpec((SWA_WIDTH, r), feat),
            wide,
            pl.BlockSpec((w, SWA_WIDTH), halo),
            pl.BlockSpec((v_rows, r), feat),
            pl.BlockSpec((v_rows, w), feat_halo),
            wide,
            pl.BlockSpec((r, MLA_WIDTH), row),
            pl.BlockSpec((r, d), row),
            pl.BlockSpec(w_o.shape, const),
            pl.BlockSpec((1, d), const),
        ],
        out_specs=pl.BlockSpec((r, d), row),
        out_shape=jax.ShapeDtypeStruct((n, d), jnp.float32),
        scratch_shapes=[pltpu.VMEM((r, SWA_WIDTH), jnp.bfloat16),
                        pltpu.VMEM((r + w, SWA_WIDTH), jnp.bfloat16),
                        pltpu.VMEM((SWA_KV_HEADS, SWA_D + ones_rows, r + w), jnp.bfloat16)],
        compiler_params=pltpu.CompilerParams(
            dimension_semantics=("arbitrary", "arbitrary"),
            vmem_limit_bytes=V7X_VMEM_LIMIT_BYTES),
        name="swa_out",
    )(sinks, qst, ks4, ks4, vst, vst, gs, mm, x2, w_o, fg)


def _pair_rope_pieces(src, dst, n_heads, dim):
    half = dim // 2
    out = []
    for pair in range(n_heads // 2):
        for which in range(2):
            head = src + (2 * pair + which) * dim
            chunk = dst + pair * 2 * dim
            out.append((head, chunk + which * half, half))
            out.append((head + half, chunk + 2 * half + which * half, half))
    return out


def _w_in_pieces():
    pieces, qs_pieces, src, dst = [], [], 0, 0
    for name, width in (("c_q", MLA_Q_RANK), ("c_kv", MLA_KV_RANK), ("k_rope", MLA_ROPE),
                        ("g_mla", MLA_WIDTH), ("q_s", SWA_WIDTH),
                        ("k_s", SWA_KV_HEADS * SWA_D), ("v_s", SWA_KV_HEADS * SWA_D),
                        ("g_swa", SWA_WIDTH)):
        if name == "k_rope":
            half = MLA_ROPE // 2
            pieces += [(src, dst, half), (src, dst + half, half),
                       (src + half, dst + 2 * half, half), (src + half, dst + 3 * half, half)]
            dst += 2 * MLA_ROPE
        elif name == "q_s":
            qs_pieces += _pair_rope_pieces(src, 0, SWA_Q_HEADS, SWA_D)
        elif name == "k_s":
            pieces += _pair_rope_pieces(src, dst, SWA_KV_HEADS, SWA_D)
            dst += width
        else:
            pieces.append((src, dst, width))
            dst += width
        src += width
    assert dst == PROJ_COLS
    return pieces, qs_pieces


def _gather_cols(src_ref, pieces, n_dst_cols):
    rows, n_src = src_ref.shape
    lane = lax.broadcasted_iota(jnp.int32, (rows, LANES), 1)
    loaded, rolled = {}, {}

    def load(k):
        if k not in loaded:
            width = min(LANES, n_src - k * LANES)
            v = src_ref[:, k * LANES:k * LANES + width]
            if width < LANES:
                v = jnp.concatenate([v, jnp.zeros((rows, LANES - width), v.dtype)], axis=1)
            loaded[k] = v
        return loaded[k]

    def shifted(k, shift):
        if shift == 0:
            return load(k)
        if (k, shift) not in rolled:
            rolled[(k, shift)] = pltpu.roll(load(k), shift, 1)
        return rolled[(k, shift)]

    chunks = []
    for d in range(n_dst_cols // LANES):
        acc = None
        for s0, d0, w in pieces:
            lo, hi = max(d0, d * LANES), min(d0 + w, (d + 1) * LANES)
            while lo < hi:
                s_abs = s0 + lo - d0
                k, a = divmod(s_abs, LANES)
                n = min(hi - lo, LANES - a)
                a_dst = lo - d * LANES
                val = shifted(k, (a_dst - a) % LANES)
                if n == LANES:
                    acc = val
                else:
                    mask = (lane >= a_dst) & (lane < a_dst + n)
                    acc = jnp.where(mask, val, 0.0 if acc is None else acc)
                lo += n
        chunks.append(acc)
    return chunks


def _gather_rows(src_ref, pieces, d):
    parts = []
    for s0, d0, w in pieces:
        lo, hi = max(d0, d * LANES), min(d0 + w, (d + 1) * LANES)
        if lo < hi:
            parts.append((lo, src_ref[s0 + lo - d0:s0 + hi - d0, :]))
    parts.sort(key=lambda t: t[0])
    assert sum(p.shape[0] for _, p in parts) == LANES
    return jnp.concatenate([p for _, p in parts], axis=0) if len(parts) > 1 else parts[0][1]


def _prep_big_kernel(wint_ref, wout_ref, winp_ref, wqst_ref, wo_ref):
    pieces, qs_pieces = _w_in_pieces()
    for d in range(PROJ_COLS // LANES):
        winp_ref[:, d * LANES:(d + 1) * LANES] = _gather_rows(wint_ref, pieces, d).T.astype(winp_ref.dtype)
    for d in range(SWA_WIDTH // LANES):
        wqst_ref[d * LANES:(d + 1) * LANES, :] = _gather_rows(wint_ref, qs_pieces, d).astype(wqst_ref.dtype)
    wo_ref[...] = wout_ref[...].astype(wo_ref.dtype)


def _prep_small_kernel(wq_ref, wkv_ref, wqt_ref, wk_ref, wvt_ref):
    q_pieces = ([(hd * MLA_QK, hd * MLA_NOPE, MLA_NOPE) for hd in range(MLA_HEADS)]
                + [(hd * MLA_QK + MLA_NOPE, MLA_HEADS * MLA_NOPE + hd * MLA_ROPE, MLA_ROPE)
                   for hd in range(MLA_HEADS)])
    wq = jnp.concatenate(_gather_cols(wq_ref, q_pieces, MLA_HEADS * MLA_QK), axis=1)
    wqt_ref[...] = wq.T.astype(wqt_ref.dtype)
    per_head = MLA_NOPE + MLA_V
    for hd in range(MLA_HEADS):
        wk_ref[:, hd * MLA_NOPE:(hd + 1) * MLA_NOPE] = (
            wkv_ref[:, hd * per_head:hd * per_head + MLA_NOPE].astype(wk_ref.dtype))
        wvt_ref[hd * MLA_V:(hd + 1) * MLA_V, :] = (
            wkv_ref[:, hd * per_head + MLA_NOPE:(hd + 1) * per_head].T.astype(wvt_ref.dtype))


def _prep_weights(w_in, w_q_up, w_kv_up, w_out):
    bf = jnp.bfloat16
    d, n_in = w_in.shape
    rb = PREP_ROWS
    params = pltpu.CompilerParams(dimension_semantics=("arbitrary",),
                                  vmem_limit_bytes=V7X_VMEM_LIMIT_BYTES)
    w_in_p, w_qst, w_o = pl.pallas_call(
        _prep_big_kernel,
        grid=(d // rb,),
        in_specs=[pl.BlockSpec((n_in, rb), lambda i: (0, i)),
                  pl.BlockSpec((rb, w_out.shape[1]), lambda i: (i, 0))],
        out_specs=[pl.BlockSpec((rb, PROJ_COLS), lambda i: (i, 0)),
                   pl.BlockSpec((SWA_WIDTH, rb), lambda i: (0, i)),
                   pl.BlockSpec((rb, w_out.shape[1]), lambda i: (i, 0))],
        out_shape=[jax.ShapeDtypeStruct((d, PROJ_COLS), bf),
                   jax.ShapeDtypeStruct((SWA_WIDTH, d), bf),
                   jax.ShapeDtypeStruct(w_out.shape, bf)],
        compiler_params=params,
        name="prep_big",
    )(w_in.T, w_out)
    w_qt, w_k_p, w_vt = pl.pallas_call(
        _prep_small_kernel,
        out_shape=[jax.ShapeDtypeStruct((MLA_HEADS * MLA_QK, MLA_Q_RANK), bf),
                   jax.ShapeDtypeStruct((MLA_KV_RANK, MLA_HEADS * MLA_NOPE), bf),
                   jax.ShapeDtypeStruct((MLA_WIDTH, MLA_KV_RANK), bf)],
        name="prep_small",
    )(w_q_up, w_kv_up)
    return w_in_p, w_qst, w_qt, w_k_p, w_vt, w_o


def kernel(x, ln_mix, w_in, q_a_norm, w_q_up, kv_a_norm, w_kv_up, attn_sinks, w_out, final_norm):
    batch, seq, d = x.shape
    depth = ln_mix.shape[0]
    assert depth == 1, "final norm is fused into the single layer's output kernel"
    assert seq % MLA_BLOCK == 0 and seq % PROJ_ROWS == 0 and seq % SWA_OUT_ROWS == 0
    tables = {dim: _rope_tables(seq, dim) for dim in {MLA_ROPE, SWA_D}}
    tabs_t, tabs_m = tables[MLA_ROPE]
    tabs_st, tabs_s = tables[SWA_D]
    x2 = x.reshape(batch * seq, d)
    w_in_p, w_qst, w_qt, w_k_p, w_vt, w_o = _prep_weights(w_in[0], w_q_up[0], w_kv_up[0], w_out[0])
    qt, km, vt, gm, qst, ks4, vst, gs = _proj_call(
        x2, ln_mix[0].reshape(1, -1), w_in_p, w_qst, q_a_norm[0].reshape(1, -1), w_qt,
        kv_a_norm[0].reshape(1, -1), w_k_p, w_vt, tabs_m, tabs_s, tabs_t, tabs_st, seq)
    mm = _mla_call(qt, km, vt, gm, batch, seq)
    out = _swa_out_call(attn_sinks[0], qst, ks4, vst, gs, mm, x2, w_o,
                        final_norm.reshape(1, -1), batch, seq)
    return out.reshape(batch, seq, d)
```

```python
import functools
import math

import jax
import jax.numpy as jnp
from jax import lax
from jax.experimental import pallas as pl
from jax.experimental.pallas import tpu as pltpu

ROPE_THETA = 10000.0
NORM_EPS = 1e-6
NEG_INF = -1e30
LOG2E = 1.4426950408889634

MLA_HEADS = 4
MLA_NOPE = 128
MLA_ROPE = 64
MLA_V = 128
MLA_Q_RANK = 256
MLA_KV_RANK = 128
MLA_QK = MLA_NOPE + MLA_ROPE
MLA_WIDTH = MLA_HEADS * MLA_V

SWA_Q_HEADS = 8
SWA_KV_HEADS = 2
SWA_D = 64
SWA_WINDOW = 128
SWA_GROUP = SWA_Q_HEADS // SWA_KV_HEADS
SWA_WIDTH = SWA_Q_HEADS * SWA_D

LANES = 128
SUBLANES = 8
MXU_COLS = 2 * LANES
MLA_QK_PAD = MXU_COLS
V7X_VMEM_LIMIT_BYTES = 56 * 1024 * 1024

PROJ_ROWS = 1024
MLA_BLOCK = 256
MLA_GROUP_TILES = 8
MLA_HEADS_PER_STEP = 2
SWA_OUT_ROWS = 1024
SWA_OUT_SUBTILE = 256
SWA_UNITS_PER_STEP = 4
SWA_OUT_TASKS_PER_STEP = 4
SWA_STORE_LAG = 2
PREP_ROWS = 256


def _rope_tables(seq, dim):
    assert 2 * dim == LANES
    half = dim // 2
    inv_freq = 1.0 / (ROPE_THETA ** (jnp.arange(0, dim, 2, dtype=jnp.float32) / dim))
    f32 = jnp.float32
    ct, st, c_lane, s_lane = pl.pallas_call(
        _rope_table_kernel,
        out_shape=[jax.ShapeDtypeStruct((half, seq), f32), jax.ShapeDtypeStruct((half, seq), f32),
                   jax.ShapeDtypeStruct((seq, LANES), f32), jax.ShapeDtypeStruct((seq, LANES), f32)],
        name="rope_tables",
    )(inv_freq.reshape(half, 1))
    return (ct, st), (c_lane, s_lane)


def _rope_table_kernel(inv_ref, ct_ref, st_ref, cl_ref, sl_ref):
    half, seq = ct_ref.shape
    pos = lax.broadcasted_iota(jnp.int32, (half, seq), 1).astype(jnp.float32)
    ang = pos * inv_ref[...]
    c, s = jnp.cos(ang), jnp.sin(ang)
    ct_ref[...] = c
    st_ref[...] = s
    cl_ref[...] = jnp.concatenate([c, c, c, c], axis=0).T
    sl_ref[...] = jnp.concatenate([-s, -s, s, s], axis=0).T


def _rope(x, c, s):
    return x * c + pltpu.roll(x, LANES // 2, 1) * s


def _rms(x, g):
    return x * lax.rsqrt(jnp.mean(x * x, axis=-1, keepdims=True) + NORM_EPS) * g


def _silu(g):
    return g / (1.0 + jnp.exp(-g))


def _dot(a, b):
    return jnp.dot(a, b, preferred_element_type=jnp.float32)


def _dot_nt(a, b):
    return lax.dot_general(a, b, (((1,), (1,)), ((), ())),
                           preferred_element_type=jnp.float32)


PROJ_COLS = 1792


def _proj_kernel(x_ref, ln_ref, win_ref, wqst_ref, qg_ref, wqt_ref, kvg_ref, wkv_ref, wvt_ref,
                 cm_ref, sm_ref, cs_ref, ss_ref, ct_ref, st_ref, cst_ref, sst_ref,
                 qt_ref, km_ref, vt_ref, gm_ref, qst_ref, ks_ref, vst_ref, gs_ref):
    bf = jnp.bfloat16
    x = x_ref[...]
    h = _rms(x, ln_ref[...]).astype(bf)

    cm, sm, cs, ss = cm_ref[...], sm_ref[...], cs_ref[...], ss_ref[...]
    lane = lax.broadcasted_iota(jnp.int32, (x.shape[0], LANES), 1)
    first = (lane % (LANES // 2)) < (LANES // 4)
    grp = MXU_COLS

    def cols(a):
        return win_ref[:, a:a + grp]

    c_q = _dot(h, cols(0))
    lat_b = _dot(h, cols(256))
    cqn = _rms(c_q, qg_ref[...]).astype(bf)
    ckvn = _rms(lat_b[:, 0:MLA_KV_RANK], kvg_ref[...]).astype(bf)
    kr = _rope(lat_b[:, LANES:2 * LANES], cm, sm)
    kr_first = jnp.where(first, kr, 0.0).astype(bf)

    s_scale = LOG2E / math.sqrt(SWA_D)
    qst = _dot_nt(wqst_ref[...], h)
    cst2 = jnp.concatenate([cst_ref[...], cst_ref[...]], axis=0)
    sst2 = jnp.concatenate([sst_ref[...], sst_ref[...]], axis=0)
    for ch in range(SWA_WIDTH // LANES):
        x1 = qst[ch * LANES:ch * LANES + SWA_D, :]
        x2 = qst[ch * LANES + SWA_D:(ch + 1) * LANES, :]
        qst_ref[ch * LANES:ch * LANES + SWA_D, :] = ((x1 * cst2 - x2 * sst2) * s_scale).astype(bf)
        qst_ref[ch * LANES + SWA_D:(ch + 1) * LANES, :] = ((x2 * cst2 + x1 * sst2) * s_scale).astype(bf)

    q_scale = LOG2E / math.sqrt(MLA_QK)
    nope_w = MLA_HEADS * MLA_NOPE
    half = MLA_ROPE // 2
    qt = _dot_nt(wqt_ref[...], cqn)
    ct, st = ct_ref[...], st_ref[...]
    zeros = jnp.zeros((half, x.shape[0]), bf)
    for hd in range(MLA_HEADS):
        qt_ref[hd * grp:hd * grp + LANES, :] = (qt[hd * MLA_NOPE:(hd + 1) * MLA_NOPE, :] * q_scale).astype(bf)
        x1 = qt[nope_w + hd * MLA_ROPE:nope_w + hd * MLA_ROPE + half, :]
        x2 = qt[nope_w + hd * MLA_ROPE + half:nope_w + (hd + 1) * MLA_ROPE, :]
        base = hd * grp + LANES
        qt_ref[base:base + half, :] = ((x1 * ct - x2 * st) * q_scale).astype(bf)
        qt_ref[base + half:base + 2 * half, :] = zeros
        qt_ref[base + 2 * half:base + 3 * half, :] = ((x2 * ct + x1 * st) * q_scale).astype(bf)
        qt_ref[base + 3 * half:base + 4 * half, :] = zeros
    for g2 in range(nope_w // grp):
        kn = _dot(ckvn, wkv_ref[:, g2 * grp:(g2 + 1) * grp])
        for j in range(2):
            hd = 2 * g2 + j
            km_ref[:, hd * grp:hd * grp + LANES] = kn[:, j * LANES:(j + 1) * LANES].astype(bf)
            km_ref[:, hd * grp + LANES:(hd + 1) * grp] = kr_first
    for g2 in range(MLA_WIDTH // grp):
        gm_ref[:, g2 * grp:(g2 + 1) * grp] = _silu(_dot(h, cols(512 + g2 * grp))).astype(bf)

    kv_s = _dot(h, cols(1024))
    k01 = _rope(kv_s[:, 0:LANES], cs, ss)
    ks_ref[:, 0 * LANES:1 * LANES] = jnp.where(first, k01, 0.0).astype(bf)
    ks_ref[:, 1 * LANES:2 * LANES] = jnp.where(first, 0.0, pltpu.roll(k01, LANES // 4, 1)).astype(bf)
    ks_ref[:, 2 * LANES:3 * LANES] = jnp.where(first, pltpu.roll(k01, 3 * LANES // 4, 1), 0.0).astype(bf)
    ks_ref[:, 3 * LANES:4 * LANES] = jnp.where(first, 0.0, k01).astype(bf)
    vst_ref[...] = kv_s[:, LANES:2 * LANES].T.astype(bf)

    for g2 in range(SWA_WIDTH // grp):
        gs_ref[:, g2 * grp:(g2 + 1) * grp] = _silu(_dot(h, cols(1280 + g2 * grp))).astype(bf)

    vt_ref[...] = _dot_nt(wvt_ref[...], ckvn).astype(bf)


def _proj_call(x2, ln_g, w_in_p, w_qst, q_g, w_qt, kv_g, w_k_p, w_vt, tabs_m, tabs_s, tabs_t,
               tabs_st, seq):
    n, d = x2.shape
    tm = PROJ_ROWS
    steps_per_seq = seq // tm
    batch = n // seq
    row = lambda i: (i, 0)
    const = lambda i: (0, 0)
    tab = lambda i: (i % steps_per_seq, 0)
    tab_t = lambda i: (0, i % steps_per_seq)
    col = lambda i: (i // steps_per_seq, i % steps_per_seq)
    bf = jnp.bfloat16
    qt_rows = MLA_HEADS * MLA_QK_PAD
    row_out = lambda w: (pl.BlockSpec((tm, w), row), jax.ShapeDtypeStruct((n, w), bf))
    col_out = lambda r: (pl.BlockSpec((r, tm), col), jax.ShapeDtypeStruct((batch * r, seq), bf))
    outs = [col_out(qt_rows), row_out(qt_rows), col_out(MLA_WIDTH), row_out(MLA_WIDTH),
            col_out(SWA_WIDTH), row_out(SWA_WIDTH), col_out(SWA_KV_HEADS * SWA_D), row_out(SWA_WIDTH)]
    return pl.pallas_call(
        _proj_kernel,
        grid=(n // tm,),
        in_specs=[
            pl.BlockSpec((tm, d), row),
            pl.BlockSpec((1, d), const),
            pl.BlockSpec(w_in_p.shape, const),
            pl.BlockSpec(w_qst.shape, const),
            pl.BlockSpec((1, MLA_Q_RANK), const),
            pl.BlockSpec(w_qt.shape, const),
            pl.BlockSpec((1, MLA_KV_RANK), const),
            pl.BlockSpec(w_k_p.shape, const),
            pl.BlockSpec(w_vt.shape, const),
        ] + [pl.BlockSpec((tm, LANES), tab)] * 4
          + [pl.BlockSpec((MLA_ROPE // 2, tm), tab_t)] * 2
          + [pl.BlockSpec((SWA_D // 2, tm), tab_t)] * 2,
        out_specs=[o[0] for o in outs],
        out_shape=[o[1] for o in outs],
        compiler_params=pltpu.CompilerParams(
            dimension_semantics=("arbitrary",),
            vmem_limit_bytes=V7X_VMEM_LIMIT_BYTES),
        name="proj",
    )(x2, ln_g, w_in_p, w_qst, q_g, w_qt, kv_g, w_k_p, w_vt, *tabs_m, *tabs_s, *tabs_t, *tabs_st)


def _tree(op, xs):
    xs = list(xs)
    while len(xs) > 1:
        xs = [op(xs[a], xs[a + 1]) if a + 1 < len(xs) else xs[a] for a in range(0, len(xs), 2)]
    return xs[0]


def _mla_block_groups(nblk):
    groups, cur, load = [], [], 0
    for i in range(nblk - 1, -1, -1):
        if load + i + 1 > MLA_GROUP_TILES:
            groups.append(cur)
            cur, load = [], 0
        cur.append(i)
        load += i + 1
    groups.append(cur)
    return groups


def _mla_kernel(qt_ref, k_ref, vt_ref, g_ref, o_ref, s_ref, p_ref, vx_ref):
    t = MLA_BLOCK
    seq = k_ref.shape[0]
    sub = SUBLANES
    qk = MLA_QK_PAD
    r = lax.broadcasted_iota(jnp.int32, (t, t), 0)
    c = lax.broadcasted_iota(jnp.int32, (t, t), 1)
    causal = r <= c
    nblk = seq // t
    units = []
    for hd in range(MLA_HEADS_PER_STEP):
        for blocks in _mla_block_groups(nblk):
            members, off = [], 0
            for i in blocks:
                members.append((hd, i, off))
                off += (i + 1) * t
            assert off <= s_ref.shape[1]
            units.append(members)
    state = [[dict(m8=None) for _ in members] for members in units]

    for hd in range(MLA_HEADS_PER_STEP):
        vx_ref[hd, 0:MLA_V, :] = vt_ref[hd * MLA_V:(hd + 1) * MLA_V, :]
        vx_ref[hd, MLA_V:, :] = jnp.ones((vx_ref.shape[1] - MLA_V, seq), vx_ref.dtype)

    def score_tile(u, b, j):
        hd, i, off = units[u][b]
        st = state[u][b]
        s = _dot(k_ref[j * t:(j + 1) * t, hd * qk:(hd + 1) * qk],
                 qt_ref[hd * qk:(hd + 1) * qk, i * t:(i + 1) * t])
        if j == i:
            s = jnp.where(causal, s, NEG_INF)
        s_ref[u % 2, off + j * t:off + (j + 1) * t, :] = s
        m8 = _tree(jnp.maximum, [s[a * sub:(a + 1) * sub, :] for a in range(t // sub)])
        st["m8"] = m8 if st["m8"] is None else jnp.maximum(st["m8"], m8)

    def prob_tile(u, b, j):
        _, _, off = units[u][b]
        st = state[u][b]
        if "m" not in st:
            st["m"] = jnp.max(st["m8"], axis=0, keepdims=True)
        rows = slice(off + j * t, off + (j + 1) * t)
        p_ref[u % 3, rows, :] = jnp.exp2(s_ref[u % 2, rows, :] - st["m"]).astype(jnp.bfloat16)

    def value_matmul(u):
        for b, (hd, i, off) in enumerate(units[u]):
            kv = (i + 1) * t
            state[u][b]["acc"] = _dot(vx_ref[hd, :, 0:kv], p_ref[u % 3, off:off + kv, :])

    def store(u):
        for b, (hd, i, _) in enumerate(units[u]):
            rows = slice(i * t, (i + 1) * t)
            acc = state[u][b]["acc"]
            out_t = acc[0:MLA_V, :] / acc[MLA_V:MLA_V + 1, :]
            gate = g_ref[rows, hd * MLA_V:(hd + 1) * MLA_V].astype(jnp.float32)
            o_ref[rows, hd * MLA_V:(hd + 1) * MLA_V] = (out_t.T * gate).astype(o_ref.dtype)
        state[u] = None

    def tiles(fn, u):
        return [functools.partial(fn, u, b, j) for b, (_, i, _) in enumerate(units[u]) for j in range(i + 1)]

    nu = len(units)
    for task in tiles(score_tile, 0):
        task()
    for u in range(nu + 2):
        if 0 <= u - 2 < nu:
            store(u - 2)
        nxt = tiles(score_tile, u + 1) if u + 1 < nu else []
        cur = tiles(prob_tile, u) if u < nu else []
        while nxt or cur:
            if cur:
                cur.pop(0)()
            if nxt:
                nxt.pop(0)()
        if 0 <= u - 1 < nu:
            value_matmul(u - 1)


def _mla_call(qt, km, vt, gm, batch, seq):
    n = km.shape[0]
    t = MLA_BLOCK
    hps = MLA_HEADS_PER_STEP
    groups = MLA_HEADS // hps
    ones_rows = 16
    feat = lambda b, h: (b * groups + h, 0)
    tok = lambda b, h: (b, h)
    return pl.pallas_call(
        _mla_kernel,
        grid=(batch, groups),
        in_specs=[
            pl.BlockSpec((hps * MLA_QK_PAD, seq), feat),
            pl.BlockSpec((seq, hps * MLA_QK_PAD), tok),
            pl.BlockSpec((hps * MLA_V, seq), feat),
            pl.BlockSpec((seq, hps * MLA_V), tok),
        ],
        out_specs=pl.BlockSpec((seq, hps * MLA_V), tok),
        out_shape=jax.ShapeDtypeStruct((n, MLA_WIDTH), jnp.bfloat16),
        scratch_shapes=[pltpu.VMEM((2, MLA_GROUP_TILES * t, t), jnp.float32),
                        pltpu.VMEM((3, MLA_GROUP_TILES * t, t), jnp.bfloat16),
                        pltpu.VMEM((hps, MLA_V + ones_rows, seq), jnp.bfloat16)],
        compiler_params=pltpu.CompilerParams(
            dimension_semantics=("arbitrary", "arbitrary"),
            vmem_limit_bytes=V7X_VMEM_LIMIT_BYTES),
        name="mla",
    )(qt, km, vt, gm)


def _swa_out_kernel(sink_ref, qt_ref, k_ref, kh_ref, vt_ref, vth_ref, g_ref, mm_ref, x_ref,
                    w_ref, fg_ref, o_ref, ms_ref, kx_ref, vx_ref):
    w = SWA_WINDOW
    chunk = pl.program_id(1)
    rows_total = k_ref.shape[0]
    bf = jnp.bfloat16
    sub = SUBLANES

    kx_ref[0:w, :] = kh_ref[...]
    kx_ref[w:, :] = k_ref[...]
    for hk in range(SWA_KV_HEADS):
        vx_ref[hk, 0:SWA_D, 0:w] = vth_ref[hk * SWA_D:(hk + 1) * SWA_D, :]
        vx_ref[hk, 0:SWA_D, w:] = vt_ref[hk * SWA_D:(hk + 1) * SWA_D, :]
        vx_ref[hk, SWA_D:, :] = jnp.ones((vx_ref.shape[1] - SWA_D, vx_ref.shape[2]), bf)

    ki = lax.broadcasted_iota(jnp.int32, (2 * w, 2 * w), 0)
    qi = lax.broadcasted_iota(jnp.int32, (2 * w, 2 * w), 1) % w
    rel = qi + w - ki
    band = (rel >= 0) & (rel < SWA_WINDOW)
    band_first = band & ((ki >= w) | (chunk > 0))
    lane = lax.broadcasted_iota(jnp.int32, (1, 2 * w), 1)

    units = [(n, hk) for n in range(rows_total // w) for hk in range(SWA_KV_HEADS)]
    state = [dict() for _ in units]

    def sink_rows(hk):
        sink = [sink_ref[hk * SWA_GROUP + i] * LOG2E for i in range(SWA_GROUP)]
        return (jnp.where(lane < w, sink[0], sink[2]), jnp.where(lane < w, sink[1], sink[3]))

    def stage_scores(u, half):
        n, hk = units[u]
        base = hk * 2 * LANES
        qcols = slice(n * w, (n + 1) * w)
        keys = slice(n * w, (n + 2) * w)
        qt = jnp.concatenate([qt_ref[base:base + LANES, qcols],
                              qt_ref[base + LANES:base + 2 * LANES, qcols]], axis=1)
        valid = band_first if n == 0 else band
        st = state[u]
        lanes = slice(base + half * LANES, base + (half + 1) * LANES)
        s = jnp.where(valid, _dot(kx_ref[keys, lanes], qt), NEG_INF)
        m8 = _tree(jnp.maximum, [s[a * sub:(a + 1) * sub, :] for a in range(2 * w // sub)])
        st.setdefault("s", {})[half] = s
        st.setdefault("m", {})[half] = jnp.maximum(jnp.max(m8, axis=0, keepdims=True),
                                                   sink_rows(hk)[half])

    def stage_values(u, half):
        n, hk = units[u]
        keys = slice(n * w, (n + 2) * w)
        st = state[u]
        p = jnp.exp2(st["s"].pop(half) - st["m"][half]).astype(bf)
        st.setdefault("acc", {})[half] = _dot(vx_ref[hk, :, keys], p)

    def stage_store(u):
        n, hk = units[u]
        base = hk * 2 * LANES
        qcols = slice(n * w, (n + 1) * w)
        sinks = sink_rows(hk)
        st = state[u]
        outs = []
        for half in range(2):
            acc = st["acc"][half]
            den = acc[SWA_D:SWA_D + 1, :] + jnp.exp2(sinks[half] - st["m"][half])
            outs.append(acc[0:SWA_D, :] / den)
        g = g_ref[qcols, base:base + 2 * LANES].astype(jnp.float32)
        for ch in range(2):
            o_t = jnp.concatenate([outs[0][:, ch * w:(ch + 1) * w],
                                   outs[1][:, ch * w:(ch + 1) * w]], axis=0)
            ms_ref[qcols, base + ch * LANES:base + (ch + 1) * LANES] = (
                o_t.T * g[:, ch * LANES:(ch + 1) * LANES]).astype(bf)
        state[u] = None

    t = SWA_OUT_SUBTILE
    units_per_tile = (t // w) * SWA_KV_HEADS

    d_model = x_ref.shape[1]
    grp = MXU_COLS
    out_state = {}

    def out_group(i, gc):
        rows = slice(i * t, (i + 1) * t)
        cols = slice(gc * grp, (gc + 1) * grp)
        y = (x_ref[rows, cols] + _dot(mm_ref[rows, :], w_ref[0:MLA_WIDTH, cols])
             + _dot(ms_ref[rows, :], w_ref[MLA_WIDTH:, cols]))
        out_state.setdefault(i, []).append(y)

    def out_norm(i):
        rows = slice(i * t, (i + 1) * t)
        ys = out_state.pop(i)
        ssq = _tree(jnp.add, [jnp.sum(y * y, axis=-1, keepdims=True) for y in ys])
        scale = lax.rsqrt(ssq / d_model + NORM_EPS)
        for gc, y in enumerate(ys):
            cols = slice(gc * grp, (gc + 1) * grp)
            o_ref[rows, cols] = y * scale * fg_ref[:, cols]

    nu = len(units)
    lag = SWA_STORE_LAG
    gsz = SWA_UNITS_PER_STEP
    pending = []
    tasks_per_step = SWA_OUT_TASKS_PER_STEP

    def retire(count):
        for _ in range(min(count, len(pending))):
            pending.pop(0)()

    for step in range(nu // gsz + lag):
        for u in range(step * gsz, (step + 1) * gsz):
            if u < nu:
                stage_scores(u, 0)
                stage_scores(u, 1)
        retire(tasks_per_step // 2)
        for u in range((step - 1) * gsz, step * gsz):
            if 0 <= u < nu:
                stage_values(u, 0)
                stage_values(u, 1)
        retire(tasks_per_step - tasks_per_step // 2)
        for u in range((step - lag) * gsz, (step - lag + 1) * gsz):
            if 0 <= u < nu:
                stage_store(u)
                if (u + 1) % units_per_tile == 0:
                    i = u // units_per_tile
                    pending += [functools.partial(out_group, i, gc) for gc in range(d_model // grp)]
                    pending.append(functools.partial(out_norm, i))
    for task in pending:
        task()


def _swa_out_call(sinks, qst, ks4, vst, gs, mm, x2, w_o, fg, batch, seq):
    n, d = x2.shape
    r = SWA_OUT_ROWS
    w = SWA_WINDOW
    cps = seq // r
    bpc = r // w
    bps = seq // w
    v_rows = SWA_KV_HEADS * SWA_D
    ones_rows = 16
    row = lambda b, c: (b * cps + c, 0)
    halo = lambda b, c: (b * bps + jnp.maximum(c * bpc - 1, 0), 0)
    feat = lambda b, c: (b, c)
    feat_halo = lambda b, c: (b, jnp.maximum(c * bpc - 1, 0))
    const = lambda b, c: (0, 0)
    wide = pl.BlockSpec((r, SWA_WIDTH), row)
    return pl.pallas_call(
        _swa_out_kernel,
        grid=(batch, cps),
        in_specs=[
            pl.BlockSpec(memory_space=pltpu.SMEM),
            pl.BlockSpec((SWA_WIDTH, r), feat),
            wide,
            pl.BlockSpec((w, SWA_WIDTH), halo),
            pl.BlockSpec((v_rows, r), feat),
            pl.BlockSpec((v_rows, w), feat_halo),
            wide,
            pl.BlockSpec((r, MLA_WIDTH), row),
            pl.BlockSpec((r, d), row),
            pl.BlockSpec(w_o.shape, const),
            pl.BlockSpec((1, d), const),
        ],
        out_specs=pl.BlockSpec((r, d), row),
        out_shape=jax.ShapeDtypeStruct((n, d), jnp.float32),
        scratch_shapes=[pltpu.VMEM((r, SWA_WIDTH), jnp.bfloat16),
                        pltpu.VMEM((r + w, SWA_WIDTH), jnp.bfloat16),
                        pltpu.VMEM((SWA_KV_HEADS, SWA_D + ones_rows, r + w), jnp.bfloat16)],
        compiler_params=pltpu.CompilerParams(
            dimension_semantics=("arbitrary", "arbitrary"),
            vmem_limit_bytes=V7X_VMEM_LIMIT_BYTES),
        name="swa_out",
    )(sinks, qst, ks4, ks4, vst, vst, gs, mm, x2, w_o, fg)


def _pair_rope_pieces(src, dst, n_heads, dim):
    half = dim // 2
    out = []
    for pair in range(n_heads // 2):
        for which in range(2):
            head = src + (2 * pair + which) * dim
            chunk = dst + pair * 2 * dim
            out.append((head, chunk + which * half, half))
            out.append((head + half, chunk + 2 * half + which * half, half))
    return out


def _w_in_pieces():
    pieces, qs_pieces, src, dst = [], [], 0, 0
    for name, width in (("c_q", MLA_Q_RANK), ("c_kv", MLA_KV_RANK), ("k_rope", MLA_ROPE),
                        ("g_mla", MLA_WIDTH), ("q_s", SWA_WIDTH),
                        ("k_s", SWA_KV_HEADS * SWA_D), ("v_s", SWA_KV_HEADS * SWA_D),
                        ("g_swa", SWA_WIDTH)):
        if name == "k_rope":
            half = MLA_ROPE // 2
            pieces += [(src, dst, half), (src, dst + half, half),
                       (src + half, dst + 2 * half, half), (src + half, dst + 3 * half, half)]
            dst += 2 * MLA_ROPE
        elif name == "q_s":
            qs_pieces += _pair_rope_pieces(src, 0, SWA_Q_HEADS, SWA_D)
        elif name == "k_s":
            pieces += _pair_rope_pieces(src, dst, SWA_KV_HEADS, SWA_D)
            dst += width
        else:
            pieces.append((src, dst, width))
            dst += width
        src += width
    assert dst == PROJ_COLS
    return pieces, qs_pieces


def _gather_cols(src_ref, pieces, n_dst_cols):
    rows, n_src = src_ref.shape
    lane = lax.broadcasted_iota(jnp.int32, (rows, LANES), 1)
    loaded, rolled = {}, {}

    def load(k):
        if k not in loaded:
            width = min(LANES, n_src - k * LANES)
            v = src_ref[:, k * LANES:k * LANES + width]
            if width < LANES:
                v = jnp.concatenate([v, jnp.zeros((rows, LANES - width), v.dtype)], axis=1)
            loaded[k] = v
        return loaded[k]

    def shifted(k, shift):
        if shift == 0:
            return load(k)
        if (k, shift) not in rolled:
            rolled[(k, shift)] = pltpu.roll(load(k), shift, 1)
        return rolled[(k, shift)]

    chunks = []
    for d in range(n_dst_cols // LANES):
        acc = None
        for s0, d0, w in pieces:
            lo, hi = max(d0, d * LANES), min(d0 + w, (d + 1) * LANES)
            while lo < hi:
                s_abs = s0 + lo - d0
                k, a = divmod(s_abs, LANES)
                n = min(hi - lo, LANES - a)
                a_dst = lo - d * LANES
                val = shifted(k, (a_dst - a) % LANES)
                if n == LANES:
                    acc = val
                else:
                    mask = (lane >= a_dst) & (lane < a_dst + n)
                    acc = jnp.where(mask, val, 0.0 if acc is None else acc)
                lo += n
        chunks.append(acc)
    return chunks


def _gather_rows(src_ref, pieces, d):
    parts = []
    for s0, d0, w in pieces:
        lo, hi = max(d0, d * LANES), min(d0 + w, (d + 1) * LANES)
        if lo < hi:
            parts.append((lo, src_ref[s0 + lo - d0:s0 + hi - d0, :]))
    parts.sort(key=lambda t: t[0])
    assert sum(p.shape[0] for _, p in parts) == LANES
    return jnp.concatenate([p for _, p in parts], axis=0) if len(parts) > 1 else parts[0][1]


def _prep_big_kernel(wint_ref, wout_ref, winp_ref, wqst_ref, wo_ref):
    pieces, qs_pieces = _w_in_pieces()
    for d in range(PROJ_COLS // LANES):
        winp_ref[:, d * LANES:(d + 1) * LANES] = _gather_rows(wint_ref, pieces, d).T.astype(winp_ref.dtype)
    for d in range(SWA_WIDTH // LANES):
        wqst_ref[d * LANES:(d + 1) * LANES, :] = _gather_rows(wint_ref, qs_pieces, d).astype(wqst_ref.dtype)
    wo_ref[...] = wout_ref[...].astype(wo_ref.dtype)


def _prep_small_kernel(wq_ref, wkv_ref, wqt_ref, wk_ref, wvt_ref):
    q_pieces = ([(hd * MLA_QK, hd * MLA_NOPE, MLA_NOPE) for hd in range(MLA_HEADS)]
                + [(hd * MLA_QK + MLA_NOPE, MLA_HEADS * MLA_NOPE + hd * MLA_ROPE, MLA_ROPE)
                   for hd in range(MLA_HEADS)])
    wq = jnp.concatenate(_gather_cols(wq_ref, q_pieces, MLA_HEADS * MLA_QK), axis=1)
    wqt_ref[...] = wq.T.astype(wqt_ref.dtype)
    per_head = MLA_NOPE + MLA_V
    for hd in range(MLA_HEADS):
        wk_ref[:, hd * MLA_NOPE:(hd + 1) * MLA_NOPE] = (
            wkv_ref[:, hd * per_head:hd * per_head + MLA_NOPE].astype(wk_ref.dtype))
        wvt_ref[hd * MLA_V:(hd + 1) * MLA_V, :] = (
            wkv_ref[:, hd * per_head + MLA_NOPE:(hd + 1) * per_head].T.astype(wvt_ref.dtype))


def _prep_weights(w_in, w_q_up, w_kv_up, w_out):
    bf = jnp.bfloat16
    d, n_in = w_in.shape
    rb = PREP_ROWS
    params = pltpu.CompilerParams(dimension_semantics=("arbitrary",),
                                  vmem_limit_bytes=V7X_VMEM_LIMIT_BYTES)
    w_in_p, w_qst, w_o = pl.pallas_call(
        _prep_big_kernel,
        grid=(d // rb,),
        in_specs=[pl.BlockSpec((n_in, rb), lambda i: (0, i)),
                  pl.BlockSpec((rb, w_out.shape[1]), lambda i: (i, 0))],
        out_specs=[pl.BlockSpec((rb, PROJ_COLS), lambda i: (i, 0)),
                   pl.BlockSpec((SWA_WIDTH, rb), lambda i: (0, i)),
                   pl.BlockSpec((rb, w_out.shape[1]), lambda i: (i, 0))],
        out_shape=[jax.ShapeDtypeStruct((d, PROJ_COLS), bf),
                   jax.ShapeDtypeStruct((SWA_WIDTH, d), bf),
                   jax.ShapeDtypeStruct(w_out.shape, bf)],
        compiler_params=params,
        name="prep_big",
    )(w_in.T, w_out)
    w_qt, w_k_p, w_vt = pl.pallas_call(
        _prep_small_kernel,
        out_shape=[jax.ShapeDtypeStruct((MLA_HEADS * MLA_QK, MLA_Q_RANK), bf),
                   jax.ShapeDtypeStruct((MLA_KV_RANK, MLA_HEADS * MLA_NOPE), bf),
                   jax.ShapeDtypeStruct((MLA_WIDTH, MLA_KV_RANK), bf)],
        name="prep_small",
    )(w_q_up, w_kv_up)
    return w_in_p, w_qst, w_qt, w_k_p, w_vt, w_o


def kernel(x, ln_mix, w_in, q_a_norm, w_q_up, kv_a_norm, w_kv_up, attn_sinks, w_out, final_norm):
    batch, seq, d = x.shape
    depth = ln_mix.shape[0]
    assert depth == 1, "final norm is fused into the single layer's output kernel"
    assert seq % MLA_BLOCK == 0 and seq % PROJ_ROWS == 0 and seq % SWA_OUT_ROWS == 0
    tables = {dim: _rope_tables(seq, dim) for dim in {MLA_ROPE, SWA_D}}
    tabs_t, tabs_m = tables[MLA_ROPE]
    tabs_st, tabs_s = tables[SWA_D]
    x2 = x.reshape(batch * seq, d)
    w_in_p, w_qst, w_qt, w_k_p, w_vt, w_o = _prep_weights(w_in[0], w_q_up[0], w_kv_up[0], w_out[0])
    qt, km, vt, gm, qst, ks4, vst, gs = _proj_call(
        x2, ln_mix[0].reshape(1, -1), w_in_p, w_qst, q_a_norm[0].reshape(1, -1), w_qt,
        kv_a_norm[0].reshape(1, -1), w_k_p, w_vt, tabs_m, tabs_s, tabs_t, tabs_st, seq)
    mm = _mla_call(qt, km, vt, gm, batch, seq)
    out = _swa_out_call(attn_sinks[0], qst, ks4, vst, gs, mm, x2, w_o,
                        final_norm.reshape(1, -1), batch, seq)
    return out.reshape(batch, seq, d)
```

```python
import functools
import math

import jax
import jax.numpy as jnp
from jax import lax
from jax.experimental import pallas as pl
from jax.experimental.pallas import tpu as pltpu

ROPE_THETA = 10000.0
NORM_EPS = 1e-6
NEG_INF = -1e30
LOG2E = 1.4426950408889634

MLA_HEADS = 4
MLA_NOPE = 128
MLA_ROPE = 64
MLA_V = 128
MLA_Q_RANK = 256
MLA_KV_RANK = 128
MLA_QK = MLA_NOPE + MLA_ROPE
MLA_WIDTH = MLA_HEADS * MLA_V

SWA_Q_HEADS = 8
SWA_KV_HEADS = 2
SWA_D = 64
SWA_WINDOW = 128
SWA_GROUP = SWA_Q_HEADS // SWA_KV_HEADS
SWA_WIDTH = SWA_Q_HEADS * SWA_D

LANES = 128
SUBLANES = 8
MXU_COLS = 2 * LANES
MLA_QK_PAD = MXU_COLS
MLA_K_COLS = MLA_HEADS * MLA_NOPE + LANES
V7X_VMEM_LIMIT_BYTES = 56 * 1024 * 1024

PROJ_ROWS = 1024
MLA_BLOCK = 256
MLA_GROUP_TILES = 8
MLA_HEADS_PER_STEP = 2
SWA_OUT_ROWS = 1024
SWA_OUT_SUBTILE = 256
SWA_UNITS_PER_STEP = 4
SWA_OUT_TASKS_PER_STEP = 4
SWA_STORE_LAG = 2
PREP_ROWS = 256


def _rope_tables(seq, dim):
    assert 2 * dim == LANES
    half = dim // 2
    inv_freq = 1.0 / (ROPE_THETA ** (jnp.arange(0, dim, 2, dtype=jnp.float32) / dim))
    f32 = jnp.float32
    ct, st, c_lane, s_lane = pl.pallas_call(
        _rope_table_kernel,
        out_shape=[jax.ShapeDtypeStruct((half, seq), f32), jax.ShapeDtypeStruct((half, seq), f32),
                   jax.ShapeDtypeStruct((seq, LANES), f32), jax.ShapeDtypeStruct((seq, LANES), f32)],
        name="rope_tables",
    )(inv_freq.reshape(half, 1))
    return (ct, st), (c_lane, s_lane)


def _rope_table_kernel(inv_ref, ct_ref, st_ref, cl_ref, sl_ref):
    half, seq = ct_ref.shape
    pos = lax.broadcasted_iota(jnp.int32, (half, seq), 1).astype(jnp.float32)
    ang = pos * inv_ref[...]
    c, s = jnp.cos(ang), jnp.sin(ang)
    ct_ref[...] = c
    st_ref[...] = s
    cl_ref[...] = jnp.concatenate([c, c, c, c], axis=0).T
    sl_ref[...] = jnp.concatenate([-s, -s, s, s], axis=0).T


def _rope(x, c, s):
    return x * c + pltpu.roll(x, LANES // 2, 1) * s


def _rms(x, g):
    return x * lax.rsqrt(jnp.mean(x * x, axis=-1, keepdims=True) + NORM_EPS) * g


def _silu(g):
    return g / (1.0 + jnp.exp(-g))


def _dot(a, b):
    return jnp.dot(a, b, preferred_element_type=jnp.float32)


def _dot_nt(a, b):
    return lax.dot_general(a, b, (((1,), (1,)), ((), ())),
                           preferred_element_type=jnp.float32)


PROJ_COLS = 1792


def _proj_kernel(x_ref, ln_ref, win_ref, wqst_ref, qg_ref, wqt_ref, kvg_ref, wkv_ref, wvt_ref,
                 cm_ref, sm_ref, cs_ref, ss_ref, ct_ref, st_ref, cst_ref, sst_ref,
                 qt_ref, km_ref, vt_ref, gm_ref, qst_ref, ks_ref, vst_ref, gs_ref):
    bf = jnp.bfloat16
    x = x_ref[...]
    h = _rms(x, ln_ref[...]).astype(bf)

    cm, sm, cs, ss = cm_ref[...], sm_ref[...], cs_ref[...], ss_ref[...]
    lane = lax.broadcasted_iota(jnp.int32, (x.shape[0], LANES), 1)
    first = (lane % (LANES // 2)) < (LANES // 4)
    grp = MXU_COLS

    def cols(a):
        return win_ref[:, a:a + grp]

    c_q = _dot(h, cols(0))
    lat_b = _dot(h, cols(256))
    cqn = _rms(c_q, qg_ref[...]).astype(bf)
    ckvn = _rms(lat_b[:, 0:MLA_KV_RANK], kvg_ref[...]).astype(bf)
    kr = _rope(lat_b[:, LANES:2 * LANES], cm, sm)
    kr_first = jnp.where(first, kr, 0.0).astype(bf)

    s_scale = LOG2E / math.sqrt(SWA_D)
    qst = _dot_nt(wqst_ref[...], h)
    cst2 = jnp.concatenate([cst_ref[...], cst_ref[...]], axis=0)
    sst2 = jnp.concatenate([sst_ref[...], sst_ref[...]], axis=0)
    for ch in range(SWA_WIDTH // LANES):
        x1 = qst[ch * LANES:ch * LANES + SWA_D, :]
        x2 = qst[ch * LANES + SWA_D:(ch + 1) * LANES, :]
        qst_ref[ch * LANES:ch * LANES + SWA_D, :] = ((x1 * cst2 - x2 * sst2) * s_scale).astype(bf)
        qst_ref[ch * LANES + SWA_D:(ch + 1) * LANES, :] = ((x2 * cst2 + x1 * sst2) * s_scale).astype(bf)

    q_scale = LOG2E / math.sqrt(MLA_QK)
    nope_w = MLA_HEADS * MLA_NOPE
    half = MLA_ROPE // 2
    qt = _dot_nt(wqt_ref[...], cqn)
    ct, st = ct_ref[...], st_ref[...]
    zeros = jnp.zeros((half, x.shape[0]), bf)
    for hd in range(MLA_HEADS):
        qt_ref[hd * grp:hd * grp + LANES, :] = (qt[hd * MLA_NOPE:(hd + 1) * MLA_NOPE, :] * q_scale).astype(bf)
        x1 = qt[nope_w + hd * MLA_ROPE:nope_w + hd * MLA_ROPE + half, :]
        x2 = qt[nope_w + hd * MLA_ROPE + half:nope_w + (hd + 1) * MLA_ROPE, :]
        base = hd * grp + LANES
        qt_ref[base:base + half, :] = ((x1 * ct - x2 * st) * q_scale).astype(bf)
        qt_ref[base + half:base + 2 * half, :] = zeros
        qt_ref[base + 2 * half:base + 3 * half, :] = ((x2 * ct + x1 * st) * q_scale).astype(bf)
        qt_ref[base + 3 * half:base + 4 * half, :] = zeros
    for g2 in range(nope_w // grp):
        km_ref[:, g2 * grp:(g2 + 1) * grp] = _dot(ckvn, wkv_ref[:, g2 * grp:(g2 + 1) * grp]).astype(bf)
    km_ref[:, nope_w:nope_w + LANES] = kr_first
    for g2 in range(MLA_WIDTH // grp):
        gm_ref[:, g2 * grp:(g2 + 1) * grp] = _silu(_dot(h, cols(512 + g2 * grp))).astype(bf)

    kv_s = _dot(h, cols(1024))
    k01 = _rope(kv_s[:, 0:LANES], cs, ss)
    ks_ref[:, 0 * LANES:1 * LANES] = jnp.where(first, k01, 0.0).astype(bf)
    ks_ref[:, 1 * LANES:2 * LANES] = jnp.where(first, 0.0, pltpu.roll(k01, LANES // 4, 1)).astype(bf)
    ks_ref[:, 2 * LANES:3 * LANES] = jnp.where(first, pltpu.roll(k01, 3 * LANES // 4, 1), 0.0).astype(bf)
    ks_ref[:, 3 * LANES:4 * LANES] = jnp.where(first, 0.0, k01).astype(bf)
    vst_ref[...] = kv_s[:, LANES:2 * LANES].T.astype(bf)

    for g2 in range(SWA_WIDTH // grp):
        gs_ref[:, g2 * grp:(g2 + 1) * grp] = _silu(_dot(h, cols(1280 + g2 * grp))).astype(bf)

    vt_ref[...] = _dot_nt(wvt_ref[...], ckvn).astype(bf)


def _proj_call(x2, ln_g, w_in_p, w_qst, q_g, w_qt, kv_g, w_k_p, w_vt, tabs_m, tabs_s, tabs_t,
               tabs_st, seq):
    n, d = x2.shape
    tm = PROJ_ROWS
    steps_per_seq = seq // tm
    batch = n // seq
    row = lambda i: (i, 0)
    const = lambda i: (0, 0)
    tab = lambda i: (i % steps_per_seq, 0)
    tab_t = lambda i: (0, i % steps_per_seq)
    col = lambda i: (i // steps_per_seq, i % steps_per_seq)
    bf = jnp.bfloat16
    qt_rows = MLA_HEADS * MLA_QK_PAD
    row_out = lambda w: (pl.BlockSpec((tm, w), row), jax.ShapeDtypeStruct((n, w), bf))
    col_out = lambda r: (pl.BlockSpec((r, tm), col), jax.ShapeDtypeStruct((batch * r, seq), bf))
    outs = [col_out(qt_rows), row_out(MLA_K_COLS), col_out(MLA_WIDTH), row_out(MLA_WIDTH),
            col_out(SWA_WIDTH), row_out(SWA_WIDTH), col_out(SWA_KV_HEADS * SWA_D), row_out(SWA_WIDTH)]
    return pl.pallas_call(
        _proj_kernel,
        grid=(n // tm,),
        in_specs=[
            pl.BlockSpec((tm, d), row),
            pl.BlockSpec((1, d), const),
            pl.BlockSpec(w_in_p.shape, const),
            pl.BlockSpec(w_qst.shape, const),
            pl.BlockSpec((1, MLA_Q_RANK), const),
            pl.BlockSpec(w_qt.shape, const),
            pl.BlockSpec((1, MLA_KV_RANK), const),
            pl.BlockSpec(w_k_p.shape, const),
            pl.BlockSpec(w_vt.shape, const),
        ] + [pl.BlockSpec((tm, LANES), tab)] * 4
          + [pl.BlockSpec((MLA_ROPE // 2, tm), tab_t)] * 2
          + [pl.BlockSpec((SWA_D // 2, tm), tab_t)] * 2,
        out_specs=[o[0] for o in outs],
        out_shape=[o[1] for o in outs],
        compiler_params=pltpu.CompilerParams(
            dimension_semantics=("arbitrary",),
            vmem_limit_bytes=V7X_VMEM_LIMIT_BYTES),
        name="proj",
    )(x2, ln_g, w_in_p, w_qst, q_g, w_qt, kv_g, w_k_p, w_vt, *tabs_m, *tabs_s, *tabs_t, *tabs_st)


def _tree(op, xs):
    xs = list(xs)
    while len(xs) > 1:
        xs = [op(xs[a], xs[a + 1]) if a + 1 < len(xs) else xs[a] for a in range(0, len(xs), 2)]
    return xs[0]


def _mla_block_groups(nblk):
    groups, cur, load = [], [], 0
    for i in range(nblk - 1, -1, -1):
        if load + i + 1 > MLA_GROUP_TILES:
            groups.append(cur)
            cur, load = [], 0
        cur.append(i)
        load += i + 1
    groups.append(cur)
    return groups


def _mla_kernel(qt_ref, k_ref, kr_ref, vt_ref, g_ref, o_ref, s_ref, p_ref, vx_ref):
    t = MLA_BLOCK
    seq = k_ref.shape[0]
    sub = SUBLANES
    qk = MLA_QK_PAD
    r = lax.broadcasted_iota(jnp.int32, (t, t), 0)
    c = lax.broadcasted_iota(jnp.int32, (t, t), 1)
    causal = r <= c
    nblk = seq // t
    units = []
    for hd in range(MLA_HEADS_PER_STEP):
        for blocks in _mla_block_groups(nblk):
            members, off = [], 0
            for i in blocks:
                members.append((hd, i, off))
                off += (i + 1) * t
            assert off <= s_ref.shape[1]
            units.append(members)
    state = [[dict(m8=None) for _ in members] for members in units]

    for hd in range(MLA_HEADS_PER_STEP):
        vx_ref[hd, 0:MLA_V, :] = vt_ref[hd * MLA_V:(hd + 1) * MLA_V, :]
        vx_ref[hd, MLA_V:, :] = jnp.ones((vx_ref.shape[1] - MLA_V, seq), vx_ref.dtype)

    def score_tile(u, b, j):
        hd, i, off = units[u][b]
        st = state[u][b]
        keys = slice(j * t, (j + 1) * t)
        k_tile = jnp.concatenate([k_ref[keys, hd * MLA_NOPE:(hd + 1) * MLA_NOPE], kr_ref[keys, :]], axis=1)
        s = _dot(k_tile, qt_ref[hd * qk:(hd + 1) * qk, i * t:(i + 1) * t])
        if j == i:
            s = jnp.where(causal, s, NEG_INF)
        s_ref[u % 2, off + j * t:off + (j + 1) * t, :] = s
        m8 = _tree(jnp.maximum, [s[a * sub:(a + 1) * sub, :] for a in range(t // sub)])
        st["m8"] = m8 if st["m8"] is None else jnp.maximum(st["m8"], m8)

    def prob_tile(u, b, j):
        _, _, off = units[u][b]
        st = state[u][b]
        if "m" not in st:
            st["m"] = jnp.max(st["m8"], axis=0, keepdims=True)
        rows = slice(off + j * t, off + (j + 1) * t)
        p_ref[u % 3, rows, :] = jnp.exp2(s_ref[u % 2, rows, :] - st["m"]).astype(jnp.bfloat16)

    def value_matmul(u):
        for b, (hd, i, off) in enumerate(units[u]):
            kv = (i + 1) * t
            state[u][b]["acc"] = _dot(vx_ref[hd, :, 0:kv], p_ref[u % 3, off:off + kv, :])

    def store(u):
        for b, (hd, i, _) in enumerate(units[u]):
            rows = slice(i * t, (i + 1) * t)
            acc = state[u][b]["acc"]
            out_t = acc[0:MLA_V, :] / acc[MLA_V:MLA_V + 1, :]
            gate = g_ref[rows, hd * MLA_V:(hd + 1) * MLA_V].astype(jnp.float32)
            o_ref[rows, hd * MLA_V:(hd + 1) * MLA_V] = (out_t.T * gate).astype(o_ref.dtype)
        state[u] = None

    def tiles(fn, u):
        return [functools.partial(fn, u, b, j) for b, (_, i, _) in enumerate(units[u]) for j in range(i + 1)]

    nu = len(units)
    for task in tiles(score_tile, 0):
        task()
    for u in range(nu + 2):
        if 0 <= u - 2 < nu:
            store(u - 2)
        nxt = tiles(score_tile, u + 1) if u + 1 < nu else []
        cur = tiles(prob_tile, u) if u < nu else []
        while nxt or cur:
            if cur:
                cur.pop(0)()
            if nxt:
                nxt.pop(0)()
        if 0 <= u - 1 < nu:
            value_matmul(u - 1)


def _mla_call(qt, km, vt, gm, batch, seq):
    n = km.shape[0]
    t = MLA_BLOCK
    hps = MLA_HEADS_PER_STEP
    groups = MLA_HEADS // hps
    ones_rows = 16
    feat = lambda b, h: (b * groups + h, 0)
    tok = lambda b, h: (b, h)
    return pl.pallas_call(
        _mla_kernel,
        grid=(batch, groups),
        in_specs=[
            pl.BlockSpec((hps * MLA_QK_PAD, seq), feat),
            pl.BlockSpec((seq, hps * MLA_NOPE), tok),
            pl.BlockSpec((seq, LANES), lambda b, h: (b, MLA_HEADS * MLA_NOPE // LANES)),
            pl.BlockSpec((hps * MLA_V, seq), feat),
            pl.BlockSpec((seq, hps * MLA_V), tok),
        ],
        out_specs=pl.BlockSpec((seq, hps * MLA_V), tok),
        out_shape=jax.ShapeDtypeStruct((n, MLA_WIDTH), jnp.bfloat16),
        scratch_shapes=[pltpu.VMEM((2, MLA_GROUP_TILES * t, t), jnp.float32),
                        pltpu.VMEM((3, MLA_GROUP_TILES * t, t), jnp.bfloat16),
                        pltpu.VMEM((hps, MLA_V + ones_rows, seq), jnp.bfloat16)],
        compiler_params=pltpu.CompilerParams(
            dimension_semantics=("arbitrary", "arbitrary"),
            vmem_limit_bytes=V7X_VMEM_LIMIT_BYTES),
        name="mla",
    )(qt, km, km, vt, gm)


def _swa_out_kernel(sink_ref, qt_ref, k_ref, kh_ref, vt_ref, vth_ref, g_ref, mm_ref, x_ref,
                    w_ref, fg_ref, o_ref, ms_ref, kx_ref, vx_ref):
    w = SWA_WINDOW
    chunk = pl.program_id(1)
    rows_total = k_ref.shape[0]
    bf = jnp.bfloat16
    sub = SUBLANES

    kx_ref[0:w, :] = kh_ref[...]
    kx_ref[w:, :] = k_ref[...]
    for hk in range(SWA_KV_HEADS):
        vx_ref[hk, 0:SWA_D, 0:w] = vth_ref[hk * SWA_D:(hk + 1) * SWA_D, :]
        vx_ref[hk, 0:SWA_D, w:] = vt_ref[hk * SWA_D:(hk + 1) * SWA_D, :]
        vx_ref[hk, SWA_D:, :] = jnp.ones((vx_ref.shape[1] - SWA_D, vx_ref.shape[2]), bf)

    ki = lax.broadcasted_iota(jnp.int32, (2 * w, 2 * w), 0)
    qi = lax.broadcasted_iota(jnp.int32, (2 * w, 2 * w), 1) % w
    rel = qi + w - ki
    band = (rel >= 0) & (rel < SWA_WINDOW)
    band_first = band & ((ki >= w) | (chunk > 0))
    lane = lax.broadcasted_iota(jnp.int32, (1, 2 * w), 1)

    units = [(n, hk) for n in range(rows_total // w) for hk in range(SWA_KV_HEADS)]
    state = [dict() for _ in units]

    def sink_rows(hk):
        sink = [sink_ref[hk * SWA_GROUP + i] * LOG2E for i in range(SWA_GROUP)]
        return (jnp.where(lane < w, sink[0], sink[2]), jnp.where(lane < w, sink[1], sink[3]))

    def stage_scores(u, half):
        n, hk = units[u]
        base = hk * 2 * LANES
        qcols = slice(n * w, (n + 1) * w)
        keys = slice(n * w, (n + 2) * w)
        qt = jnp.concatenate([qt_ref[base:base + LANES, qcols],
                              qt_ref[base + LANES:base + 2 * LANES, qcols]], axis=1)
        valid = band_first if n == 0 else band
        st = state[u]
        lanes = slice(base + half * LANES, base + (half + 1) * LANES)
        s = jnp.where(valid, _dot(kx_ref[keys, lanes], qt), NEG_INF)
        m8 = _tree(jnp.maximum, [s[a * sub:(a + 1) * sub, :] for a in range(2 * w // sub)])
        st.setdefault("s", {})[half] = s
        st.setdefault("m", {})[half] = jnp.maximum(jnp.max(m8, axis=0, keepdims=True),
                                                   sink_rows(hk)[half])

    def stage_values(u, half):
        n, hk = units[u]
        keys = slice(n * w, (n + 2) * w)
        st = state[u]
        p = jnp.exp2(st["s"].pop(half) - st["m"][half]).astype(bf)
        st.setdefault("acc", {})[half] = _dot(vx_ref[hk, :, keys], p)

    def stage_store(u):
        n, hk = units[u]
        base = hk * 2 * LANES
        qcols = slice(n * w, (n + 1) * w)
        sinks = sink_rows(hk)
        st = state[u]
        outs = []
        for half in range(2):
            acc = st["acc"][half]
            den = acc[SWA_D:SWA_D + 1, :] + jnp.exp2(sinks[half] - st["m"][half])
            outs.append(acc[0:SWA_D, :] / den)
        g = g_ref[qcols, base:base + 2 * LANES].astype(jnp.float32)
        for ch in range(2):
            o_t = jnp.concatenate([outs[0][:, ch * w:(ch + 1) * w],
                                   outs[1][:, ch * w:(ch + 1) * w]], axis=0)
            ms_ref[qcols, base + ch * LANES:base + (ch + 1) * LANES] = (
                o_t.T * g[:, ch * LANES:(ch + 1) * LANES]).astype(bf)
        state[u] = None

    t = SWA_OUT_SUBTILE
    units_per_tile = (t // w) * SWA_KV_HEADS

    d_model = x_ref.shape[1]
    grp = MXU_COLS
    out_state = {}

    def out_group(i, gc):
        rows = slice(i * t, (i + 1) * t)
        cols = slice(gc * grp, (gc + 1) * grp)
        y = (x_ref[rows, cols] + _dot(mm_ref[rows, :], w_ref[0:MLA_WIDTH, cols])
             + _dot(ms_ref[rows, :], w_ref[MLA_WIDTH:, cols]))
        out_state.setdefault(i, []).append(y)

    def out_norm(i):
        rows = slice(i * t, (i + 1) * t)
        ys = out_state.pop(i)
        ssq = _tree(jnp.add, [jnp.sum(y * y, axis=-1, keepdims=True) for y in ys])
        scale = lax.rsqrt(ssq / d_model + NORM_EPS)
        for gc, y in enumerate(ys):
            cols = slice(gc * grp, (gc + 1) * grp)
            o_ref[rows, cols] = y * scale * fg_ref[:, cols]

    nu = len(units)
    lag = SWA_STORE_LAG
    gsz = SWA_UNITS_PER_STEP
    pending = []
    tasks_per_step = SWA_OUT_TASKS_PER_STEP

    def retire(count):
        for _ in range(min(count, len(pending))):
            pending.pop(0)()

    for step in range(nu // gsz + lag):
        for u in range(step * gsz, (step + 1) * gsz):
            if u < nu:
                stage_scores(u, 0)
                stage_scores(u, 1)
        retire(tasks_per_step // 2)
        for u in range((step - 1) * gsz, step * gsz):
            if 0 <= u < nu:
                stage_values(u, 0)
                stage_values(u, 1)
        retire(tasks_per_step - tasks_per_step // 2)
        for u in range((step - lag) * gsz, (step - lag + 1) * gsz):
            if 0 <= u < nu:
                stage_store(u)
                if (u + 1) % units_per_tile == 0:
                    i = u // units_per_tile
                    pending += [functools.partial(out_group, i, gc) for gc in range(d_model // grp)]
                    pending.append(functools.partial(out_norm, i))
    for task in pending:
        task()


def _swa_out_call(sinks, qst, ks4, vst, gs, mm, x2, w_o, fg, batch, seq):
    n, d = x2.shape
    r = SWA_OUT_ROWS
    w = SWA_WINDOW
    cps = seq // r
    bpc = r // w
    bps = seq // w
    v_rows = SWA_KV_HEADS * SWA_D
    ones_rows = 16
    row = lambda b, c: (b * cps + c, 0)
    halo = lambda b, c: (b * bps + jnp.maximum(c * bpc - 1, 0), 0)
    feat = lambda b, c: (b, c)
    feat_halo = lambda b, c: (b, jnp.maximum(c * bpc - 1, 0))
    const = lambda b, c: (0, 0)
    wide = pl.BlockSpec((r, SWA_WIDTH), row)
    return pl.pallas_call(
        _swa_out_kernel,
        grid=(batch, cps),
        in_specs=[
            pl.BlockSpec(memory_space=pltpu.SMEM),
            pl.BlockSpec((SWA_WIDTH, r), feat),
            wide,
            pl.BlockSpec((w, SWA_WIDTH), halo),
            pl.BlockSpec((v_rows, r), feat),
            pl.BlockSpec((v_rows, w), feat_halo),
            wide,
            pl.BlockSpec((r, MLA_WIDTH), row),
            pl.BlockSpec((r, d), row),
            pl.BlockSpec(w_o.shape, const),
            pl.BlockSpec((1, d), const),
        ],
        out_specs=pl.BlockSpec((r, d), row),
        out_shape=jax.ShapeDtypeStruct((n, d), jnp.float32),
        scratch_shapes=[pltpu.VMEM((r, SWA_WIDTH), jnp.bfloat16),
                        pltpu.VMEM((r + w, SWA_WIDTH), jnp.bfloat16),
                        pltpu.VMEM((SWA_KV_HEADS, SWA_D + ones_rows, r + w), jnp.bfloat16)],
        compiler_params=pltpu.CompilerParams(
            dimension_semantics=("arbitrary", "arbitrary"),
            vmem_limit_bytes=V7X_VMEM_LIMIT_BYTES),
        name="swa_out",
    )(sinks, qst, ks4, ks4, vst, vst, gs, mm, x2, w_o, fg)


def _pair_rope_pieces(src, dst, n_heads, dim):
    half = dim // 2
    out = []
    for pair in range(n_heads // 2):
        for which in range(2):
            head = src + (2 * pair + which) * dim
            chunk = dst + pair * 2 * dim
            out.append((head, chunk + which * half, half))
            out.append((head + half, chunk + 2 * half + which * half, half))
    return out


def _w_in_pieces():
    pieces, qs_pieces, src, dst = [], [], 0, 0
    for name, width in (("c_q", MLA_Q_RANK), ("c_kv", MLA_KV_RANK), ("k_rope", MLA_ROPE),
                        ("g_mla", MLA_WIDTH), ("q_s", SWA_WIDTH),
                        ("k_s", SWA_KV_HEADS * SWA_D), ("v_s", SWA_KV_HEADS * SWA_D),
                        ("g_swa", SWA_WIDTH)):
        if name == "k_rope":
            half = MLA_ROPE // 2
            pieces += [(src, dst, half), (src, dst + half, half),
                       (src + half, dst + 2 * half, half), (src + half, dst + 3 * half, half)]
            dst += 2 * MLA_ROPE
        elif name == "q_s":
            qs_pieces += _pair_rope_pieces(src, 0, SWA_Q_HEADS, SWA_D)
        elif name == "k_s":
            pieces += _pair_rope_pieces(src, dst, SWA_KV_HEADS, SWA_D)
            dst += width
        else:
            pieces.append((src, dst, width))
            dst += width
        src += width
    assert dst == PROJ_COLS
    return pieces, qs_pieces


def _gather_cols(src_ref, pieces, n_dst_cols):
    rows, n_src = src_ref.shape
    lane = lax.broadcasted_iota(jnp.int32, (rows, LANES), 1)
    loaded, rolled = {}, {}

    def load(k):
        if k not in loaded:
            width = min(LANES, n_src - k * LANES)
            v = src_ref[:, k * LANES:k * LANES + width]
            if width < LANES:
                v = jnp.concatenate([v, jnp.zeros((rows, LANES - width), v.dtype)], axis=1)
            loaded[k] = v
        return loaded[k]

    def shifted(k, shift):
        if shift == 0:
            return load(k)
        if (k, shift) not in rolled:
            rolled[(k, shift)] = pltpu.roll(load(k), shift, 1)
        return rolled[(k, shift)]

    chunks = []
    for d in range(n_dst_cols // LANES):
        acc = None
        for s0, d0, w in pieces:
            lo, hi = max(d0, d * LANES), min(d0 + w, (d + 1) * LANES)
            while lo < hi:
                s_abs = s0 + lo - d0
                k, a = divmod(s_abs, LANES)
                n = min(hi - lo, LANES - a)
                a_dst = lo - d * LANES
                val = shifted(k, (a_dst - a) % LANES)
                if n == LANES:
                    acc = val
                else:
                    mask = (lane >= a_dst) & (lane < a_dst + n)
                    acc = jnp.where(mask, val, 0.0 if acc is None else acc)
                lo += n
        chunks.append(acc)
    return chunks


def _gather_rows(src_ref, pieces, d):
    parts = []
    for s0, d0, w in pieces:
        lo, hi = max(d0, d * LANES), min(d0 + w, (d + 1) * LANES)
        if lo < hi:
            parts.append((lo, src_ref[s0 + lo - d0:s0 + hi - d0, :]))
    parts.sort(key=lambda t: t[0])
    assert sum(p.shape[0] for _, p in parts) == LANES
    return jnp.concatenate([p for _, p in parts], axis=0) if len(parts) > 1 else parts[0][1]


def _prep_big_kernel(wint_ref, wout_ref, winp_ref, wqst_ref, wo_ref):
    pieces, qs_pieces = _w_in_pieces()
    for d in range(PROJ_COLS // LANES):
        winp_ref[:, d * LANES:(d + 1) * LANES] = _gather_rows(wint_ref, pieces, d).T.astype(winp_ref.dtype)
    for d in range(SWA_WIDTH // LANES):
        wqst_ref[d * LANES:(d + 1) * LANES, :] = _gather_rows(wint_ref, qs_pieces, d).astype(wqst_ref.dtype)
    wo_ref[...] = wout_ref[...].astype(wo_ref.dtype)


def _prep_small_kernel(wq_ref, wkv_ref, wqt_ref, wk_ref, wvt_ref):
    q_pieces = ([(hd * MLA_QK, hd * MLA_NOPE, MLA_NOPE) for hd in range(MLA_HEADS)]
                + [(hd * MLA_QK + MLA_NOPE, MLA_HEADS * MLA_NOPE + hd * MLA_ROPE, MLA_ROPE)
                   for hd in range(MLA_HEADS)])
    wq = jnp.concatenate(_gather_cols(wq_ref, q_pieces, MLA_HEADS * MLA_QK), axis=1)
    wqt_ref[...] = wq.T.astype(wqt_ref.dtype)
    per_head = MLA_NOPE + MLA_V
    for hd in range(MLA_HEADS):
        wk_ref[:, hd * MLA_NOPE:(hd + 1) * MLA_NOPE] = (
            wkv_ref[:, hd * per_head:hd * per_head + MLA_NOPE].astype(wk_ref.dtype))
        wvt_ref[hd * MLA_V:(hd + 1) * MLA_V, :] = (
            wkv_ref[:, hd * per_head + MLA_NOPE:(hd + 1) * per_head].T.astype(wvt_ref.dtype))


def _prep_weights(w_in, w_q_up, w_kv_up, w_out):
    bf = jnp.bfloat16
    d, n_in = w_in.shape
    rb = PREP_ROWS
    params = pltpu.CompilerParams(dimension_semantics=("arbitrary",),
                                  vmem_limit_bytes=V7X_VMEM_LIMIT_BYTES)
    w_in_p, w_qst, w_o = pl.pallas_call(
        _prep_big_kernel,
        grid=(d // rb,),
        in_specs=[pl.BlockSpec((n_in, rb), lambda i: (0, i)),
                  pl.BlockSpec((rb, w_out.shape[1]), lambda i: (i, 0))],
        out_specs=[pl.BlockSpec((rb, PROJ_COLS), lambda i: (i, 0)),
                   pl.BlockSpec((SWA_WIDTH, rb), lambda i: (0, i)),
                   pl.BlockSpec((rb, w_out.shape[1]), lambda i: (i, 0))],
        out_shape=[jax.ShapeDtypeStruct((d, PROJ_COLS), bf),
                   jax.ShapeDtypeStruct((SWA_WIDTH, d), bf),
                   jax.ShapeDtypeStruct(w_out.shape, bf)],
        compiler_params=params,
        name="prep_big",
    )(w_in.T, w_out)
    w_qt, w_k_p, w_vt = pl.pallas_call(
        _prep_small_kernel,
        out_shape=[jax.ShapeDtypeStruct((MLA_HEADS * MLA_QK, MLA_Q_RANK), bf),
                   jax.ShapeDtypeStruct((MLA_KV_RANK, MLA_HEADS * MLA_NOPE), bf),
                   jax.ShapeDtypeStruct((MLA_WIDTH, MLA_KV_RANK), bf)],
        name="prep_small",
    )(w_q_up, w_kv_up)
    return w_in_p, w_qst, w_qt, w_k_p, w_vt, w_o


def kernel(x, ln_mix, w_in, q_a_norm, w_q_up, kv_a_norm, w_kv_up, attn_sinks, w_out, final_norm):
    batch, seq, d = x.shape
    depth = ln_mix.shape[0]
    assert depth == 1, "final norm is fused into the single layer's output kernel"
    assert seq % MLA_BLOCK == 0 and seq % PROJ_ROWS == 0 and seq % SWA_OUT_ROWS == 0
    tables = {dim: _rope_tables(seq, dim) for dim in {MLA_ROPE, SWA_D}}
    tabs_t, tabs_m = tables[MLA_ROPE]
    tabs_st, tabs_s = tables[SWA_D]
    x2 = x.reshape(batch * seq, d)
    w_in_p, w_qst, w_qt, w_k_p, w_vt, w_o = _prep_weights(w_in[0], w_q_up[0], w_kv_up[0], w_out[0])
    qt, km, vt, gm, qst, ks4, vst, gs = _proj_call(
        x2, ln_mix[0].reshape(1, -1), w_in_p, w_qst, q_a_norm[0].reshape(1, -1), w_qt,
        kv_a_norm[0].reshape(1, -1), w_k_p, w_vt, tabs_m, tabs_s, tabs_t, tabs_st, seq)
    mm = _mla_call(qt, km, vt, gm, batch, seq)
    out = _swa_out_call(attn_sinks[0], qst, ks4, vst, gs, mm, x2, w_o,
                        final_norm.reshape(1, -1), batch, seq)
    return out.reshape(batch, seq, d)
```

```python
import functools
import math

import jax
import jax.numpy as jnp
from jax import lax
from jax.experimental import pallas as pl
from jax.experimental.pallas import tpu as pltpu

ROPE_THETA = 10000.0
NORM_EPS = 1e-6
NEG_INF = -1e30
LOG2E = 1.4426950408889634

MLA_HEADS = 4
MLA_NOPE = 128
MLA_ROPE = 64
MLA_V = 128
MLA_Q_RANK = 256
MLA_KV_RANK = 128
MLA_QK = MLA_NOPE + MLA_ROPE
MLA_WIDTH = MLA_HEADS * MLA_V

SWA_Q_HEADS = 8
SWA_KV_HEADS = 2
SWA_D = 64
SWA_WINDOW = 128
SWA_GROUP = SWA_Q_HEADS // SWA_KV_HEADS
SWA_WIDTH = SWA_Q_HEADS * SWA_D

LANES = 128
SUBLANES = 8
MXU_COLS = 2 * LANES
MLA_QK_PAD = MXU_COLS
MLA_K_COLS = MLA_HEADS * MLA_NOPE + LANES
V7X_VMEM_LIMIT_BYTES = 56 * 1024 * 1024

PROJ_ROWS = 1024
MLA_BLOCK = 256
MLA_GROUP_TILES = 8
MLA_HEADS_PER_STEP = 4
SWA_OUT_ROWS = 1024
SWA_OUT_SUBTILE = 256
SWA_UNITS_PER_STEP = 4
SWA_OUT_TASKS_PER_STEP = 4
SWA_STORE_LAG = 2
PREP_ROWS = 256


def _rope_tables(seq, dim):
    assert 2 * dim == LANES
    half = dim // 2
    inv_freq = 1.0 / (ROPE_THETA ** (jnp.arange(0, dim, 2, dtype=jnp.float32) / dim))
    f32 = jnp.float32
    ct, st, c_lane, s_lane = pl.pallas_call(
        _rope_table_kernel,
        out_shape=[jax.ShapeDtypeStruct((half, seq), f32), jax.ShapeDtypeStruct((half, seq), f32),
                   jax.ShapeDtypeStruct((seq, LANES), f32), jax.ShapeDtypeStruct((seq, LANES), f32)],
        name="rope_tables",
    )(inv_freq.reshape(half, 1))
    return (ct, st), (c_lane, s_lane)


def _rope_table_kernel(inv_ref, ct_ref, st_ref, cl_ref, sl_ref):
    half, seq = ct_ref.shape
    pos = lax.broadcasted_iota(jnp.int32, (half, seq), 1).astype(jnp.float32)
    ang = pos * inv_ref[...]
    c, s = jnp.cos(ang), jnp.sin(ang)
    ct_ref[...] = c
    st_ref[...] = s
    cl_ref[...] = jnp.concatenate([c, c, c, c], axis=0).T
    sl_ref[...] = jnp.concatenate([-s, -s, s, s], axis=0).T


def _rope(x, c, s):
    return x * c + pltpu.roll(x, LANES // 2, 1) * s


def _rms(x, g):
    return x * lax.rsqrt(jnp.mean(x * x, axis=-1, keepdims=True) + NORM_EPS) * g


def _silu(g):
    return g / (1.0 + jnp.exp(-g))


def _dot(a, b):
    return jnp.dot(a, b, preferred_element_type=jnp.float32)


def _dot_nt(a, b):
    return lax.dot_general(a, b, (((1,), (1,)), ((), ())),
                           preferred_element_type=jnp.float32)


PROJ_COLS = 1792


def _proj_kernel(x_ref, ln_ref, win_ref, wqst_ref, qg_ref, wqt_ref, kvg_ref, wkv_ref, wvt_ref,
                 cm_ref, sm_ref, cs_ref, ss_ref, ct_ref, st_ref, cst_ref, sst_ref,
                 qt_ref, km_ref, vt_ref, gm_ref, qst_ref, ks_ref, vst_ref, gs_ref):
    bf = jnp.bfloat16
    x = x_ref[...]
    h = _rms(x, ln_ref[...]).astype(bf)

    cm, sm, cs, ss = cm_ref[...], sm_ref[...], cs_ref[...], ss_ref[...]
    lane = lax.broadcasted_iota(jnp.int32, (x.shape[0], LANES), 1)
    first = (lane % (LANES // 2)) < (LANES // 4)
    grp = MXU_COLS

    def cols(a):
        return win_ref[:, a:a + grp]

    c_q = _dot(h, cols(0))
    lat_b = _dot(h, cols(256))
    cqn = _rms(c_q, qg_ref[...]).astype(bf)
    ckvn = _rms(lat_b[:, 0:MLA_KV_RANK], kvg_ref[...]).astype(bf)
    kr = _rope(lat_b[:, LANES:2 * LANES], cm, sm)
    kr_first = jnp.where(first, kr, 0.0).astype(bf)

    s_scale = LOG2E / math.sqrt(SWA_D)
    qst = _dot_nt(wqst_ref[...], h)
    cst2 = jnp.concatenate([cst_ref[...], cst_ref[...]], axis=0)
    sst2 = jnp.concatenate([sst_ref[...], sst_ref[...]], axis=0)
    for ch in range(SWA_WIDTH // LANES):
        x1 = qst[ch * LANES:ch * LANES + SWA_D, :]
        x2 = qst[ch * LANES + SWA_D:(ch + 1) * LANES, :]
        qst_ref[ch * LANES:ch * LANES + SWA_D, :] = ((x1 * cst2 - x2 * sst2) * s_scale).astype(bf)
        qst_ref[ch * LANES + SWA_D:(ch + 1) * LANES, :] = ((x2 * cst2 + x1 * sst2) * s_scale).astype(bf)

    q_scale = LOG2E / math.sqrt(MLA_QK)
    nope_w = MLA_HEADS * MLA_NOPE
    half = MLA_ROPE // 2
    qt = _dot_nt(wqt_ref[...], cqn)
    ct, st = ct_ref[...], st_ref[...]
    zeros = jnp.zeros((half, x.shape[0]), bf)
    for hd in range(MLA_HEADS):
        qt_ref[hd * grp:hd * grp + LANES, :] = (qt[hd * MLA_NOPE:(hd + 1) * MLA_NOPE, :] * q_scale).astype(bf)
        x1 = qt[nope_w + hd * MLA_ROPE:nope_w + hd * MLA_ROPE + half, :]
        x2 = qt[nope_w + hd * MLA_ROPE + half:nope_w + (hd + 1) * MLA_ROPE, :]
        base = hd * grp + LANES
        qt_ref[base:base + half, :] = ((x1 * ct - x2 * st) * q_scale).astype(bf)
        qt_ref[base + half:base + 2 * half, :] = zeros
        qt_ref[base + 2 * half:base + 3 * half, :] = ((x2 * ct + x1 * st) * q_scale).astype(bf)
        qt_ref[base + 3 * half:base + 4 * half, :] = zeros
    for g2 in range(nope_w // grp):
        km_ref[:, g2 * grp:(g2 + 1) * grp] = _dot(ckvn, wkv_ref[:, g2 * grp:(g2 + 1) * grp]).astype(bf)
    km_ref[:, nope_w:nope_w + LANES] = kr_first
    for g2 in range(MLA_WIDTH // grp):
        gm_ref[:, g2 * grp:(g2 + 1) * grp] = _silu(_dot(h, cols(512 + g2 * grp))).astype(bf)

    kv_s = _dot(h, cols(1024))
    k01 = _rope(kv_s[:, 0:LANES], cs, ss)
    ks_ref[:, 0 * LANES:1 * LANES] = jnp.where(first, k01, 0.0).astype(bf)
    ks_ref[:, 1 * LANES:2 * LANES] = jnp.where(first, 0.0, pltpu.roll(k01, LANES // 4, 1)).astype(bf)
    ks_ref[:, 2 * LANES:3 * LANES] = jnp.where(first, pltpu.roll(k01, 3 * LANES // 4, 1), 0.0).astype(bf)
    ks_ref[:, 3 * LANES:4 * LANES] = jnp.where(first, 0.0, k01).astype(bf)
    vst_ref[...] = kv_s[:, LANES:2 * LANES].T.astype(bf)

    for g2 in range(SWA_WIDTH // grp):
        gs_ref[:, g2 * grp:(g2 + 1) * grp] = _silu(_dot(h, cols(1280 + g2 * grp))).astype(bf)

    vt_ref[...] = _dot_nt(wvt_ref[...], ckvn).astype(bf)


def _proj_call(x2, ln_g, w_in_p, w_qst, q_g, w_qt, kv_g, w_k_p, w_vt, tabs_m, tabs_s, tabs_t,
               tabs_st, seq):
    n, d = x2.shape
    tm = PROJ_ROWS
    steps_per_seq = seq // tm
    batch = n // seq
    row = lambda i: (i, 0)
    const = lambda i: (0, 0)
    tab = lambda i: (i % steps_per_seq, 0)
    tab_t = lambda i: (0, i % steps_per_seq)
    col = lambda i: (i // steps_per_seq, i % steps_per_seq)
    bf = jnp.bfloat16
    qt_rows = MLA_HEADS * MLA_QK_PAD
    row_out = lambda w: (pl.BlockSpec((tm, w), row), jax.ShapeDtypeStruct((n, w), bf))
    col_out = lambda r: (pl.BlockSpec((r, tm), col), jax.ShapeDtypeStruct((batch * r, seq), bf))
    outs = [col_out(qt_rows), row_out(MLA_K_COLS), col_out(MLA_WIDTH), row_out(MLA_WIDTH),
            col_out(SWA_WIDTH), row_out(SWA_WIDTH), col_out(SWA_KV_HEADS * SWA_D), row_out(SWA_WIDTH)]
    return pl.pallas_call(
        _proj_kernel,
        grid=(n // tm,),
        in_specs=[
            pl.BlockSpec((tm, d), row),
            pl.BlockSpec((1, d), const),
            pl.BlockSpec(w_in_p.shape, const),
            pl.BlockSpec(w_qst.shape, const),
            pl.BlockSpec((1, MLA_Q_RANK), const),
            pl.BlockSpec(w_qt.shape, const),
            pl.BlockSpec((1, MLA_KV_RANK), const),
            pl.BlockSpec(w_k_p.shape, const),
            pl.BlockSpec(w_vt.shape, const),
        ] + [pl.BlockSpec((tm, LANES), tab)] * 4
          + [pl.BlockSpec((MLA_ROPE // 2, tm), tab_t)] * 2
          + [pl.BlockSpec((SWA_D // 2, tm), tab_t)] * 2,
        out_specs=[o[0] for o in outs],
        out_shape=[o[1] for o in outs],
        compiler_params=pltpu.CompilerParams(
            dimension_semantics=("arbitrary",),
            vmem_limit_bytes=V7X_VMEM_LIMIT_BYTES),
        name="proj",
    )(x2, ln_g, w_in_p, w_qst, q_g, w_qt, kv_g, w_k_p, w_vt, *tabs_m, *tabs_s, *tabs_t, *tabs_st)


def _tree(op, xs):
    xs = list(xs)
    while len(xs) > 1:
        xs = [op(xs[a], xs[a + 1]) if a + 1 < len(xs) else xs[a] for a in range(0, len(xs), 2)]
    return xs[0]


def _mla_block_groups(nblk):
    groups, cur, load = [], [], 0
    for i in range(nblk - 1, -1, -1):
        if load + i + 1 > MLA_GROUP_TILES:
            groups.append(cur)
            cur, load = [], 0
        cur.append(i)
        load += i + 1
    groups.append(cur)
    return groups


def _mla_kernel(qt_ref, k_ref, kr_ref, vt_ref, g_ref, o_ref, s_ref, p_ref, vx_ref):
    t = MLA_BLOCK
    seq = k_ref.shape[0]
    sub = SUBLANES
    qk = MLA_QK_PAD
    r = lax.broadcasted_iota(jnp.int32, (t, t), 0)
    c = lax.broadcasted_iota(jnp.int32, (t, t), 1)
    causal = r <= c
    nblk = seq // t
    units = []
    for hd in range(MLA_HEADS_PER_STEP):
        for blocks in _mla_block_groups(nblk):
            members, off = [], 0
            for i in blocks:
                members.append((hd, i, off))
                off += (i + 1) * t
            assert off <= s_ref.shape[1]
            units.append(members)
    state = [[dict(m8=None) for _ in members] for members in units]

    for hd in range(MLA_HEADS_PER_STEP):
        vx_ref[hd, 0:MLA_V, :] = vt_ref[hd * MLA_V:(hd + 1) * MLA_V, :]
        vx_ref[hd, MLA_V:, :] = jnp.ones((vx_ref.shape[1] - MLA_V, seq), vx_ref.dtype)

    def score_tile(u, b, j):
        hd, i, off = units[u][b]
        st = state[u][b]
        keys = slice(j * t, (j + 1) * t)
        k_tile = jnp.concatenate([k_ref[keys, hd * MLA_NOPE:(hd + 1) * MLA_NOPE], kr_ref[keys, :]], axis=1)
        s = _dot(k_tile, qt_ref[hd * qk:(hd + 1) * qk, i * t:(i + 1) * t])
        if j == i:
            s = jnp.where(causal, s, NEG_INF)
        s_ref[u % 2, off + j * t:off + (j + 1) * t, :] = s
        m8 = _tree(jnp.maximum, [s[a * sub:(a + 1) * sub, :] for a in range(t // sub)])
        st["m8"] = m8 if st["m8"] is None else jnp.maximum(st["m8"], m8)

    def prob_tile(u, b, j):
        _, _, off = units[u][b]
        st = state[u][b]
        if "m" not in st:
            st["m"] = jnp.max(st["m8"], axis=0, keepdims=True)
        rows = slice(off + j * t, off + (j + 1) * t)
        p_ref[u % 3, rows, :] = jnp.exp2(s_ref[u % 2, rows, :] - st["m"]).astype(jnp.bfloat16)

    def value_matmul(u):
        for b, (hd, i, off) in enumerate(units[u]):
            kv = (i + 1) * t
            state[u][b]["acc"] = _dot(vx_ref[hd, :, 0:kv], p_ref[u % 3, off:off + kv, :])

    def store(u):
        for b, (hd, i, _) in enumerate(units[u]):
            rows = slice(i * t, (i + 1) * t)
            acc = state[u][b]["acc"]
            out_t = acc[0:MLA_V, :] / acc[MLA_V:MLA_V + 1, :]
            gate = g_ref[rows, hd * MLA_V:(hd + 1) * MLA_V].astype(jnp.float32)
            o_ref[rows, hd * MLA_V:(hd + 1) * MLA_V] = (out_t.T * gate).astype(o_ref.dtype)
        state[u] = None

    def tiles(fn, u):
        return [functools.partial(fn, u, b, j) for b, (_, i, _) in enumerate(units[u]) for j in range(i + 1)]

    nu = len(units)
    for task in tiles(score_tile, 0):
        task()
    for u in range(nu + 2):
        if 0 <= u - 2 < nu:
            store(u - 2)
        nxt = tiles(score_tile, u + 1) if u + 1 < nu else []
        cur = tiles(prob_tile, u) if u < nu else []
        while nxt or cur:
            if cur:
                cur.pop(0)()
            if nxt:
                nxt.pop(0)()
        if 0 <= u - 1 < nu:
            value_matmul(u - 1)


def _mla_call(qt, km, vt, gm, batch, seq):
    n = km.shape[0]
    t = MLA_BLOCK
    hps = MLA_HEADS_PER_STEP
    groups = MLA_HEADS // hps
    ones_rows = 16
    feat = lambda b, h: (b * groups + h, 0)
    tok = lambda b, h: (b, h)
    return pl.pallas_call(
        _mla_kernel,
        grid=(batch, groups),
        in_specs=[
            pl.BlockSpec((hps * MLA_QK_PAD, seq), feat),
            pl.BlockSpec((seq, hps * MLA_NOPE), tok),
            pl.BlockSpec((seq, LANES), lambda b, h: (b, MLA_HEADS * MLA_NOPE // LANES)),
            pl.BlockSpec((hps * MLA_V, seq), feat),
            pl.BlockSpec((seq, hps * MLA_V), tok),
        ],
        out_specs=pl.BlockSpec((seq, hps * MLA_V), tok),
        out_shape=jax.ShapeDtypeStruct((n, MLA_WIDTH), jnp.bfloat16),
        scratch_shapes=[pltpu.VMEM((2, MLA_GROUP_TILES * t, t), jnp.float32),
                        pltpu.VMEM((3, MLA_GROUP_TILES * t, t), jnp.bfloat16),
                        pltpu.VMEM((hps, MLA_V + ones_rows, seq), jnp.bfloat16)],
        compiler_params=pltpu.CompilerParams(
            dimension_semantics=("arbitrary", "arbitrary"),
            vmem_limit_bytes=V7X_VMEM_LIMIT_BYTES),
        name="mla",
    )(qt, km, km, vt, gm)


def _swa_out_kernel(sink_ref, qt_ref, k_ref, kh_ref, vt_ref, vth_ref, g_ref, mm_ref, x_ref,
                    w_ref, fg_ref, o_ref, ms_ref, kx_ref, vx_ref):
    w = SWA_WINDOW
    chunk = pl.program_id(1)
    rows_total = k_ref.shape[0]
    bf = jnp.bfloat16
    sub = SUBLANES

    kx_ref[0:w, :] = kh_ref[...]
    kx_ref[w:, :] = k_ref[...]
    for hk in range(SWA_KV_HEADS):
        vx_ref[hk, 0:SWA_D, 0:w] = vth_ref[hk * SWA_D:(hk + 1) * SWA_D, :]
        vx_ref[hk, 0:SWA_D, w:] = vt_ref[hk * SWA_D:(hk + 1) * SWA_D, :]
        vx_ref[hk, SWA_D:, :] = jnp.ones((vx_ref.shape[1] - SWA_D, vx_ref.shape[2]), bf)

    ki = lax.broadcasted_iota(jnp.int32, (2 * w, 2 * w), 0)
    qi = lax.broadcasted_iota(jnp.int32, (2 * w, 2 * w), 1) % w
    rel = qi + w - ki
    band = (rel >= 0) & (rel < SWA_WINDOW)
    band_first = band & ((ki >= w) | (chunk > 0))
    lane = lax.broadcasted_iota(jnp.int32, (1, 2 * w), 1)

    units = [(n, hk) for n in range(rows_total // w) for hk in range(SWA_KV_HEADS)]
    state = [dict() for _ in units]

    def sink_rows(hk):
        sink = [sink_ref[hk * SWA_GROUP + i] * LOG2E for i in range(SWA_GROUP)]
        return (jnp.where(lane < w, sink[0], sink[2]), jnp.where(lane < w, sink[1], sink[3]))

    def stage_scores(u, half):
        n, hk = units[u]
        base = hk * 2 * LANES
        qcols = slice(n * w, (n + 1) * w)
        keys = slice(n * w, (n + 2) * w)
        qt = jnp.concatenate([qt_ref[base:base + LANES, qcols],
                              qt_ref[base + LANES:base + 2 * LANES, qcols]], axis=1)
        valid = band_first if n == 0 else band
        st = state[u]
        lanes = slice(base + half * LANES, base + (half + 1) * LANES)
        s = jnp.where(valid, _dot(kx_ref[keys, lanes], qt), NEG_INF)
        m8 = _tree(jnp.maximum, [s[a * sub:(a + 1) * sub, :] for a in range(2 * w // sub)])
        st.setdefault("s", {})[half] = s
        st.setdefault("m", {})[half] = jnp.maximum(jnp.max(m8, axis=0, keepdims=True),
                                                   sink_rows(hk)[half])

    def stage_values(u, half):
        n, hk = units[u]
        keys = slice(n * w, (n + 2) * w)
        st = state[u]
        p = jnp.exp2(st["s"].pop(half) - st["m"][half]).astype(bf)
        st.setdefault("acc", {})[half] = _dot(vx_ref[hk, :, keys], p)

    def stage_store(u):
        n, hk = units[u]
        base = hk * 2 * LANES
        qcols = slice(n * w, (n + 1) * w)
        sinks = sink_rows(hk)
        st = state[u]
        outs = []
        for half in range(2):
            acc = st["acc"][half]
            den = acc[SWA_D:SWA_D + 1, :] + jnp.exp2(sinks[half] - st["m"][half])
            outs.append(acc[0:SWA_D, :] / den)
        g = g_ref[qcols, base:base + 2 * LANES].astype(jnp.float32)
        for ch in range(2):
            o_t = jnp.concatenate([outs[0][:, ch * w:(ch + 1) * w],
                                   outs[1][:, ch * w:(ch + 1) * w]], axis=0)
            ms_ref[qcols, base + ch * LANES:base + (ch + 1) * LANES] = (
                o_t.T * g[:, ch * LANES:(ch + 1) * LANES]).astype(bf)
        state[u] = None

    t = SWA_OUT_SUBTILE
    units_per_tile = (t // w) * SWA_KV_HEADS

    d_model = x_ref.shape[1]
    grp = MXU_COLS
    out_state = {}

    def out_group(i, gc):
        rows = slice(i * t, (i + 1) * t)
        cols = slice(gc * grp, (gc + 1) * grp)
        y = (x_ref[rows, cols] + _dot(mm_ref[rows, :], w_ref[0:MLA_WIDTH, cols])
             + _dot(ms_ref[rows, :], w_ref[MLA_WIDTH:, cols]))
        out_state.setdefault(i, []).append(y)

    def out_norm(i):
        rows = slice(i * t, (i + 1) * t)
        ys = out_state.pop(i)
        ssq = _tree(jnp.add, [jnp.sum(y * y, axis=-1, keepdims=True) for y in ys])
        scale = lax.rsqrt(ssq / d_model + NORM_EPS)
        for gc, y in enumerate(ys):
            cols = slice(gc * grp, (gc + 1) * grp)
            o_ref[rows, cols] = y * scale * fg_ref[:, cols]

    nu = len(units)
    lag = SWA_STORE_LAG
    gsz = SWA_UNITS_PER_STEP
    pending = []
    tasks_per_step = SWA_OUT_TASKS_PER_STEP

    def retire(count):
        for _ in range(min(count, len(pending))):
            pending.pop(0)()

    for step in range(nu // gsz + lag):
        for u in range(step * gsz, (step + 1) * gsz):
            if u < nu:
                stage_scores(u, 0)
                stage_scores(u, 1)
        retire(tasks_per_step // 2)
        for u in range((step - 1) * gsz, step * gsz):
            if 0 <= u < nu:
                stage_values(u, 0)
                stage_values(u, 1)
        retire(tasks_per_step - tasks_per_step // 2)
        for u in range((step - lag) * gsz, (step - lag + 1) * gsz):
            if 0 <= u < nu:
                stage_store(u)
                if (u + 1) % units_per_tile == 0:
                    i = u // units_per_tile
                    pending += [functools.partial(out_group, i, gc) for gc in range(d_model // grp)]
                    pending.append(functools.partial(out_norm, i))
    for task in pending:
        task()


def _swa_out_call(sinks, qst, ks4, vst, gs, mm, x2, w_o, fg, batch, seq):
    n, d = x2.shape
    r = SWA_OUT_ROWS
    w = SWA_WINDOW
    cps = seq // r
    bpc = r // w
    bps = seq // w
    v_rows = SWA_KV_HEADS * SWA_D
    ones_rows = 16
    row = lambda b, c: (b * cps + c, 0)
    halo = lambda b, c: (b * bps + jnp.maximum(c * bpc - 1, 0), 0)
    feat = lambda b, c: (b, c)
    feat_halo = lambda b, c: (b, jnp.maximum(c * bpc - 1, 0))
    const = lambda b, c: (0, 0)
    wide = pl.BlockSpec((r, SWA_WIDTH), row)
    return pl.pallas_call(
        _swa_out_kernel,
        grid=(batch, cps),
        in_specs=[
            pl.BlockSpec(memory_space=pltpu.SMEM),
            pl.BlockSpec((SWA_WIDTH, r), feat),
            wide,
            pl.BlockSpec((w, SWA_WIDTH), halo),
            pl.BlockSpec((v_rows, r), feat),
            pl.BlockSpec((v_rows, w), feat_halo),
            wide,
            pl.BlockSpec((r, MLA_WIDTH), row),
            pl.BlockSpec((r, d), row),
            pl.BlockSpec(w_o.shape, const),
            pl.BlockSpec((1, d), const),
        ],
        out_specs=pl.BlockSpec((r, d), row),
        out_shape=jax.ShapeDtypeStruct((n, d), jnp.float32),
        scratch_shapes=[pltpu.VMEM((r, SWA_WIDTH), jnp.bfloat16),
                        pltpu.VMEM((r + w, SWA_WIDTH), jnp.bfloat16),
                        pltpu.VMEM((SWA_KV_HEADS, SWA_D + ones_rows, r + w), jnp.bfloat16)],
        compiler_params=pltpu.CompilerParams(
            dimension_semantics=("arbitrary", "arbitrary"),
            vmem_limit_bytes=V7X_VMEM_LIMIT_BYTES),
        name="swa_out",
    )(sinks, qst, ks4, ks4, vst, vst, gs, mm, x2, w_o, fg)


def _pair_rope_pieces(src, dst, n_heads, dim):
    half = dim // 2
    out = []
    for pair in range(n_heads // 2):
        for which in range(2):
            head = src + (2 * pair + which) * dim
            chunk = dst + pair * 2 * dim
            out.append((head, chunk + which * half, half))
            out.append((head + half, chunk + 2 * half + which * half, half))
    return out


def _w_in_pieces():
    pieces, qs_pieces, src, dst = [], [], 0, 0
    for name, width in (("c_q", MLA_Q_RANK), ("c_kv", MLA_KV_RANK), ("k_rope", MLA_ROPE),
                        ("g_mla", MLA_WIDTH), ("q_s", SWA_WIDTH),
                        ("k_s", SWA_KV_HEADS * SWA_D), ("v_s", SWA_KV_HEADS * SWA_D),
                        ("g_swa", SWA_WIDTH)):
        if name == "k_rope":
            half = MLA_ROPE // 2
            pieces += [(src, dst, half), (src, dst + half, half),
                       (src + half, dst + 2 * half, half), (src + half, dst + 3 * half, half)]
            dst += 2 * MLA_ROPE
        elif name == "q_s":
            qs_pieces += _pair_rope_pieces(src, 0, SWA_Q_HEADS, SWA_D)
        elif name == "k_s":
            pieces += _pair_rope_pieces(src, dst, SWA_KV_HEADS, SWA_D)
            dst += width
        else:
            pieces.append((src, dst, width))
            dst += width
        src += width
    assert dst == PROJ_COLS
    return pieces, qs_pieces


def _gather_cols(src_ref, pieces, n_dst_cols):
    rows, n_src = src_ref.shape
    lane = lax.broadcasted_iota(jnp.int32, (rows, LANES), 1)
    loaded, rolled = {}, {}

    def load(k):
        if k not in loaded:
            width = min(LANES, n_src - k * LANES)
            v = src_ref[:, k * LANES:k * LANES + width]
            if width < LANES:
                v = jnp.concatenate([v, jnp.zeros((rows, LANES - width), v.dtype)], axis=1)
            loaded[k] = v
        return loaded[k]

    def shifted(k, shift):
        if shift == 0:
            return load(k)
        if (k, shift) not in rolled:
            rolled[(k, shift)] = pltpu.roll(load(k), shift, 1)
        return rolled[(k, shift)]

    chunks = []
    for d in range(n_dst_cols // LANES):
        acc = None
        for s0, d0, w in pieces:
            lo, hi = max(d0, d * LANES), min(d0 + w, (d + 1) * LANES)
            while lo < hi:
                s_abs = s0 + lo - d0
                k, a = divmod(s_abs, LANES)
                n = min(hi - lo, LANES - a)
                a_dst = lo - d * LANES
                val = shifted(k, (a_dst - a) % LANES)
                if n == LANES:
                    acc = val
                else:
                    mask = (lane >= a_dst) & (lane < a_dst + n)
                    acc = jnp.where(mask, val, 0.0 if acc is None else acc)
                lo += n
        chunks.append(acc)
    return chunks


def _gather_rows(src_ref, pieces, d):
    parts = []
    for s0, d0, w in pieces:
        lo, hi = max(d0, d * LANES), min(d0 + w, (d + 1) * LANES)
        if lo < hi:
            parts.append((lo, src_ref[s0 + lo - d0:s0 + hi - d0, :]))
    parts.sort(key=lambda t: t[0])
    assert sum(p.shape[0] for _, p in parts) == LANES
    return jnp.concatenate([p for _, p in parts], axis=0) if len(parts) > 1 else parts[0][1]


def _prep_big_kernel(wint_ref, wout_ref, winp_ref, wqst_ref, wo_ref):
    pieces, qs_pieces = _w_in_pieces()
    for d in range(PROJ_COLS // LANES):
        winp_ref[:, d * LANES:(d + 1) * LANES] = _gather_rows(wint_ref, pieces, d).T.astype(winp_ref.dtype)
    for d in range(SWA_WIDTH // LANES):
        wqst_ref[d * LANES:(d + 1) * LANES, :] = _gather_rows(wint_ref, qs_pieces, d).astype(wqst_ref.dtype)
    wo_ref[...] = wout_ref[...].astype(wo_ref.dtype)


def _prep_small_kernel(wq_ref, wkv_ref, wqt_ref, wk_ref, wvt_ref):
    q_pieces = ([(hd * MLA_QK, hd * MLA_NOPE, MLA_NOPE) for hd in range(MLA_HEADS)]
                + [(hd * MLA_QK + MLA_NOPE, MLA_HEADS * MLA_NOPE + hd * MLA_ROPE, MLA_ROPE)
                   for hd in range(MLA_HEADS)])
    wq = jnp.concatenate(_gather_cols(wq_ref, q_pieces, MLA_HEADS * MLA_QK), axis=1)
    wqt_ref[...] = wq.T.astype(wqt_ref.dtype)
    per_head = MLA_NOPE + MLA_V
    for hd in range(MLA_HEADS):
        wk_ref[:, hd * MLA_NOPE:(hd + 1) * MLA_NOPE] = (
            wkv_ref[:, hd * per_head:hd * per_head + MLA_NOPE].astype(wk_ref.dtype))
        wvt_ref[hd * MLA_V:(hd + 1) * MLA_V, :] = (
            wkv_ref[:, hd * per_head + MLA_NOPE:(hd + 1) * per_head].T.astype(wvt_ref.dtype))


def _prep_weights(w_in, w_q_up, w_kv_up, w_out):
    bf = jnp.bfloat16
    d, n_in = w_in.shape
    rb = PREP_ROWS
    params = pltpu.CompilerParams(dimension_semantics=("arbitrary",),
                                  vmem_limit_bytes=V7X_VMEM_LIMIT_BYTES)
    w_in_p, w_qst, w_o = pl.pallas_call(
        _prep_big_kernel,
        grid=(d // rb,),
        in_specs=[pl.BlockSpec((n_in, rb), lambda i: (0, i)),
                  pl.BlockSpec((rb, w_out.shape[1]), lambda i: (i, 0))],
        out_specs=[pl.BlockSpec((rb, PROJ_COLS), lambda i: (i, 0)),
                   pl.BlockSpec((SWA_WIDTH, rb), lambda i: (0, i)),
                   pl.BlockSpec((rb, w_out.shape[1]), lambda i: (i, 0))],
        out_shape=[jax.ShapeDtypeStruct((d, PROJ_COLS), bf),
                   jax.ShapeDtypeStruct((SWA_WIDTH, d), bf),
                   jax.ShapeDtypeStruct(w_out.shape, bf)],
        compiler_params=params,
        name="prep_big",
    )(w_in.T, w_out)
    w_qt, w_k_p, w_vt = pl.pallas_call(
        _prep_small_kernel,
        out_shape=[jax.ShapeDtypeStruct((MLA_HEADS * MLA_QK, MLA_Q_RANK), bf),
                   jax.ShapeDtypeStruct((MLA_KV_RANK, MLA_HEADS * MLA_NOPE), bf),
                   jax.ShapeDtypeStruct((MLA_WIDTH, MLA_KV_RANK), bf)],
        name="prep_small",
    )(w_q_up, w_kv_up)
    return w_in_p, w_qst, w_qt, w_k_p, w_vt, w_o


def kernel(x, ln_mix, w_in, q_a_norm, w_q_up, kv_a_norm, w_kv_up, attn_sinks, w_out, final_norm):
    batch, seq, d = x.shape
    depth = ln_mix.shape[0]
    assert depth == 1, "final norm is fused into the single layer's output kernel"
    assert seq % MLA_BLOCK == 0 and seq % PROJ_ROWS == 0 and seq % SWA_OUT_ROWS == 0
    tables = {dim: _rope_tables(seq, dim) for dim in {MLA_ROPE, SWA_D}}
    tabs_t, tabs_m = tables[MLA_ROPE]
    tabs_st, tabs_s = tables[SWA_D]
    x2 = x.reshape(batch * seq, d)
    w_in_p, w_qst, w_qt, w_k_p, w_vt, w_o = _prep_weights(w_in[0], w_q_up[0], w_kv_up[0], w_out[0])
    qt, km, vt, gm, qst, ks4, vst, gs = _proj_call(
        x2, ln_mix[0].reshape(1, -1), w_in_p, w_qst, q_a_norm[0].reshape(1, -1), w_qt,
        kv_a_norm[0].reshape(1, -1), w_k_p, w_vt, tabs_m, tabs_s, tabs_t, tabs_st, seq)
    mm = _mla_call(qt, km, vt, gm, batch, seq)
    out = _swa_out_call(attn_sinks[0], qst, ks4, vst, gs, mm, x2, w_o,
                        final_norm.reshape(1, -1), batch, seq)
    return out.reshape(batch, seq, d)
```

```python
import functools
import math

import jax
import jax.numpy as jnp
from jax import lax
from jax.experimental import pallas as pl
from jax.experimental.pallas import tpu as pltpu

ROPE_THETA = 10000.0
NORM_EPS = 1e-6
NEG_INF = -1e30
LOG2E = 1.4426950408889634

MLA_HEADS = 4
MLA_NOPE = 128
MLA_ROPE = 64
MLA_V = 128
MLA_Q_RANK = 256
MLA_KV_RANK = 128
MLA_QK = MLA_NOPE + MLA_ROPE
MLA_WIDTH = MLA_HEADS * MLA_V

SWA_Q_HEADS = 8
SWA_KV_HEADS = 2
SWA_D = 64
SWA_WINDOW = 128
SWA_GROUP = SWA_Q_HEADS // SWA_KV_HEADS
SWA_WIDTH = SWA_Q_HEADS * SWA_D

LANES = 128
SUBLANES = 8
MXU_COLS = 2 * LANES
MLA_QK_PAD = MXU_COLS
MLA_K_COLS = MLA_HEADS * MLA_NOPE + LANES
V7X_VMEM_LIMIT_BYTES = 56 * 1024 * 1024

PROJ_ROWS = 1024
MLA_BLOCK = 256
MLA_GROUP_TILES = 8
MLA_HEADS_PER_STEP = 2
SWA_OUT_ROWS = 1024
SWA_OUT_SUBTILE = 256
SWA_UNITS_PER_STEP = 4
SWA_OUT_TASKS_PER_STEP = 4
SWA_STORE_LAG = 2
PREP_ROWS = 256


def _rope_tables(seq, dim):
    assert 2 * dim == LANES
    half = dim // 2
    inv_freq = 1.0 / (ROPE_THETA ** (jnp.arange(0, dim, 2, dtype=jnp.float32) / dim))
    f32 = jnp.float32
    ct, st, c_lane, s_lane = pl.pallas_call(
        _rope_table_kernel,
        out_shape=[jax.ShapeDtypeStruct((half, seq), f32), jax.ShapeDtypeStruct((half, seq), f32),
                   jax.ShapeDtypeStruct((seq, LANES), f32), jax.ShapeDtypeStruct((seq, LANES), f32)],
        name="rope_tables",
    )(inv_freq.reshape(half, 1))
    return (ct, st), (c_lane, s_lane)


def _rope_table_kernel(inv_ref, ct_ref, st_ref, cl_ref, sl_ref):
    half, seq = ct_ref.shape
    pos = lax.broadcasted_iota(jnp.int32, (half, seq), 1).astype(jnp.float32)
    ang = pos * inv_ref[...]
    c, s = jnp.cos(ang), jnp.sin(ang)
    ct_ref[...] = c
    st_ref[...] = s
    cl_ref[...] = jnp.concatenate([c, c, c, c], axis=0).T
    sl_ref[...] = jnp.concatenate([-s, -s, s, s], axis=0).T


def _rope(x, c, s):
    return x * c + pltpu.roll(x, LANES // 2, 1) * s


def _rms(x, g):
    return x * lax.rsqrt(jnp.mean(x * x, axis=-1, keepdims=True) + NORM_EPS) * g


def _silu(g):
    return g / (1.0 + jnp.exp(-g))


def _dot(a, b):
    return jnp.dot(a, b, preferred_element_type=jnp.float32)


def _dot_nt(a, b):
    return lax.dot_general(a, b, (((1,), (1,)), ((), ())),
                           preferred_element_type=jnp.float32)


PROJ_COLS = 1792


def _proj_kernel(x_ref, ln_ref, win_ref, wqst_ref, qg_ref, wqt_ref, kvg_ref, wkv_ref, wvt_ref,
                 cm_ref, sm_ref, cs_ref, ss_ref, ct_ref, st_ref, cst_ref, sst_ref,
                 qt_ref, km_ref, vt_ref, gm_ref, qst_ref, ks_ref, vst_ref, gs_ref):
    bf = jnp.bfloat16
    x = x_ref[...]
    h = _rms(x, ln_ref[...]).astype(bf)

    cm, sm, cs, ss = cm_ref[...], sm_ref[...], cs_ref[...], ss_ref[...]
    lane = lax.broadcasted_iota(jnp.int32, (x.shape[0], LANES), 1)
    first = (lane % (LANES // 2)) < (LANES // 4)
    grp = MXU_COLS

    def cols(a):
        return win_ref[:, a:a + grp]

    c_q = _dot(h, cols(0))
    lat_b = _dot(h, cols(256))
    cqn = _rms(c_q, qg_ref[...]).astype(bf)
    ckvn = _rms(lat_b[:, 0:MLA_KV_RANK], kvg_ref[...]).astype(bf)
    kr = _rope(lat_b[:, LANES:2 * LANES], cm, sm)
    kr_first = jnp.where(first, kr, 0.0).astype(bf)

    s_scale = LOG2E / math.sqrt(SWA_D)
    qst = _dot_nt(wqst_ref[...], h)
    cst2 = jnp.concatenate([cst_ref[...], cst_ref[...]], axis=0)
    sst2 = jnp.concatenate([sst_ref[...], sst_ref[...]], axis=0)
    for ch in range(SWA_WIDTH // LANES):
        x1 = qst[ch * LANES:ch * LANES + SWA_D, :]
        x2 = qst[ch * LANES + SWA_D:(ch + 1) * LANES, :]
        qst_ref[ch * LANES:ch * LANES + SWA_D, :] = ((x1 * cst2 - x2 * sst2) * s_scale).astype(bf)
        qst_ref[ch * LANES + SWA_D:(ch + 1) * LANES, :] = ((x2 * cst2 + x1 * sst2) * s_scale).astype(bf)

    q_scale = LOG2E / math.sqrt(MLA_QK)
    nope_w = MLA_HEADS * MLA_NOPE
    half = MLA_ROPE // 2
    qt = _dot_nt(wqt_ref[...], cqn)
    ct, st = ct_ref[...], st_ref[...]
    zeros = jnp.zeros((half, x.shape[0]), bf)
    for hd in range(MLA_HEADS):
        qt_ref[hd * grp:hd * grp + LANES, :] = (qt[hd * MLA_NOPE:(hd + 1) * MLA_NOPE, :] * q_scale).astype(bf)
        x1 = qt[nope_w + hd * MLA_ROPE:nope_w + hd * MLA_ROPE + half, :]
        x2 = qt[nope_w + hd * MLA_ROPE + half:nope_w + (hd + 1) * MLA_ROPE, :]
        base = hd * grp + LANES
        qt_ref[base:base + half, :] = ((x1 * ct - x2 * st) * q_scale).astype(bf)
        qt_ref[base + half:base + 2 * half, :] = zeros
        qt_ref[base + 2 * half:base + 3 * half, :] = ((x2 * ct + x1 * st) * q_scale).astype(bf)
        qt_ref[base + 3 * half:base + 4 * half, :] = zeros
    for g2 in range(nope_w // grp):
        km_ref[:, g2 * grp:(g2 + 1) * grp] = _dot(ckvn, wkv_ref[:, g2 * grp:(g2 + 1) * grp]).astype(bf)
    km_ref[:, nope_w:nope_w + LANES] = kr_first
    for g2 in range(MLA_WIDTH // grp):
        gm_ref[:, g2 * grp:(g2 + 1) * grp] = _silu(_dot(h, cols(512 + g2 * grp))).astype(bf)

    kv_s = _dot(h, cols(1024))
    k01 = _rope(kv_s[:, 0:LANES], cs, ss)
    ks_ref[:, 0 * LANES:1 * LANES] = jnp.where(first, k01, 0.0).astype(bf)
    ks_ref[:, 1 * LANES:2 * LANES] = jnp.where(first, 0.0, pltpu.roll(k01, LANES // 4, 1)).astype(bf)
    ks_ref[:, 2 * LANES:3 * LANES] = jnp.where(first, pltpu.roll(k01, 3 * LANES // 4, 1), 0.0).astype(bf)
    ks_ref[:, 3 * LANES:4 * LANES] = jnp.where(first, 0.0, k01).astype(bf)
    vst_ref[...] = kv_s[:, LANES:2 * LANES].T.astype(bf)

    for g2 in range(SWA_WIDTH // grp):
        gs_ref[:, g2 * grp:(g2 + 1) * grp] = _silu(_dot(h, cols(1280 + g2 * grp))).astype(bf)

    vt_ref[...] = _dot_nt(wvt_ref[...], ckvn).astype(bf)


def _proj_call(x2, ln_g, w_in_p, w_qst, q_g, w_qt, kv_g, w_k_p, w_vt, tabs_m, tabs_s, tabs_t,
               tabs_st, seq):
    n, d = x2.shape
    tm = PROJ_ROWS
    steps_per_seq = seq // tm
    batch = n // seq
    row = lambda i: (i, 0)
    const = lambda i: (0, 0)
    tab = lambda i: (i % steps_per_seq, 0)
    tab_t = lambda i: (0, i % steps_per_seq)
    col = lambda i: (i // steps_per_seq, i % steps_per_seq)
    bf = jnp.bfloat16
    qt_rows = MLA_HEADS * MLA_QK_PAD
    row_out = lambda w: (pl.BlockSpec((tm, w), row), jax.ShapeDtypeStruct((n, w), bf))
    col_out = lambda r: (pl.BlockSpec((r, tm), col), jax.ShapeDtypeStruct((batch * r, seq), bf))
    outs = [col_out(qt_rows), row_out(MLA_K_COLS), col_out(MLA_WIDTH), row_out(MLA_WIDTH),
            col_out(SWA_WIDTH), row_out(SWA_WIDTH), col_out(SWA_KV_HEADS * SWA_D), row_out(SWA_WIDTH)]
    return pl.pallas_call(
        _proj_kernel,
        grid=(n // tm,),
        in_specs=[
            pl.BlockSpec((tm, d), row),
            pl.BlockSpec((1, d), const),
            pl.BlockSpec(w_in_p.shape, const),
            pl.BlockSpec(w_qst.shape, const),
            pl.BlockSpec((1, MLA_Q_RANK), const),
            pl.BlockSpec(w_qt.shape, const),
            pl.BlockSpec((1, MLA_KV_RANK), const),
            pl.BlockSpec(w_k_p.shape, const),
            pl.BlockSpec(w_vt.shape, const),
        ] + [pl.BlockSpec((tm, LANES), tab)] * 4
          + [pl.BlockSpec((MLA_ROPE // 2, tm), tab_t)] * 2
          + [pl.BlockSpec((SWA_D // 2, tm), tab_t)] * 2,
        out_specs=[o[0] for o in outs],
        out_shape=[o[1] for o in outs],
        compiler_params=pltpu.CompilerParams(
            dimension_semantics=("arbitrary",),
            vmem_limit_bytes=V7X_VMEM_LIMIT_BYTES),
        name="proj",
    )(x2, ln_g, w_in_p, w_qst, q_g, w_qt, kv_g, w_k_p, w_vt, *tabs_m, *tabs_s, *tabs_t, *tabs_st)


def _tree(op, xs):
    xs = list(xs)
    while len(xs) > 1:
        xs = [op(xs[a], xs[a + 1]) if a + 1 < len(xs) else xs[a] for a in range(0, len(xs), 2)]
    return xs[0]

def _mla_block_groups(nblk):
    groups, cur, load = [], [], 0
    for i in range(nblk - 1, -1, -1):
        if load + i + 1 > MLA_GROUP_TILES:
            groups.append(cur)
            cur, load = [], 0
        cur.append(i)
        load += i + 1
    groups.append(cur)
    return groups


def _mla_kernel(qt_ref, k_ref, kr_ref, vt_ref, g_ref, o_ref, s_ref, p_ref, vx_ref):
    t = MLA_BLOCK
    seq = k_ref.shape[0]
    sub = SUBLANES
    qk = MLA_QK_PAD
    r = lax.broadcasted_iota(jnp.int32, (t, t), 0)
    c = lax.broadcasted_iota(jnp.int32, (t, t), 1)
    causal = r <= c
    nblk = seq // t
    units = []
    for hd in range(MLA_HEADS_PER_STEP):
        for blocks in _mla_block_groups(nblk):
            members, off = [], 0
            for i in blocks:
                members.append((hd, i, off))
                off += (i + 1) * t
            assert off <= s_ref.shape[1]
            units.append(members)
    state = [[dict(m8=None) for _ in members] for members in units]

    for hd in range(MLA_HEADS_PER_STEP):
        vx_ref[hd, 0:MLA_V, :] = vt_ref[hd * MLA_V:(hd + 1) * MLA_V, :]
        vx_ref[hd, MLA_V:, :] = jnp.ones((vx_ref.shape[1] - MLA_V, seq), vx_ref.dtype)

    def score_tile(u, b, j):
        hd, i, off = units[u][b]
        st = state[u][b]
        keys = slice(j * t, (j + 1) * t)
        k_tile = jnp.concatenate([k_ref[keys, hd * MLA_NOPE:(hd + 1) * MLA_NOPE], kr_ref[keys, :]], axis=1)
        s = _dot(k_tile, qt_ref[hd * qk:(hd + 1) * qk, i * t:(i + 1) * t])
        if j == i:
            s = jnp.where(causal, s, NEG_INF)
        s_ref[u % 2, off + j * t:off + (j + 1) * t, :] = s
        m8 = _tree(jnp.maximum, [s[a * sub:(a + 1) * sub, :] for a in range(t // sub)])
        st["m8"] = m8 if st["m8"] is None else jnp.maximum(st["m8"], m8)

    def prob_tile(u, b, j):
        _, _, off = units[u][b]
        st = state[u][b]
        if "m" not in st:
            st["m"] = jnp.max(st["m8"], axis=0, keepdims=True)
        rows = slice(off + j * t, off + (j + 1) * t)
        p_ref[u % 3, rows, :] = jnp.exp2(s_ref[u % 2, rows, :] - st["m"]).astype(jnp.bfloat16)

    def value_matmul(u):
        for b, (hd, i, off) in enumerate(units[u]):
            kv = (i + 1) * t
            state[u][b]["acc"] = _dot(vx_ref[hd, :, 0:kv], p_ref[u % 3, off:off + kv, :])

    def store(u):
        for b, (hd, i, _) in enumerate(units[u]):
            rows = slice(i * t, (i + 1) * t)
            acc = state[u][b]["acc"]
            out_t = acc[0:MLA_V, :] / acc[MLA_V:MLA_V + 1, :]
            gate = g_ref[rows, hd * MLA_V:(hd + 1) * MLA_V].astype(jnp.float32)
            o_ref[rows, hd * MLA_V:(hd + 1) * MLA_V] = (out_t.T * gate).astype(o_ref.dtype)
        state[u] = None

    def tiles(fn, u):
        return [functools.partial(fn, u, b, j) for b, (_, i, _) in enumerate(units[u]) for j in range(i + 1)]

    nu = len(units)
    for task in tiles(score_tile, 0):
        task()
    for u in range(nu + 2):
        if 0 <= u - 2 < nu:
            store(u - 2)
        nxt = tiles(score_tile, u + 1) if u + 1 < nu else []
        cur = tiles(prob_tile, u) if u < nu else []
        while nxt or cur:
            if cur:
                cur.pop(0)()
            if nxt:
                nxt.pop(0)()
        if 0 <= u - 1 < nu:
            value_matmul(u - 1)


def _mla_call(qt, km, vt, gm, batch, seq):
    n = km.shape[0]
    t = MLA_BLOCK
    hps = MLA_HEADS_PER_STEP
    groups = MLA_HEADS // hps
    ones_rows = 16
    feat = lambda b, h: (b * groups + h, 0)
    tok = lambda b, h: (b, h)
    return pl.pallas_call(
        _mla_kernel,
        grid=(batch, groups),
        in_specs=[
            pl.BlockSpec((hps * MLA_QK_PAD, seq), feat),
            pl.BlockSpec((seq, hps * MLA_NOPE), tok),
            pl.BlockSpec((seq, LANES), lambda b, h: (b, MLA_HEADS * MLA_NOPE // LANES)),
            pl.BlockSpec((hps * MLA_V, seq), feat),
            pl.BlockSpec((seq, hps * MLA_V), tok),
        ],
        out_specs=pl.BlockSpec((seq, hps * MLA_V), tok),
        out_shape=jax.ShapeDtypeStruct((n, MLA_WIDTH), jnp.bfloat16),
        scratch_shapes=[pltpu.VMEM((2, MLA_GROUP_TILES * t, t), jnp.float32),
                        pltpu.VMEM((3, MLA_GROUP_TILES * t, t), jnp.bfloat16),
                        pltpu.VMEM((hps, MLA_V + ones_rows, seq), jnp.bfloat16)],
        compiler_params=pltpu.CompilerParams(
            dimension_semantics=("arbitrary", "arbitrary"),
            vmem_limit_bytes=V7X_VMEM_LIMIT_BYTES),
        name="mla",
    )(qt, km, km, vt, gm)


def _swa_out_kernel(sink_ref, qt_ref, k_ref, kh_ref, vt_ref, vth_ref, g_ref, mm_ref, x_ref,
                    w_ref, fg_ref, o_ref, ms_ref, kx_ref, vx_ref):
    w = SWA_WINDOW
    chunk = pl.program_id(1)
    rows_total = k_ref.shape[0]
    bf = jnp.bfloat16
    sub = SUBLANES

    kx_ref[0:w, :] = kh_ref[...]
    kx_ref[w:, :] = k_ref[...]
    for hk in range(SWA_KV_HEADS):
        vx_ref[hk, 0:SWA_D, 0:w] = vth_ref[hk * SWA_D:(hk + 1) * SWA_D, :]
        vx_ref[hk, 0:SWA_D, w:] = vt_ref[hk * SWA_D:(hk + 1) * SWA_D, :]
        vx_ref[hk, SWA_D:, :] = jnp.ones((vx_ref.shape[1] - SWA_D, vx_ref.shape[2]), bf)

    ki = lax.broadcasted_iota(jnp.int32, (2 * w, 2 * w), 0)
    qi = lax.broadcasted_iota(jnp.int32, (2 * w, 2 * w), 1) % w
    rel = qi + w - ki
    band = (rel >= 0) & (rel < SWA_WINDOW)
    band_first = band & ((ki >= w) | (chunk > 0))
    lane = lax.broadcasted_iota(jnp.int32, (1, 2 * w), 1)

    units = [(n, hk) for n in range(rows_total // w) for hk in range(SWA_KV_HEADS)]
    state = [dict() for _ in units]

    def sink_rows(hk):
        sink = [sink_ref[hk * SWA_GROUP + i] * LOG2E for i in range(SWA_GROUP)]
        return (jnp.where(lane < w, sink[0], sink[2]), jnp.where(lane < w, sink[1], sink[3]))

    def stage_scores(u, half):
        n, hk = units[u]
        base = hk * 2 * LANES
        qcols = slice(n * w, (n + 1) * w)
        keys = slice(n * w, (n + 2) * w)
        qt = jnp.concatenate([qt_ref[base:base + LANES, qcols],
                              qt_ref[base + LANES:base + 2 * LANES, qcols]], axis=1)
        valid = band_first if n == 0 else band
        st = state[u]
        lanes = slice(base + half * LANES, base + (half + 1) * LANES)
        s = jnp.where(valid, _dot(kx_ref[keys, lanes], qt), NEG_INF)
        m8 = _tree(jnp.maximum, [s[a * sub:(a + 1) * sub, :] for a in range(2 * w // sub)])
        st.setdefault("s", {})[half] = s
        st.setdefault("m", {})[half] = jnp.maximum(jnp.max(m8, axis=0, keepdims=True),
                                                   sink_rows(hk)[half])

    def stage_values(u, half):
        n, hk = units[u]
        keys = slice(n * w, (n + 2) * w)
        st = state[u]
        p = jnp.exp2(st["s"].pop(half) - st["m"][half]).astype(bf)
        st.setdefault("acc", {})[half] = _dot(vx_ref[hk, :, keys], p)

    def stage_store(u):
        n, hk = units[u]
        base = hk * 2 * LANES
        qcols = slice(n * w, (n + 1) * w)
        sinks = sink_rows(hk)
        st = state[u]
        outs = []
        for half in range(2):
            acc = st["acc"][half]
            den = acc[SWA_D:SWA_D + 1, :] + jnp.exp2(sinks[half] - st["m"][half])
            outs.append(acc[0:SWA_D, :] / den)
        g = g_ref[qcols, base:base + 2 * LANES].astype(jnp.float32)
        for ch in range(2):
            o_t = jnp.concatenate([outs[0][:, ch * w:(ch + 1) * w],
                                   outs[1][:, ch * w:(ch + 1) * w]], axis=0)
            ms_ref[qcols, base + ch * LANES:base + (ch + 1) * LANES] = (
                o_t.T * g[:, ch * LANES:(ch + 1) * LANES]).astype(bf)
        state[u] = None

    t = SWA_OUT_SUBTILE
    units_per_tile = (t // w) * SWA_KV_HEADS

    d_model = x_ref.shape[1]
    grp = MXU_COLS
    out_state = {}

    def out_group(i, gc):
        rows = slice(i * t, (i + 1) * t)
        cols = slice(gc * grp, (gc + 1) * grp)
        y = (x_ref[rows, cols] + _dot(mm_ref[rows, :], w_ref[0:MLA_WIDTH, cols])
             + _dot(ms_ref[rows, :], w_ref[MLA_WIDTH:, cols]))
        out_state.setdefault(i, []).append(y)

    def out_norm(i):
        rows = slice(i * t, (i + 1) * t)
        ys = out_state.pop(i)
        ssq = _tree(jnp.add, [jnp.sum(y * y, axis=-1, keepdims=True) for y in ys])
        scale = lax.rsqrt(ssq / d_model + NORM_EPS)
        for gc, y in enumerate(ys):
            cols = slice(gc * grp, (gc + 1) * grp)
            o_ref[rows, cols] = y * scale * fg_ref[:, cols]

    nu = len(units)
    lag = SWA_STORE_LAG
    gsz = SWA_UNITS_PER_STEP
    pending = []
    tasks_per_step = SWA_OUT_TASKS_PER_STEP

    def retire(count):
        for _ in range(min(count, len(pending))):
            pending.pop(0)()

    for step in range(nu // gsz + lag):
        for u in range(step * gsz, (step + 1) * gsz):
            if u < nu:
                stage_scores(u, 0)
                stage_scores(u, 1)
        retire(tasks_per_step // 2)
        for u in range((step - 1) * gsz, step * gsz):
            if 0 <= u < nu:
                stage_values(u, 0)
                stage_values(u, 1)
        retire(tasks_per_step - tasks_per_step // 2)
        for u in range((step - lag) * gsz, (step - lag + 1) * gsz):
            if 0 <= u < nu:
                stage_store(u)
                if (u + 1) % units_per_tile == 0:
                    i = u // units_per_tile
                    pending += [functools.partial(out_group, i, gc) for gc in range(d_model // grp)]
                    pending.append(functools.partial(out_norm, i))
    for task in pending:
        task()


def _swa_out_call(sinks, qst, ks4, vst, gs, mm, x2, w_o, fg, batch, seq):
    n, d = x2.shape
    r = SWA_OUT_ROWS
    w = SWA_WINDOW
    cps = seq // r
    bpc = r // w
    bps = seq // w
    v_rows = SWA_KV_HEADS * SWA_D
    ones_rows = 16
    row = lambda b, c: (b * cps + c, 0)
    halo = lambda b, c: (b * bps + jnp.maximum(c * bpc - 1, 0), 0)
    feat = lambda b, c: (b, c)
    feat_halo = lambda b, c: (b, jnp.maximum(c * bpc - 1, 0))
    const = lambda b, c: (0, 0)
    wide = pl.BlockSpec((r, SWA_WIDTH), row)
    return pl.pallas_call(
        _swa_out_kernel,
        grid=(batch, cps),
        in_specs=[
            pl.BlockSpec(memory_space=pltpu.SMEM),
            pl.BlockSpec((SWA_WIDTH, r), feat),
            wide,
            pl.BlockSpec((w, SWA_WIDTH), halo),
            pl.BlockSpec((v_rows, r), feat),
            pl.BlockSpec((v_rows, w), feat_halo),
            wide,
            pl.BlockSpec((r, MLA_WIDTH), row),
            pl.BlockSpec((r, d), row),
            pl.BlockSpec(w_o.shape, const),
            pl.BlockSpec((1, d), const),
        ],
        out_specs=pl.BlockSpec((r, d), row),
        out_shape=jax.ShapeDtypeStruct((n, d), jnp.float32),
        scratch_shapes=[pltpu.VMEM((r, SWA_WIDTH), jnp.bfloat16),
                        pltpu.VMEM((r + w, SWA_WIDTH), jnp.bfloat16),
                        pltpu.VMEM((SWA_KV_HEADS, SWA_D + ones_rows, r + w), jnp.bfloat16)],
        compiler_params=pltpu.CompilerParams(
            dimension_semantics=("arbitrary", "arbitrary"),
            vmem_limit_bytes=V7X_VMEM_LIMIT_BYTES),
        name="swa_out",
    )(sinks, qst, ks4, ks4, vst, vst, gs, mm, x2, w_o, fg)


def _pair_rope_pieces(src, dst, n_heads, dim):
    half = dim // 2
    out = []
    for pair in range(n_heads // 2):
        for which in range(2):
            head = src + (2 * pair + which) * dim
            chunk = dst + pair * 2 * dim
            out.append((head, chunk + which * half, half))
            out.append((head + half, chunk + 2 * half + which * half, half))
    return out


def _w_in_pieces():
    pieces, qs_pieces, src, dst = [], [], 0, 0
    for name, width in (("c_q", MLA_Q_RANK), ("c_kv", MLA_KV_RANK), ("k_rope", MLA_ROPE),
                        ("g_mla", MLA_WIDTH), ("q_s", SWA_WIDTH),
                        ("k_s", SWA_KV_HEADS * SWA_D), ("v_s", SWA_KV_HEADS * SWA_D),
                        ("g_swa", SWA_WIDTH)):
        if name == "k_rope":
            half = MLA_ROPE // 2
            pieces += [(src, dst, half), (src, dst + half, half),
                       (src + half, dst + 2 * half, half), (src + half, dst + 3 * half, half)]
            dst += 2 * MLA_ROPE
        elif name == "q_s":
            qs_pieces += _pair_rope_pieces(src, 0, SWA_Q_HEADS, SWA_D)
        elif name == "k_s":
            pieces += _pair_rope_pieces(src, dst, SWA_KV_HEADS, SWA_D)
            dst += width
        else:
            pieces.append((src, dst, width))
            dst += width
        src += width
    assert dst == PROJ_COLS
    return pieces, qs_pieces


def _gather_cols(src_ref, pieces, n_dst_cols):
    rows, n_src = src_ref.shape
    lane = lax.broadcasted_iota(jnp.int32, (rows, LANES), 1)
    loaded, rolled = {}, {}

    def load(k):
        if k not in loaded:
            width = min(LANES, n_src - k * LANES)
            v = src_ref[:, k * LANES:k * LANES + width]
            if width < LANES:
                v = jnp.concatenate([v, jnp.zeros((rows, LANES - width), v.dtype)], axis=1)
            loaded[k] = v
        return loaded[k]

    def shifted(k, shift):
        if shift == 0:
            return load(k)
        if (k, shift) not in rolled:
            rolled[(k, shift)] = pltpu.roll(load(k), shift, 1)
        return rolled[(k, shift)]

    chunks = []
    for d in range(n_dst_cols // LANES):
        acc = None
        for s0, d0, w in pieces:
            lo, hi = max(d0, d * LANES), min(d0 + w, (d + 1) * LANES)
            while lo < hi:
                s_abs = s0 + lo - d0
                k, a = divmod(s_abs, LANES)
                n = min(hi - lo, LANES - a)
                a_dst = lo - d * LANES
                val = shifted(k, (a_dst - a) % LANES)
                if n == LANES:
                    acc = val
                else:
                    mask = (lane >= a_dst) & (lane < a_dst + n)
                    acc = jnp.where(mask, val, 0.0 if acc is None else acc)
                lo += n
        chunks.append(acc)
    return chunks


def _gather_rows(src_ref, pieces, d):
    parts = []
    for s0, d0, w in pieces:
        lo, hi = max(d0, d * LANES), min(d0 + w, (d + 1) * LANES)
        if lo < hi:
            parts.append((lo, src_ref[s0 + lo - d0:s0 + hi - d0, :]))
    parts.sort(key=lambda t: t[0])
    assert sum(p.shape[0] for _, p in parts) == LANES
    return jnp.concatenate([p for _, p in parts], axis=0) if len(parts) > 1 else parts[0][1]


def _prep_big_kernel(wint_ref, wout_ref, winp_ref, wqst_ref, wo_ref):
    pieces, qs_pieces = _w_in_pieces()
    for d in range(PROJ_COLS // LANES):
        winp_ref[:, d * LANES:(d + 1) * LANES] = _gather_rows(wint_ref, pieces, d).T.astype(winp_ref.dtype)
    for d in range(SWA_WIDTH // LANES):
        wqst_ref[d * LANES:(d + 1) * LANES, :] = _gather_rows(wint_ref, qs_pieces, d).astype(wqst_ref.dtype)
    wo_ref[...] = wout_ref[...].astype(wo_ref.dtype)


def _prep_small_kernel(wq_ref, wkv_ref, wqt_ref, wk_ref, wvt_ref):
    q_pieces = ([(hd * MLA_QK, hd * MLA_NOPE, MLA_NOPE) for hd in range(MLA_HEADS)]
                + [(hd * MLA_QK + MLA_NOPE, MLA_HEADS * MLA_NOPE + hd * MLA_ROPE, MLA_ROPE)
                   for hd in range(MLA_HEADS)])
    wq = jnp.concatenate(_gather_cols(wq_ref, q_pieces, MLA_HEADS * MLA_QK), axis=1)
    wqt_ref[...] = wq.T.astype(wqt_ref.dtype)
    per_head = MLA_NOPE + MLA_V
    for hd in range(MLA_HEADS):
        wk_ref[:, hd * MLA_NOPE:(hd + 1) * MLA_NOPE] = (
            wkv_ref[:, hd * per_head:hd * per_head + MLA_NOPE].astype(wk_ref.dtype))
        wvt_ref[hd * MLA_V:(hd + 1) * MLA_V, :] = (
            wkv_ref[:, hd * per_head + MLA_NOPE:(hd + 1) * per_head].T.astype(wvt_ref.dtype))


def _prep_weights(w_in, w_q_up, w_kv_up, w_out):
    bf = jnp.bfloat16
    d, n_in = w_in.shape
    rb = PREP_ROWS
    params = pltpu.CompilerParams(dimension_semantics=("arbitrary",),
                                  vmem_limit_bytes=V7X_VMEM_LIMIT_BYTES)
    w_in_p, w_qst, w_o = pl.pallas_call(
        _prep_big_kernel,
        grid=(d // rb,),
        in_specs=[pl.BlockSpec((n_in, rb), lambda i: (0, i)),
                  pl.BlockSpec((rb, w_out.shape[1]), lambda i: (i, 0))],
        out_specs=[pl.BlockSpec((rb, PROJ_COLS), lambda i: (i, 0)),
                   pl.BlockSpec((SWA_WIDTH, rb), lambda i: (0, i)),
                   pl.BlockSpec((rb, w_out.shape[1]), lambda i: (i, 0))],
        out_shape=[jax.ShapeDtypeStruct((d, PROJ_COLS), bf),
                   jax.ShapeDtypeStruct((SWA_WIDTH, d), bf),
                   jax.ShapeDtypeStruct(w_out.shape, bf)],
        compiler_params=params,
        name="prep_big",
    )(w_in.T, w_out)
    w_qt, w_k_p, w_vt = pl.pallas_call(
        _prep_small_kernel,
        out_shape=[jax.ShapeDtypeStruct((MLA_HEADS * MLA_QK, MLA_Q_RANK), bf),
                   jax.ShapeDtypeStruct((MLA_KV_RANK, MLA_HEADS * MLA_NOPE), bf),
                   jax.ShapeDtypeStruct((MLA_WIDTH, MLA_KV_RANK), bf)],
        name="prep_small",
    )(w_q_up, w_kv_up)
    return w_in_p, w_qst, w_qt, w_k_p, w_vt, w_o


def kernel(x, ln_mix, w_in, q_a_norm, w_q_up, kv_a_norm, w_kv_up, attn_sinks, w_out, final_norm):
    batch, seq, d = x.shape
    depth = ln_mix.shape[0]
    assert depth == 1, "final norm is fused into the single layer's output kernel"
    assert seq % MLA_BLOCK == 0 and seq % PROJ_ROWS == 0 and seq % SWA_OUT_ROWS == 0
    tables = {dim: _rope_tables(seq, dim) for dim in {MLA_ROPE, SWA_D}}
    tabs_t, tabs_m = tables[MLA_ROPE]
    tabs_st, tabs_s = tables[SWA_D]
    x2 = x.reshape(batch * seq, d)
    w_in_p, w_qst, w_qt, w_k_p, w_vt, w_o = _prep_weights(w_in[0], w_q_up[0], w_kv_up[0], w_out[0])
    qt, km, vt, gm, qst, ks4, vst, gs = _proj_call(
        x2, ln_mix[0].reshape(1, -1), w_in_p, w_qst, q_a_norm[0].reshape(1, -1), w_qt,
        kv_a_norm[0].reshape(1, -1), w_k_p, w_vt, tabs_m, tabs_s, tabs_t, tabs_st, seq)
    mm = _mla_call(qt, km, vt, gm, batch, seq)
    out = _swa_out_call(attn_sinks[0], qst, ks4, vst, gs, mm, x2, w_o,
                        final_norm.reshape(1, -1), batch, seq)
    return out.reshape(batch, seq, d)
```

```python
import functools
import math

import jax
import jax.numpy as jnp
from jax import lax
from jax.experimental import pallas as pl
from jax.experimental.pallas import tpu as pltpu

ROPE_THETA = 10000.0
NORM_EPS = 1e-6
NEG_INF = -1e30
LOG2E = 1.4426950408889634

MLA_HEADS = 4
MLA_NOPE = 128
MLA_ROPE = 64
MLA_V = 128
MLA_Q_RANK = 256
MLA_KV_RANK = 128
MLA_QK = MLA_NOPE + MLA_ROPE
MLA_WIDTH = MLA_HEADS * MLA_V

SWA_Q_HEADS = 8
SWA_KV_HEADS = 2
SWA_D = 64
SWA_WINDOW = 128
SWA_GROUP = SWA_Q_HEADS // SWA_KV_HEADS
SWA_WIDTH = SWA_Q_HEADS * SWA_D

LANES = 128
SUBLANES = 8
MXU_COLS = 2 * LANES
MLA_QK_PAD = MXU_COLS
MLA_K_COLS = MLA_HEADS * MLA_NOPE + LANES
V7X_VMEM_LIMIT_BYTES = 56 * 1024 * 1024

PROJ_ROWS = 1024
MLA_BLOCK = 256
MLA_GROUP_TILES = 8
MLA_HEADS_PER_STEP = 2
SWA_OUT_ROWS = 1024
SWA_OUT_SUBTILE = 256
SWA_UNITS_PER_STEP = 4
SWA_OUT_TASKS_PER_STEP = 4
SWA_STORE_LAG = 2


def _rope_tables(seq, dim):
    assert 2 * dim == LANES
    half = dim // 2
    inv_freq = 1.0 / (ROPE_THETA ** (jnp.arange(0, dim, 2, dtype=jnp.float32) / dim))
    f32 = jnp.float32
    ct, st, c_lane, s_lane = pl.pallas_call(
        _rope_table_kernel,
        out_shape=[jax.ShapeDtypeStruct((half, seq), f32), jax.ShapeDtypeStruct((half, seq), f32),
                   jax.ShapeDtypeStruct((seq, LANES), f32), jax.ShapeDtypeStruct((seq, LANES), f32)],
        name="rope_tables",
    )(inv_freq.reshape(half, 1))
    return (ct, st), (c_lane, s_lane)


def _rope_table_kernel(inv_ref, ct_ref, st_ref, cl_ref, sl_ref):
    half, seq = ct_ref.shape
    pos = lax.broadcasted_iota(jnp.int32, (half, seq), 1).astype(jnp.float32)
    ang = pos * inv_ref[...]
    c, s = jnp.cos(ang), jnp.sin(ang)
    ct_ref[...] = c
    st_ref[...] = s
    cl_ref[...] = jnp.concatenate([c, c, c, c], axis=0).T
    sl_ref[...] = jnp.concatenate([-s, -s, s, s], axis=0).T


def _rope(x, c, s):
    return x * c + pltpu.roll(x, LANES // 2, 1) * s


def _rms(x, g):
    return x * lax.rsqrt(jnp.mean(x * x, axis=-1, keepdims=True) + NORM_EPS) * g


def _silu(g):
    return g / (1.0 + jnp.exp(-g))


def _dot(a, b):
    return jnp.dot(a, b, preferred_element_type=jnp.float32)


def _dot_nt(a, b):
    return lax.dot_general(a, b, (((1,), (1,)), ((), ())),
                           preferred_element_type=jnp.float32)


PROJ_COLS = 1792


def _proj_kernel(x_ref, ln_ref, wint_ref, qg_ref, wqt_ref, kvg_ref, wkv_ref, wvt_ref,
                 cm_ref, sm_ref, cs_ref, ss_ref, ct_ref, st_ref, cst_ref, sst_ref,
                 qt_ref, km_ref, vt_ref, gm_ref, qst_ref, ks_ref, vst_ref, gs_ref, wt_ref):
    bf = jnp.bfloat16

    @pl.when(pl.program_id(0) == 0)
    def _():
        pieces, qs_pieces = _w_in_pieces()
        for d in range(PROJ_COLS // LANES):
            wt_ref[d * LANES:(d + 1) * LANES, :] = _gather_rows(wint_ref, pieces, d).astype(bf)
        for d in range(SWA_WIDTH // LANES):
            wt_ref[PROJ_COLS + d * LANES:PROJ_COLS + (d + 1) * LANES, :] = (
                _gather_rows(wint_ref, qs_pieces, d).astype(bf))

    x = x_ref[...]
    h = _rms(x, ln_ref[...]).astype(bf)

    cm, sm, cs, ss = cm_ref[...], sm_ref[...], cs_ref[...], ss_ref[...]
    lane = lax.broadcasted_iota(jnp.int32, (x.shape[0], LANES), 1)
    first = (lane % (LANES // 2)) < (LANES // 4)
    grp = MXU_COLS

    def project(a):
        return _dot_nt(h, wt_ref[a:a + grp, :])

    c_q = project(0)
    lat_b = project(256)
    cqn = _rms(c_q, qg_ref[...]).astype(bf)
    ckvn = _rms(lat_b[:, 0:MLA_KV_RANK], kvg_ref[...]).astype(bf)
    kr = _rope(lat_b[:, LANES:2 * LANES], cm, sm)
    kr_first = jnp.where(first, kr, 0.0).astype(bf)

    s_scale = LOG2E / math.sqrt(SWA_D)
    qst = _dot_nt(wt_ref[PROJ_COLS:PROJ_COLS + SWA_WIDTH, :], h)
    cst2 = jnp.concatenate([cst_ref[...], cst_ref[...]], axis=0)
    sst2 = jnp.concatenate([sst_ref[...], sst_ref[...]], axis=0)
    for ch in range(SWA_WIDTH // LANES):
        x1 = qst[ch * LANES:ch * LANES + SWA_D, :]
        x2 = qst[ch * LANES + SWA_D:(ch + 1) * LANES, :]
        qst_ref[ch * LANES:ch * LANES + SWA_D, :] = ((x1 * cst2 - x2 * sst2) * s_scale).astype(bf)
        qst_ref[ch * LANES + SWA_D:(ch + 1) * LANES, :] = ((x2 * cst2 + x1 * sst2) * s_scale).astype(bf)

    q_scale = LOG2E / math.sqrt(MLA_QK)
    nope_w = MLA_HEADS * MLA_NOPE
    half = MLA_ROPE // 2
    qt = _dot_nt(wqt_ref[...], cqn)
    ct, st = ct_ref[...], st_ref[...]
    zeros = jnp.zeros((half, x.shape[0]), bf)
    for hd in range(MLA_HEADS):
        qt_ref[hd * grp:hd * grp + LANES, :] = (qt[hd * MLA_NOPE:(hd + 1) * MLA_NOPE, :] * q_scale).astype(bf)
        x1 = qt[nope_w + hd * MLA_ROPE:nope_w + hd * MLA_ROPE + half, :]
        x2 = qt[nope_w + hd * MLA_ROPE + half:nope_w + (hd + 1) * MLA_ROPE, :]
        base = hd * grp + LANES
        qt_ref[base:base + half, :] = ((x1 * ct - x2 * st) * q_scale).astype(bf)
        qt_ref[base + half:base + 2 * half, :] = zeros
        qt_ref[base + 2 * half:base + 3 * half, :] = ((x2 * ct + x1 * st) * q_scale).astype(bf)
        qt_ref[base + 3 * half:base + 4 * half, :] = zeros
    for g2 in range(nope_w // grp):
        km_ref[:, g2 * grp:(g2 + 1) * grp] = _dot(ckvn, wkv_ref[:, g2 * grp:(g2 + 1) * grp]).astype(bf)
    km_ref[:, nope_w:nope_w + LANES] = kr_first
    for g2 in range(MLA_WIDTH // grp):
        gm_ref[:, g2 * grp:(g2 + 1) * grp] = _silu(project(512 + g2 * grp)).astype(bf)

    kv_s = project(1024)
    k01 = _rope(kv_s[:, 0:LANES], cs, ss)
    ks_ref[:, 0 * LANES:1 * LANES] = jnp.where(first, k01, 0.0).astype(bf)
    ks_ref[:, 1 * LANES:2 * LANES] = jnp.where(first, 0.0, pltpu.roll(k01, LANES // 4, 1)).astype(bf)
    ks_ref[:, 2 * LANES:3 * LANES] = jnp.where(first, pltpu.roll(k01, 3 * LANES // 4, 1), 0.0).astype(bf)
    ks_ref[:, 3 * LANES:4 * LANES] = jnp.where(first, 0.0, k01).astype(bf)
    vst_ref[...] = kv_s[:, LANES:2 * LANES].T.astype(bf)

    for g2 in range(SWA_WIDTH // grp):
        gs_ref[:, g2 * grp:(g2 + 1) * grp] = _silu(project(1280 + g2 * grp)).astype(bf)

    vt_ref[...] = _dot_nt(wvt_ref[...], ckvn).astype(bf)


def _proj_call(x2, ln_g, w_in_t, q_g, w_qt, kv_g, w_k_p, w_vt, tabs_m, tabs_s, tabs_t,
               tabs_st, seq):
    n, d = x2.shape
    tm = PROJ_ROWS
    steps_per_seq = seq // tm
    batch = n // seq
    row = lambda i: (i, 0)
    const = lambda i: (0, 0)
    tab = lambda i: (i % steps_per_seq, 0)
    tab_t = lambda i: (0, i % steps_per_seq)
    col = lambda i: (i // steps_per_seq, i % steps_per_seq)
    bf = jnp.bfloat16
    qt_rows = MLA_HEADS * MLA_QK_PAD
    row_out = lambda w: (pl.BlockSpec((tm, w), row), jax.ShapeDtypeStruct((n, w), bf))
    col_out = lambda r: (pl.BlockSpec((r, tm), col), jax.ShapeDtypeStruct((batch * r, seq), bf))
    outs = [col_out(qt_rows), row_out(MLA_K_COLS), col_out(MLA_WIDTH), row_out(MLA_WIDTH),
            col_out(SWA_WIDTH), row_out(SWA_WIDTH), col_out(SWA_KV_HEADS * SWA_D), row_out(SWA_WIDTH)]
    return pl.pallas_call(
        _proj_kernel,
        grid=(n // tm,),
        in_specs=[
            pl.BlockSpec((tm, d), row),
            pl.BlockSpec((1, d), const),
            pl.BlockSpec(w_in_t.shape, const),
            pl.BlockSpec((1, MLA_Q_RANK), const),
            pl.BlockSpec(w_qt.shape, const),
            pl.BlockSpec((1, MLA_KV_RANK), const),
            pl.BlockSpec(w_k_p.shape, const),
            pl.BlockSpec(w_vt.shape, const),
        ] + [pl.BlockSpec((tm, LANES), tab)] * 4
          + [pl.BlockSpec((MLA_ROPE // 2, tm), tab_t)] * 2
          + [pl.BlockSpec((SWA_D // 2, tm), tab_t)] * 2,
        out_specs=[o[0] for o in outs],
        out_shape=[o[1] for o in outs],
        scratch_shapes=[pltpu.VMEM((PROJ_COLS + SWA_WIDTH, d), bf)],
        compiler_params=pltpu.CompilerParams(
            dimension_semantics=("arbitrary",),
            vmem_limit_bytes=V7X_VMEM_LIMIT_BYTES),
        name="proj",
    )(x2, ln_g, w_in_t, q_g, w_qt, kv_g, w_k_p, w_vt, *tabs_m, *tabs_s, *tabs_t, *tabs_st)


def _tree(op, xs):
    xs = list(xs)
    while len(xs) > 1:
        xs = [op(xs[a], xs[a + 1]) if a + 1 < len(xs) else xs[a] for a in range(0, len(xs), 2)]
    return xs[0]

def _mla_block_groups(nblk):
    groups, cur, load = [], [], 0
    for i in range(nblk - 1, -1, -1):
        if load + i + 1 > MLA_GROUP_TILES:
            groups.append(cur)
            cur, load = [], 0
        cur.append(i)
        load += i + 1
    groups.append(cur)
    return groups


def _mla_kernel(qt_ref, k_ref, kr_ref, vt_ref, g_ref, o_ref, s_ref, p_ref, vx_ref):
    t = MLA_BLOCK
    seq = k_ref.shape[0]
    sub = SUBLANES
    qk = MLA_QK_PAD
    r = lax.broadcasted_iota(jnp.int32, (t, t), 0)
    c = lax.broadcasted_iota(jnp.int32, (t, t), 1)
    causal = r <= c
    nblk = seq // t
    units = []
    for hd in range(MLA_HEADS_PER_STEP):
        for blocks in _mla_block_groups(nblk):
            members, off = [], 0
            for i in blocks:
                members.append((hd, i, off))
                off += (i + 1) * t
            assert off <= s_ref.shape[1]
            units.append(members)
    state = [[dict(m8=None) for _ in members] for members in units]

    for hd in range(MLA_HEADS_PER_STEP):
        vx_ref[hd, 0:MLA_V, :] = vt_ref[hd * MLA_V:(hd + 1) * MLA_V, :]
        vx_ref[hd, MLA_V:, :] = jnp.ones((vx_ref.shape[1] - MLA_V, seq), vx_ref.dtype)

    def score_tile(u, b, j):
        hd, i, off = units[u][b]
        st = state[u][b]
        keys = slice(j * t, (j + 1) * t)
        k_tile = jnp.concatenate([k_ref[keys, hd * MLA_NOPE:(hd + 1) * MLA_NOPE], kr_ref[keys, :]], axis=1)
        s = _dot(k_tile, qt_ref[hd * qk:(hd + 1) * qk, i * t:(i + 1) * t])
        if j == i:
            s = jnp.where(causal, s, NEG_INF)
        s_ref[u % 2, off + j * t:off + (j + 1) * t, :] = s
        m8 = _tree(jnp.maximum, [s[a * sub:(a + 1) * sub, :] for a in range(t // sub)])
        st["m8"] = m8 if st["m8"] is None else jnp.maximum(st["m8"], m8)

    def prob_tile(u, b, j):
        _, _, off = units[u][b]
        st = state[u][b]
        if "m" not in st:
            st["m"] = jnp.max(st["m8"], axis=0, keepdims=True)
        rows = slice(off + j * t, off + (j + 1) * t)
        p_ref[u % 3, rows, :] = jnp.exp2(s_ref[u % 2, rows, :] - st["m"]).astype(jnp.bfloat16)

    def value_matmul(u):
        for b, (hd, i, off) in enumerate(units[u]):
            kv = (i + 1) * t
            state[u][b]["acc"] = _dot(vx_ref[hd, :, 0:kv], p_ref[u % 3, off:off + kv, :])

    def store(u):
        for b, (hd, i, _) in enumerate(units[u]):
            rows = slice(i * t, (i + 1) * t)
            acc = state[u][b]["acc"]
            out_t = acc[0:MLA_V, :] / acc[MLA_V:MLA_V + 1, :]
            gate = g_ref[rows, hd * MLA_V:(hd + 1) * MLA_V].astype(jnp.float32)
            o_ref[rows, hd * MLA_V:(hd + 1) * MLA_V] = (out_t.T * gate).astype(o_ref.dtype)
        state[u] = None

    def tiles(fn, u):
        return [functools.partial(fn, u, b, j) for b, (_, i, _) in enumerate(units[u]) for j in range(i + 1)]

    nu = len(units)
    for task in tiles(score_tile, 0):
        task()
    for u in range(nu + 2):
        if 0 <= u - 2 < nu:
            store(u - 2)
        nxt = tiles(score_tile, u + 1) if u + 1 < nu else []
        cur = tiles(prob_tile, u) if u < nu else []
        while nxt or cur:
            if cur:
                cur.pop(0)()
            if nxt:
                nxt.pop(0)()
        if 0 <= u - 1 < nu:
            value_matmul(u - 1)


def _mla_call(qt, km, vt, gm, batch, seq):
    n = km.shape[0]
    t = MLA_BLOCK
    hps = MLA_HEADS_PER_STEP
    groups = MLA_HEADS // hps
    ones_rows = 16
    feat = lambda b, h: (b * groups + h, 0)
    tok = lambda b, h: (b, h)
    return pl.pallas_call(
        _mla_kernel,
        grid=(batch, groups),
        in_specs=[
            pl.BlockSpec((hps * MLA_QK_PAD, seq), feat),
            pl.BlockSpec((seq, hps * MLA_NOPE), tok),
            pl.BlockSpec((seq, LANES), lambda b, h: (b, MLA_HEADS * MLA_NOPE // LANES)),
            pl.BlockSpec((hps * MLA_V, seq), feat),
            pl.BlockSpec((seq, hps * MLA_V), tok),
        ],
        out_specs=pl.BlockSpec((seq, hps * MLA_V), tok),
        out_shape=jax.ShapeDtypeStruct((n, MLA_WIDTH), jnp.bfloat16),
        scratch_shapes=[pltpu.VMEM((2, MLA_GROUP_TILES * t, t), jnp.float32),
                        pltpu.VMEM((3, MLA_GROUP_TILES * t, t), jnp.bfloat16),
                        pltpu.VMEM((hps, MLA_V + ones_rows, seq), jnp.bfloat16)],
        compiler_params=pltpu.CompilerParams(
            dimension_semantics=("arbitrary", "arbitrary"),
            vmem_limit_bytes=V7X_VMEM_LIMIT_BYTES),
        name="mla",
    )(qt, km, km, vt, gm)


def _swa_out_kernel(sink_ref, qt_ref, k_ref, kh_ref, vt_ref, vth_ref, g_ref, mm_ref, x_ref,
                    wf_ref, fg_ref, o_ref, ms_ref, kx_ref, vx_ref, w_ref):
    w = SWA_WINDOW
    chunk = pl.program_id(1)
    rows_total = k_ref.shape[0]
    bf = jnp.bfloat16
    sub = SUBLANES

    @pl.when((pl.program_id(0) == 0) & (chunk == 0))
    def _():
        w_ref[...] = wf_ref[...].astype(bf)

    kx_ref[0:w, :] = kh_ref[...]
    kx_ref[w:, :] = k_ref[...]
    for hk in range(SWA_KV_HEADS):
        vx_ref[hk, 0:SWA_D, 0:w] = vth_ref[hk * SWA_D:(hk + 1) * SWA_D, :]
        vx_ref[hk, 0:SWA_D, w:] = vt_ref[hk * SWA_D:(hk + 1) * SWA_D, :]
        vx_ref[hk, SWA_D:, :] = jnp.ones((vx_ref.shape[1] - SWA_D, vx_ref.shape[2]), bf)

    ki = lax.broadcasted_iota(jnp.int32, (2 * w, 2 * w), 0)
    qi = lax.broadcasted_iota(jnp.int32, (2 * w, 2 * w), 1) % w
    rel = qi + w - ki
    band = (rel >= 0) & (rel < SWA_WINDOW)
    band_first = band & ((ki >= w) | (chunk > 0))
    lane = lax.broadcasted_iota(jnp.int32, (1, 2 * w), 1)

    units = [(n, hk) for n in range(rows_total // w) for hk in range(SWA_KV_HEADS)]
    state = [dict() for _ in units]

    def sink_rows(hk):
        sink = [sink_ref[hk * SWA_GROUP + i] * LOG2E for i in range(SWA_GROUP)]
        return (jnp.where(lane < w, sink[0], sink[2]), jnp.where(lane < w, sink[1], sink[3]))

    def stage_scores(u, half):
        n, hk = units[u]
        base = hk * 2 * LANES
        qcols = slice(n * w, (n + 1) * w)
        keys = slice(n * w, (n + 2) * w)
        qt = jnp.concatenate([qt_ref[base:base + LANES, qcols],
                              qt_ref[base + LANES:base + 2 * LANES, qcols]], axis=1)
        valid = band_first if n == 0 else band
        st = state[u]
        lanes = slice(base + half * LANES, base + (half + 1) * LANES)
        s = jnp.where(valid, _dot(kx_ref[keys, lanes], qt), NEG_INF)
        m8 = _tree(jnp.maximum, [s[a * sub:(a + 1) * sub, :] for a in range(2 * w // sub)])
        st.setdefault("s", {})[half] = s
        st.setdefault("m", {})[half] = jnp.maximum(jnp.max(m8, axis=0, keepdims=True),
                                                   sink_rows(hk)[half])

    def stage_values(u, half):
        n, hk = units[u]
        keys = slice(n * w, (n + 2) * w)
        st = state[u]
        p = jnp.exp2(st["s"].pop(half) - st["m"][half]).astype(bf)
        st.setdefault("acc", {})[half] = _dot(vx_ref[hk, :, keys], p)

    def stage_store(u):
        n, hk = units[u]
        base = hk * 2 * LANES
        qcols = slice(n * w, (n + 1) * w)
        sinks = sink_rows(hk)
        st = state[u]
        outs = []
        for half in range(2):
            acc = st["acc"][half]
            den = acc[SWA_D:SWA_D + 1, :] + jnp.exp2(sinks[half] - st["m"][half])
            outs.append(acc[0:SWA_D, :] / den)
        g = g_ref[qcols, base:base + 2 * LANES].astype(jnp.float32)
        for ch in range(2):
            o_t = jnp.concatenate([outs[0][:, ch * w:(ch + 1) * w],
                                   outs[1][:, ch * w:(ch + 1) * w]], axis=0)
            ms_ref[qcols, base + ch * LANES:base + (ch + 1) * LANES] = (
                o_t.T * g[:, ch * LANES:(ch + 1) * LANES]).astype(bf)
        state[u] = None

    t = SWA_OUT_SUBTILE
    units_per_tile = (t // w) * SWA_KV_HEADS

    d_model = x_ref.shape[1]
    grp = MXU_COLS
    out_state = {}

    def out_group(i, gc):
        rows = slice(i * t, (i + 1) * t)
        cols = slice(gc * grp, (gc + 1) * grp)
        y = (x_ref[rows, cols] + _dot(mm_ref[rows, :], w_ref[0:MLA_WIDTH, cols])
             + _dot(ms_ref[rows, :], w_ref[MLA_WIDTH:, cols]))
        out_state.setdefault(i, []).append(y)

    def out_norm(i):
        rows = slice(i * t, (i + 1) * t)
        ys = out_state.pop(i)
        ssq = _tree(jnp.add, [jnp.sum(y * y, axis=-1, keepdims=True) for y in ys])
        scale = lax.rsqrt(ssq / d_model + NORM_EPS)
        for gc, y in enumerate(ys):
            cols = slice(gc * grp, (gc + 1) * grp)
            o_ref[rows, cols] = y * scale * fg_ref[:, cols]

    nu = len(units)
    lag = SWA_STORE_LAG
    gsz = SWA_UNITS_PER_STEP
    pending = []
    tasks_per_step = SWA_OUT_TASKS_PER_STEP

    def retire(count):
        for _ in range(min(count, len(pending))):
            pending.pop(0)()

    for step in range(nu // gsz + lag):
        for u in range(step * gsz, (step + 1) * gsz):
            if u < nu:
                stage_scores(u, 0)
                stage_scores(u, 1)
        retire(tasks_per_step // 2)
        for u in range((step - 1) * gsz, step * gsz):
            if 0 <= u < nu:
                stage_values(u, 0)
                stage_values(u, 1)
        retire(tasks_per_step - tasks_per_step // 2)
        for u in range((step - lag) * gsz, (step - lag + 1) * gsz):
            if 0 <= u < nu:
                stage_store(u)
                if (u + 1) % units_per_tile == 0:
                    i = u // units_per_tile
                    pending += [functools.partial(out_group, i, gc) for gc in range(d_model // grp)]
                    pending.append(functools.partial(out_norm, i))
    for task in pending:
        task()


def _swa_out_call(sinks, qst, ks4, vst, gs, mm, x2, w_o, fg, batch, seq):
    n, d = x2.shape
    r = SWA_OUT_ROWS
    w = SWA_WINDOW
    cps = seq // r
    bpc = r // w
    bps = seq // w
    v_rows = SWA_KV_HEADS * SWA_D
    ones_rows = 16
    row = lambda b, c: (b * cps + c, 0)
    halo = lambda b, c: (b * bps + jnp.maximum(c * bpc - 1, 0), 0)
    feat = lambda b, c: (b, c)
    feat_halo = lambda b, c: (b, jnp.maximum(c * bpc - 1, 0))
    const = lambda b, c: (0, 0)
    wide = pl.BlockSpec((r, SWA_WIDTH), row)
    return pl.pallas_call(
        _swa_out_kernel,
        grid=(batch, cps),
        in_specs=[
            pl.BlockSpec(memory_space=pltpu.SMEM),
            pl.BlockSpec((SWA_WIDTH, r), feat),
            wide,
            pl.BlockSpec((w, SWA_WIDTH), halo),
            pl.BlockSpec((v_rows, r), feat),
            pl.BlockSpec((v_rows, w), feat_halo),
            wide,
            pl.BlockSpec((r, MLA_WIDTH), row),
            pl.BlockSpec((r, d), row),
            pl.BlockSpec(w_o.shape, const),
            pl.BlockSpec((1, d), const),
        ],
        out_specs=pl.BlockSpec((r, d), row),
        out_shape=jax.ShapeDtypeStruct((n, d), jnp.float32),
        scratch_shapes=[pltpu.VMEM((r, SWA_WIDTH), jnp.bfloat16),
                        pltpu.VMEM((r + w, SWA_WIDTH), jnp.bfloat16),
                        pltpu.VMEM((SWA_KV_HEADS, SWA_D + ones_rows, r + w), jnp.bfloat16),
                        pltpu.VMEM(w_o.shape, jnp.bfloat16)],
        compiler_params=pltpu.CompilerParams(
            dimension_semantics=("arbitrary", "arbitrary"),
            vmem_limit_bytes=V7X_VMEM_LIMIT_BYTES),
        name="swa_out",
    )(sinks, qst, ks4, ks4, vst, vst, gs, mm, x2, w_o, fg)


def _pair_rope_pieces(src, dst, n_heads, dim):
    half = dim // 2
    out = []
    for pair in range(n_heads // 2):
        for which in range(2):
            head = src + (2 * pair + which) * dim
            chunk = dst + pair * 2 * dim
            out.append((head, chunk + which * half, half))
            out.append((head + half, chunk + 2 * half + which * half, half))
    return out


def _w_in_pieces():
    pieces, qs_pieces, src, dst = [], [], 0, 0
    for name, width in (("c_q", MLA_Q_RANK), ("c_kv", MLA_KV_RANK), ("k_rope", MLA_ROPE),
                        ("g_mla", MLA_WIDTH), ("q_s", SWA_WIDTH),
                        ("k_s", SWA_KV_HEADS * SWA_D), ("v_s", SWA_KV_HEADS * SWA_D),
                        ("g_swa", SWA_WIDTH)):
        if name == "k_rope":
            half = MLA_ROPE // 2
            pieces += [(src, dst, half), (src, dst + half, half),
                       (src + half, dst + 2 * half, half), (src + half, dst + 3 * half, half)]
            dst += 2 * MLA_ROPE
        elif name == "q_s":
            qs_pieces += _pair_rope_pieces(src, 0, SWA_Q_HEADS, SWA_D)
        elif name == "k_s":
            pieces += _pair_rope_pieces(src, dst, SWA_KV_HEADS, SWA_D)
            dst += width
        else:
            pieces.append((src, dst, width))
            dst += width
        src += width
    assert dst == PROJ_COLS
    return pieces, qs_pieces


def _gather_cols(src_ref, pieces, n_dst_cols):
    rows, n_src = src_ref.shape
    lane = lax.broadcasted_iota(jnp.int32, (rows, LANES), 1)
    loaded, rolled = {}, {}

    def load(k):
        if k not in loaded:
            width = min(LANES, n_src - k * LANES)
            v = src_ref[:, k * LANES:k * LANES + width]
            if width < LANES:
                v = jnp.concatenate([v, jnp.zeros((rows, LANES - width), v.dtype)], axis=1)
            loaded[k] = v
        return loaded[k]

    def shifted(k, shift):
        if shift == 0:
            return load(k)
        if (k, shift) not in rolled:
            rolled[(k, shift)] = pltpu.roll(load(k), shift, 1)
        return rolled[(k, shift)]

    chunks = []
    for d in range(n_dst_cols // LANES):
        acc = None
        for s0, d0, w in pieces:
            lo, hi = max(d0, d * LANES), min(d0 + w, (d + 1) * LANES)
            while lo < hi:
                s_abs = s0 + lo - d0
                k, a = divmod(s_abs, LANES)
                n = min(hi - lo, LANES - a)
                a_dst = lo - d * LANES
                val = shifted(k, (a_dst - a) % LANES)
                if n == LANES:
                    acc = val
                else:
                    mask = (lane >= a_dst) & (lane < a_dst + n)
                    acc = jnp.where(mask, val, 0.0 if acc is None else acc)
                lo += n
        chunks.append(acc)
    return chunks


def _gather_rows(src_ref, pieces, d):
    parts = []
    for s0, d0, w in pieces:
        lo, hi = max(d0, d * LANES), min(d0 + w, (d + 1) * LANES)
        if lo < hi:
            parts.append((lo, src_ref[s0 + lo - d0:s0 + hi - d0, :]))
    parts.sort(key=lambda t: t[0])
    assert sum(p.shape[0] for _, p in parts) == LANES
    return jnp.concatenate([p for _, p in parts], axis=0) if len(parts) > 1 else parts[0][1]


def _prep_small_kernel(wq_ref, wkv_ref, wqt_ref, wk_ref, wvt_ref):
    q_pieces = ([(hd * MLA_QK, hd * MLA_NOPE, MLA_NOPE) for hd in range(MLA_HEADS)]
                + [(hd * MLA_QK + MLA_NOPE, MLA_HEADS * MLA_NOPE + hd * MLA_ROPE, MLA_ROPE)
                   for hd in range(MLA_HEADS)])
    wq = jnp.concatenate(_gather_cols(wq_ref, q_pieces, MLA_HEADS * MLA_QK), axis=1)
    wqt_ref[...] = wq.T.astype(wqt_ref.dtype)
    per_head = MLA_NOPE + MLA_V
    for hd in range(MLA_HEADS):
        wk_ref[:, hd * MLA_NOPE:(hd + 1) * MLA_NOPE] = (
            wkv_ref[:, hd * per_head:hd * per_head + MLA_NOPE].astype(wk_ref.dtype))
        wvt_ref[hd * MLA_V:(hd + 1) * MLA_V, :] = (
            wkv_ref[:, hd * per_head + MLA_NOPE:(hd + 1) * per_head].T.astype(wvt_ref.dtype))


def _prep_weights(w_q_up, w_kv_up):
    bf = jnp.bfloat16
    w_qt, w_k_p, w_vt = pl.pallas_call(
        _prep_small_kernel,
        out_shape=[jax.ShapeDtypeStruct((MLA_HEADS * MLA_QK, MLA_Q_RANK), bf),
                   jax.ShapeDtypeStruct((MLA_KV_RANK, MLA_HEADS * MLA_NOPE), bf),
                   jax.ShapeDtypeStruct((MLA_WIDTH, MLA_KV_RANK), bf)],
        name="prep_small",
    )(w_q_up, w_kv_up)
    return w_qt, w_k_p, w_vt


def kernel(x, ln_mix, w_in, q_a_norm, w_q_up, kv_a_norm, w_kv_up, attn_sinks, w_out, final_norm):
    batch, seq, d = x.shape
    depth = ln_mix.shape[0]
    assert depth == 1, "final norm is fused into the single layer's output kernel"
    assert seq % MLA_BLOCK == 0 and seq % PROJ_ROWS == 0 and seq % SWA_OUT_ROWS == 0
    tables = {dim: _rope_tables(seq, dim) for dim in {MLA_ROPE, SWA_D}}
    tabs_t, tabs_m = tables[MLA_ROPE]
    tabs_st, tabs_s = tables[SWA_D]
    x2 = x.reshape(batch * seq, d)
    w_qt, w_k_p, w_vt = _prep_weights(w_q_up[0], w_kv_up[0])
    qt, km, vt, gm, qst, ks4, vst, gs = _proj_call(
        x2, ln_mix[0].reshape(1, -1), w_in[0].T, q_a_norm[0].reshape(1, -1), w_qt,
        kv_a_norm[0].reshape(1, -1), w_k_p, w_vt, tabs_m, tabs_s, tabs_t, tabs_st, seq)
    mm = _mla_call(qt, km, vt, gm, batch, seq)
    out = _swa_out_call(attn_sinks[0], qst, ks4, vst, gs, mm, x2, w_out[0],
                        final_norm.reshape(1, -1), batch, seq)
    return out.reshape(batch, seq, d)
```

```python
import functools
import math

import jax
import jax.numpy as jnp
from jax import lax
from jax.experimental import pallas as pl
from jax.experimental.pallas import tpu as pltpu

ROPE_THETA = 10000.0
NORM_EPS = 1e-6
NEG_INF = -1e30
LOG2E = 1.4426950408889634

MLA_HEADS = 4
MLA_NOPE = 128
MLA_ROPE = 64
MLA_V = 128
MLA_Q_RANK = 256
MLA_KV_RANK = 128
MLA_QK = MLA_NOPE + MLA_ROPE
MLA_WIDTH = MLA_HEADS * MLA_V

SWA_Q_HEADS = 8
SWA_KV_HEADS = 2
SWA_D = 64
SWA_WINDOW = 128
SWA_GROUP = SWA_Q_HEADS // SWA_KV_HEADS
SWA_WIDTH = SWA_Q_HEADS * SWA_D

LANES = 128
SUBLANES = 8
MXU_COLS = 2 * LANES
MLA_QK_PAD = MXU_COLS
MLA_K_COLS = MLA_HEADS * MLA_NOPE + LANES
V7X_VMEM_LIMIT_BYTES = 56 * 1024 * 1024

PROJ_ROWS = 1024
MLA_BLOCK = 256
MLA_GROUP_TILES = 8
MLA_HEADS_PER_STEP = 2
SWA_OUT_ROWS = 1024
SWA_OUT_SUBTILE = 256
SWA_UNITS_PER_STEP = 4
SWA_OUT_TASKS_PER_STEP = 4
SWA_STORE_LAG = 2


def _rope_tables(seq, dim):
    assert 2 * dim == LANES
    half = dim // 2
    inv_freq = 1.0 / (ROPE_THETA ** (jnp.arange(0, dim, 2, dtype=jnp.float32) / dim))
    f32 = jnp.float32
    ct, st, c_lane, s_lane = pl.pallas_call(
        _rope_table_kernel,
        out_shape=[jax.ShapeDtypeStruct((half, seq), f32), jax.ShapeDtypeStruct((half, seq), f32),
                   jax.ShapeDtypeStruct((seq, LANES), f32), jax.ShapeDtypeStruct((seq, LANES), f32)],
        name="rope_tables",
    )(inv_freq.reshape(half, 1))
    return (ct, st), (c_lane, s_lane)


def _rope_table_kernel(inv_ref, ct_ref, st_ref, cl_ref, sl_ref):
    half, seq = ct_ref.shape
    pos = lax.broadcasted_iota(jnp.int32, (half, seq), 1).astype(jnp.float32)
    ang = pos * inv_ref[...]
    c, s = jnp.cos(ang), jnp.sin(ang)
    ct_ref[...] = c
    st_ref[...] = s
    cl_ref[...] = jnp.concatenate([c, c, c, c], axis=0).T
    sl_ref[...] = jnp.concatenate([-s, -s, s, s], axis=0).T


def _rope(x, c, s):
    return x * c + pltpu.roll(x, LANES // 2, 1) * s


def _rms(x, g):
    return x * lax.rsqrt(jnp.mean(x * x, axis=-1, keepdims=True) + NORM_EPS) * g


def _silu(g):
    return g / (1.0 + jnp.exp(-g))


def _dot(a, b):
    return jnp.dot(a, b, preferred_element_type=jnp.float32)


def _dot_nt(a, b):
    return lax.dot_general(a, b, (((1,), (1,)), ((), ())),
                           preferred_element_type=jnp.float32)


PROJ_COLS = 1792


def _proj_kernel(x_ref, ln_ref, wint_ref, qg_ref, wq_f32_ref, kvg_ref, wkv_f32_ref,
                 cm_ref, sm_ref, cs_ref, ss_ref, ct_ref, st_ref, cst_ref, sst_ref,
                 qt_ref, km_ref, vt_ref, gm_ref, qst_ref, ks_ref, vst_ref, gs_ref,
                 wt_ref, wqt_ref, wkv_ref, wvt_ref):
    bf = jnp.bfloat16

    @pl.when(pl.program_id(0) == 0)
    def _():
        _prep_latent_weights(wq_f32_ref, wkv_f32_ref, wqt_ref, wkv_ref, wvt_ref)
        pieces, qs_pieces = _w_in_pieces()
        for d in range(PROJ_COLS // LANES):
            wt_ref[d * LANES:(d + 1) * LANES, :] = _gather_rows(wint_ref, pieces, d).astype(bf)
        for d in range(SWA_WIDTH // LANES):
            wt_ref[PROJ_COLS + d * LANES:PROJ_COLS + (d + 1) * LANES, :] = (
                _gather_rows(wint_ref, qs_pieces, d).astype(bf))

    x = x_ref[...]
    h = _rms(x, ln_ref[...]).astype(bf)

    cm, sm, cs, ss = cm_ref[...], sm_ref[...], cs_ref[...], ss_ref[...]
    lane = lax.broadcasted_iota(jnp.int32, (x.shape[0], LANES), 1)
    first = (lane % (LANES // 2)) < (LANES // 4)
    grp = MXU_COLS

    def project(a):
        return _dot_nt(h, wt_ref[a:a + grp, :])

    c_q = project(0)
    lat_b = project(256)
    cqn = _rms(c_q, qg_ref[...]).astype(bf)
    ckvn = _rms(lat_b[:, 0:MLA_KV_RANK], kvg_ref[...]).astype(bf)
    kr = _rope(lat_b[:, LANES:2 * LANES], cm, sm)
    kr_first = jnp.where(first, kr, 0.0).astype(bf)

    s_scale = LOG2E / math.sqrt(SWA_D)
    qst = _dot_nt(wt_ref[PROJ_COLS:PROJ_COLS + SWA_WIDTH, :], h)
    cst2 = jnp.concatenate([cst_ref[...], cst_ref[...]], axis=0)
    sst2 = jnp.concatenate([sst_ref[...], sst_ref[...]], axis=0)
    for ch in range(SWA_WIDTH // LANES):
        x1 = qst[ch * LANES:ch * LANES + SWA_D, :]
        x2 = qst[ch * LANES + SWA_D:(ch + 1) * LANES, :]
        qst_ref[ch * LANES:ch * LANES + SWA_D, :] = ((x1 * cst2 - x2 * sst2) * s_scale).astype(bf)
        qst_ref[ch * LANES + SWA_D:(ch + 1) * LANES, :] = ((x2 * cst2 + x1 * sst2) * s_scale).astype(bf)

    q_scale = LOG2E / math.sqrt(MLA_QK)
    nope_w = MLA_HEADS * MLA_NOPE
    half = MLA_ROPE // 2
    qt = _dot_nt(wqt_ref[...], cqn)
    ct, st = ct_ref[...], st_ref[...]
    zeros = jnp.zeros((half, x.shape[0]), bf)
    for hd in range(MLA_HEADS):
        qt_ref[hd * grp:hd * grp + LANES, :] = (qt[hd * MLA_NOPE:(hd + 1) * MLA_NOPE, :] * q_scale).astype(bf)
        x1 = qt[nope_w + hd * MLA_ROPE:nope_w + hd * MLA_ROPE + half, :]
        x2 = qt[nope_w + hd * MLA_ROPE + half:nope_w + (hd + 1) * MLA_ROPE, :]
        base = hd * grp + LANES
        qt_ref[base:base + half, :] = ((x1 * ct - x2 * st) * q_scale).astype(bf)
        qt_ref[base + half:base + 2 * half, :] = zeros
        qt_ref[base + 2 * half:base + 3 * half, :] = ((x2 * ct + x1 * st) * q_scale).astype(bf)
        qt_ref[base + 3 * half:base + 4 * half, :] = zeros
    for g2 in range(nope_w // grp):
        km_ref[:, g2 * grp:(g2 + 1) * grp] = _dot(ckvn, wkv_ref[:, g2 * grp:(g2 + 1) * grp]).astype(bf)
    km_ref[:, nope_w:nope_w + LANES] = kr_first
    for g2 in range(MLA_WIDTH // grp):
        gm_ref[:, g2 * grp:(g2 + 1) * grp] = _silu(project(512 + g2 * grp)).astype(bf)

    kv_s = project(1024)
    k01 = _rope(kv_s[:, 0:LANES], cs, ss)
    ks_ref[:, 0 * LANES:1 * LANES] = jnp.where(first, k01, 0.0).astype(bf)
    ks_ref[:, 1 * LANES:2 * LANES] = jnp.where(first, 0.0, pltpu.roll(k01, LANES // 4, 1)).astype(bf)
    ks_ref[:, 2 * LANES:3 * LANES] = jnp.where(first, pltpu.roll(k01, 3 * LANES // 4, 1), 0.0).astype(bf)
    ks_ref[:, 3 * LANES:4 * LANES] = jnp.where(first, 0.0, k01).astype(bf)
    vst_ref[...] = kv_s[:, LANES:2 * LANES].T.astype(bf)

    for g2 in range(SWA_WIDTH // grp):
        gs_ref[:, g2 * grp:(g2 + 1) * grp] = _silu(project(1280 + g2 * grp)).astype(bf)

    vt_ref[...] = _dot_nt(wvt_ref[...], ckvn).astype(bf)


def _proj_call(x2, ln_g, w_in_t, q_g, w_q_up, kv_g, w_kv_up, tabs_m, tabs_s, tabs_t, tabs_st, seq):
    n, d = x2.shape
    tm = PROJ_ROWS
    steps_per_seq = seq // tm
    batch = n // seq
    row = lambda i: (i, 0)
    const = lambda i: (0, 0)
    tab = lambda i: (i % steps_per_seq, 0)
    tab_t = lambda i: (0, i % steps_per_seq)
    col = lambda i: (i // steps_per_seq, i % steps_per_seq)
    bf = jnp.bfloat16
    qt_rows = MLA_HEADS * MLA_QK_PAD
    row_out = lambda w: (pl.BlockSpec((tm, w), row), jax.ShapeDtypeStruct((n, w), bf))
    col_out = lambda r: (pl.BlockSpec((r, tm), col), jax.ShapeDtypeStruct((batch * r, seq), bf))
    outs = [col_out(qt_rows), row_out(MLA_K_COLS), col_out(MLA_WIDTH), row_out(MLA_WIDTH),
            col_out(SWA_WIDTH), row_out(SWA_WIDTH), col_out(SWA_KV_HEADS * SWA_D), row_out(SWA_WIDTH)]
    return pl.pallas_call(
        _proj_kernel,
        grid=(n // tm,),
        in_specs=[
            pl.BlockSpec((tm, d), row),
            pl.BlockSpec((1, d), const),
            pl.BlockSpec(w_in_t.shape, const),
            pl.BlockSpec((1, MLA_Q_RANK), const),
            pl.BlockSpec(w_q_up.shape, const),
            pl.BlockSpec((1, MLA_KV_RANK), const),
            pl.BlockSpec(w_kv_up.shape, const),
        ] + [pl.BlockSpec((tm, LANES), tab)] * 4
          + [pl.BlockSpec((MLA_ROPE // 2, tm), tab_t)] * 2
          + [pl.BlockSpec((SWA_D // 2, tm), tab_t)] * 2,
        out_specs=[o[0] for o in outs],
        out_shape=[o[1] for o in outs],
        scratch_shapes=[pltpu.VMEM((PROJ_COLS + SWA_WIDTH, d), bf),
                        pltpu.VMEM((MLA_HEADS * MLA_QK, MLA_Q_RANK), bf),
                        pltpu.VMEM((MLA_KV_RANK, MLA_HEADS * MLA_NOPE), bf),
                        pltpu.VMEM((MLA_WIDTH, MLA_KV_RANK), bf)],
        compiler_params=pltpu.CompilerParams(
            dimension_semantics=("arbitrary",),
            vmem_limit_bytes=V7X_VMEM_LIMIT_BYTES),
        name="proj",
    )(x2, ln_g, w_in_t, q_g, w_q_up, kv_g, w_kv_up, *tabs_m, *tabs_s, *tabs_t, *tabs_st)


def _tree(op, xs):
    xs = list(xs)
    while len(xs) > 1:
        xs = [op(xs[a], xs[a + 1]) if a + 1 < len(xs) else xs[a] for a in range(0, len(xs), 2)]
    return xs[0]

def _mla_block_groups(nblk):
    groups, cur, load = [], [], 0
    for i in range(nblk - 1, -1, -1):
        if load + i + 1 > MLA_GROUP_TILES:
            groups.append(cur)
            cur, load = [], 0
        cur.append(i)
        load += i + 1
    groups.append(cur)
    return groups


def _mla_kernel(qt_ref, k_ref, kr_ref, vt_ref, g_ref, o_ref, s_ref, p_ref, vx_ref):
    t = MLA_BLOCK
    seq = k_ref.shape[0]
    sub = SUBLANES
    qk = MLA_QK_PAD
    r = lax.broadcasted_iota(jnp.int32, (t, t), 0)
    c = lax.broadcasted_iota(jnp.int32, (t, t), 1)
    causal = r <= c
    nblk = seq // t
    units = []
    for hd in range(MLA_HEADS_PER_STEP):
        for blocks in _mla_block_groups(nblk):
            members, off = [], 0
            for i in blocks:
                members.append((hd, i, off))
                off += (i + 1) * t
            assert off <= s_ref.shape[1]
            units.append(members)
    state = [[dict(m8=None) for _ in members] for members in units]

    for hd in range(MLA_HEADS_PER_STEP):
        vx_ref[hd, 0:MLA_V, :] = vt_ref[hd * MLA_V:(hd + 1) * MLA_V, :]
        vx_ref[hd, MLA_V:, :] = jnp.ones((vx_ref.shape[1] - MLA_V, seq), vx_ref.dtype)

    def score_tile(u, b, j):
        hd, i, off = units[u][b]
        st = state[u][b]
        keys = slice(j * t, (j + 1) * t)
        k_tile = jnp.concatenate([k_ref[keys, hd * MLA_NOPE:(hd + 1) * MLA_NOPE], kr_ref[keys, :]], axis=1)
        s = _dot(k_tile, qt_ref[hd * qk:(hd + 1) * qk, i * t:(i + 1) * t])
        if j == i:
            s = jnp.where(causal, s, NEG_INF)
        s_ref[u % 2, off + j * t:off + (j + 1) * t, :] = s
        m8 = _tree(jnp.maximum, [s[a * sub:(a + 1) * sub, :] for a in range(t // sub)])
        st["m8"] = m8 if st["m8"] is None else jnp.maximum(st["m8"], m8)

    def prob_tile(u, b, j):
        _, _, off = units[u][b]
        st = state[u][b]
        if "m" not in st:
            st["m"] = jnp.max(st["m8"], axis=0, keepdims=True)
        rows = slice(off + j * t, off + (j + 1) * t)
        p_ref[u % 3, rows, :] = jnp.exp2(s_ref[u % 2, rows, :] - st["m"]).astype(jnp.bfloat16)

    def value_matmul(u):
        for b, (hd, i, off) in enumerate(units[u]):
            kv = (i + 1) * t
            state[u][b]["acc"] = _dot(vx_ref[hd, :, 0:kv], p_ref[u % 3, off:off + kv, :])

    def store(u):
        for b, (hd, i, _) in enumerate(units[u]):
            rows = slice(i * t, (i + 1) * t)
            acc = state[u][b]["acc"]
            out_t = acc[0:MLA_V, :] / acc[MLA_V:MLA_V + 1, :]
            gate = g_ref[rows, hd * MLA_V:(hd + 1) * MLA_V].astype(jnp.float32)
            o_ref[rows, hd * MLA_V:(hd + 1) * MLA_V] = (out_t.T * gate).astype(o_ref.dtype)
        state[u] = None

    def tiles(fn, u):
        return [functools.partial(fn, u, b, j) for b, (_, i, _) in enumerate(units[u]) for j in range(i + 1)]

    nu = len(units)
    for task in tiles(score_tile, 0):
        task()
    for u in range(nu + 2):
        if 0 <= u - 2 < nu:
            store(u - 2)
        nxt = tiles(score_tile, u + 1) if u + 1 < nu else []
        cur = tiles(prob_tile, u) if u < nu else []
        while nxt or cur:
            if cur:
                cur.pop(0)()
            if nxt:
                nxt.pop(0)()
        if 0 <= u - 1 < nu:
            value_matmul(u - 1)


def _mla_call(qt, km, vt, gm, batch, seq):
    n = km.shape[0]
    t = MLA_BLOCK
    hps = MLA_HEADS_PER_STEP
    groups = MLA_HEADS // hps
    ones_rows = 16
    feat = lambda b, h: (b * groups + h, 0)
    tok = lambda b, h: (b, h)
    return pl.pallas_call(
        _mla_kernel,
        grid=(batch, groups),
        in_specs=[
            pl.BlockSpec((hps * MLA_QK_PAD, seq), feat),
            pl.BlockSpec((seq, hps * MLA_NOPE), tok),
            pl.BlockSpec((seq, LANES), lambda b, h: (b, MLA_HEADS * MLA_NOPE // LANES)),
            pl.BlockSpec((hps * MLA_V, seq), feat),
            pl.BlockSpec((seq, hps * MLA_V), tok),
        ],
        out_specs=pl.BlockSpec((seq, hps * MLA_V), tok),
        out_shape=jax.ShapeDtypeStruct((n, MLA_WIDTH), jnp.bfloat16),
        scratch_shapes=[pltpu.VMEM((2, MLA_GROUP_TILES * t, t), jnp.float32),
                        pltpu.VMEM((3, MLA_GROUP_TILES * t, t), jnp.bfloat16),
                        pltpu.VMEM((hps, MLA_V + ones_rows, seq), jnp.bfloat16)],
        compiler_params=pltpu.CompilerParams(
            dimension_semantics=("arbitrary", "arbitrary"),
            vmem_limit_bytes=V7X_VMEM_LIMIT_BYTES),
        name="mla",
    )(qt, km, km, vt, gm)


def _swa_out_kernel(sink_ref, qt_ref, k_ref, kh_ref, vt_ref, vth_ref, g_ref, mm_ref, x_ref,
                    wf_ref, fg_ref, o_ref, ms_ref, kx_ref, vx_ref, w_ref):
    w = SWA_WINDOW
    chunk = pl.program_id(1)
    rows_total = k_ref.shape[0]
    bf = jnp.bfloat16
    sub = SUBLANES

    @pl.when((pl.program_id(0) == 0) & (chunk == 0))
    def _():
        w_ref[...] = wf_ref[...].astype(bf)

    kx_ref[0:w, :] = kh_ref[...]
    kx_ref[w:, :] = k_ref[...]
    for hk in range(SWA_KV_HEADS):
        vx_ref[hk, 0:SWA_D, 0:w] = vth_ref[hk * SWA_D:(hk + 1) * SWA_D, :]
        vx_ref[hk, 0:SWA_D, w:] = vt_ref[hk * SWA_D:(hk + 1) * SWA_D, :]
        vx_ref[hk, SWA_D:, :] = jnp.ones((vx_ref.shape[1] - SWA_D, vx_ref.shape[2]), bf)

    ki = lax.broadcasted_iota(jnp.int32, (2 * w, 2 * w), 0)
    qi = lax.broadcasted_iota(jnp.int32, (2 * w, 2 * w), 1) % w
    rel = qi + w - ki
    band = (rel >= 0) & (rel < SWA_WINDOW)
    band_first = band & ((ki >= w) | (chunk > 0))
    lane = lax.broadcasted_iota(jnp.int32, (1, 2 * w), 1)

    units = [(n, hk) for n in range(rows_total // w) for hk in range(SWA_KV_HEADS)]
    state = [dict() for _ in units]

    def sink_rows(hk):
        sink = [sink_ref[hk * SWA_GROUP + i] * LOG2E for i in range(SWA_GROUP)]
        return (jnp.where(lane < w, sink[0], sink[2]), jnp.where(lane < w, sink[1], sink[3]))

    def stage_scores(u, half):
        n, hk = units[u]
        base = hk * 2 * LANES
        qcols = slice(n * w, (n + 1) * w)
        keys = slice(n * w, (n + 2) * w)
        qt = jnp.concatenate([qt_ref[base:base + LANES, qcols],
                              qt_ref[base + LANES:base + 2 * LANES, qcols]], axis=1)
        valid = band_first if n == 0 else band
        st = state[u]
        lanes = slice(base + half * LANES, base + (half + 1) * LANES)
        s = jnp.where(valid, _dot(kx_ref[keys, lanes], qt), NEG_INF)
        m8 = _tree(jnp.maximum, [s[a * sub:(a + 1) * sub, :] for a in range(2 * w // sub)])
        st.setdefault("s", {})[half] = s
        st.setdefault("m", {})[half] = jnp.maximum(jnp.max(m8, axis=0, keepdims=True),
                                                   sink_rows(hk)[half])

    def stage_values(u, half):
        n, hk = units[u]
        keys = slice(n * w, (n + 2) * w)
        st = state[u]
        p = jnp.exp2(st["s"].pop(half) - st["m"][half]).astype(bf)
        st.setdefault("acc", {})[half] = _dot(vx_ref[hk, :, keys], p)

    def stage_store(u):
        n, hk = units[u]
        base = hk * 2 * LANES
        qcols = slice(n * w, (n + 1) * w)
        sinks = sink_rows(hk)
        st = state[u]
        outs = []
        for half in range(2):
            acc = st["acc"][half]
            den = acc[SWA_D:SWA_D + 1, :] + jnp.exp2(sinks[half] - st["m"][half])
            outs.append(acc[0:SWA_D, :] / den)
        g = g_ref[qcols, base:base + 2 * LANES].astype(jnp.float32)
        for ch in range(2):
            o_t = jnp.concatenate([outs[0][:, ch * w:(ch + 1) * w],
                                   outs[1][:, ch * w:(ch + 1) * w]], axis=0)
            ms_ref[qcols, base + ch * LANES:base + (ch + 1) * LANES] = (
                o_t.T * g[:, ch * LANES:(ch + 1) * LANES]).astype(bf)
        state[u] = None

    t = SWA_OUT_SUBTILE
    units_per_tile = (t // w) * SWA_KV_HEADS

    d_model = x_ref.shape[1]
    grp = MXU_COLS
    out_state = {}

    def out_group(i, gc):
        rows = slice(i * t, (i + 1) * t)
        cols = slice(gc * grp, (gc + 1) * grp)
        y = (x_ref[rows, cols] + _dot(mm_ref[rows, :], w_ref[0:MLA_WIDTH, cols])
             + _dot(ms_ref[rows, :], w_ref[MLA_WIDTH:, cols]))
        out_state.setdefault(i, []).append(y)

    def out_norm(i):
        rows = slice(i * t, (i + 1) * t)
        ys = out_state.pop(i)
        ssq = _tree(jnp.add, [jnp.sum(y * y, axis=-1, keepdims=True) for y in ys])
        scale = lax.rsqrt(ssq / d_model + NORM_EPS)
        for gc, y in enumerate(ys):
            cols = slice(gc * grp, (gc + 1) * grp)
            o_ref[rows, cols] = y * scale * fg_ref[:, cols]

    nu = len(units)
    lag = SWA_STORE_LAG
    gsz = SWA_UNITS_PER_STEP
    pending = []
    tasks_per_step = SWA_OUT_TASKS_PER_STEP

    def retire(count):
        for _ in range(min(count, len(pending))):
            pending.pop(0)()

    for step in range(nu // gsz + lag):
        for u in range(step * gsz, (step + 1) * gsz):
            if u < nu:
                stage_scores(u, 0)
                stage_scores(u, 1)
        retire(tasks_per_step // 2)
        for u in range((step - 1) * gsz, step * gsz):
            if 0 <= u < nu:
                stage_values(u, 0)
                stage_values(u, 1)
        retire(tasks_per_step - tasks_per_step // 2)
        for u in range((step - lag) * gsz, (step - lag + 1) * gsz):
            if 0 <= u < nu:
                stage_store(u)
                if (u + 1) % units_per_tile == 0:
                    i = u // units_per_tile
                    pending += [functools.partial(out_group, i, gc) for gc in range(d_model // grp)]
                    pending.append(functools.partial(out_norm, i))
    for task in pending:
        task()


def _swa_out_call(sinks, qst, ks4, vst, gs, mm, x2, w_o, fg, batch, seq):
    n, d = x2.shape
    r = SWA_OUT_ROWS
    w = SWA_WINDOW
    cps = seq // r
    bpc = r // w
    bps = seq // w
    v_rows = SWA_KV_HEADS * SWA_D
    ones_rows = 16
    row = lambda b, c: (b * cps + c, 0)
    halo = lambda b, c: (b * bps + jnp.maximum(c * bpc - 1, 0), 0)
    feat = lambda b, c: (b, c)
    feat_halo = lambda b, c: (b, jnp.maximum(c * bpc - 1, 0))
    const = lambda b, c: (0, 0)
    wide = pl.BlockSpec((r, SWA_WIDTH), row)
    return pl.pallas_call(
        _swa_out_kernel,
        grid=(batch, cps),
        in_specs=[
            pl.BlockSpec(memory_space=pltpu.SMEM),
            pl.BlockSpec((SWA_WIDTH, r), feat),
            wide,
            pl.BlockSpec((w, SWA_WIDTH), halo),
            pl.BlockSpec((v_rows, r), feat),
            pl.BlockSpec((v_rows, w), feat_halo),
            wide,
            pl.BlockSpec((r, MLA_WIDTH), row),
            pl.BlockSpec((r, d), row),
            pl.BlockSpec(w_o.shape, const),
            pl.BlockSpec((1, d), const),
        ],
        out_specs=pl.BlockSpec((r, d), row),
        out_shape=jax.ShapeDtypeStruct((n, d), jnp.float32),
        scratch_shapes=[pltpu.VMEM((r, SWA_WIDTH), jnp.bfloat16),
                        pltpu.VMEM((r + w, SWA_WIDTH), jnp.bfloat16),
                        pltpu.VMEM((SWA_KV_HEADS, SWA_D + ones_rows, r + w), jnp.bfloat16),
                        pltpu.VMEM(w_o.shape, jnp.bfloat16)],
        compiler_params=pltpu.CompilerParams(
            dimension_semantics=("arbitrary", "arbitrary"),
            vmem_limit_bytes=V7X_VMEM_LIMIT_BYTES),
        name="swa_out",
    )(sinks, qst, ks4, ks4, vst, vst, gs, mm, x2, w_o, fg)


def _pair_rope_pieces(src, dst, n_heads, dim):
    half = dim // 2
    out = []
    for pair in range(n_heads // 2):
        for which in range(2):
            head = src + (2 * pair + which) * dim
            chunk = dst + pair * 2 * dim
            out.append((head, chunk + which * half, half))
            out.append((head + half, chunk + 2 * half + which * half, half))
    return out


def _w_in_pieces():
    pieces, qs_pieces, src, dst = [], [], 0, 0
    for name, width in (("c_q", MLA_Q_RANK), ("c_kv", MLA_KV_RANK), ("k_rope", MLA_ROPE),
                        ("g_mla", MLA_WIDTH), ("q_s", SWA_WIDTH),
                        ("k_s", SWA_KV_HEADS * SWA_D), ("v_s", SWA_KV_HEADS * SWA_D),
                        ("g_swa", SWA_WIDTH)):
        if name == "k_rope":
            half = MLA_ROPE // 2
            pieces += [(src, dst, half), (src, dst + half, half),
                       (src + half, dst + 2 * half, half), (src + half, dst + 3 * half, half)]
            dst += 2 * MLA_ROPE
        elif name == "q_s":
            qs_pieces += _pair_rope_pieces(src, 0, SWA_Q_HEADS, SWA_D)
        elif name == "k_s":
            pieces += _pair_rope_pieces(src, dst, SWA_KV_HEADS, SWA_D)
            dst += width
        else:
            pieces.append((src, dst, width))
            dst += width
        src += width
    assert dst == PROJ_COLS
    return pieces, qs_pieces


def _gather_cols(src_ref, pieces, n_dst_cols):
    rows, n_src = src_ref.shape
    lane = lax.broadcasted_iota(jnp.int32, (rows, LANES), 1)
    loaded, rolled = {}, {}

    def load(k):
        if k not in loaded:
            width = min(LANES, n_src - k * LANES)
            v = src_ref[:, k * LANES:k * LANES + width]
            if width < LANES:
                v = jnp.concatenate([v, jnp.zeros((rows, LANES - width), v.dtype)], axis=1)
            loaded[k] = v
        return loaded[k]

    def shifted(k, shift):
        if shift == 0:
            return load(k)
        if (k, shift) not in rolled:
            rolled[(k, shift)] = pltpu.roll(load(k), shift, 1)
        return rolled[(k, shift)]

    chunks = []
    for d in range(n_dst_cols // LANES):
        acc = None
        for s0, d0, w in pieces:
            lo, hi = max(d0, d * LANES), min(d0 + w, (d + 1) * LANES)
            while lo < hi:
                s_abs = s0 + lo - d0
                k, a = divmod(s_abs, LANES)
                n = min(hi - lo, LANES - a)
                a_dst = lo - d * LANES
                val = shifted(k, (a_dst - a) % LANES)
                if n == LANES:
                    acc = val
                else:
                    mask = (lane >= a_dst) & (lane < a_dst + n)
                    acc = jnp.where(mask, val, 0.0 if acc is None else acc)
                lo += n
        chunks.append(acc)
    return chunks


def _gather_rows(src_ref, pieces, d):
    parts = []
    for s0, d0, w in pieces:
        lo, hi = max(d0, d * LANES), min(d0 + w, (d + 1) * LANES)
        if lo < hi:
            parts.append((lo, src_ref[s0 + lo - d0:s0 + hi - d0, :]))
    parts.sort(key=lambda t: t[0])
    assert sum(p.shape[0] for _, p in parts) == LANES
    return jnp.concatenate([p for _, p in parts], axis=0) if len(parts) > 1 else parts[0][1]


def _prep_latent_weights(wq_ref, wkv_ref, wqt_ref, wk_ref, wvt_ref):
    q_pieces = ([(hd * MLA_QK, hd * MLA_NOPE, MLA_NOPE) for hd in range(MLA_HEADS)]
                + [(hd * MLA_QK + MLA_NOPE, MLA_HEADS * MLA_NOPE + hd * MLA_ROPE, MLA_ROPE)
                   for hd in range(MLA_HEADS)])
    wq = jnp.concatenate(_gather_cols(wq_ref, q_pieces, MLA_HEADS * MLA_QK), axis=1)
    wqt_ref[...] = wq.T.astype(wqt_ref.dtype)
    per_head = MLA_NOPE + MLA_V
    for hd in range(MLA_HEADS):
        wk_ref[:, hd * MLA_NOPE:(hd + 1) * MLA_NOPE] = (
            wkv_ref[:, hd * per_head:hd * per_head + MLA_NOPE].astype(wk_ref.dtype))
        wvt_ref[hd * MLA_V:(hd + 1) * MLA_V, :] = (
            wkv_ref[:, hd * per_head + MLA_NOPE:(hd + 1) * per_head].T.astype(wvt_ref.dtype))


def kernel(x, ln_mix, w_in, q_a_norm, w_q_up, kv_a_norm, w_kv_up, attn_sinks, w_out, final_norm):
    batch, seq, d = x.shape
    depth = ln_mix.shape[0]
    assert depth == 1, "final norm is fused into the single layer's output kernel"
    assert seq % MLA_BLOCK == 0 and seq % PROJ_ROWS == 0 and seq % SWA_OUT_ROWS == 0
    tables = {dim: _rope_tables(seq, dim) for dim in {MLA_ROPE, SWA_D}}
    tabs_t, tabs_m = tables[MLA_ROPE]
    tabs_st, tabs_s = tables[SWA_D]
    x2 = x.reshape(batch * seq, d)
    qt, km, vt, gm, qst, ks4, vst, gs = _proj_call(
        x2, ln_mix[0].reshape(1, -1), w_in[0].T, q_a_norm[0].reshape(1, -1), w_q_up[0],
        kv_a_norm[0].reshape(1, -1), w_kv_up[0], tabs_m, tabs_s, tabs_t, tabs_st, seq)
    mm = _mla_call(qt, km, vt, gm, batch, seq)
    out = _swa_out_call(attn_sinks[0], qst, ks4, vst, gs, mm, x2, w_out[0],
                        final_norm.reshape(1, -1), batch, seq)
    return out.reshape(batch, seq, d)
```

```python
import functools
import math

import jax
import jax.numpy as jnp
from jax import lax
from jax.experimental import pallas as pl
from jax.experimental.pallas import tpu as pltpu

ROPE_THETA = 10000.0
NORM_EPS = 1e-6
NEG_INF = -1e30
LOG2E = 1.4426950408889634

MLA_HEADS = 4
MLA_NOPE = 128
MLA_ROPE = 64
MLA_V = 128
MLA_Q_RANK = 256
MLA_KV_RANK = 128
MLA_QK = MLA_NOPE + MLA_ROPE
MLA_WIDTH = MLA_HEADS * MLA_V

SWA_Q_HEADS = 8
SWA_KV_HEADS = 2
SWA_D = 64
SWA_WINDOW = 128
SWA_GROUP = SWA_Q_HEADS // SWA_KV_HEADS
SWA_WIDTH = SWA_Q_HEADS * SWA_D

LANES = 128
SUBLANES = 8
MXU_COLS = 2 * LANES
MLA_QK_PAD = MXU_COLS
MLA_K_COLS = MLA_HEADS * MLA_NOPE + LANES
V7X_VMEM_LIMIT_BYTES = 56 * 1024 * 1024

PROJ_ROWS = 1024
MLA_BLOCK = 256
MLA_GROUP_TILES = 8
MLA_HEADS_PER_STEP = 2
SWA_OUT_ROWS = 1024
SWA_OUT_SUBTILE = 256
SWA_UNITS_PER_STEP = 4
SWA_OUT_TASKS_PER_STEP = 4
SWA_STORE_LAG = 2


def _inv_freq(dim):
    assert 2 * dim == LANES
    inv = 1.0 / (ROPE_THETA ** (jnp.arange(0, dim, 2, dtype=jnp.float32) / dim))
    return inv.reshape(dim // 2, 1)


def _rope_table_shapes(seq, tile, dim):
    f32 = jnp.float32
    return [pltpu.VMEM((seq // tile, dim // 2, tile), f32), pltpu.VMEM((seq // tile, dim // 2, tile), f32),
            pltpu.VMEM((seq, LANES), f32), pltpu.VMEM((seq, LANES), f32)]


def _fill_rope_tables(inv_ref, ct_ref, st_ref, cl_ref, sl_ref):
    n_tiles, half, tile = ct_ref.shape
    for j in range(n_tiles):
        pos = (lax.broadcasted_iota(jnp.int32, (half, tile), 1) + j * tile).astype(jnp.float32)
        ang = pos * inv_ref[...]
        c, s = jnp.cos(ang), jnp.sin(ang)
        ct_ref[j] = c
        st_ref[j] = s
        cl_ref[j * tile:(j + 1) * tile, :] = jnp.concatenate([c, c, c, c], axis=0).T
        sl_ref[j * tile:(j + 1) * tile, :] = jnp.concatenate([-s, -s, s, s], axis=0).T


def _rope(x, c, s):
    return x * c + pltpu.roll(x, LANES // 2, 1) * s


def _rms(x, g):
    return x * lax.rsqrt(jnp.mean(x * x, axis=-1, keepdims=True) + NORM_EPS) * g


def _silu(g):
    return g / (1.0 + jnp.exp(-g))


def _dot(a, b):
    return jnp.dot(a, b, preferred_element_type=jnp.float32)


def _dot_nt(a, b):
    return lax.dot_general(a, b, (((1,), (1,)), ((), ())),
                           preferred_element_type=jnp.float32)


PROJ_COLS = 1792


def _proj_kernel(x_ref, ln_ref, wint_ref, qg_ref, wq_f32_ref, kvg_ref, wkv_f32_ref,
                 inv_mla_ref, inv_swa_ref,
                 qt_ref, km_ref, vt_ref, gm_ref, qst_ref, ks_ref, vst_ref, gs_ref,
                 wt_ref, wqt_ref, wkv_ref, wvt_ref, *table_refs):
    bf = jnp.bfloat16

    mla_tabs = table_refs[0:4]
    swa_tabs = table_refs[4:8] if len(table_refs) > 4 else mla_tabs

    @pl.when(pl.program_id(0) == 0)
    def _():
        _fill_rope_tables(inv_mla_ref, *mla_tabs)
        if swa_tabs is not mla_tabs:
            _fill_rope_tables(inv_swa_ref, *swa_tabs)
        _prep_latent_weights(wq_f32_ref, wkv_f32_ref, wqt_ref, wkv_ref, wvt_ref)
        pieces, qs_pieces = _w_in_pieces()
        for d in range(PROJ_COLS // LANES):
            wt_ref[d * LANES:(d + 1) * LANES, :] = _gather_rows(wint_ref, pieces, d).astype(bf)
        for d in range(SWA_WIDTH // LANES):
            wt_ref[PROJ_COLS + d * LANES:PROJ_COLS + (d + 1) * LANES, :] = (
                _gather_rows(wint_ref, qs_pieces, d).astype(bf))

    x = x_ref[...]
    h = _rms(x, ln_ref[...]).astype(bf)

    tile = pl.program_id(0) % mla_tabs[0].shape[0]
    rows = pl.ds(pl.multiple_of(tile * x.shape[0], x.shape[0]), x.shape[0])
    cm, sm = mla_tabs[2][rows, :], mla_tabs[3][rows, :]
    cs, ss = swa_tabs[2][rows, :], swa_tabs[3][rows, :]
    lane = lax.broadcasted_iota(jnp.int32, (x.shape[0], LANES), 1)
    first = (lane % (LANES // 2)) < (LANES // 4)
    grp = MXU_COLS

    def project(a):
        return _dot_nt(h, wt_ref[a:a + grp, :])

    c_q = project(0)
    lat_b = project(256)
    cqn = _rms(c_q, qg_ref[...]).astype(bf)
    ckvn = _rms(lat_b[:, 0:MLA_KV_RANK], kvg_ref[...]).astype(bf)
    kr = _rope(lat_b[:, LANES:2 * LANES], cm, sm)
    kr_first = jnp.where(first, kr, 0.0).astype(bf)

    s_scale = LOG2E / math.sqrt(SWA_D)
    qst = _dot_nt(wt_ref[PROJ_COLS:PROJ_COLS + SWA_WIDTH, :], h)
    cst2 = jnp.concatenate([swa_tabs[0][tile], swa_tabs[0][tile]], axis=0)
    sst2 = jnp.concatenate([swa_tabs[1][tile], swa_tabs[1][tile]], axis=0)
    for ch in range(SWA_WIDTH // LANES):
        x1 = qst[ch * LANES:ch * LANES + SWA_D, :]
        x2 = qst[ch * LANES + SWA_D:(ch + 1) * LANES, :]
        qst_ref[ch * LANES:ch * LANES + SWA_D, :] = ((x1 * cst2 - x2 * sst2) * s_scale).astype(bf)
        qst_ref[ch * LANES + SWA_D:(ch + 1) * LANES, :] = ((x2 * cst2 + x1 * sst2) * s_scale).astype(bf)

    q_scale = LOG2E / math.sqrt(MLA_QK)
    nope_w = MLA_HEADS * MLA_NOPE
    half = MLA_ROPE // 2
    qt = _dot_nt(wqt_ref[...], cqn)
    ct, st = mla_tabs[0][tile], mla_tabs[1][tile]
    zeros = jnp.zeros((half, x.shape[0]), bf)
    for hd in range(MLA_HEADS):
        qt_ref[hd * grp:hd * grp + LANES, :] = (qt[hd * MLA_NOPE:(hd + 1) * MLA_NOPE, :] * q_scale).astype(bf)
        x1 = qt[nope_w + hd * MLA_ROPE:nope_w + hd * MLA_ROPE + half, :]
        x2 = qt[nope_w + hd * MLA_ROPE + half:nope_w + (hd + 1) * MLA_ROPE, :]
        base = hd * grp + LANES
        qt_ref[base:base + half, :] = ((x1 * ct - x2 * st) * q_scale).astype(bf)
        qt_ref[base + half:base + 2 * half, :] = zeros
        qt_ref[base + 2 * half:base + 3 * half, :] = ((x2 * ct + x1 * st) * q_scale).astype(bf)
        qt_ref[base + 3 * half:base + 4 * half, :] = zeros
    for g2 in range(nope_w // grp):
        km_ref[:, g2 * grp:(g2 + 1) * grp] = _dot(ckvn, wkv_ref[:, g2 * grp:(g2 + 1) * grp]).astype(bf)
    km_ref[:, nope_w:nope_w + LANES] = kr_first
    for g2 in range(MLA_WIDTH // grp):
        gm_ref[:, g2 * grp:(g2 + 1) * grp] = _silu(project(512 + g2 * grp)).astype(bf)

    kv_s = project(1024)
    k01 = _rope(kv_s[:, 0:LANES], cs, ss)
    ks_ref[:, 0 * LANES:1 * LANES] = jnp.where(first, k01, 0.0).astype(bf)
    ks_ref[:, 1 * LANES:2 * LANES] = jnp.where(first, 0.0, pltpu.roll(k01, LANES // 4, 1)).astype(bf)
    ks_ref[:, 2 * LANES:3 * LANES] = jnp.where(first, pltpu.roll(k01, 3 * LANES // 4, 1), 0.0).astype(bf)
    ks_ref[:, 3 * LANES:4 * LANES] = jnp.where(first, 0.0, k01).astype(bf)
    vst_ref[...] = kv_s[:, LANES:2 * LANES].T.astype(bf)

    for g2 in range(SWA_WIDTH // grp):
        gs_ref[:, g2 * grp:(g2 + 1) * grp] = _silu(project(1280 + g2 * grp)).astype(bf)

    vt_ref[...] = _dot_nt(wvt_ref[...], ckvn).astype(bf)


def _proj_call(x2, ln_g, w_in_t, q_g, w_q_up, kv_g, w_kv_up, seq):
    n, d = x2.shape
    tm = PROJ_ROWS
    steps_per_seq = seq // tm
    batch = n // seq
    row = lambda i: (i, 0)
    const = lambda i: (0, 0)
    col = lambda i: (i // steps_per_seq, i % steps_per_seq)
    table_scratch = _rope_table_shapes(seq, tm, MLA_ROPE)
    if SWA_D != MLA_ROPE:
        table_scratch += _rope_table_shapes(seq, tm, SWA_D)
    bf = jnp.bfloat16
    qt_rows = MLA_HEADS * MLA_QK_PAD
    row_out = lambda w: (pl.BlockSpec((tm, w), row), jax.ShapeDtypeStruct((n, w), bf))
    col_out = lambda r: (pl.BlockSpec((r, tm), col), jax.ShapeDtypeStruct((batch * r, seq), bf))
    outs = [col_out(qt_rows), row_out(MLA_K_COLS), col_out(MLA_WIDTH), row_out(MLA_WIDTH),
            col_out(SWA_WIDTH), row_out(SWA_WIDTH), col_out(SWA_KV_HEADS * SWA_D), row_out(SWA_WIDTH)]
    return pl.pallas_call(
        _proj_kernel,
        grid=(n // tm,),
        in_specs=[
            pl.BlockSpec((tm, d), row),
            pl.BlockSpec((1, d), const),
            pl.BlockSpec(w_in_t.shape, const),
            pl.BlockSpec((1, MLA_Q_RANK), const),
            pl.BlockSpec(w_q_up.shape, const),
            pl.BlockSpec((1, MLA_KV_RANK), const),
            pl.BlockSpec(w_kv_up.shape, const),
            pl.BlockSpec((MLA_ROPE // 2, 1), const),
            pl.BlockSpec((SWA_D // 2, 1), const),
        ],
        out_specs=[o[0] for o in outs],
        out_shape=[o[1] for o in outs],
        scratch_shapes=[pltpu.VMEM((PROJ_COLS + SWA_WIDTH, d), bf),
                        pltpu.VMEM((MLA_HEADS * MLA_QK, MLA_Q_RANK), bf),
                        pltpu.VMEM((MLA_KV_RANK, MLA_HEADS * MLA_NOPE), bf),
                        pltpu.VMEM((MLA_WIDTH, MLA_KV_RANK), bf)] + table_scratch,
        compiler_params=pltpu.CompilerParams(
            dimension_semantics=("arbitrary",),
            vmem_limit_bytes=V7X_VMEM_LIMIT_BYTES),
        name="proj",
    )(x2, ln_g, w_in_t, q_g, w_q_up, kv_g, w_kv_up, _inv_freq(MLA_ROPE), _inv_freq(SWA_D))


def _tree(op, xs):
    xs = list(xs)
    while len(xs) > 1:
        xs = [op(xs[a], xs[a + 1]) if a + 1 < len(xs) else xs[a] for a in range(0, len(xs), 2)]
    return xs[0]

def _mla_block_groups(nblk):
    groups, cur, load = [], [], 0
    for i in range(nblk - 1, -1, -1):
        if load + i + 1 > MLA_GROUP_TILES:
            groups.append(cur)
            cur, load = [], 0
        cur.append(i)
        load += i + 1
    groups.append(cur)
    return groups


def _mla_kernel(qt_ref, k_ref, kr_ref, vt_ref, g_ref, o_ref, s_ref, p_ref, vx_ref):
    t = MLA_BLOCK
    seq = k_ref.shape[0]
    sub = SUBLANES
    qk = MLA_QK_PAD
    r = lax.broadcasted_iota(jnp.int32, (t, t), 0)
    c = lax.broadcasted_iota(jnp.int32, (t, t), 1)
    causal = r <= c
    nblk = seq // t
    units = []
    for hd in range(MLA_HEADS_PER_STEP):
        for blocks in _mla_block_groups(nblk):
            members, off = [], 0
            for i in blocks:
                members.append((hd, i, off))
                off += (i + 1) * t
            assert off <= s_ref.shape[1]
            units.append(members)
    state = [[dict(m8=None) for _ in members] for members in units]

    for hd in range(MLA_HEADS_PER_STEP):
        vx_ref[hd, 0:MLA_V, :] = vt_ref[hd * MLA_V:(hd + 1) * MLA_V, :]
        vx_ref[hd, MLA_V:, :] = jnp.ones((vx_ref.shape[1] - MLA_V, seq), vx_ref.dtype)

    def score_tile(u, b, j):
        hd, i, off = units[u][b]
        st = state[u][b]
        keys = slice(j * t, (j + 1) * t)
        k_tile = jnp.concatenate([k_ref[keys, hd * MLA_NOPE:(hd + 1) * MLA_NOPE], kr_ref[keys, :]], axis=1)
        s = _dot(k_tile, qt_ref[hd * qk:(hd + 1) * qk, i * t:(i + 1) * t])
        if j == i:
            s = jnp.where(causal, s, NEG_INF)
        s_ref[u % 2, off + j * t:off + (j + 1) * t, :] = s
        m8 = _tree(jnp.maximum, [s[a * sub:(a + 1) * sub, :] for a in range(t // sub)])
        st["m8"] = m8 if st["m8"] is None else jnp.maximum(st["m8"], m8)

    def prob_tile(u, b, j):
        _, _, off = units[u][b]
        st = state[u][b]
        if "m" not in st:
            st["m"] = jnp.max(st["m8"], axis=0, keepdims=True)
        rows = slice(off + j * t, off + (j + 1) * t)
        p_ref[u % 3, rows, :] = jnp.exp2(s_ref[u % 2, rows, :] - st["m"]).astype(jnp.bfloat16)

    def value_matmul(u):
        for b, (hd, i, off) in enumerate(units[u]):
            kv = (i + 1) * t
            state[u][b]["acc"] = _dot(vx_ref[hd, :, 0:kv], p_ref[u % 3, off:off + kv, :])

    def store(u):
        for b, (hd, i, _) in enumerate(units[u]):
            rows = slice(i * t, (i + 1) * t)
            acc = state[u][b]["acc"]
            out_t = acc[0:MLA_V, :] / acc[MLA_V:MLA_V + 1, :]
            gate = g_ref[rows, hd * MLA_V:(hd + 1) * MLA_V].astype(jnp.float32)
            o_ref[rows, hd * MLA_V:(hd + 1) * MLA_V] = (out_t.T * gate).astype(o_ref.dtype)
        state[u] = None

    def tiles(fn, u):
        return [functools.partial(fn, u, b, j) for b, (_, i, _) in enumerate(units[u]) for j in range(i + 1)]

    nu = len(units)
    for task in tiles(score_tile, 0):
        task()
    for u in range(nu + 2):
        if 0 <= u - 2 < nu:
            store(u - 2)
        nxt = tiles(score_tile, u + 1) if u + 1 < nu else []
        cur = tiles(prob_tile, u) if u < nu else []
        while nxt or cur:
            if cur:
                cur.pop(0)()
            if nxt:
                nxt.pop(0)()
        if 0 <= u - 1 < nu:
            value_matmul(u - 1)


def _mla_call(qt, km, vt, gm, batch, seq):
    n = km.shape[0]
    t = MLA_BLOCK
    hps = MLA_HEADS_PER_STEP
    groups = MLA_HEADS // hps
    ones_rows = 16
    feat = lambda b, h: (b * groups + h, 0)
    tok = lambda b, h: (b, h)
    return pl.pallas_call(
        _mla_kernel,
        grid=(batch, groups),
        in_specs=[
            pl.BlockSpec((hps * MLA_QK_PAD, seq), feat),
            pl.BlockSpec((seq, hps * MLA_NOPE), tok),
            pl.BlockSpec((seq, LANES), lambda b, h: (b, MLA_HEADS * MLA_NOPE // LANES)),
            pl.BlockSpec((hps * MLA_V, seq), feat),
            pl.BlockSpec((seq, hps * MLA_V), tok),
        ],
        out_specs=pl.BlockSpec((seq, hps * MLA_V), tok),
        out_shape=jax.ShapeDtypeStruct((n, MLA_WIDTH), jnp.bfloat16),
        scratch_shapes=[pltpu.VMEM((2, MLA_GROUP_TILES * t, t), jnp.float32),
                        pltpu.VMEM((3, MLA_GROUP_TILES * t, t), jnp.bfloat16),
                        pltpu.VMEM((hps, MLA_V + ones_rows, seq), jnp.bfloat16)],
        compiler_params=pltpu.CompilerParams(
            dimension_semantics=("arbitrary", "arbitrary"),
            vmem_limit_bytes=V7X_VMEM_LIMIT_BYTES),
        name="mla",
    )(qt, km, km, vt, gm)


def _swa_out_kernel(sink_ref, qt_ref, k_ref, kh_ref, vt_ref, vth_ref, g_ref, mm_ref, x_ref,
                    wf_ref, fg_ref, o_ref, ms_ref, kx_ref, vx_ref, w_ref):
    w = SWA_WINDOW
    chunk = pl.program_id(1)
    rows_total = k_ref.shape[0]
    bf = jnp.bfloat16
    sub = SUBLANES

    @pl.when((pl.program_id(0) == 0) & (chunk == 0))
    def _():
        w_ref[...] = wf_ref[...].astype(bf)

    kx_ref[0:w, :] = kh_ref[...]
    kx_ref[w:, :] = k_ref[...]
    for hk in range(SWA_KV_HEADS):
        vx_ref[hk, 0:SWA_D, 0:w] = vth_ref[hk * SWA_D:(hk + 1) * SWA_D, :]
        vx_ref[hk, 0:SWA_D, w:] = vt_ref[hk * SWA_D:(hk + 1) * SWA_D, :]
        vx_ref[hk, SWA_D:, :] = jnp.ones((vx_ref.shape[1] - SWA_D, vx_ref.shape[2]), bf)

    ki = lax.broadcasted_iota(jnp.int32, (2 * w, 2 * w), 0)
    qi = lax.broadcasted_iota(jnp.int32, (2 * w, 2 * w), 1) % w
    rel = qi + w - ki
    band = (rel >= 0) & (rel < SWA_WINDOW)
    band_first = band & ((ki >= w) | (chunk > 0))
    lane = lax.broadcasted_iota(jnp.int32, (1, 2 * w), 1)

    units = [(n, hk) for n in range(rows_total // w) for hk in range(SWA_KV_HEADS)]
    state = [dict() for _ in units]

    def sink_rows(hk):
        sink = [sink_ref[hk * SWA_GROUP + i] * LOG2E for i in range(SWA_GROUP)]
        return (jnp.where(lane < w, sink[0], sink[2]), jnp.where(lane < w, sink[1], sink[3]))

    def stage_scores(u, half):
        n, hk = units[u]
        base = hk * 2 * LANES
        qcols = slice(n * w, (n + 1) * w)
        keys = slice(n * w, (n + 2) * w)
        qt = jnp.concatenate([qt_ref[base:base + LANES, qcols],
                              qt_ref[base + LANES:base + 2 * LANES, qcols]], axis=1)
        valid = band_first if n == 0 else band
        st = state[u]
        lanes = slice(base + half * LANES, base + (half + 1) * LANES)
        s = jnp.where(valid, _dot(kx_ref[keys, lanes], qt), NEG_INF)
        m8 = _tree(jnp.maximum, [s[a * sub:(a + 1) * sub, :] for a in range(2 * w // sub)])
        st.setdefault("s", {})[half] = s
        st.setdefault("m", {})[half] = jnp.maximum(jnp.max(m8, axis=0, keepdims=True),
                                                   sink_rows(hk)[half])

    def stage_values(u, half):
        n, hk = units[u]
        keys = slice(n * w, (n + 2) * w)
        st = state[u]
        p = jnp.exp2(st["s"].pop(half) - st["m"][half]).astype(bf)
        st.setdefault("acc", {})[half] = _dot(vx_ref[hk, :, keys], p)

    def stage_store(u):
        n, hk = units[u]
        base = hk * 2 * LANES
        qcols = slice(n * w, (n + 1) * w)
        sinks = sink_rows(hk)
        st = state[u]
        outs = []
        for half in range(2):
            acc = st["acc"][half]
            den = acc[SWA_D:SWA_D + 1, :] + jnp.exp2(sinks[half] - st["m"][half])
            outs.append(acc[0:SWA_D, :] / den)
        g = g_ref[qcols, base:base + 2 * LANES].astype(jnp.float32)
        for ch in range(2):
            o_t = jnp.concatenate([outs[0][:, ch * w:(ch + 1) * w],
                                   outs[1][:, ch * w:(ch + 1) * w]], axis=0)
            ms_ref[qcols, base + ch * LANES:base + (ch + 1) * LANES] = (
                o_t.T * g[:, ch * LANES:(ch + 1) * LANES]).astype(bf)
        state[u] = None

    t = SWA_OUT_SUBTILE
    units_per_tile = (t // w) * SWA_KV_HEADS

    d_model = x_ref.shape[1]
    grp = MXU_COLS
    out_state = {}

    def out_group(i, gc):
        rows = slice(i * t, (i + 1) * t)
        cols = slice(gc * grp, (gc + 1) * grp)
        y = (x_ref[rows, cols] + _dot(mm_ref[rows, :], w_ref[0:MLA_WIDTH, cols])
             + _dot(ms_ref[rows, :], w_ref[MLA_WIDTH:, cols]))
        out_state.setdefault(i, []).append(y)

    def out_norm(i):
        rows = slice(i * t, (i + 1) * t)
        ys = out_state.pop(i)
        ssq = _tree(jnp.add, [jnp.sum(y * y, axis=-1, keepdims=True) for y in ys])
        scale = lax.rsqrt(ssq / d_model + NORM_EPS)
        for gc, y in enumerate(ys):
            cols = slice(gc * grp, (gc + 1) * grp)
            o_ref[rows, cols] = y * scale * fg_ref[:, cols]

    nu = len(units)
    lag = SWA_STORE_LAG
    gsz = SWA_UNITS_PER_STEP
    pending = []
    tasks_per_step = SWA_OUT_TASKS_PER_STEP

    def retire(count):
        for _ in range(min(count, len(pending))):
            pending.pop(0)()

    for step in range(nu // gsz + lag):
        for u in range(step * gsz, (step + 1) * gsz):
            if u < nu:
                stage_scores(u, 0)
                stage_scores(u, 1)
        retire(tasks_per_step // 2)
        for u in range((step - 1) * gsz, step * gsz):
            if 0 <= u < nu:
                stage_values(u, 0)
                stage_values(u, 1)
        retire(tasks_per_step - tasks_per_step // 2)
        for u in range((step - lag) * gsz, (step - lag + 1) * gsz):
            if 0 <= u < nu:
                stage_store(u)
                if (u + 1) % units_per_tile == 0:
                    i = u // units_per_tile
                    pending += [functools.partial(out_group, i, gc) for gc in range(d_model // grp)]
                    pending.append(functools.partial(out_norm, i))
    for task in pending:
        task()


def _swa_out_call(sinks, qst, ks4, vst, gs, mm, x2, w_o, fg, batch, seq):
    n, d = x2.shape
    r = SWA_OUT_ROWS
    w = SWA_WINDOW
    cps = seq // r
    bpc = r // w
    bps = seq // w
    v_rows = SWA_KV_HEADS * SWA_D
    ones_rows = 16
    row = lambda b, c: (b * cps + c, 0)
    halo = lambda b, c: (b * bps + jnp.maximum(c * bpc - 1, 0), 0)
    feat = lambda b, c: (b, c)
    feat_halo = lambda b, c: (b, jnp.maximum(c * bpc - 1, 0))
    const = lambda b, c: (0, 0)
    wide = pl.BlockSpec((r, SWA_WIDTH), row)
    return pl.pallas_call(
        _swa_out_kernel,
        grid=(batch, cps),
        in_specs=[
            pl.BlockSpec(memory_space=pltpu.SMEM),
            pl.BlockSpec((SWA_WIDTH, r), feat),
            wide,
            pl.BlockSpec((w, SWA_WIDTH), halo),
            pl.BlockSpec((v_rows, r), feat),
            pl.BlockSpec((v_rows, w), feat_halo),
            wide,
            pl.BlockSpec((r, MLA_WIDTH), row),
            pl.BlockSpec((r, d), row),
            pl.BlockSpec(w_o.shape, const),
            pl.BlockSpec((1, d), const),
        ],
        out_specs=pl.BlockSpec((r, d), row),
        out_shape=jax.ShapeDtypeStruct((n, d), jnp.float32),
        scratch_shapes=[pltpu.VMEM((r, SWA_WIDTH), jnp.bfloat16),
                        pltpu.VMEM((r + w, SWA_WIDTH), jnp.bfloat16),
                        pltpu.VMEM((SWA_KV_HEADS, SWA_D + ones_rows, r + w), jnp.bfloat16),
                        pltpu.VMEM(w_o.shape, jnp.bfloat16)],
        compiler_params=pltpu.CompilerParams(
            dimension_semantics=("arbitrary", "arbitrary"),
            vmem_limit_bytes=V7X_VMEM_LIMIT_BYTES),
        name="swa_out",
    )(sinks, qst, ks4, ks4, vst, vst, gs, mm, x2, w_o, fg)


def _pair_rope_pieces(src, dst, n_heads, dim):
    half = dim // 2
    out = []
    for pair in range(n_heads // 2):
        for which in range(2):
            head = src + (2 * pair + which) * dim
            chunk = dst + pair * 2 * dim
            out.append((head, chunk + which * half, half))
            out.append((head + half, chunk + 2 * half + which * half, half))
    return out


def _w_in_pieces():
    pieces, qs_pieces, src, dst = [], [], 0, 0
    for name, width in (("c_q", MLA_Q_RANK), ("c_kv", MLA_KV_RANK), ("k_rope", MLA_ROPE),
                        ("g_mla", MLA_WIDTH), ("q_s", SWA_WIDTH),
                        ("k_s", SWA_KV_HEADS * SWA_D), ("v_s", SWA_KV_HEADS * SWA_D),
                        ("g_swa", SWA_WIDTH)):
        if name == "k_rope":
            half = MLA_ROPE // 2
            pieces += [(src, dst, half), (src, dst + half, half),
                       (src + half, dst + 2 * half, half), (src + half, dst + 3 * half, half)]
            dst += 2 * MLA_ROPE
        elif name == "q_s":
            qs_pieces += _pair_rope_pieces(src, 0, SWA_Q_HEADS, SWA_D)
        elif name == "k_s":
            pieces += _pair_rope_pieces(src, dst, SWA_KV_HEADS, SWA_D)
            dst += width
        else:
            pieces.append((src, dst, width))
            dst += width
        src += width
    assert dst == PROJ_COLS
    return pieces, qs_pieces


def _gather_cols(src_ref, pieces, n_dst_cols):
    rows, n_src = src_ref.shape
    lane = lax.broadcasted_iota(jnp.int32, (rows, LANES), 1)
    loaded, rolled = {}, {}

    def load(k):
        if k not in loaded:
            width = min(LANES, n_src - k * LANES)
            v = src_ref[:, k * LANES:k * LANES + width]
            if width < LANES:
                v = jnp.concatenate([v, jnp.zeros((rows, LANES - width), v.dtype)], axis=1)
            loaded[k] = v
        return loaded[k]

    def shifted(k, shift):
        if shift == 0:
            return load(k)
        if (k, shift) not in rolled:
            rolled[(k, shift)] = pltpu.roll(load(k), shift, 1)
        return rolled[(k, shift)]

    chunks = []
    for d in range(n_dst_cols // LANES):
        acc = None
        for s0, d0, w in pieces:
            lo, hi = max(d0, d * LANES), min(d0 + w, (d + 1) * LANES)
            while lo < hi:
                s_abs = s0 + lo - d0
                k, a = divmod(s_abs, LANES)
                n = min(hi - lo, LANES - a)
                a_dst = lo - d * LANES
                val = shifted(k, (a_dst - a) % LANES)
                if n == LANES:
                    acc = val
                else:
                    mask = (lane >= a_dst) & (lane < a_dst + n)
                    acc = jnp.where(mask, val, 0.0 if acc is None else acc)
                lo += n
        chunks.append(acc)
    return chunks


def _gather_rows(src_ref, pieces, d):
    parts = []
    for s0, d0, w in pieces:
        lo, hi = max(d0, d * LANES), min(d0 + w, (d + 1) * LANES)
        if lo < hi:
            parts.append((lo, src_ref[s0 + lo - d0:s0 + hi - d0, :]))
    parts.sort(key=lambda t: t[0])
    assert sum(p.shape[0] for _, p in parts) == LANES
    return jnp.concatenate([p for _, p in parts], axis=0) if len(parts) > 1 else parts[0][1]


def _prep_latent_weights(wq_ref, wkv_ref, wqt_ref, wk_ref, wvt_ref):
    q_pieces = ([(hd * MLA_QK, hd * MLA_NOPE, MLA_NOPE) for hd in range(MLA_HEADS)]
                + [(hd * MLA_QK + MLA_NOPE, MLA_HEADS * MLA_NOPE + hd * MLA_ROPE, MLA_ROPE)
                   for hd in range(MLA_HEADS)])
    wq = jnp.concatenate(_gather_cols(wq_ref, q_pieces, MLA_HEADS * MLA_QK), axis=1)
    wqt_ref[...] = wq.T.astype(wqt_ref.dtype)
    per_head = MLA_NOPE + MLA_V
    for hd in range(MLA_HEADS):
        wk_ref[:, hd * MLA_NOPE:(hd + 1) * MLA_NOPE] = (
            wkv_ref[:, hd * per_head:hd * per_head + MLA_NOPE].astype(wk_ref.dtype))
        wvt_ref[hd * MLA_V:(hd + 1) * MLA_V, :] = (
            wkv_ref[:, hd * per_head + MLA_NOPE:(hd + 1) * per_head].T.astype(wvt_ref.dtype))


def kernel(x, ln_mix, w_in, q_a_norm, w_q_up, kv_a_norm, w_kv_up, attn_sinks, w_out, final_norm):
    batch, seq, d = x.shape
    depth = ln_mix.shape[0]
    assert depth == 1, "final norm is fused into the single layer's output kernel"
    assert seq % MLA_BLOCK == 0 and seq % PROJ_ROWS == 0 and seq % SWA_OUT_ROWS == 0
    x2 = x.reshape(batch * seq, d)
    qt, km, vt, gm, qst, ks4, vst, gs = _proj_call(
        x2, ln_mix[0].reshape(1, -1), w_in[0].T, q_a_norm[0].reshape(1, -1), w_q_up[0],
        kv_a_norm[0].reshape(1, -1), w_kv_up[0], seq)
    mm = _mla_call(qt, km, vt, gm, batch, seq)
    out = _swa_out_call(attn_sinks[0], qst, ks4, vst, gs, mm, x2, w_out[0],
                        final_norm.reshape(1, -1), batch, seq)
    return out.reshape(batch, seq, d)
```

```python
import functools
import math

import jax
import jax.numpy as jnp
from jax import lax
from jax.experimental import pallas as pl
from jax.experimental.pallas import tpu as pltpu

ROPE_THETA = 10000.0
NORM_EPS = 1e-6
NEG_INF = -1e30
LOG2E = 1.4426950408889634

MLA_HEADS = 4
MLA_NOPE = 128
MLA_ROPE = 64
MLA_V = 128
MLA_Q_RANK = 256
MLA_KV_RANK = 128
MLA_QK = MLA_NOPE + MLA_ROPE
MLA_WIDTH = MLA_HEADS * MLA_V

SWA_Q_HEADS = 8
SWA_KV_HEADS = 2
SWA_D = 64
SWA_WINDOW = 128
SWA_GROUP = SWA_Q_HEADS // SWA_KV_HEADS
SWA_WIDTH = SWA_Q_HEADS * SWA_D

LANES = 128
SUBLANES = 8
MXU_COLS = 2 * LANES
MLA_QK_PAD = MXU_COLS
MLA_K_COLS = MLA_HEADS * MLA_NOPE + LANES
V7X_VMEM_LIMIT_BYTES = 56 * 1024 * 1024

PROJ_ROWS = 1024
MLA_BLOCK = 256
MLA_GROUP_TILES = 8
MLA_HEADS_PER_STEP = 2
SWA_OUT_ROWS = 1024
SWA_OUT_SUBTILE = 256
SWA_UNITS_PER_STEP = 4
SWA_OUT_TASKS_PER_STEP = 4
SWA_STORE_LAG = 2


def _inv_freq(dim):
    assert 2 * dim == LANES
    inv = 1.0 / (ROPE_THETA ** (jnp.arange(0, dim, 2, dtype=jnp.float32) / dim))
    return inv.reshape(dim // 2, 1)


def _rope_table_shapes(seq, tile, dim):
    f32 = jnp.float32
    return [pltpu.VMEM((seq // tile, dim // 2, tile), f32), pltpu.VMEM((seq // tile, dim // 2, tile), f32),
            pltpu.VMEM((seq, LANES), f32), pltpu.VMEM((seq, LANES), f32)]


def _fill_rope_tables(inv_ref, ct_ref, st_ref, cl_ref, sl_ref):
    n_tiles, half, tile = ct_ref.shape
    for j in range(n_tiles):
        pos = (lax.broadcasted_iota(jnp.int32, (half, tile), 1) + j * tile).astype(jnp.float32)
        ang = pos * inv_ref[...]
        c, s = jnp.cos(ang), jnp.sin(ang)
        ct_ref[j] = c
        st_ref[j] = s
        cl_ref[j * tile:(j + 1) * tile, :] = jnp.concatenate([c, c, c, c], axis=0).T
        sl_ref[j * tile:(j + 1) * tile, :] = jnp.concatenate([-s, -s, s, s], axis=0).T


def _rope(x, c, s):
    return x * c + pltpu.roll(x, LANES // 2, 1) * s


def _rms(x, g):
    return x * lax.rsqrt(jnp.mean(x * x, axis=-1, keepdims=True) + NORM_EPS) * g


def _silu(g):
    return g / (1.0 + jnp.exp(-g))


def _dot(a, b):
    return jnp.dot(a, b, preferred_element_type=jnp.float32)


def _dot_nt(a, b):
    return lax.dot_general(a, b, (((1,), (1,)), ((), ())),
                           preferred_element_type=jnp.float32)


PROJ_COLS = 1792


def _proj_kernel(x_ref, ln_ref, wint_ref, qg_ref, wq_f32_ref, kvg_ref, wkv_f32_ref,
                 inv_mla_ref, inv_swa_ref,
                 qt_ref, km_ref, vt_ref, gm_ref, qst_ref, ks_ref, vst_ref, gs_ref,
                 wt_ref, wqt_ref, wkv_ref, wvt_ref, *table_refs):
    bf = jnp.bfloat16

    mla_tabs = table_refs[0:4]
    swa_tabs = table_refs[4:8] if len(table_refs) > 4 else mla_tabs

    @pl.when(pl.program_id(0) == 0)
    def _():
        _fill_rope_tables(inv_mla_ref, *mla_tabs)
        if swa_tabs is not mla_tabs:
            _fill_rope_tables(inv_swa_ref, *swa_tabs)
        _prep_latent_weights(wq_f32_ref, wkv_f32_ref, wqt_ref, wkv_ref, wvt_ref)
        pieces, qs_pieces = _w_in_pieces()
        for d in range(PROJ_COLS // LANES):
            wt_ref[d * LANES:(d + 1) * LANES, :] = _gather_rows(wint_ref, pieces, d).astype(bf)
        for d in range(SWA_WIDTH // LANES):
            wt_ref[PROJ_COLS + d * LANES:PROJ_COLS + (d + 1) * LANES, :] = (
                _gather_rows(wint_ref, qs_pieces, d).astype(bf))

    x = x_ref[...]
    h = _rms(x, ln_ref[...]).astype(bf)

    tile = pl.program_id(0) % mla_tabs[0].shape[0]
    rows = pl.ds(pl.multiple_of(tile * x.shape[0], x.shape[0]), x.shape[0])
    cm, sm = mla_tabs[2][rows, :], mla_tabs[3][rows, :]
    cs, ss = swa_tabs[2][rows, :], swa_tabs[3][rows, :]
    lane = lax.broadcasted_iota(jnp.int32, (x.shape[0], LANES), 1)
    first = (lane % (LANES // 2)) < (LANES // 4)
    grp = MXU_COLS

    def project(a):
        return _dot_nt(h, wt_ref[a:a + grp, :])

    c_q = project(0)
    lat_b = project(256)
    cqn = _rms(c_q, qg_ref[...]).astype(bf)
    ckvn = _rms(lat_b[:, 0:MLA_KV_RANK], kvg_ref[...]).astype(bf)
    kr = _rope(lat_b[:, LANES:2 * LANES], cm, sm)
    kr_first = jnp.where(first, kr, 0.0).astype(bf)

    s_scale = LOG2E / math.sqrt(SWA_D)
    qst = _dot_nt(wt_ref[PROJ_COLS:PROJ_COLS + SWA_WIDTH, :], h)
    cst2 = jnp.concatenate([swa_tabs[0][tile], swa_tabs[0][tile]], axis=0)
    sst2 = jnp.concatenate([swa_tabs[1][tile], swa_tabs[1][tile]], axis=0)
    for ch in range(SWA_WIDTH // LANES):
        x1 = qst[ch * LANES:ch * LANES + SWA_D, :]
        x2 = qst[ch * LANES + SWA_D:(ch + 1) * LANES, :]
        qst_ref[ch * LANES:ch * LANES + SWA_D, :] = ((x1 * cst2 - x2 * sst2) * s_scale).astype(bf)
        qst_ref[ch * LANES + SWA_D:(ch + 1) * LANES, :] = ((x2 * cst2 + x1 * sst2) * s_scale).astype(bf)

    q_scale = LOG2E / math.sqrt(MLA_QK)
    nope_w = MLA_HEADS * MLA_NOPE
    half = MLA_ROPE // 2
    qt = _dot_nt(wqt_ref[...], cqn)
    ct, st = mla_tabs[0][tile], mla_tabs[1][tile]
    zeros = jnp.zeros((half, x.shape[0]), bf)
    for hd in range(MLA_HEADS):
        qt_ref[hd * grp:hd * grp + LANES, :] = (qt[hd * MLA_NOPE:(hd + 1) * MLA_NOPE, :] * q_scale).astype(bf)
        x1 = qt[nope_w + hd * MLA_ROPE:nope_w + hd * MLA_ROPE + half, :]
        x2 = qt[nope_w + hd * MLA_ROPE + half:nope_w + (hd + 1) * MLA_ROPE, :]
        base = hd * grp + LANES
        qt_ref[base:base + half, :] = ((x1 * ct - x2 * st) * q_scale).astype(bf)
        qt_ref[base + half:base + 2 * half, :] = zeros
        qt_ref[base + 2 * half:base + 3 * half, :] = ((x2 * ct + x1 * st) * q_scale).astype(bf)
        qt_ref[base + 3 * half:base + 4 * half, :] = zeros
    for g2 in range(nope_w // grp):
        km_ref[:, g2 * grp:(g2 + 1) * grp] = _dot(ckvn, wkv_ref[:, g2 * grp:(g2 + 1) * grp]).astype(bf)
    km_ref[:, nope_w:nope_w + LANES] = kr_first
    for g2 in range(MLA_WIDTH // grp):
        gm_ref[:, g2 * grp:(g2 + 1) * grp] = _silu(project(512 + g2 * grp)).astype(bf)

    kv_s = project(1024)
    k01 = _rope(kv_s[:, 0:LANES], cs, ss)
    ks_ref[:, 0 * LANES:1 * LANES] = jnp.where(first, k01, 0.0).astype(bf)
    ks_ref[:, 1 * LANES:2 * LANES] = jnp.where(first, 0.0, pltpu.roll(k01, LANES // 4, 1)).astype(bf)
    ks_ref[:, 2 * LANES:3 * LANES] = jnp.where(first, pltpu.roll(k01, 3 * LANES // 4, 1), 0.0).astype(bf)
    ks_ref[:, 3 * LANES:4 * LANES] = jnp.where(first, 0.0, k01).astype(bf)
    vst_ref[...] = kv_s[:, LANES:2 * LANES].T.astype(bf)

    for g2 in range(SWA_WIDTH // grp):
        gs_ref[:, g2 * grp:(g2 + 1) * grp] = _silu(project(1280 + g2 * grp)).astype(bf)

    vt_ref[...] = _dot_nt(wvt_ref[...], ckvn).astype(bf)


def _proj_call(x2, ln_g, w_in_t, q_g, w_q_up, kv_g, w_kv_up, seq):
    n, d = x2.shape
    tm = PROJ_ROWS
    steps_per_seq = seq // tm
    batch = n // seq
    row = lambda i: (i, 0)
    const = lambda i: (0, 0)
    col = lambda i: (i // steps_per_seq, i % steps_per_seq)
    table_scratch = _rope_table_shapes(seq, tm, MLA_ROPE)
    if SWA_D != MLA_ROPE:
        table_scratch += _rope_table_shapes(seq, tm, SWA_D)
    bf = jnp.bfloat16
    qt_rows = MLA_HEADS * MLA_QK_PAD
    row_out = lambda w: (pl.BlockSpec((tm, w), row), jax.ShapeDtypeStruct((n, w), bf))
    col_out = lambda r: (pl.BlockSpec((r, tm), col), jax.ShapeDtypeStruct((batch * r, seq), bf))
    outs = [col_out(qt_rows), row_out(MLA_K_COLS), col_out(MLA_WIDTH), row_out(MLA_WIDTH),
            col_out(SWA_WIDTH), row_out(SWA_WIDTH), col_out(SWA_KV_HEADS * SWA_D), row_out(SWA_WIDTH)]
    return pl.pallas_call(
        _proj_kernel,
        grid=(n // tm,),
        in_specs=[
            pl.BlockSpec((tm, d), row),
            pl.BlockSpec((1, d), const),
            pl.BlockSpec(w_in_t.shape, const),
            pl.BlockSpec((1, MLA_Q_RANK), const),
            pl.BlockSpec(w_q_up.shape, const),
            pl.BlockSpec((1, MLA_KV_RANK), const),
            pl.BlockSpec(w_kv_up.shape, const),
            pl.BlockSpec((MLA_ROPE // 2, 1), const),
            pl.BlockSpec((SWA_D // 2, 1), const),
        ],
        out_specs=[o[0] for o in outs],
        out_shape=[o[1] for o in outs],
        scratch_shapes=[pltpu.VMEM((PROJ_COLS + SWA_WIDTH, d), bf),
                        pltpu.VMEM((MLA_HEADS * MLA_QK, MLA_Q_RANK), bf),
                        pltpu.VMEM((MLA_KV_RANK, MLA_HEADS * MLA_NOPE), bf),
                        pltpu.VMEM((MLA_WIDTH, MLA_KV_RANK), bf)] + table_scratch,
        compiler_params=pltpu.CompilerParams(
            dimension_semantics=("arbitrary",),
            vmem_limit_bytes=V7X_VMEM_LIMIT_BYTES),
        name="proj",
    )(x2, ln_g, w_in_t, q_g, w_q_up, kv_g, w_kv_up, _inv_freq(MLA_ROPE), _inv_freq(SWA_D))


def _tree(op, xs):
    xs = list(xs)
    while len(xs) > 1:
        xs = [op(xs[a], xs[a + 1]) if a + 1 < len(xs) else xs[a] for a in range(0, len(xs), 2)]
    return xs[0]

def _mla_block_groups(nblk):
    groups, cur, load = [], [], 0
    for i in range(nblk - 1, -1, -1):
        if load + i + 1 > MLA_GROUP_TILES:
            groups.append(cur)
            cur, load = [], 0
        cur.append(i)
        load += i + 1
    groups.append(cur)
    return groups


def _mla_kernel(qt_ref, k_ref, kr_ref, vt_ref, g_ref, o_ref, s_ref, p_ref, vx_ref):
    t = MLA_BLOCK
    seq = k_ref.shape[0]
    sub = SUBLANES
    qk = MLA_QK_PAD
    r = lax.broadcasted_iota(jnp.int32, (t, t), 0)
    c = lax.broadcasted_iota(jnp.int32, (t, t), 1)
    causal = r <= c
    nblk = seq // t
    units = []
    for hd in range(MLA_HEADS_PER_STEP):
        for blocks in _mla_block_groups(nblk):
            members, off = [], 0
            for i in blocks:
                members.append((hd, i, off))
                off += (i + 1) * t
            assert off <= s_ref.shape[1]
            units.append(members)
    state = [[dict(m8=None) for _ in members] for members in units]

    for hd in range(MLA_HEADS_PER_STEP):
        vx_ref[hd, 0:MLA_V, :] = vt_ref[hd * MLA_V:(hd + 1) * MLA_V, :]
        vx_ref[hd, MLA_V:, :] = jnp.ones((vx_ref.shape[1] - MLA_V, seq), vx_ref.dtype)

    def score_tile(u, b, j):
        hd, i, off = units[u][b]
        st = state[u][b]
        keys = slice(j * t, (j + 1) * t)
        k_tile = jnp.concatenate([k_ref[keys, hd * MLA_NOPE:(hd + 1) * MLA_NOPE], kr_ref[keys, :]], axis=1)
        s = _dot(k_tile, qt_ref[hd * qk:(hd + 1) * qk, i * t:(i + 1) * t])
        if j == i:
            s = jnp.where(causal, s, NEG_INF)
        s_ref[u % 2, off + j * t:off + (j + 1) * t, :] = s
        m8 = _tree(jnp.maximum, [s[a * sub:(a + 1) * sub, :] for a in range(t // sub)])
        st["m8"] = m8 if st["m8"] is None else jnp.maximum(st["m8"], m8)

    def prob_tile(u, b, j):
        _, _, off = units[u][b]
        st = state[u][b]
        if "m" not in st:
            st["m"] = jnp.max(st["m8"], axis=0, keepdims=True)
        rows = slice(off + j * t, off + (j + 1) * t)
        p_ref[u % 3, rows, :] = jnp.exp2(s_ref[u % 2, rows, :] - st["m"]).astype(jnp.bfloat16)

    def value_matmul(u):
        for b, (hd, i, off) in enumerate(units[u]):
            kv = (i + 1) * t
            state[u][b]["acc"] = _dot(vx_ref[hd, :, 0:kv], p_ref[u % 3, off:off + kv, :])

    def store(u):
        for b, (hd, i, _) in enumerate(units[u]):
            rows = slice(i * t, (i + 1) * t)
            acc = state[u][b]["acc"]
            out_t = (acc[0:MLA_V, :] / acc[MLA_V:MLA_V + 1, :]).astype(o_ref.dtype)
            gate = g_ref[rows, hd * MLA_V:(hd + 1) * MLA_V]
            o_ref[rows, hd * MLA_V:(hd + 1) * MLA_V] = out_t.T * gate
        state[u] = None

    def tiles(fn, u):
        return [functools.partial(fn, u, b, j) for b, (_, i, _) in enumerate(units[u]) for j in range(i + 1)]

    nu = len(units)
    for task in tiles(score_tile, 0):
        task()
    for u in range(nu + 2):
        if 0 <= u - 2 < nu:
            store(u - 2)
        nxt = tiles(score_tile, u + 1) if u + 1 < nu else []
        cur = tiles(prob_tile, u) if u < nu else []
        while nxt or cur:
            if cur:
                cur.pop(0)()
            if nxt:
                nxt.pop(0)()
        if 0 <= u - 1 < nu:
            value_matmul(u - 1)


def _mla_call(qt, km, vt, gm, batch, seq):
    n = km.shape[0]
    t = MLA_BLOCK
    hps = MLA_HEADS_PER_STEP
    groups = MLA_HEADS // hps
    ones_rows = 16
    feat = lambda b, h: (b * groups + h, 0)
    tok = lambda b, h: (b, h)
    return pl.pallas_call(
        _mla_kernel,
        grid=(batch, groups),
        in_specs=[
            pl.BlockSpec((hps * MLA_QK_PAD, seq), feat),
            pl.BlockSpec((seq, hps * MLA_NOPE), tok),
            pl.BlockSpec((seq, LANES), lambda b, h: (b, MLA_HEADS * MLA_NOPE // LANES)),
            pl.BlockSpec((hps * MLA_V, seq), feat),
            pl.BlockSpec((seq, hps * MLA_V), tok),
        ],
        out_specs=pl.BlockSpec((seq, hps * MLA_V), tok),
        out_shape=jax.ShapeDtypeStruct((n, MLA_WIDTH), jnp.bfloat16),
        scratch_shapes=[pltpu.VMEM((2, MLA_GROUP_TILES * t, t), jnp.float32),
                        pltpu.VMEM((3, MLA_GROUP_TILES * t, t), jnp.bfloat16),
                        pltpu.VMEM((hps, MLA_V + ones_rows, seq), jnp.bfloat16)],
        compiler_params=pltpu.CompilerParams(
            dimension_semantics=("arbitrary", "arbitrary"),
            vmem_limit_bytes=V7X_VMEM_LIMIT_BYTES),
        name="mla",
    )(qt, km, km, vt, gm)


def _swa_out_kernel(sink_ref, qt_ref, k_ref, kh_ref, vt_ref, vth_ref, g_ref, mm_ref, x_ref,
                    wf_ref, fg_ref, o_ref, ms_ref, kx_ref, vx_ref, w_ref):
    w = SWA_WINDOW
    chunk = pl.program_id(1)
    rows_total = k_ref.shape[0]
    bf = jnp.bfloat16
    sub = SUBLANES

    @pl.when((pl.program_id(0) == 0) & (chunk == 0))
    def _():
        w_ref[...] = wf_ref[...].astype(bf)

    kx_ref[0:w, :] = kh_ref[...]
    kx_ref[w:, :] = k_ref[...]
    for hk in range(SWA_KV_HEADS):
        vx_ref[hk, 0:SWA_D, 0:w] = vth_ref[hk * SWA_D:(hk + 1) * SWA_D, :]
        vx_ref[hk, 0:SWA_D, w:] = vt_ref[hk * SWA_D:(hk + 1) * SWA_D, :]
        vx_ref[hk, SWA_D:, :] = jnp.ones((vx_ref.shape[1] - SWA_D, vx_ref.shape[2]), bf)

    ki = lax.broadcasted_iota(jnp.int32, (2 * w, 2 * w), 0)
    qi = lax.broadcasted_iota(jnp.int32, (2 * w, 2 * w), 1) % w
    rel = qi + w - ki
    band = (rel >= 0) & (rel < SWA_WINDOW)
    band_first = band & ((ki >= w) | (chunk > 0))
    lane = lax.broadcasted_iota(jnp.int32, (1, 2 * w), 1)

    units = [(n, hk) for n in range(rows_total // w) for hk in range(SWA_KV_HEADS)]
    state = [dict() for _ in units]

    def sink_rows(hk):
        sink = [sink_ref[hk * SWA_GROUP + i] * LOG2E for i in range(SWA_GROUP)]
        return (jnp.where(lane < w, sink[0], sink[2]), jnp.where(lane < w, sink[1], sink[3]))

    def stage_scores(u, half):
        n, hk = units[u]
        base = hk * 2 * LANES
        qcols = slice(n * w, (n + 1) * w)
        keys = slice(n * w, (n + 2) * w)
        qt = jnp.concatenate([qt_ref[base:base + LANES, qcols],
                              qt_ref[base + LANES:base + 2 * LANES, qcols]], axis=1)
        valid = band_first if n == 0 else band
        st = state[u]
        lanes = slice(base + half * LANES, base + (half + 1) * LANES)
        s = jnp.where(valid, _dot(kx_ref[keys, lanes], qt), NEG_INF)
        m8 = _tree(jnp.maximum, [s[a * sub:(a + 1) * sub, :] for a in range(2 * w // sub)])
        st.setdefault("s", {})[half] = s
        st.setdefault("m", {})[half] = jnp.maximum(jnp.max(m8, axis=0, keepdims=True),
                                                   sink_rows(hk)[half])

    def stage_values(u, half):
        n, hk = units[u]
        keys = slice(n * w, (n + 2) * w)
        st = state[u]
        p = jnp.exp2(st["s"].pop(half) - st["m"][half]).astype(bf)
        st.setdefault("acc", {})[half] = _dot(vx_ref[hk, :, keys], p)

    def stage_store(u):
        n, hk = units[u]
        base = hk * 2 * LANES
        qcols = slice(n * w, (n + 1) * w)
        sinks = sink_rows(hk)
        st = state[u]
        outs = []
        for half in range(2):
            acc = st["acc"][half]
            den = acc[SWA_D:SWA_D + 1, :] + jnp.exp2(sinks[half] - st["m"][half])
            outs.append(acc[0:SWA_D, :] / den)
        g = g_ref[qcols, base:base + 2 * LANES].astype(jnp.float32)
        for ch in range(2):
            o_t = jnp.concatenate([outs[0][:, ch * w:(ch + 1) * w],
                                   outs[1][:, ch * w:(ch + 1) * w]], axis=0)
            ms_ref[qcols, base + ch * LANES:base + (ch + 1) * LANES] = (
                o_t.T * g[:, ch * LANES:(ch + 1) * LANES]).astype(bf)
        state[u] = None

    t = SWA_OUT_SUBTILE
    units_per_tile = (t // w) * SWA_KV_HEADS

    d_model = x_ref.shape[1]
    grp = MXU_COLS
    out_state = {}

    def out_group(i, gc):
        rows = slice(i * t, (i + 1) * t)
        cols = slice(gc * grp, (gc + 1) * grp)
        y = (x_ref[rows, cols] + _dot(mm_ref[rows, :], w_ref[0:MLA_WIDTH, cols])
             + _dot(ms_ref[rows, :], w_ref[MLA_WIDTH:, cols]))
        out_state.setdefault(i, []).append(y)

    def out_norm(i):
        rows = slice(i * t, (i + 1) * t)
        ys = out_state.pop(i)
        ssq = _tree(jnp.add, [jnp.sum(y * y, axis=-1, keepdims=True) for y in ys])
        scale = lax.rsqrt(ssq / d_model + NORM_EPS)
        for gc, y in enumerate(ys):
            cols = slice(gc * grp, (gc + 1) * grp)
            o_ref[rows, cols] = y * scale * fg_ref[:, cols]

    nu = len(units)
    lag = SWA_STORE_LAG
    gsz = SWA_UNITS_PER_STEP
    pending = []
    tasks_per_step = SWA_OUT_TASKS_PER_STEP

    def retire(count):
        for _ in range(min(count, len(pending))):
            pending.pop(0)()

    for step in range(nu // gsz + lag):
        for u in range(step * gsz, (step + 1) * gsz):
            if u < nu:
                stage_scores(u, 0)
                stage_scores(u, 1)
        retire(tasks_per_step // 2)
        for u in range((step - 1) * gsz, step * gsz):
            if 0 <= u < nu:
                stage_values(u, 0)
                stage_values(u, 1)
        retire(tasks_per_step - tasks_per_step // 2)
        for u in range((step - lag) * gsz, (step - lag + 1) * gsz):
            if 0 <= u < nu:
                stage_store(u)
                if (u + 1) % units_per_tile == 0:
                    i = u // units_per_tile
                    pending += [functools.partial(out_group, i, gc) for gc in range(d_model // grp)]
                    pending.append(functools.partial(out_norm, i))
    for task in pending:
        task()


def _swa_out_call(sinks, qst, ks4, vst, gs, mm, x2, w_o, fg, batch, seq):
    n, d = x2.shape
    r = SWA_OUT_ROWS
    w = SWA_WINDOW
    cps = seq // r
    bpc = r // w
    bps = seq // w
    v_rows = SWA_KV_HEADS * SWA_D
    ones_rows = 16
    row = lambda b, c: (b * cps + c, 0)
    halo = lambda b, c: (b * bps + jnp.maximum(c * bpc - 1, 0), 0)
    feat = lambda b, c: (b, c)
    feat_halo = lambda b, c: (b, jnp.maximum(c * bpc - 1, 0))
    const = lambda b, c: (0, 0)
    wide = pl.BlockSpec((r, SWA_WIDTH), row)
    return pl.pallas_call(
        _swa_out_kernel,
        grid=(batch, cps),
        in_specs=[
            pl.BlockSpec(memory_space=pltpu.SMEM),
            pl.BlockSpec((SWA_WIDTH, r), feat),
            wide,
            pl.BlockSpec((w, SWA_WIDTH), halo),
            pl.BlockSpec((v_rows, r), feat),
            pl.BlockSpec((v_rows, w), feat_halo),
            wide,
            pl.BlockSpec((r, MLA_WIDTH), row),
            pl.BlockSpec((r, d), row),
            pl.BlockSpec(w_o.shape, const),
            pl.BlockSpec((1, d), const),
        ],
        out_specs=pl.BlockSpec((r, d), row),
        out_shape=jax.ShapeDtypeStruct((n, d), jnp.float32),
        scratch_shapes=[pltpu.VMEM((r, SWA_WIDTH), jnp.bfloat16),
                        pltpu.VMEM((r + w, SWA_WIDTH), jnp.bfloat16),
                        pltpu.VMEM((SWA_KV_HEADS, SWA_D + ones_rows, r + w), jnp.bfloat16),
                        pltpu.VMEM(w_o.shape, jnp.bfloat16)],
        compiler_params=pltpu.CompilerParams(
            dimension_semantics=("arbitrary", "arbitrary"),
            vmem_limit_bytes=V7X_VMEM_LIMIT_BYTES),
        name="swa_out",
    )(sinks, qst, ks4, ks4, vst, vst, gs, mm, x2, w_o, fg)


def _pair_rope_pieces(src, dst, n_heads, dim):
    half = dim // 2
    out = []
    for pair in range(n_heads // 2):
        for which in range(2):
            head = src + (2 * pair + which) * dim
            chunk = dst + pair * 2 * dim
            out.append((head, chunk + which * half, half))
            out.append((head + half, chunk + 2 * half + which * half, half))
    return out


def _w_in_pieces():
    pieces, qs_pieces, src, dst = [], [], 0, 0
    for name, width in (("c_q", MLA_Q_RANK), ("c_kv", MLA_KV_RANK), ("k_rope", MLA_ROPE),
                        ("g_mla", MLA_WIDTH), ("q_s", SWA_WIDTH),
                        ("k_s", SWA_KV_HEADS * SWA_D), ("v_s", SWA_KV_HEADS * SWA_D),
                        ("g_swa", SWA_WIDTH)):
        if name == "k_rope":
            half = MLA_ROPE // 2
            pieces += [(src, dst, half), (src, dst + half, half),
                       (src + half, dst + 2 * half, half), (src + half, dst + 3 * half, half)]
            dst += 2 * MLA_ROPE
        elif name == "q_s":
            qs_pieces += _pair_rope_pieces(src, 0, SWA_Q_HEADS, SWA_D)
        elif name == "k_s":
            pieces += _pair_rope_pieces(src, dst, SWA_KV_HEADS, SWA_D)
            dst += width
        else:
            pieces.append((src, dst, width))
            dst += width
        src += width
    assert dst == PROJ_COLS
    return pieces, qs_pieces


def _gather_cols(src_ref, pieces, n_dst_cols):
    rows, n_src = src_ref.shape
    lane = lax.broadcasted_iota(jnp.int32, (rows, LANES), 1)
    loaded, rolled = {}, {}

    def load(k):
        if k not in loaded:
            width = min(LANES, n_src - k * LANES)
            v = src_ref[:, k * LANES:k * LANES + width]
            if width < LANES:
                v = jnp.concatenate([v, jnp.zeros((rows, LANES - width), v.dtype)], axis=1)
            loaded[k] = v
        return loaded[k]

    def shifted(k, shift):
        if shift == 0:
            return load(k)
        if (k, shift) not in rolled:
            rolled[(k, shift)] = pltpu.roll(load(k), shift, 1)
        return rolled[(k, shift)]

    chunks = []
    for d in range(n_dst_cols // LANES):
        acc = None
        for s0, d0, w in pieces:
            lo, hi = max(d0, d * LANES), min(d0 + w, (d + 1) * LANES)
            while lo < hi:
                s_abs = s0 + lo - d0
                k, a = divmod(s_abs, LANES)
                n = min(hi - lo, LANES - a)
                a_dst = lo - d * LANES
                val = shifted(k, (a_dst - a) % LANES)
                if n == LANES:
                    acc = val
                else:
                    mask = (lane >= a_dst) & (lane < a_dst + n)
                    acc = jnp.where(mask, val, 0.0 if acc is None else acc)
                lo += n
        chunks.append(acc)
    return chunks


def _gather_rows(src_ref, pieces, d):
    parts = []
    for s0, d0, w in pieces:
        lo, hi = max(d0, d * LANES), min(d0 + w, (d + 1) * LANES)
        if lo < hi:
            parts.append((lo, src_ref[s0 + lo - d0:s0 + hi - d0, :]))
    parts.sort(key=lambda t: t[0])
    assert sum(p.shape[0] for _, p in parts) == LANES
    return jnp.concatenate([p for _, p in parts], axis=0) if len(parts) > 1 else parts[0][1]


def _prep_latent_weights(wq_ref, wkv_ref, wqt_ref, wk_ref, wvt_ref):
    q_pieces = ([(hd * MLA_QK, hd * MLA_NOPE, MLA_NOPE) for hd in range(MLA_HEADS)]
                + [(hd * MLA_QK + MLA_NOPE, MLA_HEADS * MLA_NOPE + hd * MLA_ROPE, MLA_ROPE)
                   for hd in range(MLA_HEADS)])
    wq = jnp.concatenate(_gather_cols(wq_ref, q_pieces, MLA_HEADS * MLA_QK), axis=1)
    wqt_ref[...] = wq.T.astype(wqt_ref.dtype)
    per_head = MLA_NOPE + MLA_V
    for hd in range(MLA_HEADS):
        wk_ref[:, hd * MLA_NOPE:(hd + 1) * MLA_NOPE] = (
            wkv_ref[:, hd * per_head:hd * per_head + MLA_NOPE].astype(wk_ref.dtype))
        wvt_ref[hd * MLA_V:(hd + 1) * MLA_V, :] = (
            wkv_ref[:, hd * per_head + MLA_NOPE:(hd + 1) * per_head].T.astype(wvt_ref.dtype))


def kernel(x, ln_mix, w_in, q_a_norm, w_q_up, kv_a_norm, w_kv_up, attn_sinks, w_out, final_norm):
    batch, seq, d = x.shape
    depth = ln_mix.shape[0]
    assert depth == 1, "final norm is fused into the single layer's output kernel"
    assert seq % MLA_BLOCK == 0 and seq % PROJ_ROWS == 0 and seq % SWA_OUT_ROWS == 0
    x2 = x.reshape(batch * seq, d)
    qt, km, vt, gm, qst, ks4, vst, gs = _proj_call(
        x2, ln_mix[0].reshape(1, -1), w_in[0].T, q_a_norm[0].reshape(1, -1), w_q_up[0],
        kv_a_norm[0].reshape(1, -1), w_kv_up[0], seq)
    mm = _mla_call(qt, km, vt, gm, batch, seq)
    out = _swa_out_call(attn_sinks[0], qst, ks4, vst, gs, mm, x2, w_out[0],
                        final_norm.reshape(1, -1), batch, seq)
    return out.reshape(batch, seq, d)
```

```python
import functools
import math

import jax
import jax.numpy as jnp
from jax import lax
from jax.experimental import pallas as pl
from jax.experimental.pallas import tpu as pltpu

ROPE_THETA = 10000.0
NORM_EPS = 1e-6
NEG_INF = -1e30
LOG2E = 1.4426950408889634

MLA_HEADS = 4
MLA_NOPE = 128
MLA_ROPE = 64
MLA_V = 128
MLA_Q_RANK = 256
MLA_KV_RANK = 128
MLA_QK = MLA_NOPE + MLA_ROPE
MLA_WIDTH = MLA_HEADS * MLA_V

SWA_Q_HEADS = 8
SWA_KV_HEADS = 2
SWA_D = 64
SWA_WINDOW = 128
SWA_GROUP = SWA_Q_HEADS // SWA_KV_HEADS
SWA_WIDTH = SWA_Q_HEADS * SWA_D

LANES = 128
SUBLANES = 8
MXU_COLS = 2 * LANES
MLA_QK_PAD = MXU_COLS
assert MLA_KV_RANK + LANES == MLA_QK_PAD
MLA_K_COLS = MLA_QK_PAD
V7X_VMEM_LIMIT_BYTES = 56 * 1024 * 1024

PROJ_ROWS = 1024
MLA_BLOCK = 256
MLA_GROUP_TILES = 8
MLA_HEADS_PER_STEP = 2
SWA_OUT_ROWS = 1024
SWA_OUT_SUBTILE = 256
SWA_UNITS_PER_STEP = 4
SWA_OUT_TASKS_PER_STEP = 4
SWA_STORE_LAG = 2


def _inv_freq(dim):
    assert 2 * dim == LANES
    inv = 1.0 / (ROPE_THETA ** (jnp.arange(0, dim, 2, dtype=jnp.float32) / dim))
    return inv.reshape(dim // 2, 1)


def _rope_table_shapes(seq, tile, dim):
    f32 = jnp.float32
    return [pltpu.VMEM((seq // tile, dim // 2, tile), f32), pltpu.VMEM((seq // tile, dim // 2, tile), f32),
            pltpu.VMEM((seq, LANES), f32), pltpu.VMEM((seq, LANES), f32)]


def _fill_rope_tables(inv_ref, ct_ref, st_ref, cl_ref, sl_ref):
    n_tiles, half, tile = ct_ref.shape
    for j in range(n_tiles):
        pos = (lax.broadcasted_iota(jnp.int32, (half, tile), 1) + j * tile).astype(jnp.float32)
        ang = pos * inv_ref[...]
        c, s = jnp.cos(ang), jnp.sin(ang)
        ct_ref[j] = c
        st_ref[j] = s
        cl_ref[j * tile:(j + 1) * tile, :] = jnp.concatenate([c, c, c, c], axis=0).T
        sl_ref[j * tile:(j + 1) * tile, :] = jnp.concatenate([-s, -s, s, s], axis=0).T


def _rope(x, c, s):
    return x * c + pltpu.roll(x, LANES // 2, 1) * s


def _rms(x, g):
    return x * lax.rsqrt(jnp.mean(x * x, axis=-1, keepdims=True) + NORM_EPS) * g


def _silu(g):
    return g / (1.0 + jnp.exp(-g))


def _dot(a, b):
    return jnp.dot(a, b, preferred_element_type=jnp.float32)


def _dot_nt(a, b):
    return lax.dot_general(a, b, (((1,), (1,)), ((), ())),
                           preferred_element_type=jnp.float32)


PROJ_COLS = 1792


def _proj_kernel(x_ref, ln_ref, wint_ref, qg_ref, wq_f32_ref, kvg_ref, wkv_f32_ref,
                 inv_mla_ref, inv_swa_ref,
                 qt_ref, km_ref, vt_ref, gm_ref, qst_ref, ks_ref, vst_ref, gs_ref,
                 wt_ref, wqt_ref, wvt_ref, *table_refs):
    bf = jnp.bfloat16

    mla_tabs = table_refs[0:4]
    swa_tabs = table_refs[4:8] if len(table_refs) > 4 else mla_tabs

    @pl.when(pl.program_id(0) == 0)
    def _():
        _fill_rope_tables(inv_mla_ref, *mla_tabs)
        if swa_tabs is not mla_tabs:
            _fill_rope_tables(inv_swa_ref, *swa_tabs)
        _prep_latent_weights(wq_f32_ref, wkv_f32_ref, wqt_ref, wvt_ref)
        pieces, qs_pieces = _w_in_pieces()
        for d in range(PROJ_COLS // LANES):
            wt_ref[d * LANES:(d + 1) * LANES, :] = _gather_rows(wint_ref, pieces, d).astype(bf)
        for d in range(SWA_WIDTH // LANES):
            wt_ref[PROJ_COLS + d * LANES:PROJ_COLS + (d + 1) * LANES, :] = (
                _gather_rows(wint_ref, qs_pieces, d).astype(bf))

    x = x_ref[...]
    h = _rms(x, ln_ref[...]).astype(bf)

    tile = pl.program_id(0) % mla_tabs[0].shape[0]
    rows = pl.ds(pl.multiple_of(tile * x.shape[0], x.shape[0]), x.shape[0])
    cm, sm = mla_tabs[2][rows, :], mla_tabs[3][rows, :]
    cs, ss = swa_tabs[2][rows, :], swa_tabs[3][rows, :]
    lane = lax.broadcasted_iota(jnp.int32, (x.shape[0], LANES), 1)
    first = (lane % (LANES // 2)) < (LANES // 4)
    grp = MXU_COLS

    def project(a):
        return _dot_nt(h, wt_ref[a:a + grp, :])

    c_q = project(0)
    lat_b = project(256)
    cqn = _rms(c_q, qg_ref[...]).astype(bf)
    ckvn = _rms(lat_b[:, 0:MLA_KV_RANK], kvg_ref[...]).astype(bf)
    kr = _rope(lat_b[:, LANES:2 * LANES], cm, sm)
    kr_first = jnp.where(first, kr, 0.0).astype(bf)

    s_scale = LOG2E / math.sqrt(SWA_D)
    qst = _dot_nt(wt_ref[PROJ_COLS:PROJ_COLS + SWA_WIDTH, :], h)
    cst2 = jnp.concatenate([swa_tabs[0][tile], swa_tabs[0][tile]], axis=0)
    sst2 = jnp.concatenate([swa_tabs[1][tile], swa_tabs[1][tile]], axis=0)
    for ch in range(SWA_WIDTH // LANES):
        x1 = qst[ch * LANES:ch * LANES + SWA_D, :]
        x2 = qst[ch * LANES + SWA_D:(ch + 1) * LANES, :]
        qst_ref[ch * LANES:ch * LANES + SWA_D, :] = ((x1 * cst2 - x2 * sst2) * s_scale).astype(bf)
        qst_ref[ch * LANES + SWA_D:(ch + 1) * LANES, :] = ((x2 * cst2 + x1 * sst2) * s_scale).astype(bf)

    q_scale = LOG2E / math.sqrt(MLA_QK)
    lat_w = MLA_HEADS * MLA_KV_RANK
    half = MLA_ROPE // 2
    qt = _dot_nt(wqt_ref[...], cqn)
    ct, st = mla_tabs[0][tile], mla_tabs[1][tile]
    zeros = jnp.zeros((half, x.shape[0]), bf)
    for hd in range(MLA_HEADS):
        qt_ref[hd * grp:hd * grp + LANES, :] = (
            qt[hd * MLA_KV_RANK:(hd + 1) * MLA_KV_RANK, :] * q_scale).astype(bf)
        x1 = qt[lat_w + hd * MLA_ROPE:lat_w + hd * MLA_ROPE + half, :]
        x2 = qt[lat_w + hd * MLA_ROPE + half:lat_w + (hd + 1) * MLA_ROPE, :]
        base = hd * grp + LANES
        qt_ref[base:base + half, :] = ((x1 * ct - x2 * st) * q_scale).astype(bf)
        qt_ref[base + half:base + 2 * half, :] = zeros
        qt_ref[base + 2 * half:base + 3 * half, :] = ((x2 * ct + x1 * st) * q_scale).astype(bf)
        qt_ref[base + 3 * half:base + 4 * half, :] = zeros
    km_ref[:, 0:MLA_KV_RANK] = ckvn
    km_ref[:, MLA_KV_RANK:MLA_KV_RANK + LANES] = kr_first
    for g2 in range(MLA_WIDTH // grp):
        gm_ref[:, g2 * grp:(g2 + 1) * grp] = _silu(project(512 + g2 * grp)).astype(bf)

    kv_s = project(1024)
    k01 = _rope(kv_s[:, 0:LANES], cs, ss)
    ks_ref[:, 0 * LANES:1 * LANES] = jnp.where(first, k01, 0.0).astype(bf)
    ks_ref[:, 1 * LANES:2 * LANES] = jnp.where(first, 0.0, pltpu.roll(k01, LANES // 4, 1)).astype(bf)
    ks_ref[:, 2 * LANES:3 * LANES] = jnp.where(first, pltpu.roll(k01, 3 * LANES // 4, 1), 0.0).astype(bf)
    ks_ref[:, 3 * LANES:4 * LANES] = jnp.where(first, 0.0, k01).astype(bf)
    vst_ref[...] = kv_s[:, LANES:2 * LANES].T.astype(bf)

    for g2 in range(SWA_WIDTH // grp):
        gs_ref[:, g2 * grp:(g2 + 1) * grp] = _silu(project(1280 + g2 * grp)).astype(bf)

    vt_ref[...] = _dot_nt(wvt_ref[...], ckvn).astype(bf)


def _proj_call(x2, ln_g, w_in_t, q_g, w_q_up, kv_g, w_kv_up, seq):
    n, d = x2.shape
    tm = PROJ_ROWS
    steps_per_seq = seq // tm
    batch = n // seq
    row = lambda i: (i, 0)
    const = lambda i: (0, 0)
    col = lambda i: (i // steps_per_seq, i % steps_per_seq)
    table_scratch = _rope_table_shapes(seq, tm, MLA_ROPE)
    if SWA_D != MLA_ROPE:
        table_scratch += _rope_table_shapes(seq, tm, SWA_D)
    bf = jnp.bfloat16
    qt_rows = MLA_HEADS * MLA_QK_PAD
    row_out = lambda w: (pl.BlockSpec((tm, w), row), jax.ShapeDtypeStruct((n, w), bf))
    col_out = lambda r: (pl.BlockSpec((r, tm), col), jax.ShapeDtypeStruct((batch * r, seq), bf))
    outs = [col_out(qt_rows), row_out(MLA_K_COLS), col_out(MLA_WIDTH), row_out(MLA_WIDTH),
            col_out(SWA_WIDTH), row_out(SWA_WIDTH), col_out(SWA_KV_HEADS * SWA_D), row_out(SWA_WIDTH)]
    return pl.pallas_call(
        _proj_kernel,
        grid=(n // tm,),
        in_specs=[
            pl.BlockSpec((tm, d), row),
            pl.BlockSpec((1, d), const),
            pl.BlockSpec(w_in_t.shape, const),
            pl.BlockSpec((1, MLA_Q_RANK), const),
            pl.BlockSpec(w_q_up.shape, const),
            pl.BlockSpec((1, MLA_KV_RANK), const),
            pl.BlockSpec(w_kv_up.shape, const),
            pl.BlockSpec((MLA_ROPE // 2, 1), const),
            pl.BlockSpec((SWA_D // 2, 1), const),
        ],
        out_specs=[o[0] for o in outs],
        out_shape=[o[1] for o in outs],
        scratch_shapes=[pltpu.VMEM((PROJ_COLS + SWA_WIDTH, d), bf),
                        pltpu.VMEM((MLA_HEADS * (MLA_KV_RANK + MLA_ROPE), MLA_Q_RANK), bf),
                        pltpu.VMEM((MLA_WIDTH, MLA_KV_RANK), bf)] + table_scratch,
        compiler_params=pltpu.CompilerParams(
            dimension_semantics=("arbitrary",),
            vmem_limit_bytes=V7X_VMEM_LIMIT_BYTES),
        name="proj",
    )(x2, ln_g, w_in_t, q_g, w_q_up, kv_g, w_kv_up, _inv_freq(MLA_ROPE), _inv_freq(SWA_D))


def _tree(op, xs):
    xs = list(xs)
    while len(xs) > 1:
        xs = [op(xs[a], xs[a + 1]) if a + 1 < len(xs) else xs[a] for a in range(0, len(xs), 2)]
    return xs[0]

def _mla_block_groups(nblk):
    groups, cur, load = [], [], 0
    for i in range(nblk - 1, -1, -1):
        if load + i + 1 > MLA_GROUP_TILES:
            groups.append(cur)
            cur, load = [], 0
        cur.append(i)
        load += i + 1
    groups.append(cur)
    return groups


def _mla_kernel(qt_ref, k_ref, vt_ref, g_ref, o_ref, s_ref, p_ref, vx_ref):
    t = MLA_BLOCK
    seq = k_ref.shape[0]
    sub = SUBLANES
    qk = MLA_QK_PAD
    r = lax.broadcasted_iota(jnp.int32, (t, t), 0)
    c = lax.broadcasted_iota(jnp.int32, (t, t), 1)
    causal = r <= c
    nblk = seq // t
    units = []
    for hd in range(MLA_HEADS_PER_STEP):
        for blocks in _mla_block_groups(nblk):
            members, off = [], 0
            for i in blocks:
                members.append((hd, i, off))
                off += (i + 1) * t
            assert off <= s_ref.shape[1]
            units.append(members)
    state = [[dict(m8=None) for _ in members] for members in units]

    for hd in range(MLA_HEADS_PER_STEP):
        vx_ref[hd, 0:MLA_V, :] = vt_ref[hd * MLA_V:(hd + 1) * MLA_V, :]
        vx_ref[hd, MLA_V:, :] = jnp.ones((vx_ref.shape[1] - MLA_V, seq), vx_ref.dtype)

    def score_tile(u, b, j):
        hd, i, off = units[u][b]
        st = state[u][b]
        keys = slice(j * t, (j + 1) * t)
        s = _dot(k_ref[keys, :], qt_ref[hd * qk:(hd + 1) * qk, i * t:(i + 1) * t])
        if j == i:
            s = jnp.where(causal, s, NEG_INF)
        s_ref[u % 2, off + j * t:off + (j + 1) * t, :] = s
        m8 = _tree(jnp.maximum, [s[a * sub:(a + 1) * sub, :] for a in range(t // sub)])
        st["m8"] = m8 if st["m8"] is None else jnp.maximum(st["m8"], m8)

    def prob_tile(u, b, j):
        _, _, off = units[u][b]
        st = state[u][b]
        if "m" not in st:
            st["m"] = jnp.max(st["m8"], axis=0, keepdims=True)
        rows = slice(off + j * t, off + (j + 1) * t)
        p_ref[u % 3, rows, :] = jnp.exp2(s_ref[u % 2, rows, :] - st["m"]).astype(jnp.bfloat16)

    def value_matmul(u):
        for b, (hd, i, off) in enumerate(units[u]):
            kv = (i + 1) * t
            state[u][b]["acc"] = _dot(vx_ref[hd, :, 0:kv], p_ref[u % 3, off:off + kv, :])

    def store(u):
        for b, (hd, i, _) in enumerate(units[u]):
            rows = slice(i * t, (i + 1) * t)
            acc = state[u][b]["acc"]
            out_t = (acc[0:MLA_V, :] / acc[MLA_V:MLA_V + 1, :]).astype(o_ref.dtype)
            gate = g_ref[rows, hd * MLA_V:(hd + 1) * MLA_V]
            o_ref[rows, hd * MLA_V:(hd + 1) * MLA_V] = out_t.T * gate
        state[u] = None

    def tiles(fn, u):
        return [functools.partial(fn, u, b, j) for b, (_, i, _) in enumerate(units[u]) for j in range(i + 1)]

    nu = len(units)
    for task in tiles(score_tile, 0):
        task()
    for u in range(nu + 2):
        if 0 <= u - 2 < nu:
            store(u - 2)
        nxt = tiles(score_tile, u + 1) if u + 1 < nu else []
        cur = tiles(prob_tile, u) if u < nu else []
        while nxt or cur:
            if cur:
                cur.pop(0)()
            if nxt:
                nxt.pop(0)()
        if 0 <= u - 1 < nu:
            value_matmul(u - 1)


def _mla_call(qt, km, vt, gm, batch, seq):
    n = km.shape[0]
    t = MLA_BLOCK
    hps = MLA_HEADS_PER_STEP
    groups = MLA_HEADS // hps
    ones_rows = 16
    feat = lambda b, h: (b * groups + h, 0)
    tok = lambda b, h: (b, h)
    return pl.pallas_call(
        _mla_kernel,
        grid=(batch, groups),
        in_specs=[
            pl.BlockSpec((hps * MLA_QK_PAD, seq), feat),
            pl.BlockSpec((seq, MLA_K_COLS), lambda b, h: (b, 0)),
            pl.BlockSpec((hps * MLA_V, seq), feat),
            pl.BlockSpec((seq, hps * MLA_V), tok),
        ],
        out_specs=pl.BlockSpec((seq, hps * MLA_V), tok),
        out_shape=jax.ShapeDtypeStruct((n, MLA_WIDTH), jnp.bfloat16),
        scratch_shapes=[pltpu.VMEM((2, MLA_GROUP_TILES * t, t), jnp.float32),
                        pltpu.VMEM((3, MLA_GROUP_TILES * t, t), jnp.bfloat16),
                        pltpu.VMEM((hps, MLA_V + ones_rows, seq), jnp.bfloat16)],
        compiler_params=pltpu.CompilerParams(
            dimension_semantics=("arbitrary", "arbitrary"),
            vmem_limit_bytes=V7X_VMEM_LIMIT_BYTES),
        name="mla",
    )(qt, km, vt, gm)


def _swa_out_kernel(sink_ref, qt_ref, k_ref, kh_ref, vt_ref, vth_ref, g_ref, mm_ref, x_ref,
                    wf_ref, fg_ref, o_ref, ms_ref, kx_ref, vx_ref, w_ref):
    w = SWA_WINDOW
    chunk = pl.program_id(1)
    rows_total = k_ref.shape[0]
    bf = jnp.bfloat16
    sub = SUBLANES

    @pl.when((pl.program_id(0) == 0) & (chunk == 0))
    def _():
        w_ref[...] = wf_ref[...].astype(bf)

    kx_ref[0:w, :] = kh_ref[...]
    kx_ref[w:, :] = k_ref[...]
    for hk in range(SWA_KV_HEADS):
        vx_ref[hk, 0:SWA_D, 0:w] = vth_ref[hk * SWA_D:(hk + 1) * SWA_D, :]
        vx_ref[hk, 0:SWA_D, w:] = vt_ref[hk * SWA_D:(hk + 1) * SWA_D, :]
        vx_ref[hk, SWA_D:, :] = jnp.ones((vx_ref.shape[1] - SWA_D, vx_ref.shape[2]), bf)

    ki = lax.broadcasted_iota(jnp.int32, (2 * w, 2 * w), 0)
    qi = lax.broadcasted_iota(jnp.int32, (2 * w, 2 * w), 1) % w
    rel = qi + w - ki
    band = (rel >= 0) & (rel < SWA_WINDOW)
    band_first = band & ((ki >= w) | (chunk > 0))
    lane = lax.broadcasted_iota(jnp.int32, (1, 2 * w), 1)

    units = [(n, hk) for n in range(rows_total // w) for hk in range(SWA_KV_HEADS)]
    state = [dict() for _ in units]

    def sink_rows(hk):
        sink = [sink_ref[hk * SWA_GROUP + i] * LOG2E for i in range(SWA_GROUP)]
        return (jnp.where(lane < w, sink[0], sink[2]), jnp.where(lane < w, sink[1], sink[3]))

    def stage_scores(u, half):
        n, hk = units[u]
        base = hk * 2 * LANES
        qcols = slice(n * w, (n + 1) * w)
        keys = slice(n * w, (n + 2) * w)
        qt = jnp.concatenate([qt_ref[base:base + LANES, qcols],
                              qt_ref[base + LANES:base + 2 * LANES, qcols]], axis=1)
        valid = band_first if n == 0 else band
        st = state[u]
        lanes = slice(base + half * LANES, base + (half + 1) * LANES)
        s = jnp.where(valid, _dot(kx_ref[keys, lanes], qt), NEG_INF)
        m8 = _tree(jnp.maximum, [s[a * sub:(a + 1) * sub, :] for a in range(2 * w // sub)])
        st.setdefault("s", {})[half] = s
        st.setdefault("m", {})[half] = jnp.maximum(jnp.max(m8, axis=0, keepdims=True),
                                                   sink_rows(hk)[half])

    def stage_values(u, half):
        n, hk = units[u]
        keys = slice(n * w, (n + 2) * w)
        st = state[u]
        p = jnp.exp2(st["s"].pop(half) - st["m"][half]).astype(bf)
        st.setdefault("acc", {})[half] = _dot(vx_ref[hk, :, keys], p)

    def stage_store(u):
        n, hk = units[u]
        base = hk * 2 * LANES
        qcols = slice(n * w, (n + 1) * w)
        sinks = sink_rows(hk)
        st = state[u]
        outs = []
        for half in range(2):
            acc = st["acc"][half]
            den = acc[SWA_D:SWA_D + 1, :] + jnp.exp2(sinks[half] - st["m"][half])
            outs.append(acc[0:SWA_D, :] / den)
        g = g_ref[qcols, base:base + 2 * LANES].astype(jnp.float32)
        for ch in range(2):
            o_t = jnp.concatenate([outs[0][:, ch * w:(ch + 1) * w],
                                   outs[1][:, ch * w:(ch + 1) * w]], axis=0)
            ms_ref[qcols, base + ch * LANES:base + (ch + 1) * LANES] = (
                o_t.T * g[:, ch * LANES:(ch + 1) * LANES]).astype(bf)
        state[u] = None

    t = SWA_OUT_SUBTILE
    units_per_tile = (t // w) * SWA_KV_HEADS

    d_model = x_ref.shape[1]
    grp = MXU_COLS
    out_state = {}

    def out_group(i, gc):
        rows = slice(i * t, (i + 1) * t)
        cols = slice(gc * grp, (gc + 1) * grp)
        y = (x_ref[rows, cols] + _dot(mm_ref[rows, :], w_ref[0:MLA_WIDTH, cols])
             + _dot(ms_ref[rows, :], w_ref[MLA_WIDTH:, cols]))
        out_state.setdefault(i, []).append(y)

    def out_norm(i):
        rows = slice(i * t, (i + 1) * t)
        ys = out_state.pop(i)
        ssq = _tree(jnp.add, [jnp.sum(y * y, axis=-1, keepdims=True) for y in ys])
        scale = lax.rsqrt(ssq / d_model + NORM_EPS)
        for gc, y in enumerate(ys):
            cols = slice(gc * grp, (gc + 1) * grp)
            o_ref[rows, cols] = y * scale * fg_ref[:, cols]

    nu = len(units)
    lag = SWA_STORE_LAG
    gsz = SWA_UNITS_PER_STEP
    pending = []
    tasks_per_step = SWA_OUT_TASKS_PER_STEP

    def retire(count):
        for _ in range(min(count, len(pending))):
            pending.pop(0)()

    for step in range(nu // gsz + lag):
        for u in range(step * gsz, (step + 1) * gsz):
            if u < nu:
                stage_scores(u, 0)
                stage_scores(u, 1)
        retire(tasks_per_step // 2)
        for u in range((step - 1) * gsz, step * gsz):
            if 0 <= u < nu:
                stage_values(u, 0)
                stage_values(u, 1)
        retire(tasks_per_step - tasks_per_step // 2)
        for u in range((step - lag) * gsz, (step - lag + 1) * gsz):
            if 0 <= u < nu:
                stage_store(u)
                if (u + 1) % units_per_tile == 0:
                    i = u // units_per_tile
                    pending += [functools.partial(out_group, i, gc) for gc in range(d_model // grp)]
                    pending.append(functools.partial(out_norm, i))
    for task in pending:
        task()


def _swa_out_call(sinks, qst, ks4, vst, gs, mm, x2, w_o, fg, batch, seq):
    n, d = x2.shape
    r = SWA_OUT_ROWS
    w = SWA_WINDOW
    cps = seq // r
    bpc = r // w
    bps = seq // w
    v_rows = SWA_KV_HEADS * SWA_D
    ones_rows = 16
    row = lambda b, c: (b * cps + c, 0)
    halo = lambda b, c: (b * bps + jnp.maximum(c * bpc - 1, 0), 0)
    feat = lambda b, c: (b, c)
    feat_halo = lambda b, c: (b, jnp.maximum(c * bpc - 1, 0))
    const = lambda b, c: (0, 0)
    wide = pl.BlockSpec((r, SWA_WIDTH), row)
    return pl.pallas_call(
        _swa_out_kernel,
        grid=(batch, cps),
        in_specs=[
            pl.BlockSpec(memory_space=pltpu.SMEM),
            pl.BlockSpec((SWA_WIDTH, r), feat),
            wide,
            pl.BlockSpec((w, SWA_WIDTH), halo),
            pl.BlockSpec((v_rows, r), feat),
            pl.BlockSpec((v_rows, w), feat_halo),
            wide,
            pl.BlockSpec((r, MLA_WIDTH), row),
            pl.BlockSpec((r, d), row),
            pl.BlockSpec(w_o.shape, const),
            pl.BlockSpec((1, d), const),
        ],
        out_specs=pl.BlockSpec((r, d), row),
        out_shape=jax.ShapeDtypeStruct((n, d), jnp.float32),
        scratch_shapes=[pltpu.VMEM((r, SWA_WIDTH), jnp.bfloat16),
                        pltpu.VMEM((r + w, SWA_WIDTH), jnp.bfloat16),
                        pltpu.VMEM((SWA_KV_HEADS, SWA_D + ones_rows, r + w), jnp.bfloat16),
                        pltpu.VMEM(w_o.shape, jnp.bfloat16)],
        compiler_params=pltpu.CompilerParams(
            dimension_semantics=("arbitrary", "arbitrary"),
            vmem_limit_bytes=V7X_VMEM_LIMIT_BYTES),
        name="swa_out",
    )(sinks, qst, ks4, ks4, vst, vst, gs, mm, x2, w_o, fg)


def _pair_rope_pieces(src, dst, n_heads, dim):
    half = dim // 2
    out = []
    for pair in range(n_heads // 2):
        for which in range(2):
            head = src + (2 * pair + which) * dim
            chunk = dst + pair * 2 * dim
            out.append((head, chunk + which * half, half))
            out.append((head + half, chunk + 2 * half + which * half, half))
    return out


def _w_in_pieces():
    pieces, qs_pieces, src, dst = [], [], 0, 0
    for name, width in (("c_q", MLA_Q_RANK), ("c_kv", MLA_KV_RANK), ("k_rope", MLA_ROPE),
                        ("g_mla", MLA_WIDTH), ("q_s", SWA_WIDTH),
                        ("k_s", SWA_KV_HEADS * SWA_D), ("v_s", SWA_KV_HEADS * SWA_D),
                        ("g_swa", SWA_WIDTH)):
        if name == "k_rope":
            half = MLA_ROPE // 2
            pieces += [(src, dst, half), (src, dst + half, half),
                       (src + half, dst + 2 * half, half), (src + half, dst + 3 * half, half)]
            dst += 2 * MLA_ROPE
        elif name == "q_s":
            qs_pieces += _pair_rope_pieces(src, 0, SWA_Q_HEADS, SWA_D)
        elif name == "k_s":
            pieces += _pair_rope_pieces(src, dst, SWA_KV_HEADS, SWA_D)
            dst += width
        else:
            pieces.append((src, dst, width))
            dst += width
        src += width
    assert dst == PROJ_COLS
    return pieces, qs_pieces


def _gather_cols(src_ref, pieces, n_dst_cols):
    rows, n_src = src_ref.shape
    lane = lax.broadcasted_iota(jnp.int32, (rows, LANES), 1)
    loaded, rolled = {}, {}

    def load(k):
        if k not in loaded:
            width = min(LANES, n_src - k * LANES)
            v = src_ref[:, k * LANES:k * LANES + width]
            if width < LANES:
                v = jnp.concatenate([v, jnp.zeros((rows, LANES - width), v.dtype)], axis=1)
            loaded[k] = v
        return loaded[k]

    def shifted(k, shift):
        if shift == 0:
            return load(k)
        if (k, shift) not in rolled:
            rolled[(k, shift)] = pltpu.roll(load(k), shift, 1)
        return rolled[(k, shift)]

    chunks = []
    for d in range(n_dst_cols // LANES):
        acc = None
        for s0, d0, w in pieces:
            lo, hi = max(d0, d * LANES), min(d0 + w, (d + 1) * LANES)
            while lo < hi:
                s_abs = s0 + lo - d0
                k, a = divmod(s_abs, LANES)
                n = min(hi - lo, LANES - a)
                a_dst = lo - d * LANES
                val = shifted(k, (a_dst - a) % LANES)
                if n == LANES:
                    acc = val
                else:
                    mask = (lane >= a_dst) & (lane < a_dst + n)
                    acc = jnp.where(mask, val, 0.0 if acc is None else acc)
                lo += n
        chunks.append(acc)
    return chunks


def _gather_rows(src_ref, pieces, d):
    parts = []
    for s0, d0, w in pieces:
        lo, hi = max(d0, d * LANES), min(d0 + w, (d + 1) * LANES)
        if lo < hi:
            parts.append((lo, src_ref[s0 + lo - d0:s0 + hi - d0, :]))
    parts.sort(key=lambda t: t[0])
    assert sum(p.shape[0] for _, p in parts) == LANES
    return jnp.concatenate([p for _, p in parts], axis=0) if len(parts) > 1 else parts[0][1]


def _prep_latent_weights(wq_ref, wkv_ref, wqt_ref, wvt_ref):
    q_pieces = ([(hd * MLA_QK, hd * MLA_NOPE, MLA_NOPE) for hd in range(MLA_HEADS)]
                + [(hd * MLA_QK + MLA_NOPE, MLA_HEADS * MLA_NOPE + hd * MLA_ROPE, MLA_ROPE)
                   for hd in range(MLA_HEADS)])
    wq = jnp.concatenate(_gather_cols(wq_ref, q_pieces, MLA_HEADS * MLA_QK), axis=1)
    nope_w = MLA_HEADS * MLA_NOPE
    per_head = MLA_NOPE + MLA_V
    for hd in range(MLA_HEADS):
        wk = wkv_ref[:, hd * per_head:hd * per_head + MLA_NOPE]
        absorbed_t = lax.dot_general(wk, wq[:, hd * MLA_NOPE:(hd + 1) * MLA_NOPE],
                                     (((1,), (1,)), ((), ())), precision=lax.Precision.HIGHEST,
                                     preferred_element_type=jnp.float32)
        wqt_ref[hd * MLA_KV_RANK:(hd + 1) * MLA_KV_RANK, :] = absorbed_t.astype(wqt_ref.dtype)
        wvt_ref[hd * MLA_V:(hd + 1) * MLA_V, :] = (
            wkv_ref[:, hd * per_head + MLA_NOPE:(hd + 1) * per_head].T.astype(wvt_ref.dtype))
    rope_rows = MLA_HEADS * MLA_KV_RANK
    wqt_ref[rope_rows:, :] = wq[:, nope_w:].T.astype(wqt_ref.dtype)


def kernel(x, ln_mix, w_in, q_a_norm, w_q_up, kv_a_norm, w_kv_up, attn_sinks, w_out, final_norm):
    batch, seq, d = x.shape
    depth = ln_mix.shape[0]
    assert depth == 1, "final norm is fused into the single layer's output kernel"
    assert seq % MLA_BLOCK == 0 and seq % PROJ_ROWS == 0 and seq % SWA_OUT_ROWS == 0
    x2 = x.reshape(batch * seq, d)
    qt, km, vt, gm, qst, ks4, vst, gs = _proj_call(
        x2, ln_mix[0].reshape(1, -1), w_in[0].T, q_a_norm[0].reshape(1, -1), w_q_up[0],
        kv_a_norm[0].reshape(1, -1), w_kv_up[0], seq)
    mm = _mla_call(qt, km, vt, gm, batch, seq)
    out = _swa_out_call(attn_sinks[0], qst, ks4, vst, gs, mm, x2, w_out[0],
                        final_norm.reshape(1, -1), batch, seq)
    return out.reshape(batch, seq, d)
```

```python
import functools
import math

import jax
import jax.numpy as jnp
from jax import lax
from jax.experimental import pallas as pl
from jax.experimental.pallas import tpu as pltpu

ROPE_THETA = 10000.0
NORM_EPS = 1e-6
NEG_INF = -1e30
LOG2E = 1.4426950408889634

MLA_HEADS = 4
MLA_NOPE = 128
MLA_ROPE = 64
MLA_V = 128
MLA_Q_RANK = 256
MLA_KV_RANK = 128
MLA_QK = MLA_NOPE + MLA_ROPE
MLA_WIDTH = MLA_HEADS * MLA_V

SWA_Q_HEADS = 8
SWA_KV_HEADS = 2
SWA_D = 64
SWA_WINDOW = 128
SWA_GROUP = SWA_Q_HEADS // SWA_KV_HEADS
SWA_WIDTH = SWA_Q_HEADS * SWA_D

LANES = 128
SUBLANES = 8
MXU_COLS = 2 * LANES
MLA_QK_PAD = MXU_COLS
assert MLA_KV_RANK + LANES == MLA_QK_PAD
MLA_K_COLS = MLA_QK_PAD
V7X_VMEM_LIMIT_BYTES = 56 * 1024 * 1024

PROJ_ROWS = 1024
MLA_BLOCK = 256
MLA_GROUP_TILES = 8
MLA_HEADS_PER_STEP = 2
SWA_OUT_ROWS = 1024
SWA_OUT_SUBTILE = 256
SWA_UNITS_PER_STEP = 4
SWA_OUT_TASKS_PER_STEP = 4
SWA_STORE_LAG = 2


def _inv_freq(dim):
    assert 2 * dim == LANES
    inv = 1.0 / (ROPE_THETA ** (jnp.arange(0, dim, 2, dtype=jnp.float32) / dim))
    return inv.reshape(dim // 2, 1)


def _rope_table_shapes(seq, tile, dim):
    f32 = jnp.float32
    return [pltpu.VMEM((seq // tile, dim // 2, tile), f32), pltpu.VMEM((seq // tile, dim // 2, tile), f32),
            pltpu.VMEM((seq, LANES), f32), pltpu.VMEM((seq, LANES), f32)]


def _fill_rope_tables(inv_ref, ct_ref, st_ref, cl_ref, sl_ref):
    n_tiles, half, tile = ct_ref.shape
    for j in range(n_tiles):
        pos = (lax.broadcasted_iota(jnp.int32, (half, tile), 1) + j * tile).astype(jnp.float32)
        ang = pos * inv_ref[...]
        c, s = jnp.cos(ang), jnp.sin(ang)
        ct_ref[j] = c
        st_ref[j] = s
        cl_ref[j * tile:(j + 1) * tile, :] = jnp.concatenate([c, c, c, c], axis=0).T
        sl_ref[j * tile:(j + 1) * tile, :] = jnp.concatenate([-s, -s, s, s], axis=0).T


def _rope(x, c, s):
    return x * c + pltpu.roll(x, LANES // 2, 1) * s


def _unit_rms(x):
    return x * lax.rsqrt(jnp.mean(x * x, axis=-1, keepdims=True) + NORM_EPS)


def _rms(x, g):
    return _unit_rms(x) * g


def _silu(g):
    return g / (1.0 + jnp.exp(-g))


def _dot(a, b):
    return jnp.dot(a, b, preferred_element_type=jnp.float32)


def _dot_nt(a, b):
    return lax.dot_general(a, b, (((1,), (1,)), ((), ())),
                           preferred_element_type=jnp.float32)


PROJ_COLS = 1792


def _proj_kernel(x_ref, ln_ref, wint_ref, qg_ref, wq_f32_ref, kvg_ref, wkv_f32_ref,
                 inv_mla_ref, inv_swa_ref,
                 qt_ref, km_ref, vt_ref, gm_ref, qst_ref, ks_ref, vst_ref, gs_ref,
                 wt_ref, wqt_ref, wvt_ref, *table_refs):
    bf = jnp.bfloat16

    mla_tabs = table_refs[0:4]
    swa_tabs = table_refs[4:8] if len(table_refs) > 4 else mla_tabs

    @pl.when(pl.program_id(0) == 0)
    def _():
        _fill_rope_tables(inv_mla_ref, *mla_tabs)
        if swa_tabs is not mla_tabs:
            _fill_rope_tables(inv_swa_ref, *swa_tabs)
        _prep_latent_weights(wq_f32_ref, wkv_f32_ref, wqt_ref, wvt_ref)
        pieces, qs_pieces = _w_in_pieces()
        gain = ln_ref[...]
        for d in range(PROJ_COLS // LANES):
            wt_ref[d * LANES:(d + 1) * LANES, :] = (_gather_rows(wint_ref, pieces, d) * gain).astype(bf)
        for d in range(SWA_WIDTH // LANES):
            wt_ref[PROJ_COLS + d * LANES:PROJ_COLS + (d + 1) * LANES, :] = (
                _gather_rows(wint_ref, qs_pieces, d) * gain).astype(bf)

    x = x_ref[...]
    h = _unit_rms(x).astype(bf)

    tile = pl.program_id(0) % mla_tabs[0].shape[0]
    rows = pl.ds(pl.multiple_of(tile * x.shape[0], x.shape[0]), x.shape[0])
    cm, sm = mla_tabs[2][rows, :], mla_tabs[3][rows, :]
    cs, ss = swa_tabs[2][rows, :], swa_tabs[3][rows, :]
    lane = lax.broadcasted_iota(jnp.int32, (x.shape[0], LANES), 1)
    first = (lane % (LANES // 2)) < (LANES // 4)
    grp = MXU_COLS

    def project(a):
        return _dot_nt(h, wt_ref[a:a + grp, :])

    c_q = project(0)
    lat_b = project(256)
    cqn = _rms(c_q, qg_ref[...]).astype(bf)
    ckvn = _rms(lat_b[:, 0:MLA_KV_RANK], kvg_ref[...]).astype(bf)
    kr = _rope(lat_b[:, LANES:2 * LANES], cm, sm)
    kr_first = jnp.where(first, kr, 0.0).astype(bf)

    s_scale = LOG2E / math.sqrt(SWA_D)
    qst = _dot_nt(wt_ref[PROJ_COLS:PROJ_COLS + SWA_WIDTH, :], h)
    cst2 = jnp.concatenate([swa_tabs[0][tile], swa_tabs[0][tile]], axis=0)
    sst2 = jnp.concatenate([swa_tabs[1][tile], swa_tabs[1][tile]], axis=0)
    for ch in range(SWA_WIDTH // LANES):
        x1 = qst[ch * LANES:ch * LANES + SWA_D, :]
        x2 = qst[ch * LANES + SWA_D:(ch + 1) * LANES, :]
        qst_ref[ch * LANES:ch * LANES + SWA_D, :] = ((x1 * cst2 - x2 * sst2) * s_scale).astype(bf)
        qst_ref[ch * LANES + SWA_D:(ch + 1) * LANES, :] = ((x2 * cst2 + x1 * sst2) * s_scale).astype(bf)

    q_scale = LOG2E / math.sqrt(MLA_QK)
    lat_w = MLA_HEADS * MLA_KV_RANK
    half = MLA_ROPE // 2
    qt = _dot_nt(wqt_ref[...], cqn)
    ct, st = mla_tabs[0][tile], mla_tabs[1][tile]
    zeros = jnp.zeros((half, x.shape[0]), bf)
    for hd in range(MLA_HEADS):
        qt_ref[hd * grp:hd * grp + LANES, :] = (
            qt[hd * MLA_KV_RANK:(hd + 1) * MLA_KV_RANK, :] * q_scale).astype(bf)
        x1 = qt[lat_w + hd * MLA_ROPE:lat_w + hd * MLA_ROPE + half, :]
        x2 = qt[lat_w + hd * MLA_ROPE + half:lat_w + (hd + 1) * MLA_ROPE, :]
        base = hd * grp + LANES
        qt_ref[base:base + half, :] = ((x1 * ct - x2 * st) * q_scale).astype(bf)
        qt_ref[base + half:base + 2 * half, :] = zeros
        qt_ref[base + 2 * half:base + 3 * half, :] = ((x2 * ct + x1 * st) * q_scale).astype(bf)
        qt_ref[base + 3 * half:base + 4 * half, :] = zeros
    km_ref[:, 0:MLA_KV_RANK] = ckvn
    km_ref[:, MLA_KV_RANK:MLA_KV_RANK + LANES] = kr_first
    for g2 in range(MLA_WIDTH // grp):
        gm_ref[:, g2 * grp:(g2 + 1) * grp] = _silu(project(512 + g2 * grp)).astype(bf)

    kv_s = project(1024)
    k01 = _rope(kv_s[:, 0:LANES], cs, ss)
    ks_ref[:, 0 * LANES:1 * LANES] = jnp.where(first, k01, 0.0).astype(bf)
    ks_ref[:, 1 * LANES:2 * LANES] = jnp.where(first, 0.0, pltpu.roll(k01, LANES // 4, 1)).astype(bf)
    ks_ref[:, 2 * LANES:3 * LANES] = jnp.where(first, pltpu.roll(k01, 3 * LANES // 4, 1), 0.0).astype(bf)
    ks_ref[:, 3 * LANES:4 * LANES] = jnp.where(first, 0.0, k01).astype(bf)
    vst_ref[...] = kv_s[:, LANES:2 * LANES].T.astype(bf)

    for g2 in range(SWA_WIDTH // grp):
        gs_ref[:, g2 * grp:(g2 + 1) * grp] = _silu(project(1280 + g2 * grp)).astype(bf)

    vt_ref[...] = _dot_nt(wvt_ref[...], ckvn).astype(bf)


def _proj_call(x2, ln_g, w_in_t, q_g, w_q_up, kv_g, w_kv_up, seq):
    n, d = x2.shape
    tm = PROJ_ROWS
    steps_per_seq = seq // tm
    batch = n // seq
    row = lambda i: (i, 0)
    const = lambda i: (0, 0)
    col = lambda i: (i // steps_per_seq, i % steps_per_seq)
    table_scratch = _rope_table_shapes(seq, tm, MLA_ROPE)
    if SWA_D != MLA_ROPE:
        table_scratch += _rope_table_shapes(seq, tm, SWA_D)
    bf = jnp.bfloat16
    qt_rows = MLA_HEADS * MLA_QK_PAD
    row_out = lambda w: (pl.BlockSpec((tm, w), row), jax.ShapeDtypeStruct((n, w), bf))
    col_out = lambda r: (pl.BlockSpec((r, tm), col), jax.ShapeDtypeStruct((batch * r, seq), bf))
    outs = [col_out(qt_rows), row_out(MLA_K_COLS), col_out(MLA_WIDTH), row_out(MLA_WIDTH),
            col_out(SWA_WIDTH), row_out(SWA_WIDTH), col_out(SWA_KV_HEADS * SWA_D), row_out(SWA_WIDTH)]
    return pl.pallas_call(
        _proj_kernel,
        grid=(n // tm,),
        in_specs=[
            pl.BlockSpec((tm, d), row),
            pl.BlockSpec((1, d), const),
            pl.BlockSpec(w_in_t.shape, const),
            pl.BlockSpec((1, MLA_Q_RANK), const),
            pl.BlockSpec(w_q_up.shape, const),
            pl.BlockSpec((1, MLA_KV_RANK), const),
            pl.BlockSpec(w_kv_up.shape, const),
            pl.BlockSpec((MLA_ROPE // 2, 1), const),
            pl.BlockSpec((SWA_D // 2, 1), const),
        ],
        out_specs=[o[0] for o in outs],
        out_shape=[o[1] for o in outs],
        scratch_shapes=[pltpu.VMEM((PROJ_COLS + SWA_WIDTH, d), bf),
                        pltpu.VMEM((MLA_HEADS * (MLA_KV_RANK + MLA_ROPE), MLA_Q_RANK), bf),
                        pltpu.VMEM((MLA_WIDTH, MLA_KV_RANK), bf)] + table_scratch,
        compiler_params=pltpu.CompilerParams(
            dimension_semantics=("arbitrary",),
            vmem_limit_bytes=V7X_VMEM_LIMIT_BYTES),
        name="proj",
    )(x2, ln_g, w_in_t, q_g, w_q_up, kv_g, w_kv_up, _inv_freq(MLA_ROPE), _inv_freq(SWA_D))


def _tree(op, xs):
    xs = list(xs)
    while len(xs) > 1:
        xs = [op(xs[a], xs[a + 1]) if a + 1 < len(xs) else xs[a] for a in range(0, len(xs), 2)]
    return xs[0]

def _mla_block_groups(nblk):
    groups, cur, load = [], [], 0
    for i in range(nblk - 1, -1, -1):
        if load + i + 1 > MLA_GROUP_TILES:
            groups.append(cur)
            cur, load = [], 0
        cur.append(i)
        load += i + 1
    groups.append(cur)
    return groups


def _mla_kernel(qt_ref, k_ref, vt_ref, g_ref, o_ref, s_ref, p_ref, vx_ref):
    t = MLA_BLOCK
    seq = k_ref.shape[0]
    sub = SUBLANES
    qk = MLA_QK_PAD
    r = lax.broadcasted_iota(jnp.int32, (t, t), 0)
    c = lax.broadcasted_iota(jnp.int32, (t, t), 1)
    causal = r <= c
    nblk = seq // t
    units = []
    for hd in range(MLA_HEADS_PER_STEP):
        for blocks in _mla_block_groups(nblk):
            members, off = [], 0
            for i in blocks:
                members.append((hd, i, off))
                off += (i + 1) * t
            assert off <= s_ref.shape[1]
            units.append(members)
    state = [[dict(m8=None) for _ in members] for members in units]

    for hd in range(MLA_HEADS_PER_STEP):
        vx_ref[hd, 0:MLA_V, :] = vt_ref[hd * MLA_V:(hd + 1) * MLA_V, :]
        vx_ref[hd, MLA_V:, :] = jnp.ones((vx_ref.shape[1] - MLA_V, seq), vx_ref.dtype)

    def score_tile(u, b, j):
        hd, i, off = units[u][b]
        st = state[u][b]
        keys = slice(j * t, (j + 1) * t)
        s = _dot(k_ref[keys, :], qt_ref[hd * qk:(hd + 1) * qk, i * t:(i + 1) * t])
        if j == i:
            s = jnp.where(causal, s, NEG_INF)
        s_ref[u % 2, off + j * t:off + (j + 1) * t, :] = s
        m8 = _tree(jnp.maximum, [s[a * sub:(a + 1) * sub, :] for a in range(t // sub)])
        st["m8"] = m8 if st["m8"] is None else jnp.maximum(st["m8"], m8)

    def prob_tile(u, b, j):
        _, _, off = units[u][b]
        st = state[u][b]
        if "m" not in st:
            st["m"] = jnp.max(st["m8"], axis=0, keepdims=True)
        rows = slice(off + j * t, off + (j + 1) * t)
        p_ref[u % 3, rows, :] = jnp.exp2(s_ref[u % 2, rows, :] - st["m"]).astype(jnp.bfloat16)

    def value_matmul(u):
        for b, (hd, i, off) in enumerate(units[u]):
            kv = (i + 1) * t
            state[u][b]["acc"] = _dot(vx_ref[hd, :, 0:kv], p_ref[u % 3, off:off + kv, :])

    def store(u):
        for b, (hd, i, _) in enumerate(units[u]):
            rows = slice(i * t, (i + 1) * t)
            acc = state[u][b]["acc"]
            out_t = (acc[0:MLA_V, :] / acc[MLA_V:MLA_V + 1, :]).astype(o_ref.dtype)
            gate = g_ref[rows, hd * MLA_V:(hd + 1) * MLA_V]
            o_ref[rows, hd * MLA_V:(hd + 1) * MLA_V] = out_t.T * gate
        state[u] = None

    def tiles(fn, u):
        return [functools.partial(fn, u, b, j) for b, (_, i, _) in enumerate(units[u]) for j in range(i + 1)]

    nu = len(units)
    for task in tiles(score_tile, 0):
        task()
    for u in range(nu + 2):
        if 0 <= u - 2 < nu:
            store(u - 2)
        nxt = tiles(score_tile, u + 1) if u + 1 < nu else []
        cur = tiles(prob_tile, u) if u < nu else []
        while nxt or cur:
            if cur:
                cur.pop(0)()
            if nxt:
                nxt.pop(0)()
        if 0 <= u - 1 < nu:
            value_matmul(u - 1)


def _mla_call(qt, km, vt, gm, batch, seq):
    n = km.shape[0]
    t = MLA_BLOCK
    hps = MLA_HEADS_PER_STEP
    groups = MLA_HEADS // hps
    ones_rows = 16
    feat = lambda b, h: (b * groups + h, 0)
    tok = lambda b, h: (b, h)
    return pl.pallas_call(
        _mla_kernel,
        grid=(batch, groups),
        in_specs=[
            pl.BlockSpec((hps * MLA_QK_PAD, seq), feat),
            pl.BlockSpec((seq, MLA_K_COLS), lambda b, h: (b, 0)),
            pl.BlockSpec((hps * MLA_V, seq), feat),
            pl.BlockSpec((seq, hps * MLA_V), tok),
        ],
        out_specs=pl.BlockSpec((seq, hps * MLA_V), tok),
        out_shape=jax.ShapeDtypeStruct((n, MLA_WIDTH), jnp.bfloat16),
        scratch_shapes=[pltpu.VMEM((2, MLA_GROUP_TILES * t, t), jnp.float32),
                        pltpu.VMEM((3, MLA_GROUP_TILES * t, t), jnp.bfloat16),
                        pltpu.VMEM((hps, MLA_V + ones_rows, seq), jnp.bfloat16)],
        compiler_params=pltpu.CompilerParams(
            dimension_semantics=("arbitrary", "arbitrary"),
            vmem_limit_bytes=V7X_VMEM_LIMIT_BYTES),
        name="mla",
    )(qt, km, vt, gm)


def _swa_out_kernel(sink_ref, qt_ref, k_ref, kh_ref, vt_ref, vth_ref, g_ref, mm_ref, x_ref,
                    wf_ref, fg_ref, o_ref, ms_ref, kx_ref, vx_ref, w_ref):
    w = SWA_WINDOW
    chunk = pl.program_id(1)
    rows_total = k_ref.shape[0]
    bf = jnp.bfloat16
    sub = SUBLANES

    @pl.when((pl.program_id(0) == 0) & (chunk == 0))
    def _():
        w_ref[...] = wf_ref[...].astype(bf)

    kx_ref[0:w, :] = kh_ref[...]
    kx_ref[w:, :] = k_ref[...]
    for hk in range(SWA_KV_HEADS):
        vx_ref[hk, 0:SWA_D, 0:w] = vth_ref[hk * SWA_D:(hk + 1) * SWA_D, :]
        vx_ref[hk, 0:SWA_D, w:] = vt_ref[hk * SWA_D:(hk + 1) * SWA_D, :]
        vx_ref[hk, SWA_D:, :] = jnp.ones((vx_ref.shape[1] - SWA_D, vx_ref.shape[2]), bf)

    ki = lax.broadcasted_iota(jnp.int32, (2 * w, 2 * w), 0)
    qi = lax.broadcasted_iota(jnp.int32, (2 * w, 2 * w), 1) % w
    rel = qi + w - ki
    band = (rel >= 0) & (rel < SWA_WINDOW)
    band_first = band & ((ki >= w) | (chunk > 0))
    lane = lax.broadcasted_iota(jnp.int32, (1, 2 * w), 1)

    units = [(n, hk) for n in range(rows_total // w) for hk in range(SWA_KV_HEADS)]
    state = [dict() for _ in units]

    def sink_rows(hk):
        sink = [sink_ref[hk * SWA_GROUP + i] * LOG2E for i in range(SWA_GROUP)]
        return (jnp.where(lane < w, sink[0], sink[2]), jnp.where(lane < w, sink[1], sink[3]))

    def stage_scores(u, half):
        n, hk = units[u]
        base = hk * 2 * LANES
        qcols = slice(n * w, (n + 1) * w)
        keys = slice(n * w, (n + 2) * w)
        qt = jnp.concatenate([qt_ref[base:base + LANES, qcols],
                              qt_ref[base + LANES:base + 2 * LANES, qcols]], axis=1)
        valid = band_first if n == 0 else band
        st = state[u]
        lanes = slice(base + half * LANES, base + (half + 1) * LANES)
        s = jnp.where(valid, _dot(kx_ref[keys, lanes], qt), NEG_INF)
        m8 = _tree(jnp.maximum, [s[a * sub:(a + 1) * sub, :] for a in range(2 * w // sub)])
        st.setdefault("s", {})[half] = s
        st.setdefault("m", {})[half] = jnp.maximum(jnp.max(m8, axis=0, keepdims=True),
                                                   sink_rows(hk)[half])

    def stage_values(u, half):
        n, hk = units[u]
        keys = slice(n * w, (n + 2) * w)
        st = state[u]
        p = jnp.exp2(st["s"].pop(half) - st["m"][half]).astype(bf)
        st.setdefault("acc", {})[half] = _dot(vx_ref[hk, :, keys], p)

    def stage_store(u):
        n, hk = units[u]
        base = hk * 2 * LANES
        qcols = slice(n * w, (n + 1) * w)
        sinks = sink_rows(hk)
        st = state[u]
        outs = []
        for half in range(2):
            acc = st["acc"][half]
            den = acc[SWA_D:SWA_D + 1, :] + jnp.exp2(sinks[half] - st["m"][half])
            outs.append(acc[0:SWA_D, :] / den)
        g = g_ref[qcols, base:base + 2 * LANES].astype(jnp.float32)
        for ch in range(2):
            o_t = jnp.concatenate([outs[0][:, ch * w:(ch + 1) * w],
                                   outs[1][:, ch * w:(ch + 1) * w]], axis=0)
            ms_ref[qcols, base + ch * LANES:base + (ch + 1) * LANES] = (
                o_t.T * g[:, ch * LANES:(ch + 1) * LANES]).astype(bf)
        state[u] = None

    t = SWA_OUT_SUBTILE
    units_per_tile = (t // w) * SWA_KV_HEADS

    d_model = x_ref.shape[1]
    grp = MXU_COLS
    out_state = {}

    def out_group(i, gc):
        rows = slice(i * t, (i + 1) * t)
        cols = slice(gc * grp, (gc + 1) * grp)
        y = (x_ref[rows, cols] + _dot(mm_ref[rows, :], w_ref[0:MLA_WIDTH, cols])
             + _dot(ms_ref[rows, :], w_ref[MLA_WIDTH:, cols]))
        out_state.setdefault(i, []).append(y)

    def out_norm(i):
        rows = slice(i * t, (i + 1) * t)
        ys = out_state.pop(i)
        ssq = _tree(jnp.add, [jnp.sum(y * y, axis=-1, keepdims=True) for y in ys])
        scale = lax.rsqrt(ssq / d_model + NORM_EPS)
        for gc, y in enumerate(ys):
            cols = slice(gc * grp, (gc + 1) * grp)
            o_ref[rows, cols] = y * scale * fg_ref[:, cols]

    nu = len(units)
    lag = SWA_STORE_LAG
    gsz = SWA_UNITS_PER_STEP
    pending = []
    tasks_per_step = SWA_OUT_TASKS_PER_STEP

    def retire(count):
        for _ in range(min(count, len(pending))):
            pending.pop(0)()

    for step in range(nu // gsz + lag):
        for u in range(step * gsz, (step + 1) * gsz):
            if u < nu:
                stage_scores(u, 0)
                stage_scores(u, 1)
        retire(tasks_per_step // 2)
        for u in range((step - 1) * gsz, step * gsz):
            if 0 <= u < nu:
                stage_values(u, 0)
                stage_values(u, 1)
        retire(tasks_per_step - tasks_per_step // 2)
        for u in range((step - lag) * gsz, (step - lag + 1) * gsz):
            if 0 <= u < nu:
                stage_store(u)
                if (u + 1) % units_per_tile == 0:
                    i = u // units_per_tile
                    pending += [functools.partial(out_group, i, gc) for gc in range(d_model // grp)]
                    pending.append(functools.partial(out_norm, i))
    for task in pending:
        task()


def _swa_out_call(sinks, qst, ks4, vst, gs, mm, x2, w_o, fg, batch, seq):
    n, d = x2.shape
    r = SWA_OUT_ROWS
    w = SWA_WINDOW
    cps = seq // r
    bpc = r // w
    bps = seq // w
    v_rows = SWA_KV_HEADS * SWA_D
    ones_rows = 16
    row = lambda b, c: (b * cps + c, 0)
    halo = lambda b, c: (b * bps + jnp.maximum(c * bpc - 1, 0), 0)
    feat = lambda b, c: (b, c)
    feat_halo = lambda b, c: (b, jnp.maximum(c * bpc - 1, 0))
    const = lambda b, c: (0, 0)
    wide = pl.BlockSpec((r, SWA_WIDTH), row)
    return pl.pallas_call(
        _swa_out_kernel,
        grid=(batch, cps),
        in_specs=[
            pl.BlockSpec(memory_space=pltpu.SMEM),
            pl.BlockSpec((SWA_WIDTH, r), feat),
            wide,
            pl.BlockSpec((w, SWA_WIDTH), halo),
            pl.BlockSpec((v_rows, r), feat),
            pl.BlockSpec((v_rows, w), feat_halo),
            wide,
            pl.BlockSpec((r, MLA_WIDTH), row),
            pl.BlockSpec((r, d), row),
            pl.BlockSpec(w_o.shape, const),
            pl.BlockSpec((1, d), const),
        ],
        out_specs=pl.BlockSpec((r, d), row),
        out_shape=jax.ShapeDtypeStruct((n, d), jnp.float32),
        scratch_shapes=[pltpu.VMEM((r, SWA_WIDTH), jnp.bfloat16),
                        pltpu.VMEM((r + w, SWA_WIDTH), jnp.bfloat16),
                        pltpu.VMEM((SWA_KV_HEADS, SWA_D + ones_rows, r + w), jnp.bfloat16),
                        pltpu.VMEM(w_o.shape, jnp.bfloat16)],
        compiler_params=pltpu.CompilerParams(
            dimension_semantics=("arbitrary", "arbitrary"),
            vmem_limit_bytes=V7X_VMEM_LIMIT_BYTES),
        name="swa_out",
    )(sinks, qst, ks4, ks4, vst, vst, gs, mm, x2, w_o, fg)


def _pair_rope_pieces(src, dst, n_heads, dim):
    half = dim // 2
    out = []
    for pair in range(n_heads // 2):
        for which in range(2):
            head = src + (2 * pair + which) * dim
            chunk = dst + pair * 2 * dim
            out.append((head, chunk + which * half, half))
            out.append((head + half, chunk + 2 * half + which * half, half))
    return out


def _w_in_pieces():
    pieces, qs_pieces, src, dst = [], [], 0, 0
    for name, width in (("c_q", MLA_Q_RANK), ("c_kv", MLA_KV_RANK), ("k_rope", MLA_ROPE),
                        ("g_mla", MLA_WIDTH), ("q_s", SWA_WIDTH),
                        ("k_s", SWA_KV_HEADS * SWA_D), ("v_s", SWA_KV_HEADS * SWA_D),
                        ("g_swa", SWA_WIDTH)):
        if name == "k_rope":
            half = MLA_ROPE // 2
            pieces += [(src, dst, half), (src, dst + half, half),
                       (src + half, dst + 2 * half, half), (src + half, dst + 3 * half, half)]
            dst += 2 * MLA_ROPE
        elif name == "q_s":
            qs_pieces += _pair_rope_pieces(src, 0, SWA_Q_HEADS, SWA_D)
        elif name == "k_s":
            pieces += _pair_rope_pieces(src, dst, SWA_KV_HEADS, SWA_D)
            dst += width
        else:
            pieces.append((src, dst, width))
            dst += width
        src += width
    assert dst == PROJ_COLS
    return pieces, qs_pieces


def _gather_cols(src_ref, pieces, n_dst_cols):
    rows, n_src = src_ref.shape
    lane = lax.broadcasted_iota(jnp.int32, (rows, LANES), 1)
    loaded, rolled = {}, {}

    def load(k):
        if k not in loaded:
            width = min(LANES, n_src - k * LANES)
            v = src_ref[:, k * LANES:k * LANES + width]
            if width < LANES:
                v = jnp.concatenate([v, jnp.zeros((rows, LANES - width), v.dtype)], axis=1)
            loaded[k] = v
        return loaded[k]

    def shifted(k, shift):
        if shift == 0:
            return load(k)
        if (k, shift) not in rolled:
            rolled[(k, shift)] = pltpu.roll(load(k), shift, 1)
        return rolled[(k, shift)]

    chunks = []
    for d in range(n_dst_cols // LANES):
        acc = None
        for s0, d0, w in pieces:
            lo, hi = max(d0, d * LANES), min(d0 + w, (d + 1) * LANES)
            while lo < hi:
                s_abs = s0 + lo - d0
                k, a = divmod(s_abs, LANES)
                n = min(hi - lo, LANES - a)
                a_dst = lo - d * LANES
                val = shifted(k, (a_dst - a) % LANES)
                if n == LANES:
                    acc = val
                else:
                    mask = (lane >= a_dst) & (lane < a_dst + n)
                    acc = jnp.where(mask, val, 0.0 if acc is None else acc)
                lo += n
        chunks.append(acc)
    return chunks


def _gather_rows(src_ref, pieces, d):
    parts = []
    for s0, d0, w in pieces:
        lo, hi = max(d0, d * LANES), min(d0 + w, (d + 1) * LANES)
        if lo < hi:
            parts.append((lo, src_ref[s0 + lo - d0:s0 + hi - d0, :]))
    parts.sort(key=lambda t: t[0])
    assert sum(p.shape[0] for _, p in parts) == LANES
    return jnp.concatenate([p for _, p in parts], axis=0) if len(parts) > 1 else parts[0][1]


def _prep_latent_weights(wq_ref, wkv_ref, wqt_ref, wvt_ref):
    q_pieces = ([(hd * MLA_QK, hd * MLA_NOPE, MLA_NOPE) for hd in range(MLA_HEADS)]
                + [(hd * MLA_QK + MLA_NOPE, MLA_HEADS * MLA_NOPE + hd * MLA_ROPE, MLA_ROPE)
                   for hd in range(MLA_HEADS)])
    wq = jnp.concatenate(_gather_cols(wq_ref, q_pieces, MLA_HEADS * MLA_QK), axis=1)
    nope_w = MLA_HEADS * MLA_NOPE
    per_head = MLA_NOPE + MLA_V
    for hd in range(MLA_HEADS):
        wk = wkv_ref[:, hd * per_head:hd * per_head + MLA_NOPE]
        absorbed_t = lax.dot_general(wk, wq[:, hd * MLA_NOPE:(hd + 1) * MLA_NOPE],
                                     (((1,), (1,)), ((), ())), precision=lax.Precision.HIGHEST,
                                     preferred_element_type=jnp.float32)
        wqt_ref[hd * MLA_KV_RANK:(hd + 1) * MLA_KV_RANK, :] = absorbed_t.astype(wqt_ref.dtype)
        wvt_ref[hd * MLA_V:(hd + 1) * MLA_V, :] = (
            wkv_ref[:, hd * per_head + MLA_NOPE:(hd + 1) * per_head].T.astype(wvt_ref.dtype))
    rope_rows = MLA_HEADS * MLA_KV_RANK
    wqt_ref[rope_rows:, :] = wq[:, nope_w:].T.astype(wqt_ref.dtype)


def kernel(x, ln_mix, w_in, q_a_norm, w_q_up, kv_a_norm, w_kv_up, attn_sinks, w_out, final_norm):
    batch, seq, d = x.shape
    depth = ln_mix.shape[0]
    assert depth == 1, "final norm is fused into the single layer's output kernel"
    assert seq % MLA_BLOCK == 0 and seq % PROJ_ROWS == 0 and seq % SWA_OUT_ROWS == 0
    x2 = x.reshape(batch * seq, d)
    qt, km, vt, gm, qst, ks4, vst, gs = _proj_call(
        x2, ln_mix[0].reshape(1, -1), w_in[0].T, q_a_norm[0].reshape(1, -1), w_q_up[0],
        kv_a_norm[0].reshape(1, -1), w_kv_up[0], seq)
    mm = _mla_call(qt, km, vt, gm, batch, seq)
    out = _swa_out_call(attn_sinks[0], qst, ks4, vst, gs, mm, x2, w_out[0],
                        final_norm.reshape(1, -1), batch, seq)
    return out.reshape(batch, seq, d)
```

```python
import functools
import math

import jax
import jax.numpy as jnp
from jax import lax
from jax.experimental import pallas as pl
from jax.experimental.pallas import tpu as pltpu

ROPE_THETA = 10000.0
NORM_EPS = 1e-6
NEG_INF = -1e30
LOG2E = 1.4426950408889634

MLA_HEADS = 4
MLA_NOPE = 128
MLA_ROPE = 64
MLA_V = 128
MLA_Q_RANK = 256
MLA_KV_RANK = 128
MLA_QK = MLA_NOPE + MLA_ROPE
MLA_WIDTH = MLA_HEADS * MLA_V

SWA_Q_HEADS = 8
SWA_KV_HEADS = 2
SWA_D = 64
SWA_WINDOW = 128
SWA_GROUP = SWA_Q_HEADS // SWA_KV_HEADS
SWA_WIDTH = SWA_Q_HEADS * SWA_D

LANES = 128
SUBLANES = 8
MXU_COLS = 2 * LANES
MLA_QK_PAD = MXU_COLS
assert MLA_KV_RANK + LANES == MLA_QK_PAD
MLA_K_COLS = MLA_QK_PAD
V7X_VMEM_LIMIT_BYTES = 56 * 1024 * 1024

PROJ_ROWS = 1024
MLA_BLOCK = 256
MLA_GROUP_TILES = 8
MLA_HEADS_PER_STEP = 2
SWA_OUT_ROWS = 1024
SWA_OUT_SUBTILE = 256
SWA_UNITS_PER_STEP = 4
SWA_OUT_TASKS_PER_STEP = 4
SWA_STORE_LAG = 2


def _inv_freq(dim):
    assert 2 * dim == LANES
    inv = 1.0 / (ROPE_THETA ** (jnp.arange(0, dim, 2, dtype=jnp.float32) / dim))
    return inv.reshape(dim // 2, 1)


def _rope_table_shapes(seq, tile, dim):
    f32 = jnp.float32
    return [pltpu.VMEM((seq // tile, dim // 2, tile), f32), pltpu.VMEM((seq // tile, dim // 2, tile), f32),
            pltpu.VMEM((seq, LANES), f32), pltpu.VMEM((seq, LANES), f32)]


def _fill_rope_tables(inv_ref, ct_ref, st_ref, cl_ref, sl_ref):
    n_tiles, half, tile = ct_ref.shape
    for j in range(n_tiles):
        pos = (lax.broadcasted_iota(jnp.int32, (half, tile), 1) + j * tile).astype(jnp.float32)
        ang = pos * inv_ref[...]
        c, s = jnp.cos(ang), jnp.sin(ang)
        ct_ref[j] = c
        st_ref[j] = s
        cl_ref[j * tile:(j + 1) * tile, :] = jnp.concatenate([c, c, c, c], axis=0).T
        sl_ref[j * tile:(j + 1) * tile, :] = jnp.concatenate([-s, -s, s, s], axis=0).T


def _rope(x, c, s):
    return x * c + pltpu.roll(x, LANES // 2, 1) * s


def _rms(x, g):
    return x * lax.rsqrt(jnp.mean(x * x, axis=-1, keepdims=True) + NORM_EPS) * g


def _silu(g):
    return g / (1.0 + jnp.exp(-g))


def _dot(a, b):
    return jnp.dot(a, b, preferred_element_type=jnp.float32)


def _dot_nt(a, b):
    return lax.dot_general(a, b, (((1,), (1,)), ((), ())),
                           preferred_element_type=jnp.float32)


PROJ_COLS = 1792


def _proj_kernel(x_ref, ln_ref, wint_ref, qg_ref, wq_f32_ref, kvg_ref, wkv_f32_ref,
                 inv_mla_ref, inv_swa_ref,
                 qt_ref, km_ref, vt_ref, gm_ref, qst_ref, ks_ref, vst_ref, gs_ref,
                 wt_ref, wqt_ref, wvt_ref, *table_refs):
    bf = jnp.bfloat16

    mla_tabs = table_refs[0:4]
    swa_tabs = table_refs[4:8] if len(table_refs) > 4 else mla_tabs

    @pl.when(pl.program_id(0) == 0)
    def _():
        _fill_rope_tables(inv_mla_ref, *mla_tabs)
        if swa_tabs is not mla_tabs:
            _fill_rope_tables(inv_swa_ref, *swa_tabs)
        _prep_latent_weights(wq_f32_ref, wkv_f32_ref, wqt_ref, wvt_ref)
        pieces, qs_pieces = _w_in_pieces()
        for d in range(PROJ_COLS // LANES):
            wt_ref[d * LANES:(d + 1) * LANES, :] = _gather_rows(wint_ref, pieces, d).astype(bf)
        for d in range(SWA_WIDTH // LANES):
            wt_ref[PROJ_COLS + d * LANES:PROJ_COLS + (d + 1) * LANES, :] = (
                _gather_rows(wint_ref, qs_pieces, d).astype(bf))

    x = x_ref[...]
    h = _rms(x, ln_ref[...]).astype(bf)

    tile = pl.program_id(0) % mla_tabs[0].shape[0]
    rows = pl.ds(pl.multiple_of(tile * x.shape[0], x.shape[0]), x.shape[0])
    cm, sm = mla_tabs[2][rows, :], mla_tabs[3][rows, :]
    cs, ss = swa_tabs[2][rows, :], swa_tabs[3][rows, :]
    lane = lax.broadcasted_iota(jnp.int32, (x.shape[0], LANES), 1)
    first = (lane % (LANES // 2)) < (LANES // 4)
    grp = MXU_COLS

    def project(a):
        return _dot_nt(h, wt_ref[a:a + grp, :])

    c_q = project(0)
    lat_b = project(256)
    cqn = _rms(c_q, qg_ref[...]).astype(bf)
    ckvn = _rms(lat_b[:, 0:MLA_KV_RANK], kvg_ref[...]).astype(bf)
    kr = _rope(lat_b[:, LANES:2 * LANES], cm, sm)
    kr_first = jnp.where(first, kr, 0.0).astype(bf)

    s_scale = LOG2E / math.sqrt(SWA_D)
    qst = _dot_nt(wt_ref[PROJ_COLS:PROJ_COLS + SWA_WIDTH, :], h)
    cst2 = jnp.concatenate([swa_tabs[0][tile], swa_tabs[0][tile]], axis=0)
    sst2 = jnp.concatenate([swa_tabs[1][tile], swa_tabs[1][tile]], axis=0)
    for ch in range(SWA_WIDTH // LANES):
        x1 = qst[ch * LANES:ch * LANES + SWA_D, :]
        x2 = qst[ch * LANES + SWA_D:(ch + 1) * LANES, :]
        qst_ref[ch * LANES:ch * LANES + SWA_D, :] = ((x1 * cst2 - x2 * sst2) * s_scale).astype(bf)
        qst_ref[ch * LANES + SWA_D:(ch + 1) * LANES, :] = ((x2 * cst2 + x1 * sst2) * s_scale).astype(bf)

    q_scale = LOG2E / math.sqrt(MLA_QK)
    lat_w = MLA_HEADS * MLA_KV_RANK
    half = MLA_ROPE // 2
    qt = _dot_nt(wqt_ref[...], cqn)
    ct, st = mla_tabs[0][tile], mla_tabs[1][tile]
    zeros = jnp.zeros((half, x.shape[0]), bf)
    for hd in range(MLA_HEADS):
        qt_ref[hd * grp:hd * grp + LANES, :] = (
            qt[hd * MLA_KV_RANK:(hd + 1) * MLA_KV_RANK, :] * q_scale).astype(bf)
        x1 = qt[lat_w + hd * MLA_ROPE:lat_w + hd * MLA_ROPE + half, :]
        x2 = qt[lat_w + hd * MLA_ROPE + half:lat_w + (hd + 1) * MLA_ROPE, :]
        base = hd * grp + LANES
        qt_ref[base:base + half, :] = ((x1 * ct - x2 * st) * q_scale).astype(bf)
        qt_ref[base + half:base + 2 * half, :] = zeros
        qt_ref[base + 2 * half:base + 3 * half, :] = ((x2 * ct + x1 * st) * q_scale).astype(bf)
        qt_ref[base + 3 * half:base + 4 * half, :] = zeros
    km_ref[:, 0:MLA_KV_RANK] = ckvn
    km_ref[:, MLA_KV_RANK:MLA_KV_RANK + LANES] = kr_first
    for g2 in range(MLA_WIDTH // grp):
        gm_ref[:, g2 * grp:(g2 + 1) * grp] = _silu(project(512 + g2 * grp)).astype(bf)

    kv_s = project(1024)
    k01 = _rope(kv_s[:, 0:LANES], cs, ss)
    ks_ref[:, 0 * LANES:1 * LANES] = jnp.where(first, k01, 0.0).astype(bf)
    ks_ref[:, 1 * LANES:2 * LANES] = jnp.where(first, 0.0, pltpu.roll(k01, LANES // 4, 1)).astype(bf)
    ks_ref[:, 2 * LANES:3 * LANES] = jnp.where(first, pltpu.roll(k01, 3 * LANES // 4, 1), 0.0).astype(bf)
    ks_ref[:, 3 * LANES:4 * LANES] = jnp.where(first, 0.0, k01).astype(bf)
    vst_ref[...] = kv_s[:, LANES:2 * LANES].T.astype(bf)

    for g2 in range(SWA_WIDTH // grp):
        gs_ref[:, g2 * grp:(g2 + 1) * grp] = _silu(project(1280 + g2 * grp)).astype(bf)

    vt_ref[...] = _dot_nt(wvt_ref[...], ckvn).astype(bf)


def _proj_call(x2, ln_g, w_in_t, q_g, w_q_up, kv_g, w_kv_up, seq):
    n, d = x2.shape
    tm = PROJ_ROWS
    steps_per_seq = seq // tm
    batch = n // seq
    row = lambda i: (i, 0)
    const = lambda i: (0, 0)
    col = lambda i: (i // steps_per_seq, i % steps_per_seq)
    table_scratch = _rope_table_shapes(seq, tm, MLA_ROPE)
    if SWA_D != MLA_ROPE:
        table_scratch += _rope_table_shapes(seq, tm, SWA_D)
    bf = jnp.bfloat16
    qt_rows = MLA_HEADS * MLA_QK_PAD
    row_out = lambda w: (pl.BlockSpec((tm, w), row), jax.ShapeDtypeStruct((n, w), bf))
    col_out = lambda r: (pl.BlockSpec((r, tm), col), jax.ShapeDtypeStruct((batch * r, seq), bf))
    outs = [col_out(qt_rows), row_out(MLA_K_COLS), col_out(MLA_WIDTH), row_out(MLA_WIDTH),
            col_out(SWA_WIDTH), row_out(SWA_WIDTH), col_out(SWA_KV_HEADS * SWA_D), row_out(SWA_WIDTH)]
    return pl.pallas_call(
        _proj_kernel,
        grid=(n // tm,),
        in_specs=[
            pl.BlockSpec((tm, d), row),
            pl.BlockSpec((1, d), const),
            pl.BlockSpec(w_in_t.shape, const),
            pl.BlockSpec((1, MLA_Q_RANK), const),
            pl.BlockSpec(w_q_up.shape, const),
            pl.BlockSpec((1, MLA_KV_RANK), const),
            pl.BlockSpec(w_kv_up.shape, const),
            pl.BlockSpec((MLA_ROPE // 2, 1), const),
            pl.BlockSpec((SWA_D // 2, 1), const),
        ],
        out_specs=[o[0] for o in outs],
        out_shape=[o[1] for o in outs],
        scratch_shapes=[pltpu.VMEM((PROJ_COLS + SWA_WIDTH, d), bf),
                        pltpu.VMEM((MLA_HEADS * (MLA_KV_RANK + MLA_ROPE), MLA_Q_RANK), bf),
                        pltpu.VMEM((MLA_WIDTH, MLA_KV_RANK), bf)] + table_scratch,
        compiler_params=pltpu.CompilerParams(
            dimension_semantics=("arbitrary",),
            vmem_limit_bytes=V7X_VMEM_LIMIT_BYTES),
        name="proj",
    )(x2, ln_g, w_in_t, q_g, w_q_up, kv_g, w_kv_up, _inv_freq(MLA_ROPE), _inv_freq(SWA_D))


def _tree(op, xs):
    xs = list(xs)
    while len(xs) > 1:
        xs = [op(xs[a], xs[a + 1]) if a + 1 < len(xs) else xs[a] for a in range(0, len(xs), 2)]
    return xs[0]

def _mla_block_groups(nblk):
    groups, cur, load = [], [], 0
    for i in range(nblk - 1, -1, -1):
        if load + i + 1 > MLA_GROUP_TILES:
            groups.append(cur)
            cur, load = [], 0
        cur.append(i)
        load += i + 1
    groups.append(cur)
    return groups


def _mla_kernel(qt_ref, k_ref, vt_ref, g_ref, o_ref, s_ref, p_ref, vx_ref):
    t = MLA_BLOCK
    seq = k_ref.shape[0]
    sub = SUBLANES
    qk = MLA_QK_PAD
    r = lax.broadcasted_iota(jnp.int32, (t, t), 0)
    c = lax.broadcasted_iota(jnp.int32, (t, t), 1)
    causal = r <= c
    nblk = seq // t
    units = []
    for hd in range(MLA_HEADS_PER_STEP):
        for blocks in _mla_block_groups(nblk):
            members, off = [], 0
            for i in blocks:
                members.append((hd, i, off))
                off += (i + 1) * t
            assert off <= s_ref.shape[1]
            units.append(members)
    state = [[dict(m8=None) for _ in members] for members in units]

    for hd in range(MLA_HEADS_PER_STEP):
        vx_ref[hd, 0:MLA_V, :] = vt_ref[hd * MLA_V:(hd + 1) * MLA_V, :]
        vx_ref[hd, MLA_V:, :] = jnp.ones((vx_ref.shape[1] - MLA_V, seq), vx_ref.dtype)

    def score_tile(u, b, j):
        hd, i, off = units[u][b]
        st = state[u][b]
        keys = slice(j * t, (j + 1) * t)
        s = _dot(k_ref[keys, :], qt_ref[hd * qk:(hd + 1) * qk, i * t:(i + 1) * t])
        if j == i:
            s = jnp.where(causal, s, NEG_INF)
        s_ref[u % 2, off + j * t:off + (j + 1) * t, :] = s
        m8 = _tree(jnp.maximum, [s[a * sub:(a + 1) * sub, :] for a in range(t // sub)])
        st["m8"] = m8 if st["m8"] is None else jnp.maximum(st["m8"], m8)

    def prob_tile(u, b, j):
        _, _, off = units[u][b]
        st = state[u][b]
        if "m" not in st:
            st["m"] = jnp.max(st["m8"], axis=0, keepdims=True)
        rows = slice(off + j * t, off + (j + 1) * t)
        p_ref[u % 3, rows, :] = jnp.exp2(s_ref[u % 2, rows, :] - st["m"]).astype(jnp.bfloat16)

    def value_matmul(u):
        for b, (hd, i, off) in enumerate(units[u]):
            kv = (i + 1) * t
            state[u][b]["acc"] = _dot(vx_ref[hd, :, 0:kv], p_ref[u % 3, off:off + kv, :])

    def store(u):
        for b, (hd, i, _) in enumerate(units[u]):
            rows = slice(i * t, (i + 1) * t)
            acc = state[u][b]["acc"]
            out_t = (acc[0:MLA_V, :] / acc[MLA_V:MLA_V + 1, :]).astype(o_ref.dtype)
            gate = g_ref[rows, hd * MLA_V:(hd + 1) * MLA_V]
            o_ref[rows, hd * MLA_V:(hd + 1) * MLA_V] = out_t.T * gate
        state[u] = None

    def tiles(fn, u):
        return [functools.partial(fn, u, b, j) for b, (_, i, _) in enumerate(units[u]) for j in range(i + 1)]

    nu = len(units)
    for task in tiles(score_tile, 0):
        task()
    for u in range(nu + 2):
        if 0 <= u - 2 < nu:
            store(u - 2)
        nxt = tiles(score_tile, u + 1) if u + 1 < nu else []
        cur = tiles(prob_tile, u) if u < nu else []
        while nxt or cur:
            if cur:
                cur.pop(0)()
            if nxt:
                nxt.pop(0)()
        if 0 <= u - 1 < nu:
            value_matmul(u - 1)


def _mla_call(qt, km, vt, gm, batch, seq):
    n = km.shape[0]
    t = MLA_BLOCK
    hps = MLA_HEADS_PER_STEP
    groups = MLA_HEADS // hps
    ones_rows = 16
    feat = lambda b, h: (b * groups + h, 0)
    tok = lambda b, h: (b, h)
    return pl.pallas_call(
        _mla_kernel,
        grid=(batch, groups),
        in_specs=[
            pl.BlockSpec((hps * MLA_QK_PAD, seq), feat),
            pl.BlockSpec((seq, MLA_K_COLS), lambda b, h: (b, 0)),
            pl.BlockSpec((hps * MLA_V, seq), feat),
            pl.BlockSpec((seq, hps * MLA_V), tok),
        ],
        out_specs=pl.BlockSpec((seq, hps * MLA_V), tok),
        out_shape=jax.ShapeDtypeStruct((n, MLA_WIDTH), jnp.bfloat16),
        scratch_shapes=[pltpu.VMEM((2, MLA_GROUP_TILES * t, t), jnp.float32),
                        pltpu.VMEM((3, MLA_GROUP_TILES * t, t), jnp.bfloat16),
                        pltpu.VMEM((hps, MLA_V + ones_rows, seq), jnp.bfloat16)],
        compiler_params=pltpu.CompilerParams(
            dimension_semantics=("arbitrary", "arbitrary"),
            vmem_limit_bytes=V7X_VMEM_LIMIT_BYTES),
        name="mla",
    )(qt, km, vt, gm)


def _swa_out_kernel(sink_ref, qt_ref, k_ref, kh_ref, vt_ref, vth_ref, g_ref, mm_ref, x_ref,
                    wf_ref, fg_ref, o_ref, ms_ref, kx_ref, vx_ref, w_ref):
    w = SWA_WINDOW
    chunk = pl.program_id(1)
    rows_total = k_ref.shape[0]
    bf = jnp.bfloat16
    sub = SUBLANES

    @pl.when((pl.program_id(0) == 0) & (chunk == 0))
    def _():
        w_ref[...] = wf_ref[...].astype(bf)

    kx_ref[0:w, :] = kh_ref[...]
    kx_ref[w:, :] = k_ref[...]
    for hk in range(SWA_KV_HEADS):
        vx_ref[hk, 0:SWA_D, 0:w] = vth_ref[hk * SWA_D:(hk + 1) * SWA_D, :]
        vx_ref[hk, 0:SWA_D, w:] = vt_ref[hk * SWA_D:(hk + 1) * SWA_D, :]
        vx_ref[hk, SWA_D:, :] = jnp.ones((vx_ref.shape[1] - SWA_D, vx_ref.shape[2]), bf)

    ki = lax.broadcasted_iota(jnp.int32, (2 * w, 2 * w), 0)
    qi = lax.broadcasted_iota(jnp.int32, (2 * w, 2 * w), 1) % w
    rel = qi + w - ki
    band = (rel >= 0) & (rel < SWA_WINDOW)
    band_first = band & ((ki >= w) | (chunk > 0))
    lane = lax.broadcasted_iota(jnp.int32, (1, 2 * w), 1)

    units = [(n, hk) for n in range(rows_total // w) for hk in range(SWA_KV_HEADS)]
    state = [dict() for _ in units]

    def sink_rows(hk):
        sink = [sink_ref[hk * SWA_GROUP + i] * LOG2E for i in range(SWA_GROUP)]
        return (jnp.where(lane < w, sink[0], sink[2]), jnp.where(lane < w, sink[1], sink[3]))

    def stage_scores(u, half):
        n, hk = units[u]
        base = hk * 2 * LANES
        qcols = slice(n * w, (n + 1) * w)
        keys = slice(n * w, (n + 2) * w)
        qt = jnp.concatenate([qt_ref[base:base + LANES, qcols],
                              qt_ref[base + LANES:base + 2 * LANES, qcols]], axis=1)
        valid = band_first if n == 0 else band
        st = state[u]
        lanes = slice(base + half * LANES, base + (half + 1) * LANES)
        s = jnp.where(valid, _dot(kx_ref[keys, lanes], qt), NEG_INF)
        m8 = _tree(jnp.maximum, [s[a * sub:(a + 1) * sub, :] for a in range(2 * w // sub)])
        st.setdefault("s", {})[half] = s
        st.setdefault("m", {})[half] = jnp.maximum(jnp.max(m8, axis=0, keepdims=True),
                                                   sink_rows(hk)[half])

    def stage_values(u, half):
        n, hk = units[u]
        keys = slice(n * w, (n + 2) * w)
        st = state[u]
        p = jnp.exp2(st["s"].pop(half) - st["m"][half]).astype(bf)
        st.setdefault("acc", {})[half] = _dot(vx_ref[hk, :, keys], p)

    def stage_store(u):
        n, hk = units[u]
        base = hk * 2 * LANES
        qcols = slice(n * w, (n + 1) * w)
        sinks = sink_rows(hk)
        st = state[u]
        outs = []
        for half in range(2):
            acc = st["acc"][half]
            den = acc[SWA_D:SWA_D + 1, :] + jnp.exp2(sinks[half] - st["m"][half])
            outs.append(acc[0:SWA_D, :] / den)
        g = g_ref[qcols, base:base + 2 * LANES].astype(jnp.float32)
        for ch in range(2):
            o_t = jnp.concatenate([outs[0][:, ch * w:(ch + 1) * w],
                                   outs[1][:, ch * w:(ch + 1) * w]], axis=0)
            ms_ref[qcols, base + ch * LANES:base + (ch + 1) * LANES] = (
                o_t.T * g[:, ch * LANES:(ch + 1) * LANES]).astype(bf)
        state[u] = None

    t = SWA_OUT_SUBTILE
    units_per_tile = (t // w) * SWA_KV_HEADS

    d_model = x_ref.shape[1]
    grp = MXU_COLS
    out_state = {}

    def out_group(i, gc):
        rows = slice(i * t, (i + 1) * t)
        cols = slice(gc * grp, (gc + 1) * grp)
        y = (x_ref[rows, cols] + _dot(mm_ref[rows, :], w_ref[0:MLA_WIDTH, cols])
             + _dot(ms_ref[rows, :], w_ref[MLA_WIDTH:, cols]))
        out_state.setdefault(i, []).append(y)

    def out_norm(i):
        rows = slice(i * t, (i + 1) * t)
        ys = out_state.pop(i)
        ssq = _tree(jnp.add, [jnp.sum(y * y, axis=-1, keepdims=True) for y in ys])
        scale = lax.rsqrt(ssq / d_model + NORM_EPS)
        for gc, y in enumerate(ys):
            cols = slice(gc * grp, (gc + 1) * grp)
            o_ref[rows, cols] = y * scale * fg_ref[:, cols]

    nu = len(units)
    lag = SWA_STORE_LAG
    gsz = SWA_UNITS_PER_STEP
    pending = []
    tasks_per_step = SWA_OUT_TASKS_PER_STEP

    def retire(count):
        for _ in range(min(count, len(pending))):
            pending.pop(0)()

    for step in range(nu // gsz + lag):
        for u in range(step * gsz, (step + 1) * gsz):
            if u < nu:
                stage_scores(u, 0)
                stage_scores(u, 1)
        retire(tasks_per_step // 2)
        for u in range((step - 1) * gsz, step * gsz):
            if 0 <= u < nu:
                stage_values(u, 0)
                stage_values(u, 1)
        retire(tasks_per_step - tasks_per_step // 2)
        for u in range((step - lag) * gsz, (step - lag + 1) * gsz):
            if 0 <= u < nu:
                stage_store(u)
                if (u + 1) % units_per_tile == 0:
                    i = u // units_per_tile
                    pending += [functools.partial(out_group, i, gc) for gc in range(d_model // grp)]
                    pending.append(functools.partial(out_norm, i))
    for task in pending:
        task()


def _swa_out_call(sinks, qst, ks4, vst, gs, mm, x2, w_o, fg, batch, seq):
    n, d = x2.shape
    r = SWA_OUT_ROWS
    w = SWA_WINDOW
    cps = seq // r
    bpc = r // w
    bps = seq // w
    v_rows = SWA_KV_HEADS * SWA_D
    ones_rows = 16
    row = lambda b, c: (b * cps + c, 0)
    halo = lambda b, c: (b * bps + jnp.maximum(c * bpc - 1, 0), 0)
    feat = lambda b, c: (b, c)
    feat_halo = lambda b, c: (b, jnp.maximum(c * bpc - 1, 0))
    const = lambda b, c: (0, 0)
    wide = pl.BlockSpec((r, SWA_WIDTH), row)
    return pl.pallas_call(
        _swa_out_kernel,
        grid=(batch, cps),
        in_specs=[
            pl.BlockSpec(memory_space=pltpu.SMEM),
            pl.BlockSpec((SWA_WIDTH, r), feat),
            wide,
            pl.BlockSpec((w, SWA_WIDTH), halo),
            pl.BlockSpec((v_rows, r), feat),
            pl.BlockSpec((v_rows, w), feat_halo),
            wide,
            pl.BlockSpec((r, MLA_WIDTH), row),
            pl.BlockSpec((r, d), row),
            pl.BlockSpec(w_o.shape, const),
            pl.BlockSpec((1, d), const),
        ],
        out_specs=pl.BlockSpec((r, d), row),
        out_shape=jax.ShapeDtypeStruct((n, d), jnp.float32),
        scratch_shapes=[pltpu.VMEM((r, SWA_WIDTH), jnp.bfloat16),
                        pltpu.VMEM((r + w, SWA_WIDTH), jnp.bfloat16),
                        pltpu.VMEM((SWA_KV_HEADS, SWA_D + ones_rows, r + w), jnp.bfloat16),
                        pltpu.VMEM(w_o.shape, jnp.bfloat16)],
        compiler_params=pltpu.CompilerParams(
            dimension_semantics=("arbitrary", "arbitrary"),
            vmem_limit_bytes=V7X_VMEM_LIMIT_BYTES),
        name="swa_out",
    )(sinks, qst, ks4, ks4, vst, vst, gs, mm, x2, w_o, fg)


def _pair_rope_pieces(src, dst, n_heads, dim):
    half = dim // 2
    out = []
    for pair in range(n_heads // 2):
        for which in range(2):
            head = src + (2 * pair + which) * dim
            chunk = dst + pair * 2 * dim
            out.append((head, chunk + which * half, half))
            out.append((head + half, chunk + 2 * half + which * half, half))
    return out


def _w_in_pieces():
    pieces, qs_pieces, src, dst = [], [], 0, 0
    for name, width in (("c_q", MLA_Q_RANK), ("c_kv", MLA_KV_RANK), ("k_rope", MLA_ROPE),
                        ("g_mla", MLA_WIDTH), ("q_s", SWA_WIDTH),
                        ("k_s", SWA_KV_HEADS * SWA_D), ("v_s", SWA_KV_HEADS * SWA_D),
                        ("g_swa", SWA_WIDTH)):
        if name == "k_rope":
            half = MLA_ROPE // 2
            pieces += [(src, dst, half), (src, dst + half, half),
                       (src + half, dst + 2 * half, half), (src + half, dst + 3 * half, half)]
            dst += 2 * MLA_ROPE
        elif name == "q_s":
            qs_pieces += _pair_rope_pieces(src, 0, SWA_Q_HEADS, SWA_D)
        elif name == "k_s":
            pieces += _pair_rope_pieces(src, dst, SWA_KV_HEADS, SWA_D)
            dst += width
        else:
            pieces.append((src, dst, width))
            dst += width
        src += width
    assert dst == PROJ_COLS
    return pieces, qs_pieces


def _gather_cols(src_ref, pieces, n_dst_cols):
    rows, n_src = src_ref.shape
    lane = lax.broadcasted_iota(jnp.int32, (rows, LANES), 1)
    loaded, rolled = {}, {}

    def load(k):
        if k not in loaded:
            width = min(LANES, n_src - k * LANES)
            v = src_ref[:, k * LANES:k * LANES + width]
            if width < LANES:
                v = jnp.concatenate([v, jnp.zeros((rows, LANES - width), v.dtype)], axis=1)
            loaded[k] = v
        return loaded[k]

    def shifted(k, shift):
        if shift == 0:
            return load(k)
        if (k, shift) not in rolled:
            rolled[(k, shift)] = pltpu.roll(load(k), shift, 1)
        return rolled[(k, shift)]

    chunks = []
    for d in range(n_dst_cols // LANES):
        acc = None
        for s0, d0, w in pieces:
            lo, hi = max(d0, d * LANES), min(d0 + w, (d + 1) * LANES)
            while lo < hi:
                s_abs = s0 + lo - d0
                k, a = divmod(s_abs, LANES)
                n = min(hi - lo, LANES - a)
                a_dst = lo - d * LANES
                val = shifted(k, (a_dst - a) % LANES)
                if n == LANES:
                    acc = val
                else:
                    mask = (lane >= a_dst) & (lane < a_dst + n)
                    acc = jnp.where(mask, val, 0.0 if acc is None else acc)
                lo += n
        chunks.append(acc)
    return chunks


def _gather_rows(src_ref, pieces, d):
    parts = []
    for s0, d0, w in pieces:
        lo, hi = max(d0, d * LANES), min(d0 + w, (d + 1) * LANES)
        if lo < hi:
            parts.append((lo, src_ref[s0 + lo - d0:s0 + hi - d0, :]))
    parts.sort(key=lambda t: t[0])
    assert sum(p.shape[0] for _, p in parts) == LANES
    return jnp.concatenate([p for _, p in parts], axis=0) if len(parts) > 1 else parts[0][1]


def _prep_latent_weights(wq_ref, wkv_ref, wqt_ref, wvt_ref):
    q_pieces = ([(hd * MLA_QK, hd * MLA_NOPE, MLA_NOPE) for hd in range(MLA_HEADS)]
                + [(hd * MLA_QK + MLA_NOPE, MLA_HEADS * MLA_NOPE + hd * MLA_ROPE, MLA_ROPE)
                   for hd in range(MLA_HEADS)])
    wq = jnp.concatenate(_gather_cols(wq_ref, q_pieces, MLA_HEADS * MLA_QK), axis=1)
    nope_w = MLA_HEADS * MLA_NOPE
    per_head = MLA_NOPE + MLA_V
    for hd in range(MLA_HEADS):
        wk = wkv_ref[:, hd * per_head:hd * per_head + MLA_NOPE]
        absorbed_t = lax.dot_general(wk, wq[:, hd * MLA_NOPE:(hd + 1) * MLA_NOPE],
                                     (((1,), (1,)), ((), ())), precision=lax.Precision.HIGHEST,
                                     preferred_element_type=jnp.float32)
        wqt_ref[hd * MLA_KV_RANK:(hd + 1) * MLA_KV_RANK, :] = absorbed_t.astype(wqt_ref.dtype)
        wvt_ref[hd * MLA_V:(hd + 1) * MLA_V, :] = (
            wkv_ref[:, hd * per_head + MLA_NOPE:(hd + 1) * per_head].T.astype(wvt_ref.dtype))
    rope_rows = MLA_HEADS * MLA_KV_RANK
    wqt_ref[rope_rows:, :] = wq[:, nope_w:].T.astype(wqt_ref.dtype)


def kernel(x, ln_mix, w_in, q_a_norm, w_q_up, kv_a_norm, w_kv_up, attn_sinks, w_out, final_norm):
    batch, seq, d = x.shape
    depth = ln_mix.shape[0]
    assert depth == 1, "final norm is fused into the single layer's output kernel"
    assert seq % MLA_BLOCK == 0 and seq % PROJ_ROWS == 0 and seq % SWA_OUT_ROWS == 0
    x2 = x.reshape(batch * seq, d)
    qt, km, vt, gm, qst, ks4, vst, gs = _proj_call(
        x2, ln_mix[0].reshape(1, -1), w_in[0].T, q_a_norm[0].reshape(1, -1), w_q_up[0],
        kv_a_norm[0].reshape(1, -1), w_kv_up[0], seq)
    mm = _mla_call(qt, km, vt, gm, batch, seq)
    out = _swa_out_call(attn_sinks[0], qst, ks4, vst, gs, mm, x2, w_out[0],
                        final_norm.reshape(1, -1), batch, seq)
    return out.reshape(batch, seq, d)
```

```python
import functools
import math

import jax
import jax.numpy as jnp
from jax import lax
from jax.experimental import pallas as pl
from jax.experimental.pallas import tpu as pltpu

ROPE_THETA = 10000.0
NORM_EPS = 1e-6
NEG_INF = -1e30
LOG2E = 1.4426950408889634

MLA_HEADS = 4
MLA_NOPE = 128
MLA_ROPE = 64
MLA_V = 128
MLA_Q_RANK = 256
MLA_KV_RANK = 128
MLA_QK = MLA_NOPE + MLA_ROPE
MLA_WIDTH = MLA_HEADS * MLA_V

SWA_Q_HEADS = 8
SWA_KV_HEADS = 2
SWA_D = 64
SWA_WINDOW = 128
SWA_GROUP = SWA_Q_HEADS // SWA_KV_HEADS
SWA_WIDTH = SWA_Q_HEADS * SWA_D

LANES = 128
SUBLANES = 8
MXU_COLS = 2 * LANES
MLA_QK_PAD = MXU_COLS
assert MLA_KV_RANK + LANES == MLA_QK_PAD
MLA_K_COLS = MLA_QK_PAD
V7X_VMEM_LIMIT_BYTES = 56 * 1024 * 1024

PROJ_ROWS = 1024
MLA_BLOCK = 256
MLA_GROUP_TILES = 8
MLA_HEADS_PER_STEP = 4
SWA_OUT_ROWS = 1024
SWA_OUT_SUBTILE = 256
SWA_UNITS_PER_STEP = 4
SWA_OUT_TASKS_PER_STEP = 4
SWA_STORE_LAG = 2


def _inv_freq(dim):
    assert 2 * dim == LANES
    inv = 1.0 / (ROPE_THETA ** (jnp.arange(0, dim, 2, dtype=jnp.float32) / dim))
    return inv.reshape(dim // 2, 1)


def _rope_table_shapes(seq, tile, dim):
    f32 = jnp.float32
    return [pltpu.VMEM((seq // tile, dim // 2, tile), f32), pltpu.VMEM((seq // tile, dim // 2, tile), f32),
            pltpu.VMEM((seq, LANES), f32), pltpu.VMEM((seq, LANES), f32)]


def _fill_rope_tables(inv_ref, ct_ref, st_ref, cl_ref, sl_ref):
    n_tiles, half, tile = ct_ref.shape
    for j in range(n_tiles):
        pos = (lax.broadcasted_iota(jnp.int32, (half, tile), 1) + j * tile).astype(jnp.float32)
        ang = pos * inv_ref[...]
        c, s = jnp.cos(ang), jnp.sin(ang)
        ct_ref[j] = c
        st_ref[j] = s
        cl_ref[j * tile:(j + 1) * tile, :] = jnp.concatenate([c, c, c, c], axis=0).T
        sl_ref[j * tile:(j + 1) * tile, :] = jnp.concatenate([-s, -s, s, s], axis=0).T


def _rope(x, c, s):
    return x * c + pltpu.roll(x, LANES // 2, 1) * s


def _rms(x, g):
    return x * lax.rsqrt(jnp.mean(x * x, axis=-1, keepdims=True) + NORM_EPS) * g


def _silu(g):
    return g / (1.0 + jnp.exp(-g))


def _dot(a, b):
    return jnp.dot(a, b, preferred_element_type=jnp.float32)


def _dot_nt(a, b):
    return lax.dot_general(a, b, (((1,), (1,)), ((), ())),
                           preferred_element_type=jnp.float32)


PROJ_COLS = 1792


def _proj_kernel(x_ref, ln_ref, wint_ref, qg_ref, wq_f32_ref, kvg_ref, wkv_f32_ref,
                 inv_mla_ref, inv_swa_ref,
                 qt_ref, km_ref, vt_ref, gm_ref, qst_ref, ks_ref, vst_ref, gs_ref,
                 wt_ref, wqt_ref, wvt_ref, *table_refs):
    bf = jnp.bfloat16

    mla_tabs = table_refs[0:4]
    swa_tabs = table_refs[4:8] if len(table_refs) > 4 else mla_tabs

    @pl.when(pl.program_id(0) == 0)
    def _():
        _fill_rope_tables(inv_mla_ref, *mla_tabs)
        if swa_tabs is not mla_tabs:
            _fill_rope_tables(inv_swa_ref, *swa_tabs)
        _prep_latent_weights(wq_f32_ref, wkv_f32_ref, wqt_ref, wvt_ref)
        pieces, qs_pieces = _w_in_pieces()
        for d in range(PROJ_COLS // LANES):
            wt_ref[d * LANES:(d + 1) * LANES, :] = _gather_rows(wint_ref, pieces, d).astype(bf)
        for d in range(SWA_WIDTH // LANES):
            wt_ref[PROJ_COLS + d * LANES:PROJ_COLS + (d + 1) * LANES, :] = (
                _gather_rows(wint_ref, qs_pieces, d).astype(bf))

    x = x_ref[...]
    h = _rms(x, ln_ref[...]).astype(bf)

    tile = pl.program_id(0) % mla_tabs[0].shape[0]
    rows = pl.ds(pl.multiple_of(tile * x.shape[0], x.shape[0]), x.shape[0])
    cm, sm = mla_tabs[2][rows, :], mla_tabs[3][rows, :]
    cs, ss = swa_tabs[2][rows, :], swa_tabs[3][rows, :]
    lane = lax.broadcasted_iota(jnp.int32, (x.shape[0], LANES), 1)
    first = (lane % (LANES // 2)) < (LANES // 4)
    grp = MXU_COLS

    def project(a):
        return _dot_nt(h, wt_ref[a:a + grp, :])

    c_q = project(0)
    lat_b = project(256)
    cqn = _rms(c_q, qg_ref[...]).astype(bf)
    ckvn = _rms(lat_b[:, 0:MLA_KV_RANK], kvg_ref[...]).astype(bf)
    kr = _rope(lat_b[:, LANES:2 * LANES], cm, sm)
    kr_first = jnp.where(first, kr, 0.0).astype(bf)

    s_scale = LOG2E / math.sqrt(SWA_D)
    qst = _dot_nt(wt_ref[PROJ_COLS:PROJ_COLS + SWA_WIDTH, :], h)
    cst2 = jnp.concatenate([swa_tabs[0][tile], swa_tabs[0][tile]], axis=0)
    sst2 = jnp.concatenate([swa_tabs[1][tile], swa_tabs[1][tile]], axis=0)
    for ch in range(SWA_WIDTH // LANES):
        x1 = qst[ch * LANES:ch * LANES + SWA_D, :]
        x2 = qst[ch * LANES + SWA_D:(ch + 1) * LANES, :]
        qst_ref[ch * LANES:ch * LANES + SWA_D, :] = ((x1 * cst2 - x2 * sst2) * s_scale).astype(bf)
        qst_ref[ch * LANES + SWA_D:(ch + 1) * LANES, :] = ((x2 * cst2 + x1 * sst2) * s_scale).astype(bf)

    q_scale = LOG2E / math.sqrt(MLA_QK)
    lat_w = MLA_HEADS * MLA_KV_RANK
    half = MLA_ROPE // 2
    qt = _dot_nt(wqt_ref[...], cqn)
    ct, st = mla_tabs[0][tile], mla_tabs[1][tile]
    zeros = jnp.zeros((half, x.shape[0]), bf)
    for hd in range(MLA_HEADS):
        qt_ref[hd * grp:hd * grp + LANES, :] = (
            qt[hd * MLA_KV_RANK:(hd + 1) * MLA_KV_RANK, :] * q_scale).astype(bf)
        x1 = qt[lat_w + hd * MLA_ROPE:lat_w + hd * MLA_ROPE + half, :]
        x2 = qt[lat_w + hd * MLA_ROPE + half:lat_w + (hd + 1) * MLA_ROPE, :]
        base = hd * grp + LANES
        qt_ref[base:base + half, :] = ((x1 * ct - x2 * st) * q_scale).astype(bf)
        qt_ref[base + half:base + 2 * half, :] = zeros
        qt_ref[base + 2 * half:base + 3 * half, :] = ((x2 * ct + x1 * st) * q_scale).astype(bf)
        qt_ref[base + 3 * half:base + 4 * half, :] = zeros
    km_ref[:, 0:MLA_KV_RANK] = ckvn
    km_ref[:, MLA_KV_RANK:MLA_KV_RANK + LANES] = kr_first
    for g2 in range(MLA_WIDTH // grp):
        gm_ref[:, g2 * grp:(g2 + 1) * grp] = _silu(project(512 + g2 * grp)).astype(bf)

    kv_s = project(1024)
    k01 = _rope(kv_s[:, 0:LANES], cs, ss)
    ks_ref[:, 0 * LANES:1 * LANES] = jnp.where(first, k01, 0.0).astype(bf)
    ks_ref[:, 1 * LANES:2 * LANES] = jnp.where(first, 0.0, pltpu.roll(k01, LANES // 4, 1)).astype(bf)
    ks_ref[:, 2 * LANES:3 * LANES] = jnp.where(first, pltpu.roll(k01, 3 * LANES // 4, 1), 0.0).astype(bf)
    ks_ref[:, 3 * LANES:4 * LANES] = jnp.where(first, 0.0, k01).astype(bf)
    vst_ref[...] = kv_s[:, LANES:2 * LANES].T.astype(bf)

    for g2 in range(SWA_WIDTH // grp):
        gs_ref[:, g2 * grp:(g2 + 1) * grp] = _silu(project(1280 + g2 * grp)).astype(bf)

    vt_ref[...] = _dot_nt(wvt_ref[...], ckvn).astype(bf)


def _proj_call(x2, ln_g, w_in_t, q_g, w_q_up, kv_g, w_kv_up, seq):
    n, d = x2.shape
    tm = PROJ_ROWS
    steps_per_seq = seq // tm
    batch = n // seq
    row = lambda i: (i, 0)
    const = lambda i: (0, 0)
    col = lambda i: (i // steps_per_seq, i % steps_per_seq)
    table_scratch = _rope_table_shapes(seq, tm, MLA_ROPE)
    if SWA_D != MLA_ROPE:
        table_scratch += _rope_table_shapes(seq, tm, SWA_D)
    bf = jnp.bfloat16
    qt_rows = MLA_HEADS * MLA_QK_PAD
    row_out = lambda w: (pl.BlockSpec((tm, w), row), jax.ShapeDtypeStruct((n, w), bf))
    col_out = lambda r: (pl.BlockSpec((r, tm), col), jax.ShapeDtypeStruct((batch * r, seq), bf))
    outs = [col_out(qt_rows), row_out(MLA_K_COLS), col_out(MLA_WIDTH), row_out(MLA_WIDTH),
            col_out(SWA_WIDTH), row_out(SWA_WIDTH), col_out(SWA_KV_HEADS * SWA_D), row_out(SWA_WIDTH)]
    return pl.pallas_call(
        _proj_kernel,
        grid=(n // tm,),
        in_specs=[
            pl.BlockSpec((tm, d), row),
            pl.BlockSpec((1, d), const),
            pl.BlockSpec(w_in_t.shape, const),
            pl.BlockSpec((1, MLA_Q_RANK), const),
            pl.BlockSpec(w_q_up.shape, const),
            pl.BlockSpec((1, MLA_KV_RANK), const),
            pl.BlockSpec(w_kv_up.shape, const),
            pl.BlockSpec((MLA_ROPE // 2, 1), const),
            pl.BlockSpec((SWA_D // 2, 1), const),
        ],
        out_specs=[o[0] for o in outs],
        out_shape=[o[1] for o in outs],
        scratch_shapes=[pltpu.VMEM((PROJ_COLS + SWA_WIDTH, d), bf),
                        pltpu.VMEM((MLA_HEADS * (MLA_KV_RANK + MLA_ROPE), MLA_Q_RANK), bf),
                        pltpu.VMEM((MLA_WIDTH, MLA_KV_RANK), bf)] + table_scratch,
        compiler_params=pltpu.CompilerParams(
            dimension_semantics=("arbitrary",),
            vmem_limit_bytes=V7X_VMEM_LIMIT_BYTES),
        name="proj",
    )(x2, ln_g, w_in_t, q_g, w_q_up, kv_g, w_kv_up, _inv_freq(MLA_ROPE), _inv_freq(SWA_D))


def _tree(op, xs):
    xs = list(xs)
    while len(xs) > 1:
        xs = [op(xs[a], xs[a + 1]) if a + 1 < len(xs) else xs[a] for a in range(0, len(xs), 2)]
    return xs[0]

def _mla_block_groups(nblk):
    groups, cur, load = [], [], 0
    for i in range(nblk - 1, -1, -1):
        if load + i + 1 > MLA_GROUP_TILES:
            groups.append(cur)
            cur, load = [], 0
        cur.append(i)
        load += i + 1
    groups.append(cur)
    return groups


def _mla_kernel(qt_ref, k_ref, vt_ref, g_ref, o_ref, s_ref, p_ref, vx_ref):
    t = MLA_BLOCK
    seq = k_ref.shape[0]
    sub = SUBLANES
    qk = MLA_QK_PAD
    r = lax.broadcasted_iota(jnp.int32, (t, t), 0)
    c = lax.broadcasted_iota(jnp.int32, (t, t), 1)
    causal = r <= c
    nblk = seq // t
    units = []
    for hd in range(MLA_HEADS_PER_STEP):
        for blocks in _mla_block_groups(nblk):
            members, off = [], 0
            for i in blocks:
                members.append((hd, i, off))
                off += (i + 1) * t
            assert off <= s_ref.shape[1]
            units.append(members)
    state = [[dict(m8=None) for _ in members] for members in units]

    for hd in range(MLA_HEADS_PER_STEP):
        vx_ref[hd, 0:MLA_V, :] = vt_ref[hd * MLA_V:(hd + 1) * MLA_V, :]
        vx_ref[hd, MLA_V:, :] = jnp.ones((vx_ref.shape[1] - MLA_V, seq), vx_ref.dtype)

    def score_tile(u, b, j):
        hd, i, off = units[u][b]
        st = state[u][b]
        keys = slice(j * t, (j + 1) * t)
        s = _dot(k_ref[keys, :], qt_ref[hd * qk:(hd + 1) * qk, i * t:(i + 1) * t])
        if j == i:
            s = jnp.where(causal, s, NEG_INF)
        s_ref[u % 2, off + j * t:off + (j + 1) * t, :] = s
        m8 = _tree(jnp.maximum, [s[a * sub:(a + 1) * sub, :] for a in range(t // sub)])
        st["m8"] = m8 if st["m8"] is None else jnp.maximum(st["m8"], m8)

    def prob_tile(u, b, j):
        _, _, off = units[u][b]
        st = state[u][b]
        if "m" not in st:
            st["m"] = jnp.max(st["m8"], axis=0, keepdims=True)
        rows = slice(off + j * t, off + (j + 1) * t)
        p_ref[u % 3, rows, :] = jnp.exp2(s_ref[u % 2, rows, :] - st["m"]).astype(jnp.bfloat16)

    def value_matmul(u):
        for b, (hd, i, off) in enumerate(units[u]):
            kv = (i + 1) * t
            state[u][b]["acc"] = _dot(vx_ref[hd, :, 0:kv], p_ref[u % 3, off:off + kv, :])

    def store(u):
        for b, (hd, i, _) in enumerate(units[u]):
            rows = slice(i * t, (i + 1) * t)
            acc = state[u][b]["acc"]
            out_t = (acc[0:MLA_V, :] / acc[MLA_V:MLA_V + 1, :]).astype(o_ref.dtype)
            gate = g_ref[rows, hd * MLA_V:(hd + 1) * MLA_V]
            o_ref[rows, hd * MLA_V:(hd + 1) * MLA_V] = out_t.T * gate
        state[u] = None

    def tiles(fn, u):
        return [functools.partial(fn, u, b, j) for b, (_, i, _) in enumerate(units[u]) for j in range(i + 1)]

    nu = len(units)
    for task in tiles(score_tile, 0):
        task()
    for u in range(nu + 2):
        if 0 <= u - 2 < nu:
            store(u - 2)
        nxt = tiles(score_tile, u + 1) if u + 1 < nu else []
        cur = tiles(prob_tile, u) if u < nu else []
        while nxt or cur:
            if cur:
                cur.pop(0)()
            if nxt:
                nxt.pop(0)()
        if 0 <= u - 1 < nu:
            value_matmul(u - 1)


def _mla_call(qt, km, vt, gm, batch, seq):
    n = km.shape[0]
    t = MLA_BLOCK
    hps = MLA_HEADS_PER_STEP
    groups = MLA_HEADS // hps
    ones_rows = 16
    feat = lambda b, h: (b * groups + h, 0)
    tok = lambda b, h: (b, h)
    return pl.pallas_call(
        _mla_kernel,
        grid=(batch, groups),
        in_specs=[
            pl.BlockSpec((hps * MLA_QK_PAD, seq), feat),
            pl.BlockSpec((seq, MLA_K_COLS), lambda b, h: (b, 0)),
            pl.BlockSpec((hps * MLA_V, seq), feat),
            pl.BlockSpec((seq, hps * MLA_V), tok),
        ],
        out_specs=pl.BlockSpec((seq, hps * MLA_V), tok),
        out_shape=jax.ShapeDtypeStruct((n, MLA_WIDTH), jnp.bfloat16),
        scratch_shapes=[pltpu.VMEM((2, MLA_GROUP_TILES * t, t), jnp.float32),
                        pltpu.VMEM((3, MLA_GROUP_TILES * t, t), jnp.bfloat16),
                        pltpu.VMEM((hps, MLA_V + ones_rows, seq), jnp.bfloat16)],
        compiler_params=pltpu.CompilerParams(
            dimension_semantics=("arbitrary", "arbitrary"),
            vmem_limit_bytes=V7X_VMEM_LIMIT_BYTES),
        name="mla",
    )(qt, km, vt, gm)


def _swa_out_kernel(sink_ref, qt_ref, k_ref, kh_ref, vt_ref, vth_ref, g_ref, mm_ref, x_ref,
                    wf_ref, fg_ref, o_ref, ms_ref, kx_ref, vx_ref, w_ref):
    w = SWA_WINDOW
    chunk = pl.program_id(1)
    rows_total = k_ref.shape[0]
    bf = jnp.bfloat16
    sub = SUBLANES

    @pl.when((pl.program_id(0) == 0) & (chunk == 0))
    def _():
        w_ref[...] = wf_ref[...].astype(bf)

    kx_ref[0:w, :] = kh_ref[...]
    kx_ref[w:, :] = k_ref[...]
    for hk in range(SWA_KV_HEADS):
        vx_ref[hk, 0:SWA_D, 0:w] = vth_ref[hk * SWA_D:(hk + 1) * SWA_D, :]
        vx_ref[hk, 0:SWA_D, w:] = vt_ref[hk * SWA_D:(hk + 1) * SWA_D, :]
        vx_ref[hk, SWA_D:, :] = jnp.ones((vx_ref.shape[1] - SWA_D, vx_ref.shape[2]), bf)

    ki = lax.broadcasted_iota(jnp.int32, (2 * w, 2 * w), 0)
    qi = lax.broadcasted_iota(jnp.int32, (2 * w, 2 * w), 1) % w
    rel = qi + w - ki
    band = (rel >= 0) & (rel < SWA_WINDOW)
    band_first = band & ((ki >= w) | (chunk > 0))
    lane = lax.broadcasted_iota(jnp.int32, (1, 2 * w), 1)

    units = [(n, hk) for n in range(rows_total // w) for hk in range(SWA_KV_HEADS)]
    state = [dict() for _ in units]

    def sink_rows(hk):
        sink = [sink_ref[hk * SWA_GROUP + i] * LOG2E for i in range(SWA_GROUP)]
        return (jnp.where(lane < w, sink[0], sink[2]), jnp.where(lane < w, sink[1], sink[3]))

    def stage_scores(u, half):
        n, hk = units[u]
        base = hk * 2 * LANES
        qcols = slice(n * w, (n + 1) * w)
        keys = slice(n * w, (n + 2) * w)
        qt = jnp.concatenate([qt_ref[base:base + LANES, qcols],
                              qt_ref[base + LANES:base + 2 * LANES, qcols]], axis=1)
        valid = band_first if n == 0 else band
        st = state[u]
        lanes = slice(base + half * LANES, base + (half + 1) * LANES)
        s = jnp.where(valid, _dot(kx_ref[keys, lanes], qt), NEG_INF)
        m8 = _tree(jnp.maximum, [s[a * sub:(a + 1) * sub, :] for a in range(2 * w // sub)])
        st.setdefault("s", {})[half] = s
        st.setdefault("m", {})[half] = jnp.maximum(jnp.max(m8, axis=0, keepdims=True),
                                                   sink_rows(hk)[half])

    def stage_values(u, half):
        n, hk = units[u]
        keys = slice(n * w, (n + 2) * w)
        st = state[u]
        p = jnp.exp2(st["s"].pop(half) - st["m"][half]).astype(bf)
        st.setdefault("acc", {})[half] = _dot(vx_ref[hk, :, keys], p)

    def stage_store(u):
        n, hk = units[u]
        base = hk * 2 * LANES
        qcols = slice(n * w, (n + 1) * w)
        sinks = sink_rows(hk)
        st = state[u]
        outs = []
        for half in range(2):
            acc = st["acc"][half]
            den = acc[SWA_D:SWA_D + 1, :] + jnp.exp2(sinks[half] - st["m"][half])
            outs.append(acc[0:SWA_D, :] / den)
        g = g_ref[qcols, base:base + 2 * LANES].astype(jnp.float32)
        for ch in range(2):
            o_t = jnp.concatenate([outs[0][:, ch * w:(ch + 1) * w],
                                   outs[1][:, ch * w:(ch + 1) * w]], axis=0)
            ms_ref[qcols, base + ch * LANES:base + (ch + 1) * LANES] = (
                o_t.T * g[:, ch * LANES:(ch + 1) * LANES]).astype(bf)
        state[u] = None

    t = SWA_OUT_SUBTILE
    units_per_tile = (t // w) * SWA_KV_HEADS

    d_model = x_ref.shape[1]
    grp = MXU_COLS
    out_state = {}

    def out_group(i, gc):
        rows = slice(i * t, (i + 1) * t)
        cols = slice(gc * grp, (gc + 1) * grp)
        y = (x_ref[rows, cols] + _dot(mm_ref[rows, :], w_ref[0:MLA_WIDTH, cols])
             + _dot(ms_ref[rows, :], w_ref[MLA_WIDTH:, cols]))
        out_state.setdefault(i, []).append(y)

    def out_norm(i):
        rows = slice(i * t, (i + 1) * t)
        ys = out_state.pop(i)
        ssq = _tree(jnp.add, [jnp.sum(y * y, axis=-1, keepdims=True) for y in ys])
        scale = lax.rsqrt(ssq / d_model + NORM_EPS)
        for gc, y in enumerate(ys):
            cols = slice(gc * grp, (gc + 1) * grp)
            o_ref[rows, cols] = y * scale * fg_ref[:, cols]

    nu = len(units)
    lag = SWA_STORE_LAG
    gsz = SWA_UNITS_PER_STEP
    pending = []
    tasks_per_step = SWA_OUT_TASKS_PER_STEP

    def retire(count):
        for _ in range(min(count, len(pending))):
            pending.pop(0)()

    for step in range(nu // gsz + lag):
        for u in range(step * gsz, (step + 1) * gsz):
            if u < nu:
                stage_scores(u, 0)
                stage_scores(u, 1)
        retire(tasks_per_step // 2)
        for u in range((step - 1) * gsz, step * gsz):
            if 0 <= u < nu:
                stage_values(u, 0)
                stage_values(u, 1)
        retire(tasks_per_step - tasks_per_step // 2)
        for u in range((step - lag) * gsz, (step - lag + 1) * gsz):
            if 0 <= u < nu:
                stage_store(u)
                if (u + 1) % units_per_tile == 0:
                    i = u // units_per_tile
                    pending += [functools.partial(out_group, i, gc) for gc in range(d_model // grp)]
                    pending.append(functools.partial(out_norm, i))
    for task in pending:
        task()


def _swa_out_call(sinks, qst, ks4, vst, gs, mm, x2, w_o, fg, batch, seq):
    n, d = x2.shape
    r = SWA_OUT_ROWS
    w = SWA_WINDOW
    cps = seq // r
    bpc = r // w
    bps = seq // w
    v_rows = SWA_KV_HEADS * SWA_D
    ones_rows = 16
    row = lambda b, c: (b * cps + c, 0)
    halo = lambda b, c: (b * bps + jnp.maximum(c * bpc - 1, 0), 0)
    feat = lambda b, c: (b, c)
    feat_halo = lambda b, c: (b, jnp.maximum(c * bpc - 1, 0))
    const = lambda b, c: (0, 0)
    wide = pl.BlockSpec((r, SWA_WIDTH), row)
    return pl.pallas_call(
        _swa_out_kernel,
        grid=(batch, cps),
        in_specs=[
            pl.BlockSpec(memory_space=pltpu.SMEM),
            pl.BlockSpec((SWA_WIDTH, r), feat),
            wide,
            pl.BlockSpec((w, SWA_WIDTH), halo),
            pl.BlockSpec((v_rows, r), feat),
            pl.BlockSpec((v_rows, w), feat_halo),
            wide,
            pl.BlockSpec((r, MLA_WIDTH), row),
            pl.BlockSpec((r, d), row),
            pl.BlockSpec(w_o.shape, const),
            pl.BlockSpec((1, d), const),
        ],
        out_specs=pl.BlockSpec((r, d), row),
        out_shape=jax.ShapeDtypeStruct((n, d), jnp.float32),
        scratch_shapes=[pltpu.VMEM((r, SWA_WIDTH), jnp.bfloat16),
                        pltpu.VMEM((r + w, SWA_WIDTH), jnp.bfloat16),
                        pltpu.VMEM((SWA_KV_HEADS, SWA_D + ones_rows, r + w), jnp.bfloat16),
                        pltpu.VMEM(w_o.shape, jnp.bfloat16)],
        compiler_params=pltpu.CompilerParams(
            dimension_semantics=("arbitrary", "arbitrary"),
            vmem_limit_bytes=V7X_VMEM_LIMIT_BYTES),
        name="swa_out",
    )(sinks, qst, ks4, ks4, vst, vst, gs, mm, x2, w_o, fg)


def _pair_rope_pieces(src, dst, n_heads, dim):
    half = dim // 2
    out = []
    for pair in range(n_heads // 2):
        for which in range(2):
            head = src + (2 * pair + which) * dim
            chunk = dst + pair * 2 * dim
            out.append((head, chunk + which * half, half))
            out.append((head + half, chunk + 2 * half + which * half, half))
    return out


def _w_in_pieces():
    pieces, qs_pieces, src, dst = [], [], 0, 0
    for name, width in (("c_q", MLA_Q_RANK), ("c_kv", MLA_KV_RANK), ("k_rope", MLA_ROPE),
                        ("g_mla", MLA_WIDTH), ("q_s", SWA_WIDTH),
                        ("k_s", SWA_KV_HEADS * SWA_D), ("v_s", SWA_KV_HEADS * SWA_D),
                        ("g_swa", SWA_WIDTH)):
        if name == "k_rope":
            half = MLA_ROPE // 2
            pieces += [(src, dst, half), (src, dst + half, half),
                       (src + half, dst + 2 * half, half), (src + half, dst + 3 * half, half)]
            dst += 2 * MLA_ROPE
        elif name == "q_s":
            qs_pieces += _pair_rope_pieces(src, 0, SWA_Q_HEADS, SWA_D)
        elif name == "k_s":
            pieces += _pair_rope_pieces(src, dst, SWA_KV_HEADS, SWA_D)
            dst += width
        else:
            pieces.append((src, dst, width))
            dst += width
        src += width
    assert dst == PROJ_COLS
    return pieces, qs_pieces


def _gather_cols(src_ref, pieces, n_dst_cols):
    rows, n_src = src_ref.shape
    lane = lax.broadcasted_iota(jnp.int32, (rows, LANES), 1)
    loaded, rolled = {}, {}

    def load(k):
        if k not in loaded:
            width = min(LANES, n_src - k * LANES)
            v = src_ref[:, k * LANES:k * LANES + width]
            if width < LANES:
                v = jnp.concatenate([v, jnp.zeros((rows, LANES - width), v.dtype)], axis=1)
            loaded[k] = v
        return loaded[k]

    def shifted(k, shift):
        if shift == 0:
            return load(k)
        if (k, shift) not in rolled:
            rolled[(k, shift)] = pltpu.roll(load(k), shift, 1)
        return rolled[(k, shift)]

    chunks = []
    for d in range(n_dst_cols // LANES):
        acc = None
        for s0, d0, w in pieces:
            lo, hi = max(d0, d * LANES), min(d0 + w, (d + 1) * LANES)
            while lo < hi:
                s_abs = s0 + lo - d0
                k, a = divmod(s_abs, LANES)
                n = min(hi - lo, LANES - a)
                a_dst = lo - d * LANES
                val = shifted(k, (a_dst - a) % LANES)
                if n == LANES:
                    acc = val
                else:
                    mask = (lane >= a_dst) & (lane < a_dst + n)
                    acc = jnp.where(mask, val, 0.0 if acc is None else acc)
                lo += n
        chunks.append(acc)
    return chunks


def _gather_rows(src_ref, pieces, d):
    parts = []
    for s0, d0, w in pieces:
        lo, hi = max(d0, d * LANES), min(d0 + w, (d + 1) * LANES)
        if lo < hi:
            parts.append((lo, src_ref[s0 + lo - d0:s0 + hi - d0, :]))
    parts.sort(key=lambda t: t[0])
    assert sum(p.shape[0] for _, p in parts) == LANES
    return jnp.concatenate([p for _, p in parts], axis=0) if len(parts) > 1 else parts[0][1]


def _prep_latent_weights(wq_ref, wkv_ref, wqt_ref, wvt_ref):
    q_pieces = ([(hd * MLA_QK, hd * MLA_NOPE, MLA_NOPE) for hd in range(MLA_HEADS)]
                + [(hd * MLA_QK + MLA_NOPE, MLA_HEADS * MLA_NOPE + hd * MLA_ROPE, MLA_ROPE)
                   for hd in range(MLA_HEADS)])
    wq = jnp.concatenate(_gather_cols(wq_ref, q_pieces, MLA_HEADS * MLA_QK), axis=1)
    nope_w = MLA_HEADS * MLA_NOPE
    per_head = MLA_NOPE + MLA_V
    for hd in range(MLA_HEADS):
        wk = wkv_ref[:, hd * per_head:hd * per_head + MLA_NOPE]
        absorbed_t = lax.dot_general(wk, wq[:, hd * MLA_NOPE:(hd + 1) * MLA_NOPE],
                                     (((1,), (1,)), ((), ())), precision=lax.Precision.HIGHEST,
                                     preferred_element_type=jnp.float32)
        wqt_ref[hd * MLA_KV_RANK:(hd + 1) * MLA_KV_RANK, :] = absorbed_t.astype(wqt_ref.dtype)
        wvt_ref[hd * MLA_V:(hd + 1) * MLA_V, :] = (
            wkv_ref[:, hd * per_head + MLA_NOPE:(hd + 1) * per_head].T.astype(wvt_ref.dtype))
    rope_rows = MLA_HEADS * MLA_KV_RANK
    wqt_ref[rope_rows:, :] = wq[:, nope_w:].T.astype(wqt_ref.dtype)


def kernel(x, ln_mix, w_in, q_a_norm, w_q_up, kv_a_norm, w_kv_up, attn_sinks, w_out, final_norm):
    batch, seq, d = x.shape
    depth = ln_mix.shape[0]
    assert depth == 1, "final norm is fused into the single layer's output kernel"
    assert seq % MLA_BLOCK == 0 and seq % PROJ_ROWS == 0 and seq % SWA_OUT_ROWS == 0
    x2 = x.reshape(batch * seq, d)
    qt, km, vt, gm, qst, ks4, vst, gs = _proj_call(
        x2, ln_mix[0].reshape(1, -1), w_in[0].T, q_a_norm[0].reshape(1, -1), w_q_up[0],
        kv_a_norm[0].reshape(1, -1), w_kv_up[0], seq)
    mm = _mla_call(qt, km, vt, gm, batch, seq)
    out = _swa_out_call(attn_sinks[0], qst, ks4, vst, gs, mm, x2, w_out[0],
                        final_norm.reshape(1, -1), batch, seq)
    return out.reshape(batch, seq, d)
```

```python
import functools
import math

import jax
import jax.numpy as jnp
from jax import lax
from jax.experimental import pallas as pl
from jax.experimental.pallas import tpu as pltpu

ROPE_THETA = 10000.0
NORM_EPS = 1e-6
NEG_INF = -1e30
LOG2E = 1.4426950408889634

MLA_HEADS = 4
MLA_NOPE = 128
MLA_ROPE = 64
MLA_V = 128
MLA_Q_RANK = 256
MLA_KV_RANK = 128
MLA_QK = MLA_NOPE + MLA_ROPE
MLA_WIDTH = MLA_HEADS * MLA_V

SWA_Q_HEADS = 8
SWA_KV_HEADS = 2
SWA_D = 64
SWA_WINDOW = 128
SWA_GROUP = SWA_Q_HEADS // SWA_KV_HEADS
SWA_WIDTH = SWA_Q_HEADS * SWA_D

LANES = 128
SUBLANES = 8
MXU_COLS = 2 * LANES
MLA_QK_PAD = MXU_COLS
assert MLA_KV_RANK + LANES == MLA_QK_PAD
MLA_K_COLS = MLA_QK_PAD
V7X_VMEM_LIMIT_BYTES = 56 * 1024 * 1024

PROJ_ROWS = 1024
MLA_BLOCK = 256
MLA_GROUP_TILES = 8
MLA_HEADS_PER_STEP = 2
SWA_OUT_ROWS = 1024
SWA_OUT_SUBTILE = 256
SWA_UNITS_PER_STEP = 4
SWA_OUT_TASKS_PER_STEP = 4
SWA_STORE_LAG = 2


def _inv_freq(dim):
    assert 2 * dim == LANES
    inv = 1.0 / (ROPE_THETA ** (jnp.arange(0, dim, 2, dtype=jnp.float32) / dim))
    return inv.reshape(dim // 2, 1)


def _rope_table_shapes(seq, tile, dim):
    f32 = jnp.float32
    return [pltpu.VMEM((seq // tile, dim // 2, tile), f32), pltpu.VMEM((seq // tile, dim // 2, tile), f32),
            pltpu.VMEM((seq, LANES), f32), pltpu.VMEM((seq, LANES), f32)]


def _fill_rope_tables(inv_ref, ct_ref, st_ref, cl_ref, sl_ref):
    n_tiles, half, tile = ct_ref.shape
    for j in range(n_tiles):
        pos = (lax.broadcasted_iota(jnp.int32, (half, tile), 1) + j * tile).astype(jnp.float32)
        ang = pos * inv_ref[...]
        c, s = jnp.cos(ang), jnp.sin(ang)
        ct_ref[j] = c
        st_ref[j] = s
        cl_ref[j * tile:(j + 1) * tile, :] = jnp.concatenate([c, c, c, c], axis=0).T
        sl_ref[j * tile:(j + 1) * tile, :] = jnp.concatenate([-s, -s, s, s], axis=0).T


def _rope(x, c, s):
    return x * c + pltpu.roll(x, LANES // 2, 1) * s


def _rms(x, g):
    return x * lax.rsqrt(jnp.mean(x * x, axis=-1, keepdims=True) + NORM_EPS) * g


def _silu(g):
    return g / (1.0 + jnp.exp(-g))


def _dot(a, b):
    return jnp.dot(a, b, preferred_element_type=jnp.float32)


def _dot_nt(a, b):
    return lax.dot_general(a, b, (((1,), (1,)), ((), ())),
                           preferred_element_type=jnp.float32)


PROJ_COLS = 1792


def _proj_kernel(x_ref, ln_ref, wint_ref, qg_ref, wq_f32_ref, kvg_ref, wkv_f32_ref,
                 inv_mla_ref, inv_swa_ref,
                 qt_ref, km_ref, vt_ref, gm_ref, qst_ref, ks_ref, vst_ref, gs_ref,
                 wt_ref, wqt_ref, wvt_ref, *table_refs):
    bf = jnp.bfloat16

    mla_tabs = table_refs[0:4]
    swa_tabs = table_refs[4:8] if len(table_refs) > 4 else mla_tabs

    @pl.when(pl.program_id(0) == 0)
    def _():
        _fill_rope_tables(inv_mla_ref, *mla_tabs)
        if swa_tabs is not mla_tabs:
            _fill_rope_tables(inv_swa_ref, *swa_tabs)
        _prep_latent_weights(wq_f32_ref, wkv_f32_ref, wqt_ref, wvt_ref)
        pieces, qs_pieces = _w_in_pieces()
        for d in range(PROJ_COLS // LANES):
            wt_ref[d * LANES:(d + 1) * LANES, :] = _gather_rows(wint_ref, pieces, d).astype(bf)
        for d in range(SWA_WIDTH // LANES):
            wt_ref[PROJ_COLS + d * LANES:PROJ_COLS + (d + 1) * LANES, :] = (
                _gather_rows(wint_ref, qs_pieces, d).astype(bf))

    x = x_ref[...]
    h = _rms(x, ln_ref[...]).astype(bf)

    tile = pl.program_id(0) % mla_tabs[0].shape[0]
    rows = pl.ds(pl.multiple_of(tile * x.shape[0], x.shape[0]), x.shape[0])
    cm, sm = mla_tabs[2][rows, :], mla_tabs[3][rows, :]
    cs, ss = swa_tabs[2][rows, :], swa_tabs[3][rows, :]
    lane = lax.broadcasted_iota(jnp.int32, (x.shape[0], LANES), 1)
    first = (lane % (LANES // 2)) < (LANES // 4)
    grp = MXU_COLS

    def project(a):
        return _dot_nt(h, wt_ref[a:a + grp, :])

    c_q = project(0)
    lat_b = project(256)
    cqn = _rms(c_q, qg_ref[...]).astype(bf)
    ckvn = _rms(lat_b[:, 0:MLA_KV_RANK], kvg_ref[...]).astype(bf)
    kr = _rope(lat_b[:, LANES:2 * LANES], cm, sm)
    kr_first = jnp.where(first, kr, 0.0).astype(bf)

    s_scale = LOG2E / math.sqrt(SWA_D)
    qst = _dot_nt(wt_ref[PROJ_COLS:PROJ_COLS + SWA_WIDTH, :], h)
    cst2 = jnp.concatenate([swa_tabs[0][tile], swa_tabs[0][tile]], axis=0)
    sst2 = jnp.concatenate([swa_tabs[1][tile], swa_tabs[1][tile]], axis=0)
    for ch in range(SWA_WIDTH // LANES):
        x1 = qst[ch * LANES:ch * LANES + SWA_D, :]
        x2 = qst[ch * LANES + SWA_D:(ch + 1) * LANES, :]
        qst_ref[ch * LANES:ch * LANES + SWA_D, :] = ((x1 * cst2 - x2 * sst2) * s_scale).astype(bf)
        qst_ref[ch * LANES + SWA_D:(ch + 1) * LANES, :] = ((x2 * cst2 + x1 * sst2) * s_scale).astype(bf)

    q_scale = LOG2E / math.sqrt(MLA_QK)
    lat_w = MLA_HEADS * MLA_KV_RANK
    half = MLA_ROPE // 2
    qt = _dot_nt(wqt_ref[...], cqn)
    ct, st = mla_tabs[0][tile], mla_tabs[1][tile]
    zeros = jnp.zeros((half, x.shape[0]), bf)
    for hd in range(MLA_HEADS):
        qt_ref[hd * grp:hd * grp + LANES, :] = (
            qt[hd * MLA_KV_RANK:(hd + 1) * MLA_KV_RANK, :] * q_scale).astype(bf)
        x1 = qt[lat_w + hd * MLA_ROPE:lat_w + hd * MLA_ROPE + half, :]
        x2 = qt[lat_w + hd * MLA_ROPE + half:lat_w + (hd + 1) * MLA_ROPE, :]
        base = hd * grp + LANES
        qt_ref[base:base + half, :] = ((x1 * ct - x2 * st) * q_scale).astype(bf)
        qt_ref[base + half:base + 2 * half, :] = zeros
        qt_ref[base + 2 * half:base + 3 * half, :] = ((x2 * ct + x1 * st) * q_scale).astype(bf)
        qt_ref[base + 3 * half:base + 4 * half, :] = zeros
    km_ref[:, 0:MLA_KV_RANK] = ckvn
    km_ref[:, MLA_KV_RANK:MLA_KV_RANK + LANES] = kr_first
    for g2 in range(MLA_WIDTH // grp):
        gm_ref[:, g2 * grp:(g2 + 1) * grp] = _silu(project(512 + g2 * grp)).astype(bf)

    kv_s = project(1024)
    k01 = _rope(kv_s[:, 0:LANES], cs, ss)
    ks_ref[:, 0 * LANES:1 * LANES] = jnp.where(first, k01, 0.0).astype(bf)
    ks_ref[:, 1 * LANES:2 * LANES] = jnp.where(first, 0.0, pltpu.roll(k01, LANES // 4, 1)).astype(bf)
    ks_ref[:, 2 * LANES:3 * LANES] = jnp.where(first, pltpu.roll(k01, 3 * LANES // 4, 1), 0.0).astype(bf)
    ks_ref[:, 3 * LANES:4 * LANES] = jnp.where(first, 0.0, k01).astype(bf)
    vst_ref[...] = kv_s[:, LANES:2 * LANES].T.astype(bf)

    for g2 in range(SWA_WIDTH // grp):
        gs_ref[:, g2 * grp:(g2 + 1) * grp] = _silu(project(1280 + g2 * grp)).astype(bf)

    vt_ref[...] = _dot_nt(wvt_ref[...], ckvn).astype(bf)


def _proj_call(x2, ln_g, w_in_t, q_g, w_q_up, kv_g, w_kv_up, seq):
    n, d = x2.shape
    tm = PROJ_ROWS
    steps_per_seq = seq // tm
    batch = n // seq
    row = lambda i: (i, 0)
    const = lambda i: (0, 0)
    col = lambda i: (i // steps_per_seq, i % steps_per_seq)
    table_scratch = _rope_table_shapes(seq, tm, MLA_ROPE)
    if SWA_D != MLA_ROPE:
        table_scratch += _rope_table_shapes(seq, tm, SWA_D)
    bf = jnp.bfloat16
    qt_rows = MLA_HEADS * MLA_QK_PAD
    row_out = lambda w: (pl.BlockSpec((tm, w), row), jax.ShapeDtypeStruct((n, w), bf))
    col_out = lambda r: (pl.BlockSpec((r, tm), col), jax.ShapeDtypeStruct((batch * r, seq), bf))
    outs = [col_out(qt_rows), row_out(MLA_K_COLS), col_out(MLA_WIDTH), row_out(MLA_WIDTH),
            col_out(SWA_WIDTH), row_out(SWA_WIDTH), col_out(SWA_KV_HEADS * SWA_D), row_out(SWA_WIDTH)]
    return pl.pallas_call(
        _proj_kernel,
        grid=(n // tm,),
        in_specs=[
            pl.BlockSpec((tm, d), row),
            pl.BlockSpec((1, d), const),
            pl.BlockSpec(w_in_t.shape, const),
            pl.BlockSpec((1, MLA_Q_RANK), const),
            pl.BlockSpec(w_q_up.shape, const),
            pl.BlockSpec((1, MLA_KV_RANK), const),
            pl.BlockSpec(w_kv_up.shape, const),
            pl.BlockSpec((MLA_ROPE // 2, 1), const),
            pl.BlockSpec((SWA_D // 2, 1), const),
        ],
        out_specs=[o[0] for o in outs],
        out_shape=[o[1] for o in outs],
        scratch_shapes=[pltpu.VMEM((PROJ_COLS + SWA_WIDTH, d), bf),
                        pltpu.VMEM((MLA_HEADS * (MLA_KV_RANK + MLA_ROPE), MLA_Q_RANK), bf),
                        pltpu.VMEM((MLA_WIDTH, MLA_KV_RANK), bf)] + table_scratch,
        compiler_params=pltpu.CompilerParams(
            dimension_semantics=("arbitrary",),
            vmem_limit_bytes=V7X_VMEM_LIMIT_BYTES),
        name="proj",
    )(x2, ln_g, w_in_t, q_g, w_q_up, kv_g, w_kv_up, _inv_freq(MLA_ROPE), _inv_freq(SWA_D))


def _tree(op, xs):
    xs = list(xs)
    while len(xs) > 1:
        xs = [op(xs[a], xs[a + 1]) if a + 1 < len(xs) else xs[a] for a in range(0, len(xs), 2)]
    return xs[0]

def _mla_block_groups(nblk):
    groups, cur, load = [], [], 0
    for i in range(nblk - 1, -1, -1):
        if load + i + 1 > MLA_GROUP_TILES:
            groups.append(cur)
            cur, load = [], 0
        cur.append(i)
        load += i + 1
    groups.append(cur)
    return groups


def _mla_kernel(qt_ref, k_ref, vt_ref, g_ref, o_ref, s_ref, p_ref, vx_ref):
    t = MLA_BLOCK
    seq = k_ref.shape[0]
    sub = SUBLANES
    qk = MLA_QK_PAD
    r = lax.broadcasted_iota(jnp.int32, (t, t), 0)
    c = lax.broadcasted_iota(jnp.int32, (t, t), 1)
    causal = r <= c
    nblk = seq // t
    units = []
    for hd in range(MLA_HEADS_PER_STEP):
        for blocks in _mla_block_groups(nblk):
            members, off = [], 0
            for i in blocks:
                members.append((hd, i, off))
                off += (i + 1) * t
            assert off <= s_ref.shape[1]
            units.append(members)
    state = [[dict(m8=None) for _ in members] for members in units]

    for hd in range(MLA_HEADS_PER_STEP):
        vx_ref[hd, 0:MLA_V, :] = vt_ref[hd * MLA_V:(hd + 1) * MLA_V, :]
        vx_ref[hd, MLA_V:, :] = jnp.ones((vx_ref.shape[1] - MLA_V, seq), vx_ref.dtype)

    def score_tile(u, b, j):
        hd, i, off = units[u][b]
        st = state[u][b]
        keys = slice(j * t, (j + 1) * t)
        s = _dot(k_ref[keys, :], qt_ref[hd * qk:(hd + 1) * qk, i * t:(i + 1) * t])
        if j == i:
            s = jnp.where(causal, s, NEG_INF)
        s_ref[u % 2, off + j * t:off + (j + 1) * t, :] = s
        m8 = _tree(jnp.maximum, [s[a * sub:(a + 1) * sub, :] for a in range(t // sub)])
        st["m8"] = m8 if st["m8"] is None else jnp.maximum(st["m8"], m8)

    def prob_tile(u, b, j):
        _, _, off = units[u][b]
        st = state[u][b]
        if "m" not in st:
            st["m"] = jnp.max(st["m8"], axis=0, keepdims=True)
        rows = slice(off + j * t, off + (j + 1) * t)
        p_ref[u % 3, rows, :] = jnp.exp2(s_ref[u % 2, rows, :] - st["m"]).astype(jnp.bfloat16)

    def value_matmul(u):
        for b, (hd, i, off) in enumerate(units[u]):
            kv = (i + 1) * t
            state[u][b]["acc"] = _dot(vx_ref[hd, :, 0:kv], p_ref[u % 3, off:off + kv, :])

    def store(u):
        for b, (hd, i, _) in enumerate(units[u]):
            rows = slice(i * t, (i + 1) * t)
            acc = state[u][b]["acc"]
            out_t = (acc[0:MLA_V, :] / acc[MLA_V:MLA_V + 1, :]).astype(o_ref.dtype)
            gate = g_ref[rows, hd * MLA_V:(hd + 1) * MLA_V]
            o_ref[rows, hd * MLA_V:(hd + 1) * MLA_V] = out_t.T * gate
        state[u] = None

    def tiles(fn, u):
        return [functools.partial(fn, u, b, j) for b, (_, i, _) in enumerate(units[u]) for j in range(i + 1)]

    nu = len(units)
    for task in tiles(score_tile, 0):
        task()
    for u in range(nu + 2):
        if 0 <= u - 2 < nu:
            store(u - 2)
        nxt = tiles(score_tile, u + 1) if u + 1 < nu else []
        cur = tiles(prob_tile, u) if u < nu else []
        while nxt or cur:
            if cur:
                cur.pop(0)()
            if nxt:
                nxt.pop(0)()
        if 0 <= u - 1 < nu:
            value_matmul(u - 1)


def _mla_call(qt, km, vt, gm, batch, seq):
    n = km.shape[0]
    t = MLA_BLOCK
    hps = MLA_HEADS_PER_STEP
    groups = MLA_HEADS // hps
    ones_rows = 16
    feat = lambda b, h: (b * groups + h, 0)
    tok = lambda b, h: (b, h)
    return pl.pallas_call(
        _mla_kernel,
        grid=(batch, groups),
        in_specs=[
            pl.BlockSpec((hps * MLA_QK_PAD, seq), feat),
            pl.BlockSpec((seq, MLA_K_COLS), lambda b, h: (b, 0)),
            pl.BlockSpec((hps * MLA_V, seq), feat),
            pl.BlockSpec((seq, hps * MLA_V), tok),
        ],
        out_specs=pl.BlockSpec((seq, hps * MLA_V), tok),
        out_shape=jax.ShapeDtypeStruct((n, MLA_WIDTH), jnp.bfloat16),
        scratch_shapes=[pltpu.VMEM((2, MLA_GROUP_TILES * t, t), jnp.float32),
                        pltpu.VMEM((3, MLA_GROUP_TILES * t, t), jnp.bfloat16),
                        pltpu.VMEM((hps, MLA_V + ones_rows, seq), jnp.bfloat16)],
        compiler_params=pltpu.CompilerParams(
            dimension_semantics=("arbitrary", "arbitrary"),
            vmem_limit_bytes=V7X_VMEM_LIMIT_BYTES),
        name="mla",
    )(qt, km, vt, gm)


def _swa_out_kernel(sink_ref, qt_ref, k_ref, kh_ref, vt_ref, vth_ref, g_ref, mm_ref, x_ref,
                    wf_ref, fg_ref, o_ref, ms_ref, kx_ref, vx_ref, w_ref):
    w = SWA_WINDOW
    chunk = pl.program_id(1)
    rows_total = k_ref.shape[0]
    bf = jnp.bfloat16
    sub = SUBLANES

    @pl.when((pl.program_id(0) == 0) & (chunk == 0))
    def _():
        w_ref[...] = wf_ref[...].astype(bf)

    kx_ref[0:w, :] = kh_ref[...]
    kx_ref[w:, :] = k_ref[...]
    for hk in range(SWA_KV_HEADS):
        vx_ref[hk, 0:SWA_D, 0:w] = vth_ref[hk * SWA_D:(hk + 1) * SWA_D, :]
        vx_ref[hk, 0:SWA_D, w:] = vt_ref[hk * SWA_D:(hk + 1) * SWA_D, :]
        vx_ref[hk, SWA_D:, :] = jnp.ones((vx_ref.shape[1] - SWA_D, vx_ref.shape[2]), bf)

    ki = lax.broadcasted_iota(jnp.int32, (2 * w, 2 * w), 0)
    qi = lax.broadcasted_iota(jnp.int32, (2 * w, 2 * w), 1) % w
    rel = qi + w - ki
    band = (rel >= 0) & (rel < SWA_WINDOW)
    band_first = band & ((ki >= w) | (chunk > 0))
    lane = lax.broadcasted_iota(jnp.int32, (1, 2 * w), 1)

    units = [(n, hk) for n in range(rows_total // w) for hk in range(SWA_KV_HEADS)]
    state = [dict() for _ in units]

    def sink_rows(hk):
        sink = [sink_ref[hk * SWA_GROUP + i] * LOG2E for i in range(SWA_GROUP)]
        return (jnp.where(lane < w, sink[0], sink[2]), jnp.where(lane < w, sink[1], sink[3]))

    def stage_scores(u, half):
        n, hk = units[u]
        base = hk * 2 * LANES
        qcols = slice(n * w, (n + 1) * w)
        keys = slice(n * w, (n + 2) * w)
        qt = jnp.concatenate([qt_ref[base:base + LANES, qcols],
                              qt_ref[base + LANES:base + 2 * LANES, qcols]], axis=1)
        valid = band_first if n == 0 else band
        st = state[u]
        lanes = slice(base + half * LANES, base + (half + 1) * LANES)
        s = jnp.where(valid, _dot(kx_ref[keys, lanes], qt), NEG_INF)
        m8 = _tree(jnp.maximum, [s[a * sub:(a + 1) * sub, :] for a in range(2 * w // sub)])
        st.setdefault("s", {})[half] = s
        st.setdefault("m", {})[half] = jnp.maximum(jnp.max(m8, axis=0, keepdims=True),
                                                   sink_rows(hk)[half])

    def stage_values(u, half):
        n, hk = units[u]
        keys = slice(n * w, (n + 2) * w)
        st = state[u]
        p = jnp.exp2(st["s"].pop(half) - st["m"][half]).astype(bf)
        st.setdefault("acc", {})[half] = _dot(vx_ref[hk, :, keys], p)

    def stage_store(u):
        n, hk = units[u]
        base = hk * 2 * LANES
        qcols = slice(n * w, (n + 1) * w)
        sinks = sink_rows(hk)
        st = state[u]
        outs = []
        for half in range(2):
            acc = st["acc"][half]
            den = acc[SWA_D:SWA_D + 1, :] + jnp.exp2(sinks[half] - st["m"][half])
            outs.append((acc[0:SWA_D, :] / den).astype(bf))
        g = g_ref[qcols, base:base + 2 * LANES]
        for ch in range(2):
            o_t = jnp.concatenate([outs[0][:, ch * w:(ch + 1) * w],
                                   outs[1][:, ch * w:(ch + 1) * w]], axis=0)
            ms_ref[qcols, base + ch * LANES:base + (ch + 1) * LANES] = (
                o_t.T * g[:, ch * LANES:(ch + 1) * LANES])
        state[u] = None

    t = SWA_OUT_SUBTILE
    units_per_tile = (t // w) * SWA_KV_HEADS

    d_model = x_ref.shape[1]
    grp = MXU_COLS
    out_state = {}

    def out_group(i, gc):
        rows = slice(i * t, (i + 1) * t)
        cols = slice(gc * grp, (gc + 1) * grp)
        y = (x_ref[rows, cols] + _dot(mm_ref[rows, :], w_ref[0:MLA_WIDTH, cols])
             + _dot(ms_ref[rows, :], w_ref[MLA_WIDTH:, cols]))
        out_state.setdefault(i, []).append(y)

    def out_norm(i):
        rows = slice(i * t, (i + 1) * t)
        ys = out_state.pop(i)
        ssq = _tree(jnp.add, [jnp.sum(y * y, axis=-1, keepdims=True) for y in ys])
        scale = lax.rsqrt(ssq / d_model + NORM_EPS)
        for gc, y in enumerate(ys):
            cols = slice(gc * grp, (gc + 1) * grp)
            o_ref[rows, cols] = y * scale * fg_ref[:, cols]

    nu = len(units)
    lag = SWA_STORE_LAG
    gsz = SWA_UNITS_PER_STEP
    pending = []
    tasks_per_step = SWA_OUT_TASKS_PER_STEP

    def retire(count):
        for _ in range(min(count, len(pending))):
            pending.pop(0)()

    for step in range(nu // gsz + lag):
        for u in range(step * gsz, (step + 1) * gsz):
            if u < nu:
                stage_scores(u, 0)
                stage_scores(u, 1)
        retire(tasks_per_step // 2)
        for u in range((step - 1) * gsz, step * gsz):
            if 0 <= u < nu:
                stage_values(u, 0)
                stage_values(u, 1)
        retire(tasks_per_step - tasks_per_step // 2)
        for u in range((step - lag) * gsz, (step - lag + 1) * gsz):
            if 0 <= u < nu:
                stage_store(u)
                if (u + 1) % units_per_tile == 0:
                    i = u // units_per_tile
                    pending += [functools.partial(out_group, i, gc) for gc in range(d_model // grp)]
                    pending.append(functools.partial(out_norm, i))
    for task in pending:
        task()


def _swa_out_call(sinks, qst, ks4, vst, gs, mm, x2, w_o, fg, batch, seq):
    n, d = x2.shape
    r = SWA_OUT_ROWS
    w = SWA_WINDOW
    cps = seq // r
    bpc = r // w
    bps = seq // w
    v_rows = SWA_KV_HEADS * SWA_D
    ones_rows = 16
    row = lambda b, c: (b * cps + c, 0)
    halo = lambda b, c: (b * bps + jnp.maximum(c * bpc - 1, 0), 0)
    feat = lambda b, c: (b, c)
    feat_halo = lambda b, c: (b, jnp.maximum(c * bpc - 1, 0))
    const = lambda b, c: (0, 0)
    wide = pl.BlockSpec((r, SWA_WIDTH), row)
    return pl.pallas_call(
        _swa_out_kernel,
        grid=(batch, cps),
        in_specs=[
            pl.BlockSpec(memory_space=pltpu.SMEM),
            pl.BlockSpec((SWA_WIDTH, r), feat),
            wide,
            pl.BlockSpec((w, SWA_WIDTH), halo),
            pl.BlockSpec((v_rows, r), feat),
            pl.BlockSpec((v_rows, w), feat_halo),
            wide,
            pl.BlockSpec((r, MLA_WIDTH), row),
            pl.BlockSpec((r, d), row),
            pl.BlockSpec(w_o.shape, const),
            pl.BlockSpec((1, d), const),
        ],
        out_specs=pl.BlockSpec((r, d), row),
        out_shape=jax.ShapeDtypeStruct((n, d), jnp.float32),
        scratch_shapes=[pltpu.VMEM((r, SWA_WIDTH), jnp.bfloat16),
                        pltpu.VMEM((r + w, SWA_WIDTH), jnp.bfloat16),
                        pltpu.VMEM((SWA_KV_HEADS, SWA_D + ones_rows, r + w), jnp.bfloat16),
                        pltpu.VMEM(w_o.shape, jnp.bfloat16)],
        compiler_params=pltpu.CompilerParams(
            dimension_semantics=("arbitrary", "arbitrary"),
            vmem_limit_bytes=V7X_VMEM_LIMIT_BYTES),
        name="swa_out",
    )(sinks, qst, ks4, ks4, vst, vst, gs, mm, x2, w_o, fg)


def _pair_rope_pieces(src, dst, n_heads, dim):
    half = dim // 2
    out = []
    for pair in range(n_heads // 2):
        for which in range(2):
            head = src + (2 * pair + which) * dim
            chunk = dst + pair * 2 * dim
            out.append((head, chunk + which * half, half))
            out.append((head + half, chunk + 2 * half + which * half, half))
    return out


def _w_in_pieces():
    pieces, qs_pieces, src, dst = [], [], 0, 0
    for name, width in (("c_q", MLA_Q_RANK), ("c_kv", MLA_KV_RANK), ("k_rope", MLA_ROPE),
                        ("g_mla", MLA_WIDTH), ("q_s", SWA_WIDTH),
                        ("k_s", SWA_KV_HEADS * SWA_D), ("v_s", SWA_KV_HEADS * SWA_D),
                        ("g_swa", SWA_WIDTH)):
        if name == "k_rope":
            half = MLA_ROPE // 2
            pieces += [(src, dst, half), (src, dst + half, half),
                       (src + half, dst + 2 * half, half), (src + half, dst + 3 * half, half)]
            dst += 2 * MLA_ROPE
        elif name == "q_s":
            qs_pieces += _pair_rope_pieces(src, 0, SWA_Q_HEADS, SWA_D)
        elif name == "k_s":
            pieces += _pair_rope_pieces(src, dst, SWA_KV_HEADS, SWA_D)
            dst += width
        else:
            pieces.append((src, dst, width))
            dst += width
        src += width
    assert dst == PROJ_COLS
    return pieces, qs_pieces


def _gather_cols(src_ref, pieces, n_dst_cols):
    rows, n_src = src_ref.shape
    lane = lax.broadcasted_iota(jnp.int32, (rows, LANES), 1)
    loaded, rolled = {}, {}

    def load(k):
        if k not in loaded:
            width = min(LANES, n_src - k * LANES)
            v = src_ref[:, k * LANES:k * LANES + width]
            if width < LANES:
                v = jnp.concatenate([v, jnp.zeros((rows, LANES - width), v.dtype)], axis=1)
            loaded[k] = v
        return loaded[k]

    def shifted(k, shift):
        if shift == 0:
            return load(k)
        if (k, shift) not in rolled:
            rolled[(k, shift)] = pltpu.roll(load(k), shift, 1)
        return rolled[(k, shift)]

    chunks = []
    for d in range(n_dst_cols // LANES):
        acc = None
        for s0, d0, w in pieces:
            lo, hi = max(d0, d * LANES), min(d0 + w, (d + 1) * LANES)
            while lo < hi:
                s_abs = s0 + lo - d0
                k, a = divmod(s_abs, LANES)
                n = min(hi - lo, LANES - a)
                a_dst = lo - d * LANES
                val = shifted(k, (a_dst - a) % LANES)
                if n == LANES:
                    acc = val
                else:
                    mask = (lane >= a_dst) & (lane < a_dst + n)
                    acc = jnp.where(mask, val, 0.0 if acc is None else acc)
                lo += n
        chunks.append(acc)
    return chunks


def _gather_rows(src_ref, pieces, d):
    parts = []
    for s0, d0, w in pieces:
        lo, hi = max(d0, d * LANES), min(d0 + w, (d + 1) * LANES)
        if lo < hi:
            parts.append((lo, src_ref[s0 + lo - d0:s0 + hi - d0, :]))
    parts.sort(key=lambda t: t[0])
    assert sum(p.shape[0] for _, p in parts) == LANES
    return jnp.concatenate([p for _, p in parts], axis=0) if len(parts) > 1 else parts[0][1]


def _prep_latent_weights(wq_ref, wkv_ref, wqt_ref, wvt_ref):
    q_pieces = ([(hd * MLA_QK, hd * MLA_NOPE, MLA_NOPE) for hd in range(MLA_HEADS)]
                + [(hd * MLA_QK + MLA_NOPE, MLA_HEADS * MLA_NOPE + hd * MLA_ROPE, MLA_ROPE)
                   for hd in range(MLA_HEADS)])
    wq = jnp.concatenate(_gather_cols(wq_ref, q_pieces, MLA_HEADS * MLA_QK), axis=1)
    nope_w = MLA_HEADS * MLA_NOPE
    per_head = MLA_NOPE + MLA_V
    for hd in range(MLA_HEADS):
        wk = wkv_ref[:, hd * per_head:hd * per_head + MLA_NOPE]
        absorbed_t = lax.dot_general(wk, wq[:, hd * MLA_NOPE:(hd + 1) * MLA_NOPE],
                                     (((1,), (1,)), ((), ())), precision=lax.Precision.HIGHEST,
                                     preferred_element_type=jnp.float32)
        wqt_ref[hd * MLA_KV_RANK:(hd + 1) * MLA_KV_RANK, :] = absorbed_t.astype(wqt_ref.dtype)
        wvt_ref[hd * MLA_V:(hd + 1) * MLA_V, :] = (
            wkv_ref[:, hd * per_head + MLA_NOPE:(hd + 1) * per_head].T.astype(wvt_ref.dtype))
    rope_rows = MLA_HEADS * MLA_KV_RANK
    wqt_ref[rope_rows:, :] = wq[:, nope_w:].T.astype(wqt_ref.dtype)


def kernel(x, ln_mix, w_in, q_a_norm, w_q_up, kv_a_norm, w_kv_up, attn_sinks, w_out, final_norm):
    batch, seq, d = x.shape
    depth = ln_mix.shape[0]
    assert depth == 1, "final norm is fused into the single layer's output kernel"
    assert seq % MLA_BLOCK == 0 and seq % PROJ_ROWS == 0 and seq % SWA_OUT_ROWS == 0
    x2 = x.reshape(batch * seq, d)
    qt, km, vt, gm, qst, ks4, vst, gs = _proj_call(
        x2, ln_mix[0].reshape(1, -1), w_in[0].T, q_a_norm[0].reshape(1, -1), w_q_up[0],
        kv_a_norm[0].reshape(1, -1), w_kv_up[0], seq)
    mm = _mla_call(qt, km, vt, gm, batch, seq)
    out = _swa_out_call(attn_sinks[0], qst, ks4, vst, gs, mm, x2, w_out[0],
                        final_norm.reshape(1, -1), batch, seq)
    return out.reshape(batch, seq, d)
```

```python
import functools
import math

import jax
import jax.numpy as jnp
from jax import lax
from jax.experimental import pallas as pl
from jax.experimental.pallas import tpu as pltpu

ROPE_THETA = 10000.0
NORM_EPS = 1e-6
NEG_INF = -1e30
LOG2E = 1.4426950408889634

MLA_HEADS = 4
MLA_NOPE = 128
MLA_ROPE = 64
MLA_V = 128
MLA_Q_RANK = 256
MLA_KV_RANK = 128
MLA_QK = MLA_NOPE + MLA_ROPE
MLA_WIDTH = MLA_HEADS * MLA_V

SWA_Q_HEADS = 8
SWA_KV_HEADS = 2
SWA_D = 64
SWA_WINDOW = 128
SWA_GROUP = SWA_Q_HEADS // SWA_KV_HEADS
SWA_WIDTH = SWA_Q_HEADS * SWA_D

LANES = 128
SUBLANES = 8
MXU_COLS = 2 * LANES
MLA_QK_PAD = MXU_COLS
assert MLA_KV_RANK + LANES == MLA_QK_PAD
MLA_K_COLS = MLA_QK_PAD
V7X_VMEM_LIMIT_BYTES = 56 * 1024 * 1024

PROJ_ROWS = 1024
MLA_BLOCK = 256
MLA_GROUP_TILES = 8
MLA_HEADS_PER_STEP = 2
MLA_STORE_PIN_UNITS = 4
SWA_OUT_ROWS = 1024
SWA_OUT_SUBTILE = 256
SWA_UNITS_PER_STEP = 4
SWA_OUT_TASKS_PER_STEP = 4
SWA_STORE_LAG = 2


def _inv_freq(dim):
    assert 2 * dim == LANES
    inv = 1.0 / (ROPE_THETA ** (jnp.arange(0, dim, 2, dtype=jnp.float32) / dim))
    return inv.reshape(dim // 2, 1)


def _rope_table_shapes(seq, tile, dim):
    f32 = jnp.float32
    return [pltpu.VMEM((seq // tile, dim // 2, tile), f32), pltpu.VMEM((seq // tile, dim // 2, tile), f32),
            pltpu.VMEM((seq, LANES), f32), pltpu.VMEM((seq, LANES), f32)]


def _fill_rope_tables(inv_ref, ct_ref, st_ref, cl_ref, sl_ref):
    n_tiles, half, tile = ct_ref.shape
    for j in range(n_tiles):
        pos = (lax.broadcasted_iota(jnp.int32, (half, tile), 1) + j * tile).astype(jnp.float32)
        ang = pos * inv_ref[...]
        c, s = jnp.cos(ang), jnp.sin(ang)
        ct_ref[j] = c
        st_ref[j] = s
        cl_ref[j * tile:(j + 1) * tile, :] = jnp.concatenate([c, c, c, c], axis=0).T
        sl_ref[j * tile:(j + 1) * tile, :] = jnp.concatenate([-s, -s, s, s], axis=0).T


def _rope(x, c, s):
    return x * c + pltpu.roll(x, LANES // 2, 1) * s


def _rms(x, g):
    return x * lax.rsqrt(jnp.mean(x * x, axis=-1, keepdims=True) + NORM_EPS) * g


def _silu(g):
    return g / (1.0 + jnp.exp(-g))


def _dot(a, b):
    return jnp.dot(a, b, preferred_element_type=jnp.float32)


def _dot_nt(a, b):
    return lax.dot_general(a, b, (((1,), (1,)), ((), ())),
                           preferred_element_type=jnp.float32)


PROJ_COLS = 1792


def _proj_kernel(x_ref, ln_ref, wint_ref, qg_ref, wq_f32_ref, kvg_ref, wkv_f32_ref,
                 inv_mla_ref, inv_swa_ref,
                 qt_ref, km_ref, vt_ref, gm_ref, qst_ref, ks_ref, vst_ref, gs_ref,
                 wt_ref, wqt_ref, wvt_ref, *table_refs):
    bf = jnp.bfloat16

    mla_tabs = table_refs[0:4]
    swa_tabs = table_refs[4:8] if len(table_refs) > 4 else mla_tabs

    @pl.when(pl.program_id(0) == 0)
    def _():
        _fill_rope_tables(inv_mla_ref, *mla_tabs)
        if swa_tabs is not mla_tabs:
            _fill_rope_tables(inv_swa_ref, *swa_tabs)
        _prep_latent_weights(wq_f32_ref, wkv_f32_ref, wqt_ref, wvt_ref)
        pieces, qs_pieces = _w_in_pieces()
        for d in range(PROJ_COLS // LANES):
            wt_ref[d * LANES:(d + 1) * LANES, :] = _gather_rows(wint_ref, pieces, d).astype(bf)
        for d in range(SWA_WIDTH // LANES):
            wt_ref[PROJ_COLS + d * LANES:PROJ_COLS + (d + 1) * LANES, :] = (
                _gather_rows(wint_ref, qs_pieces, d).astype(bf))

    x = x_ref[...]
    h = _rms(x, ln_ref[...]).astype(bf)

    tile = pl.program_id(0) % mla_tabs[0].shape[0]
    rows = pl.ds(pl.multiple_of(tile * x.shape[0], x.shape[0]), x.shape[0])
    cm, sm = mla_tabs[2][rows, :], mla_tabs[3][rows, :]
    cs, ss = swa_tabs[2][rows, :], swa_tabs[3][rows, :]
    lane = lax.broadcasted_iota(jnp.int32, (x.shape[0], LANES), 1)
    first = (lane % (LANES // 2)) < (LANES // 4)
    grp = MXU_COLS

    def project(a):
        return _dot_nt(h, wt_ref[a:a + grp, :])

    c_q = project(0)
    lat_b = project(256)
    cqn = _rms(c_q, qg_ref[...]).astype(bf)
    ckvn = _rms(lat_b[:, 0:MLA_KV_RANK], kvg_ref[...]).astype(bf)
    kr = _rope(lat_b[:, LANES:2 * LANES], cm, sm)
    kr_first = jnp.where(first, kr, 0.0).astype(bf)

    s_scale = LOG2E / math.sqrt(SWA_D)
    qst = _dot_nt(wt_ref[PROJ_COLS:PROJ_COLS + SWA_WIDTH, :], h)
    cst2 = jnp.concatenate([swa_tabs[0][tile], swa_tabs[0][tile]], axis=0)
    sst2 = jnp.concatenate([swa_tabs[1][tile], swa_tabs[1][tile]], axis=0)
    for ch in range(SWA_WIDTH // LANES):
        x1 = qst[ch * LANES:ch * LANES + SWA_D, :]
        x2 = qst[ch * LANES + SWA_D:(ch + 1) * LANES, :]
        qst_ref[ch * LANES:ch * LANES + SWA_D, :] = ((x1 * cst2 - x2 * sst2) * s_scale).astype(bf)
        qst_ref[ch * LANES + SWA_D:(ch + 1) * LANES, :] = ((x2 * cst2 + x1 * sst2) * s_scale).astype(bf)

    q_scale = LOG2E / math.sqrt(MLA_QK)
    lat_w = MLA_HEADS * MLA_KV_RANK
    half = MLA_ROPE // 2
    qt = _dot_nt(wqt_ref[...], cqn)
    ct, st = mla_tabs[0][tile], mla_tabs[1][tile]
    zeros = jnp.zeros((half, x.shape[0]), bf)
    for hd in range(MLA_HEADS):
        qt_ref[hd * grp:hd * grp + LANES, :] = (
            qt[hd * MLA_KV_RANK:(hd + 1) * MLA_KV_RANK, :] * q_scale).astype(bf)
        x1 = qt[lat_w + hd * MLA_ROPE:lat_w + hd * MLA_ROPE + half, :]
        x2 = qt[lat_w + hd * MLA_ROPE + half:lat_w + (hd + 1) * MLA_ROPE, :]
        base = hd * grp + LANES
        qt_ref[base:base + half, :] = ((x1 * ct - x2 * st) * q_scale).astype(bf)
        qt_ref[base + half:base + 2 * half, :] = zeros
        qt_ref[base + 2 * half:base + 3 * half, :] = ((x2 * ct + x1 * st) * q_scale).astype(bf)
        qt_ref[base + 3 * half:base + 4 * half, :] = zeros
    km_ref[:, 0:MLA_KV_RANK] = ckvn
    km_ref[:, MLA_KV_RANK:MLA_KV_RANK + LANES] = kr_first
    for g2 in range(MLA_WIDTH // grp):
        gm_ref[:, g2 * grp:(g2 + 1) * grp] = _silu(project(512 + g2 * grp)).astype(bf)

    kv_s = project(1024)
    k01 = _rope(kv_s[:, 0:LANES], cs, ss)
    ks_ref[:, 0 * LANES:1 * LANES] = jnp.where(first, k01, 0.0).astype(bf)
    ks_ref[:, 1 * LANES:2 * LANES] = jnp.where(first, 0.0, pltpu.roll(k01, LANES // 4, 1)).astype(bf)
    ks_ref[:, 2 * LANES:3 * LANES] = jnp.where(first, pltpu.roll(k01, 3 * LANES // 4, 1), 0.0).astype(bf)
    ks_ref[:, 3 * LANES:4 * LANES] = jnp.where(first, 0.0, k01).astype(bf)
    vst_ref[...] = kv_s[:, LANES:2 * LANES].T.astype(bf)

    for g2 in range(SWA_WIDTH // grp):
        gs_ref[:, g2 * grp:(g2 + 1) * grp] = _silu(project(1280 + g2 * grp)).astype(bf)

    vt_ref[...] = _dot_nt(wvt_ref[...], ckvn).astype(bf)


def _proj_call(x2, ln_g, w_in_t, q_g, w_q_up, kv_g, w_kv_up, seq):
    n, d = x2.shape
    tm = PROJ_ROWS
    steps_per_seq = seq // tm
    batch = n // seq
    row = lambda i: (i, 0)
    const = lambda i: (0, 0)
    col = lambda i: (i // steps_per_seq, i % steps_per_seq)
    table_scratch = _rope_table_shapes(seq, tm, MLA_ROPE)
    if SWA_D != MLA_ROPE:
        table_scratch += _rope_table_shapes(seq, tm, SWA_D)
    bf = jnp.bfloat16
    qt_rows = MLA_HEADS * MLA_QK_PAD
    row_out = lambda w: (pl.BlockSpec((tm, w), row), jax.ShapeDtypeStruct((n, w), bf))
    col_out = lambda r: (pl.BlockSpec((r, tm), col), jax.ShapeDtypeStruct((batch * r, seq), bf))
    outs = [col_out(qt_rows), row_out(MLA_K_COLS), col_out(MLA_WIDTH), row_out(MLA_WIDTH),
            col_out(SWA_WIDTH), row_out(SWA_WIDTH), col_out(SWA_KV_HEADS * SWA_D), row_out(SWA_WIDTH)]
    return pl.pallas_call(
        _proj_kernel,
        grid=(n // tm,),
        in_specs=[
            pl.BlockSpec((tm, d), row),
            pl.BlockSpec((1, d), const),
            pl.BlockSpec(w_in_t.shape, const),
            pl.BlockSpec((1, MLA_Q_RANK), const),
            pl.BlockSpec(w_q_up.shape, const),
            pl.BlockSpec((1, MLA_KV_RANK), const),
            pl.BlockSpec(w_kv_up.shape, const),
            pl.BlockSpec((MLA_ROPE // 2, 1), const),
            pl.BlockSpec((SWA_D // 2, 1), const),
        ],
        out_specs=[o[0] for o in outs],
        out_shape=[o[1] for o in outs],
        scratch_shapes=[pltpu.VMEM((PROJ_COLS + SWA_WIDTH, d), bf),
                        pltpu.VMEM((MLA_HEADS * (MLA_KV_RANK + MLA_ROPE), MLA_Q_RANK), bf),
                        pltpu.VMEM((MLA_WIDTH, MLA_KV_RANK), bf)] + table_scratch,
        compiler_params=pltpu.CompilerParams(
            dimension_semantics=("arbitrary",),
            vmem_limit_bytes=V7X_VMEM_LIMIT_BYTES),
        name="proj",
    )(x2, ln_g, w_in_t, q_g, w_q_up, kv_g, w_kv_up, _inv_freq(MLA_ROPE), _inv_freq(SWA_D))


def _tree(op, xs):
    xs = list(xs)
    while len(xs) > 1:
        xs = [op(xs[a], xs[a + 1]) if a + 1 < len(xs) else xs[a] for a in range(0, len(xs), 2)]
    return xs[0]

def _derived_zero(val):
    half = val.shape[0] // 2
    rows = jnp.concatenate([val[0:2 * SUBLANES, :], val[half:half + 2 * SUBLANES, :]], axis=1)
    bits = lax.bitcast_convert_type(rows.astype(jnp.float32), jnp.uint32)
    sixteen = jnp.full(bits.shape, 16, jnp.uint32)
    bits = lax.shift_right_logical(lax.shift_right_logical(bits, sixteen), sixteen)
    return lax.bitcast_convert_type(bits, jnp.float32)[0:SUBLANES, :]


def _mla_block_groups(nblk):
    groups, cur, load = [], [], 0
    for i in range(nblk - 1, -1, -1):
        if load + i + 1 > MLA_GROUP_TILES:
            groups.append(cur)
            cur, load = [], 0
        cur.append(i)
        load += i + 1
    groups.append(cur)
    return groups


def _mla_kernel(qt_ref, k_ref, vt_ref, g_ref, o_ref, s_ref, p_ref, vx_ref):
    t = MLA_BLOCK
    seq = k_ref.shape[0]
    sub = SUBLANES
    qk = MLA_QK_PAD
    r = lax.broadcasted_iota(jnp.int32, (t, t), 0)
    c = lax.broadcasted_iota(jnp.int32, (t, t), 1)
    causal = r <= c
    nblk = seq // t
    units = []
    for hd in range(MLA_HEADS_PER_STEP):
        for blocks in _mla_block_groups(nblk):
            members, off = [], 0
            for i in blocks:
                members.append((hd, i, off))
                off += (i + 1) * t
            assert off <= s_ref.shape[1]
            units.append(members)
    state = [[dict(m8=None) for _ in members] for members in units]
    pins = {}

    for hd in range(MLA_HEADS_PER_STEP):
        vx_ref[hd, 0:MLA_V, :] = vt_ref[hd * MLA_V:(hd + 1) * MLA_V, :]
        vx_ref[hd, MLA_V:, :] = jnp.ones((vx_ref.shape[1] - MLA_V, seq), vx_ref.dtype)

    def score_tile(u, b, j):
        hd, i, off = units[u][b]
        st = state[u][b]
        keys = slice(j * t, (j + 1) * t)
        s = _dot(k_ref[keys, :], qt_ref[hd * qk:(hd + 1) * qk, i * t:(i + 1) * t])
        if j == i:
            s = jnp.where(causal, s, NEG_INF)
        s_ref[u % 2, off + j * t:off + (j + 1) * t, :] = s
        m8 = _tree(jnp.maximum, [s[a * sub:(a + 1) * sub, :] for a in range(t // sub)])
        st["m8"] = m8 if st["m8"] is None else jnp.maximum(st["m8"], m8)

    def prob_tile(u, b, j):
        _, _, off = units[u][b]
        st = state[u][b]
        if "m" not in st:
            m8 = _tree(jnp.add, [st["m8"]] + pins.pop(u, []))
            st["m"] = jnp.max(m8, axis=0, keepdims=True)
        rows = slice(off + j * t, off + (j + 1) * t)
        p_ref[u % 3, rows, :] = jnp.exp2(s_ref[u % 2, rows, :] - st["m"]).astype(jnp.bfloat16)

    def value_matmul(u):
        for b, (hd, i, off) in enumerate(units[u]):
            kv = (i + 1) * t
            state[u][b]["acc"] = _dot(vx_ref[hd, :, 0:kv], p_ref[u % 3, off:off + kv, :])

    def store(u):
        for b, (hd, i, _) in enumerate(units[u]):
            rows = slice(i * t, (i + 1) * t)
            acc = state[u][b]["acc"]
            out_t = (acc[0:MLA_V, :] / acc[MLA_V:MLA_V + 1, :]).astype(o_ref.dtype)
            gate = g_ref[rows, hd * MLA_V:(hd + 1) * MLA_V]
            val = out_t.T * gate
            o_ref[rows, hd * MLA_V:(hd + 1) * MLA_V] = val
            pins.setdefault(u + MLA_STORE_PIN_UNITS, []).append(_derived_zero(val))
        state[u] = None

    def tiles(fn, u):
        return [functools.partial(fn, u, b, j) for b, (_, i, _) in enumerate(units[u]) for j in range(i + 1)]

    nu = len(units)
    for task in tiles(score_tile, 0):
        task()
    for u in range(nu + 2):
        if 0 <= u - 2 < nu:
            store(u - 2)
        nxt = tiles(score_tile, u + 1) if u + 1 < nu else []
        cur = tiles(prob_tile, u) if u < nu else []
        while nxt or cur:
            if cur:
                cur.pop(0)()
            if nxt:
                nxt.pop(0)()
        if 0 <= u - 1 < nu:
            value_matmul(u - 1)


def _mla_call(qt, km, vt, gm, batch, seq):
    n = km.shape[0]
    t = MLA_BLOCK
    hps = MLA_HEADS_PER_STEP
    groups = MLA_HEADS // hps
    ones_rows = 16
    feat = lambda b, h: (b * groups + h, 0)
    tok = lambda b, h: (b, h)
    return pl.pallas_call(
        _mla_kernel,
        grid=(batch, groups),
        in_specs=[
            pl.BlockSpec((hps * MLA_QK_PAD, seq), feat),
            pl.BlockSpec((seq, MLA_K_COLS), lambda b, h: (b, 0)),
            pl.BlockSpec((hps * MLA_V, seq), feat),
            pl.BlockSpec((seq, hps * MLA_V), tok),
        ],
        out_specs=pl.BlockSpec((seq, hps * MLA_V), tok),
        out_shape=jax.ShapeDtypeStruct((n, MLA_WIDTH), jnp.bfloat16),
        scratch_shapes=[pltpu.VMEM((2, MLA_GROUP_TILES * t, t), jnp.float32),
                        pltpu.VMEM((3, MLA_GROUP_TILES * t, t), jnp.bfloat16),
                        pltpu.VMEM((hps, MLA_V + ones_rows, seq), jnp.bfloat16)],
        compiler_params=pltpu.CompilerParams(
            dimension_semantics=("arbitrary", "arbitrary"),
            vmem_limit_bytes=V7X_VMEM_LIMIT_BYTES),
        name="mla",
    )(qt, km, vt, gm)


def _swa_out_kernel(sink_ref, qt_ref, k_ref, kh_ref, vt_ref, vth_ref, g_ref, mm_ref, x_ref,
                    wf_ref, fg_ref, o_ref, ms_ref, kx_ref, vx_ref, w_ref):
    w = SWA_WINDOW
    chunk = pl.program_id(1)
    rows_total = k_ref.shape[0]
    bf = jnp.bfloat16
    sub = SUBLANES

    @pl.when((pl.program_id(0) == 0) & (chunk == 0))
    def _():
        w_ref[...] = wf_ref[...].astype(bf)

    kx_ref[0:w, :] = kh_ref[...]
    kx_ref[w:, :] = k_ref[...]
    for hk in range(SWA_KV_HEADS):
        vx_ref[hk, 0:SWA_D, 0:w] = vth_ref[hk * SWA_D:(hk + 1) * SWA_D, :]
        vx_ref[hk, 0:SWA_D, w:] = vt_ref[hk * SWA_D:(hk + 1) * SWA_D, :]
        vx_ref[hk, SWA_D:, :] = jnp.ones((vx_ref.shape[1] - SWA_D, vx_ref.shape[2]), bf)

    ki = lax.broadcasted_iota(jnp.int32, (2 * w, 2 * w), 0)
    qi = lax.broadcasted_iota(jnp.int32, (2 * w, 2 * w), 1) % w
    rel = qi + w - ki
    band = (rel >= 0) & (rel < SWA_WINDOW)
    band_first = band & ((ki >= w) | (chunk > 0))
    lane = lax.broadcasted_iota(jnp.int32, (1, 2 * w), 1)

    units = [(n, hk) for n in range(rows_total // w) for hk in range(SWA_KV_HEADS)]
    state = [dict() for _ in units]

    def sink_rows(hk):
        sink = [sink_ref[hk * SWA_GROUP + i] * LOG2E for i in range(SWA_GROUP)]
        return (jnp.where(lane < w, sink[0], sink[2]), jnp.where(lane < w, sink[1], sink[3]))

    def stage_scores(u, half):
        n, hk = units[u]
        base = hk * 2 * LANES
        qcols = slice(n * w, (n + 1) * w)
        keys = slice(n * w, (n + 2) * w)
        qt = jnp.concatenate([qt_ref[base:base + LANES, qcols],
                              qt_ref[base + LANES:base + 2 * LANES, qcols]], axis=1)
        valid = band_first if n == 0 else band
        st = state[u]
        lanes = slice(base + half * LANES, base + (half + 1) * LANES)
        s = jnp.where(valid, _dot(kx_ref[keys, lanes], qt), NEG_INF)
        m8 = _tree(jnp.maximum, [s[a * sub:(a + 1) * sub, :] for a in range(2 * w // sub)])
        st.setdefault("s", {})[half] = s
        st.setdefault("m", {})[half] = jnp.maximum(jnp.max(m8, axis=0, keepdims=True),
                                                   sink_rows(hk)[half])

    def stage_values(u, half):
        n, hk = units[u]
        keys = slice(n * w, (n + 2) * w)
        st = state[u]
        p = jnp.exp2(st["s"].pop(half) - st["m"][half]).astype(bf)
        st.setdefault("acc", {})[half] = _dot(vx_ref[hk, :, keys], p)

    def stage_store(u):
        n, hk = units[u]
        base = hk * 2 * LANES
        qcols = slice(n * w, (n + 1) * w)
        sinks = sink_rows(hk)
        st = state[u]
        outs = []
        for half in range(2):
            acc = st["acc"][half]
            den = acc[SWA_D:SWA_D + 1, :] + jnp.exp2(sinks[half] - st["m"][half])
            outs.append(acc[0:SWA_D, :] / den)
        g = g_ref[qcols, base:base + 2 * LANES].astype(jnp.float32)
        for ch in range(2):
            o_t = jnp.concatenate([outs[0][:, ch * w:(ch + 1) * w],
                                   outs[1][:, ch * w:(ch + 1) * w]], axis=0)
            ms_ref[qcols, base + ch * LANES:base + (ch + 1) * LANES] = (
                o_t.T * g[:, ch * LANES:(ch + 1) * LANES]).astype(bf)
        state[u] = None

    t = SWA_OUT_SUBTILE
    units_per_tile = (t // w) * SWA_KV_HEADS

    d_model = x_ref.shape[1]
    grp = MXU_COLS
    out_state = {}

    def out_group(i, gc):
        rows = slice(i * t, (i + 1) * t)
        cols = slice(gc * grp, (gc + 1) * grp)
        y = (x_ref[rows, cols] + _dot(mm_ref[rows, :], w_ref[0:MLA_WIDTH, cols])
             + _dot(ms_ref[rows, :], w_ref[MLA_WIDTH:, cols]))
        out_state.setdefault(i, []).append(y)

    def out_norm(i):
        rows = slice(i * t, (i + 1) * t)
        ys = out_state.pop(i)
        ssq = _tree(jnp.add, [jnp.sum(y * y, axis=-1, keepdims=True) for y in ys])
        scale = lax.rsqrt(ssq / d_model + NORM_EPS)
        for gc, y in enumerate(ys):
            cols = slice(gc * grp, (gc + 1) * grp)
            o_ref[rows, cols] = y * scale * fg_ref[:, cols]

    nu = len(units)
    lag = SWA_STORE_LAG
    gsz = SWA_UNITS_PER_STEP
    pending = []
    tasks_per_step = SWA_OUT_TASKS_PER_STEP

    def retire(count):
        for _ in range(min(count, len(pending))):
            pending.pop(0)()

    for step in range(nu // gsz + lag):
        for u in range(step * gsz, (step + 1) * gsz):
            if u < nu:
                stage_scores(u, 0)
                stage_scores(u, 1)
        retire(tasks_per_step // 2)
        for u in range((step - 1) * gsz, step * gsz):
            if 0 <= u < nu:
                stage_values(u, 0)
                stage_values(u, 1)
        retire(tasks_per_step - tasks_per_step // 2)
        for u in range((step - lag) * gsz, (step - lag + 1) * gsz):
            if 0 <= u < nu:
                stage_store(u)
                if (u + 1) % units_per_tile == 0:
                    i = u // units_per_tile
                    pending += [functools.partial(out_group, i, gc) for gc in range(d_model // grp)]
                    pending.append(functools.partial(out_norm, i))
    for task in pending:
        task()


def _swa_out_call(sinks, qst, ks4, vst, gs, mm, x2, w_o, fg, batch, seq):
    n, d = x2.shape
    r = SWA_OUT_ROWS
    w = SWA_WINDOW
    cps = seq // r
    bpc = r // w
    bps = seq // w
    v_rows = SWA_KV_HEADS * SWA_D
    ones_rows = 16
    row = lambda b, c: (b * cps + c, 0)
    halo = lambda b, c: (b * bps + jnp.maximum(c * bpc - 1, 0), 0)
    feat = lambda b, c: (b, c)
    feat_halo = lambda b, c: (b, jnp.maximum(c * bpc - 1, 0))
    const = lambda b, c: (0, 0)
    wide = pl.BlockSpec((r, SWA_WIDTH), row)
    return pl.pallas_call(
        _swa_out_kernel,
        grid=(batch, cps),
        in_specs=[
            pl.BlockSpec(memory_space=pltpu.SMEM),
            pl.BlockSpec((SWA_WIDTH, r), feat),
            wide,
            pl.BlockSpec((w, SWA_WIDTH), halo),
            pl.BlockSpec((v_rows, r), feat),
            pl.BlockSpec((v_rows, w), feat_halo),
            wide,
            pl.BlockSpec((r, MLA_WIDTH), row),
            pl.BlockSpec((r, d), row),
            pl.BlockSpec(w_o.shape, const),
            pl.BlockSpec((1, d), const),
        ],
        out_specs=pl.BlockSpec((r, d), row),
        out_shape=jax.ShapeDtypeStruct((n, d), jnp.float32),
        scratch_shapes=[pltpu.VMEM((r, SWA_WIDTH), jnp.bfloat16),
                        pltpu.VMEM((r + w, SWA_WIDTH), jnp.bfloat16),
                        pltpu.VMEM((SWA_KV_HEADS, SWA_D + ones_rows, r + w), jnp.bfloat16),
                        pltpu.VMEM(w_o.shape, jnp.bfloat16)],
        compiler_params=pltpu.CompilerParams(
            dimension_semantics=("arbitrary", "arbitrary"),
            vmem_limit_bytes=V7X_VMEM_LIMIT_BYTES),
        name="swa_out",
    )(sinks, qst, ks4, ks4, vst, vst, gs, mm, x2, w_o, fg)


def _pair_rope_pieces(src, dst, n_heads, dim):
    half = dim // 2
    out = []
    for pair in range(n_heads // 2):
        for which in range(2):
            head = src + (2 * pair + which) * dim
            chunk = dst + pair * 2 * dim
            out.append((head, chunk + which * half, half))
            out.append((head + half, chunk + 2 * half + which * half, half))
    return out


def _w_in_pieces():
    pieces, qs_pieces, src, dst = [], [], 0, 0
    for name, width in (("c_q", MLA_Q_RANK), ("c_kv", MLA_KV_RANK), ("k_rope", MLA_ROPE),
                        ("g_mla", MLA_WIDTH), ("q_s", SWA_WIDTH),
                        ("k_s", SWA_KV_HEADS * SWA_D), ("v_s", SWA_KV_HEADS * SWA_D),
                        ("g_swa", SWA_WIDTH)):
        if name == "k_rope":
            half = MLA_ROPE // 2
            pieces += [(src, dst, half), (src, dst + half, half),
                       (src + half, dst + 2 * half, half), (src + half, dst + 3 * half, half)]
            dst += 2 * MLA_ROPE
        elif name == "q_s":
            qs_pieces += _pair_rope_pieces(src, 0, SWA_Q_HEADS, SWA_D)
        elif name == "k_s":
            pieces += _pair_rope_pieces(src, dst, SWA_KV_HEADS, SWA_D)
            dst += width
        else:
            pieces.append((src, dst, width))
            dst += width
        src += width
    assert dst == PROJ_COLS
    return pieces, qs_pieces


def _gather_cols(src_ref, pieces, n_dst_cols):
    rows, n_src = src_ref.shape
    lane = lax.broadcasted_iota(jnp.int32, (rows, LANES), 1)
    loaded, rolled = {}, {}

    def load(k):
        if k not in loaded:
            width = min(LANES, n_src - k * LANES)
            v = src_ref[:, k * LANES:k * LANES + width]
            if width < LANES:
                v = jnp.concatenate([v, jnp.zeros((rows, LANES - width), v.dtype)], axis=1)
            loaded[k] = v
        return loaded[k]

    def shifted(k, shift):
        if shift == 0:
            return load(k)
        if (k, shift) not in rolled:
            rolled[(k, shift)] = pltpu.roll(load(k), shift, 1)
        return rolled[(k, shift)]

    chunks = []
    for d in range(n_dst_cols // LANES):
        acc = None
        for s0, d0, w in pieces:
            lo, hi = max(d0, d * LANES), min(d0 + w, (d + 1) * LANES)
            while lo < hi:
                s_abs = s0 + lo - d0
                k, a = divmod(s_abs, LANES)
                n = min(hi - lo, LANES - a)
                a_dst = lo - d * LANES
                val = shifted(k, (a_dst - a) % LANES)
                if n == LANES:
                    acc = val
                else:
                    mask = (lane >= a_dst) & (lane < a_dst + n)
                    acc = jnp.where(mask, val, 0.0 if acc is None else acc)
                lo += n
        chunks.append(acc)
    return chunks


def _gather_rows(src_ref, pieces, d):
    parts = []
    for s0, d0, w in pieces:
        lo, hi = max(d0, d * LANES), min(d0 + w, (d + 1) * LANES)
        if lo < hi:
            parts.append((lo, src_ref[s0 + lo - d0:s0 + hi - d0, :]))
    parts.sort(key=lambda t: t[0])
    assert sum(p.shape[0] for _, p in parts) == LANES
    return jnp.concatenate([p for _, p in parts], axis=0) if len(parts) > 1 else parts[0][1]


def _prep_latent_weights(wq_ref, wkv_ref, wqt_ref, wvt_ref):
    q_pieces = ([(hd * MLA_QK, hd * MLA_NOPE, MLA_NOPE) for hd in range(MLA_HEADS)]
                + [(hd * MLA_QK + MLA_NOPE, MLA_HEADS * MLA_NOPE + hd * MLA_ROPE, MLA_ROPE)
                   for hd in range(MLA_HEADS)])
    wq = jnp.concatenate(_gather_cols(wq_ref, q_pieces, MLA_HEADS * MLA_QK), axis=1)
    nope_w = MLA_HEADS * MLA_NOPE
    per_head = MLA_NOPE + MLA_V
    for hd in range(MLA_HEADS):
        wk = wkv_ref[:, hd * per_head:hd * per_head + MLA_NOPE]
        absorbed_t = lax.dot_general(wk, wq[:, hd * MLA_NOPE:(hd + 1) * MLA_NOPE],
                                     (((1,), (1,)), ((), ())), precision=lax.Precision.HIGHEST,
                                     preferred_element_type=jnp.float32)
        wqt_ref[hd * MLA_KV_RANK:(hd + 1) * MLA_KV_RANK, :] = absorbed_t.astype(wqt_ref.dtype)
        wvt_ref[hd * MLA_V:(hd + 1) * MLA_V, :] = (
            wkv_ref[:, hd * per_head + MLA_NOPE:(hd + 1) * per_head].T.astype(wvt_ref.dtype))
    rope_rows = MLA_HEADS * MLA_KV_RANK
    wqt_ref[rope_rows:, :] = wq[:, nope_w:].T.astype(wqt_ref.dtype)


def kernel(x, ln_mix, w_in, q_a_norm, w_q_up, kv_a_norm, w_kv_up, attn_sinks, w_out, final_norm):
    batch, seq, d = x.shape
    depth = ln_mix.shape[0]
    assert depth == 1, "final norm is fused into the single layer's output kernel"
    assert seq % MLA_BLOCK == 0 and seq % PROJ_ROWS == 0 and seq % SWA_OUT_ROWS == 0
    x2 = x.reshape(batch * seq, d)
    qt, km, vt, gm, qst, ks4, vst, gs = _proj_call(
        x2, ln_mix[0].reshape(1, -1), w_in[0].T, q_a_norm[0].reshape(1, -1), w_q_up[0],
        kv_a_norm[0].reshape(1, -1), w_kv_up[0], seq)
    mm = _mla_call(qt, km, vt, gm, batch, seq)
    out = _swa_out_call(attn_sinks[0], qst, ks4, vst, gs, mm, x2, w_out[0],
                        final_norm.reshape(1, -1), batch, seq)
    return out.reshape(batch, seq, d)
```

```python
import functools
import math

import jax
import jax.numpy as jnp
from jax import lax
from jax.experimental import pallas as pl
from jax.experimental.pallas import tpu as pltpu

ROPE_THETA = 10000.0
NORM_EPS = 1e-6
NEG_INF = -1e30
LOG2E = 1.4426950408889634

MLA_HEADS = 4
MLA_NOPE = 128
MLA_ROPE = 64
MLA_V = 128
MLA_Q_RANK = 256
MLA_KV_RANK = 128
MLA_QK = MLA_NOPE + MLA_ROPE
MLA_WIDTH = MLA_HEADS * MLA_V

SWA_Q_HEADS = 8
SWA_KV_HEADS = 2
SWA_D = 64
SWA_WINDOW = 128
SWA_GROUP = SWA_Q_HEADS // SWA_KV_HEADS
SWA_WIDTH = SWA_Q_HEADS * SWA_D

LANES = 128
SUBLANES = 8
MXU_COLS = 2 * LANES
MLA_QK_PAD = MXU_COLS
assert MLA_KV_RANK + LANES == MLA_QK_PAD
MLA_K_COLS = MLA_QK_PAD
V7X_VMEM_LIMIT_BYTES = 56 * 1024 * 1024

PROJ_ROWS = 1024
MLA_BLOCK = 256
MLA_GROUP_TILES = 8
MLA_HEADS_PER_STEP = 2
SWA_OUT_ROWS = 1024
SWA_OUT_SUBTILE = 256
SWA_UNITS_PER_STEP = 4
SWA_OUT_TASKS_PER_STEP = 4
SWA_STORE_LAG = 2


def _inv_freq(dim):
    assert 2 * dim == LANES
    inv = 1.0 / (ROPE_THETA ** (jnp.arange(0, dim, 2, dtype=jnp.float32) / dim))
    return inv.reshape(dim // 2, 1)


def _rope_table_shapes(seq, tile, dim):
    f32 = jnp.float32
    return [pltpu.VMEM((seq // tile, dim // 2, tile), f32), pltpu.VMEM((seq // tile, dim // 2, tile), f32),
            pltpu.VMEM((seq, LANES), f32), pltpu.VMEM((seq, LANES), f32)]


def _fill_rope_tables(inv_ref, ct_ref, st_ref, cl_ref, sl_ref):
    n_tiles, half, tile = ct_ref.shape
    for j in range(n_tiles):
        pos = (lax.broadcasted_iota(jnp.int32, (half, tile), 1) + j * tile).astype(jnp.float32)
        ang = pos * inv_ref[...]
        c, s = jnp.cos(ang), jnp.sin(ang)
        ct_ref[j] = c
        st_ref[j] = s
        cl_ref[j * tile:(j + 1) * tile, :] = jnp.concatenate([c, c, c, c], axis=0).T
        sl_ref[j * tile:(j + 1) * tile, :] = jnp.concatenate([-s, -s, s, s], axis=0).T


def _rope(x, c, s):
    return x * c + pltpu.roll(x, LANES // 2, 1) * s


def _rms(x, g):
    return x * lax.rsqrt(jnp.mean(x * x, axis=-1, keepdims=True) + NORM_EPS) * g


def _silu(g):
    return g / (1.0 + jnp.exp(-g))


def _dot(a, b):
    return jnp.dot(a, b, preferred_element_type=jnp.float32)


def _dot_nt(a, b):
    return lax.dot_general(a, b, (((1,), (1,)), ((), ())),
                           preferred_element_type=jnp.float32)


PROJ_COLS = 1792


def _proj_kernel(x_ref, ln_ref, wint_ref, qg_ref, wq_f32_ref, kvg_ref, wkv_f32_ref,
                 inv_mla_ref, inv_swa_ref,
                 qt_ref, km_ref, vt_ref, gm_ref, qst_ref, ks_ref, vst_ref, gs_ref,
                 wt_ref, wqt_ref, wvt_ref, *table_refs):
    bf = jnp.bfloat16

    mla_tabs = table_refs[0:4]
    swa_tabs = table_refs[4:8] if len(table_refs) > 4 else mla_tabs

    @pl.when(pl.program_id(0) == 0)
    def _():
        _fill_rope_tables(inv_mla_ref, *mla_tabs)
        if swa_tabs is not mla_tabs:
            _fill_rope_tables(inv_swa_ref, *swa_tabs)
        _prep_latent_weights(wq_f32_ref, wkv_f32_ref, wqt_ref, wvt_ref)
        pieces, qs_pieces = _w_in_pieces()
        for d in range(PROJ_COLS // LANES):
            wt_ref[d * LANES:(d + 1) * LANES, :] = _gather_rows(wint_ref, pieces, d).astype(bf)
        for d in range(SWA_WIDTH // LANES):
            wt_ref[PROJ_COLS + d * LANES:PROJ_COLS + (d + 1) * LANES, :] = (
                _gather_rows(wint_ref, qs_pieces, d).astype(bf))

    x = x_ref[...]
    h = _rms(x, ln_ref[...]).astype(bf)

    tile = pl.program_id(0) % mla_tabs[0].shape[0]
    rows = pl.ds(pl.multiple_of(tile * x.shape[0], x.shape[0]), x.shape[0])
    cm, sm = mla_tabs[2][rows, :], mla_tabs[3][rows, :]
    cs, ss = swa_tabs[2][rows, :], swa_tabs[3][rows, :]
    lane = lax.broadcasted_iota(jnp.int32, (x.shape[0], LANES), 1)
    first = (lane % (LANES // 2)) < (LANES // 4)
    grp = MXU_COLS

    def project(a):
        return _dot_nt(h, wt_ref[a:a + grp, :])

    c_q = project(0)
    lat_b = project(256)
    cqn = _rms(c_q, qg_ref[...]).astype(bf)
    ckvn = _rms(lat_b[:, 0:MLA_KV_RANK], kvg_ref[...]).astype(bf)
    kr = _rope(lat_b[:, LANES:2 * LANES], cm, sm)
    kr_first = jnp.where(first, kr, 0.0).astype(bf)

    s_scale = LOG2E / math.sqrt(SWA_D)
    qst = _dot_nt(wt_ref[PROJ_COLS:PROJ_COLS + SWA_WIDTH, :], h)
    cst2 = jnp.concatenate([swa_tabs[0][tile], swa_tabs[0][tile]], axis=0)
    sst2 = jnp.concatenate([swa_tabs[1][tile], swa_tabs[1][tile]], axis=0)
    for ch in range(SWA_WIDTH // LANES):
        x1 = qst[ch * LANES:ch * LANES + SWA_D, :]
        x2 = qst[ch * LANES + SWA_D:(ch + 1) * LANES, :]
        qst_ref[ch * LANES:ch * LANES + SWA_D, :] = ((x1 * cst2 - x2 * sst2) * s_scale).astype(bf)
        qst_ref[ch * LANES + SWA_D:(ch + 1) * LANES, :] = ((x2 * cst2 + x1 * sst2) * s_scale).astype(bf)

    q_scale = LOG2E / math.sqrt(MLA_QK)
    lat_w = MLA_HEADS * MLA_KV_RANK
    half = MLA_ROPE // 2
    qt = _dot_nt(wqt_ref[...], cqn)
    ct, st = mla_tabs[0][tile], mla_tabs[1][tile]
    zeros = jnp.zeros((half, x.shape[0]), bf)
    for hd in range(MLA_HEADS):
        qt_ref[hd * grp:hd * grp + LANES, :] = (
            qt[hd * MLA_KV_RANK:(hd + 1) * MLA_KV_RANK, :] * q_scale).astype(bf)
        x1 = qt[lat_w + hd * MLA_ROPE:lat_w + hd * MLA_ROPE + half, :]
        x2 = qt[lat_w + hd * MLA_ROPE + half:lat_w + (hd + 1) * MLA_ROPE, :]
        base = hd * grp + LANES
        qt_ref[base:base + half, :] = ((x1 * ct - x2 * st) * q_scale).astype(bf)
        qt_ref[base + half:base + 2 * half, :] = zeros
        qt_ref[base + 2 * half:base + 3 * half, :] = ((x2 * ct + x1 * st) * q_scale).astype(bf)
        qt_ref[base + 3 * half:base + 4 * half, :] = zeros
    km_ref[:, 0:MLA_KV_RANK] = ckvn
    km_ref[:, MLA_KV_RANK:MLA_KV_RANK + LANES] = kr_first
    for g2 in range(MLA_WIDTH // grp):
        gm_ref[:, g2 * grp:(g2 + 1) * grp] = _silu(project(512 + g2 * grp)).astype(bf)

    kv_s = project(1024)
    k01 = _rope(kv_s[:, 0:LANES], cs, ss)
    ks_ref[:, 0 * LANES:1 * LANES] = jnp.where(first, k01, 0.0).astype(bf)
    ks_ref[:, 1 * LANES:2 * LANES] = jnp.where(first, 0.0, pltpu.roll(k01, LANES // 4, 1)).astype(bf)
    ks_ref[:, 2 * LANES:3 * LANES] = jnp.where(first, pltpu.roll(k01, 3 * LANES // 4, 1), 0.0).astype(bf)
    ks_ref[:, 3 * LANES:4 * LANES] = jnp.where(first, 0.0, k01).astype(bf)
    vst_ref[...] = kv_s[:, LANES:2 * LANES].T.astype(bf)

    for g2 in range(SWA_WIDTH // grp):
        gs_ref[:, g2 * grp:(g2 + 1) * grp] = _silu(project(1280 + g2 * grp)).astype(bf)

    vt_ref[...] = _dot_nt(wvt_ref[...], ckvn).astype(bf)


def _proj_call(x2, ln_g, w_in_t, q_g, w_q_up, kv_g, w_kv_up, seq):
    n, d = x2.shape
    tm = PROJ_ROWS
    steps_per_seq = seq // tm
    batch = n // seq
    row = lambda i: (i, 0)
    const = lambda i: (0, 0)
    col = lambda i: (i // steps_per_seq, i % steps_per_seq)
    table_scratch = _rope_table_shapes(seq, tm, MLA_ROPE)
    if SWA_D != MLA_ROPE:
        table_scratch += _rope_table_shapes(seq, tm, SWA_D)
    bf = jnp.bfloat16
    qt_rows = MLA_HEADS * MLA_QK_PAD
    row_out = lambda w: (pl.BlockSpec((tm, w), row), jax.ShapeDtypeStruct((n, w), bf))
    col_out = lambda r: (pl.BlockSpec((r, tm), col), jax.ShapeDtypeStruct((batch * r, seq), bf))
    outs = [col_out(qt_rows), row_out(MLA_K_COLS), col_out(MLA_WIDTH), row_out(MLA_WIDTH),
            col_out(SWA_WIDTH), row_out(SWA_WIDTH), col_out(SWA_KV_HEADS * SWA_D), row_out(SWA_WIDTH)]
    return pl.pallas_call(
        _proj_kernel,
        grid=(n // tm,),
        in_specs=[
            pl.BlockSpec((tm, d), row),
            pl.BlockSpec((1, d), const),
            pl.BlockSpec(w_in_t.shape, const),
            pl.BlockSpec((1, MLA_Q_RANK), const),
            pl.BlockSpec(w_q_up.shape, const),
            pl.BlockSpec((1, MLA_KV_RANK), const),
            pl.BlockSpec(w_kv_up.shape, const),
            pl.BlockSpec((MLA_ROPE // 2, 1), const),
            pl.BlockSpec((SWA_D // 2, 1), const),
        ],
        out_specs=[o[0] for o in outs],
        out_shape=[o[1] for o in outs],
        scratch_shapes=[pltpu.VMEM((PROJ_COLS + SWA_WIDTH, d), bf),
                        pltpu.VMEM((MLA_HEADS * (MLA_KV_RANK + MLA_ROPE), MLA_Q_RANK), bf),
                        pltpu.VMEM((MLA_WIDTH, MLA_KV_RANK), bf)] + table_scratch,
        compiler_params=pltpu.CompilerParams(
            dimension_semantics=("arbitrary",),
            vmem_limit_bytes=V7X_VMEM_LIMIT_BYTES),
        name="proj",
    )(x2, ln_g, w_in_t, q_g, w_q_up, kv_g, w_kv_up, _inv_freq(MLA_ROPE), _inv_freq(SWA_D))


def _tree(op, xs):
    xs = list(xs)
    while len(xs) > 1:
        xs = [op(xs[a], xs[a + 1]) if a + 1 < len(xs) else xs[a] for a in range(0, len(xs), 2)]
    return xs[0]

def _mla_block_groups(nblk):
    groups, cur, load = [], [], 0
    for i in range(nblk - 1, -1, -1):
        if load + i + 1 > MLA_GROUP_TILES:
            groups.append(cur)
            cur, load = [], 0
        cur.append(i)
        load += i + 1
    groups.append(cur)
    return groups


def _mla_kernel(qt_ref, k_ref, vt_ref, g_ref, o_ref, s_ref, p_ref, vx_ref):
    t = MLA_BLOCK
    seq = k_ref.shape[0]
    sub = SUBLANES
    qk = MLA_QK_PAD
    r = lax.broadcasted_iota(jnp.int32, (t, t), 0)
    c = lax.broadcasted_iota(jnp.int32, (t, t), 1)
    causal = r <= c
    nblk = seq // t
    units = []
    for hd in range(MLA_HEADS_PER_STEP):
        for blocks in _mla_block_groups(nblk):
            members, off = [], 0
            for i in blocks:
                members.append((hd, i, off))
                off += (i + 1) * t
            assert off <= s_ref.shape[1]
            units.append(members)
    state = [[dict(m8=None) for _ in members] for members in units]

    for hd in range(MLA_HEADS_PER_STEP):
        vx_ref[hd, 0:MLA_V, :] = vt_ref[hd * MLA_V:(hd + 1) * MLA_V, :]
        vx_ref[hd, MLA_V:, :] = jnp.ones((vx_ref.shape[1] - MLA_V, seq), vx_ref.dtype)

    def score_tile(u, b, j):
        hd, i, off = units[u][b]
        st = state[u][b]
        keys = slice(j * t, (j + 1) * t)
        s = _dot(k_ref[keys, :], qt_ref[hd * qk:(hd + 1) * qk, i * t:(i + 1) * t])
        if j == i:
            s = jnp.where(causal, s, NEG_INF)
        s_ref[u % 2, off + j * t:off + (j + 1) * t, :] = s
        m8 = _tree(jnp.maximum, [s[a * sub:(a + 1) * sub, :] for a in range(t // sub)])
        st["m8"] = m8 if st["m8"] is None else jnp.maximum(st["m8"], m8)

    def prob_tile(u, b, j):
        _, _, off = units[u][b]
        st = state[u][b]
        if "m" not in st:
            st["m"] = jnp.max(st["m8"], axis=0, keepdims=True)
        rows = slice(off + j * t, off + (j + 1) * t)
        p_ref[u % 3, rows, :] = jnp.exp2(s_ref[u % 2, rows, :] - st["m"]).astype(jnp.bfloat16)

    def value_matmul(u):
        for b, (hd, i, off) in enumerate(units[u]):
            kv = (i + 1) * t
            state[u][b]["acc"] = _dot(vx_ref[hd, :, 0:kv], p_ref[u % 3, off:off + kv, :])

    def store(u):
        for b, (hd, i, _) in enumerate(units[u]):
            rows = slice(i * t, (i + 1) * t)
            acc = state[u][b]["acc"]
            out_t = (acc[0:MLA_V, :] / acc[MLA_V:MLA_V + 1, :]).astype(o_ref.dtype)
            gate = g_ref[rows, hd * MLA_V:(hd + 1) * MLA_V]
            o_ref[rows, hd * MLA_V:(hd + 1) * MLA_V] = out_t.T * gate
        state[u] = None

    def tiles(fn, u):
        return [functools.partial(fn, u, b, j) for b, (_, i, _) in enumerate(units[u]) for j in range(i + 1)]

    nu = len(units)
    for task in tiles(score_tile, 0):
        task()
    for u in range(nu + 2):
        if 0 <= u - 2 < nu:
            store(u - 2)
        nxt = tiles(score_tile, u + 1) if u + 1 < nu else []
        cur = tiles(prob_tile, u) if u < nu else []
        while nxt or cur:
            if cur:
                cur.pop(0)()
            if nxt:
                nxt.pop(0)()
        if 0 <= u - 1 < nu:
            value_matmul(u - 1)


def _mla_call(qt, km, vt, gm, batch, seq):
    n = km.shape[0]
    t = MLA_BLOCK
    hps = MLA_HEADS_PER_STEP
    groups = MLA_HEADS // hps
    ones_rows = 16
    feat = lambda b, h: (b * groups + h, 0)
    tok = lambda b, h: (b, h)
    return pl.pallas_call(
        _mla_kernel,
        grid=(batch, groups),
        in_specs=[
            pl.BlockSpec((hps * MLA_QK_PAD, seq), feat),
            pl.BlockSpec((seq, MLA_K_COLS), lambda b, h: (b, 0)),
            pl.BlockSpec((hps * MLA_V, seq), feat),
            pl.BlockSpec((seq, hps * MLA_V), tok),
        ],
        out_specs=pl.BlockSpec((seq, hps * MLA_V), tok),
        out_shape=jax.ShapeDtypeStruct((n, MLA_WIDTH), jnp.bfloat16),
        scratch_shapes=[pltpu.VMEM((2, MLA_GROUP_TILES * t, t), jnp.float32),
                        pltpu.VMEM((3, MLA_GROUP_TILES * t, t), jnp.bfloat16),
                        pltpu.VMEM((hps, MLA_V + ones_rows, seq), jnp.bfloat16)],
        compiler_params=pltpu.CompilerParams(
            dimension_semantics=("arbitrary", "arbitrary"),
            vmem_limit_bytes=V7X_VMEM_LIMIT_BYTES),
        name="mla",
    )(qt, km, vt, gm)


def _swa_out_kernel(sink_ref, qt_ref, k_ref, kh_ref, vt_ref, vth_ref, g_ref, mm_ref, x_ref,
                    wf_ref, fg_ref, o_ref, ms_ref, kx_ref, w_ref, vx_ref):
    w = SWA_WINDOW
    chunk = pl.program_id(1)
    rows_total = k_ref.shape[0]
    bf = jnp.bfloat16
    sub = SUBLANES

    @pl.when((pl.program_id(0) == 0) & (chunk == 0))
    def _():
        w_ref[...] = wf_ref[...].astype(bf)

    kx_ref[0:w, :] = kh_ref[...]
    kx_ref[w:, :] = k_ref[...]
    for hk in range(SWA_KV_HEADS):
        vx_ref[hk, 0:SWA_D, 0:w] = vth_ref[hk * SWA_D:(hk + 1) * SWA_D, :]
        vx_ref[hk, 0:SWA_D, w:] = vt_ref[hk * SWA_D:(hk + 1) * SWA_D, :]
        vx_ref[hk, SWA_D:, :] = jnp.ones((vx_ref.shape[1] - SWA_D, vx_ref.shape[2]), bf)

    ki = lax.broadcasted_iota(jnp.int32, (2 * w, 2 * w), 0)
    qi = lax.broadcasted_iota(jnp.int32, (2 * w, 2 * w), 1) % w
    rel = qi + w - ki
    band = (rel >= 0) & (rel < SWA_WINDOW)
    band_first = band & ((ki >= w) | (chunk > 0))
    lane = lax.broadcasted_iota(jnp.int32, (1, 2 * w), 1)

    units = [(n, hk) for n in range(rows_total // w) for hk in range(SWA_KV_HEADS)]
    state = [dict() for _ in units]

    def sink_rows(hk):
        sink = [sink_ref[hk * SWA_GROUP + i] * LOG2E for i in range(SWA_GROUP)]
        return (jnp.where(lane < w, sink[0], sink[2]), jnp.where(lane < w, sink[1], sink[3]))

    def stage_scores(u, half):
        n, hk = units[u]
        base = hk * 2 * LANES
        qcols = slice(n * w, (n + 1) * w)
        keys = slice(n * w, (n + 2) * w)
        qt = jnp.concatenate([qt_ref[base:base + LANES, qcols],
                              qt_ref[base + LANES:base + 2 * LANES, qcols]], axis=1)
        valid = band_first if n == 0 else band
        st = state[u]
        lanes = slice(base + half * LANES, base + (half + 1) * LANES)
        s = jnp.where(valid, _dot(kx_ref[keys, lanes], qt), NEG_INF)
        m8 = _tree(jnp.maximum, [s[a * sub:(a + 1) * sub, :] for a in range(2 * w // sub)])
        st.setdefault("s", {})[half] = s
        st.setdefault("m", {})[half] = jnp.maximum(jnp.max(m8, axis=0, keepdims=True),
                                                   sink_rows(hk)[half])

    def stage_values(u, half):
        n, hk = units[u]
        keys = slice(n * w, (n + 2) * w)
        st = state[u]
        p = jnp.exp2(st["s"].pop(half) - st["m"][half]).astype(bf)
        st.setdefault("acc", {})[half] = _dot(vx_ref[hk, :, keys], p)

    def stage_store(u):
        n, hk = units[u]
        base = hk * 2 * LANES
        qcols = slice(n * w, (n + 1) * w)
        sinks = sink_rows(hk)
        st = state[u]
        outs = []
        for half in range(2):
            acc = st["acc"][half]
            den = acc[SWA_D:SWA_D + 1, :] + jnp.exp2(sinks[half] - st["m"][half])
            outs.append(acc[0:SWA_D, :] / den)
        g = g_ref[qcols, base:base + 2 * LANES].astype(jnp.float32)
        for ch in range(2):
            o_t = jnp.concatenate([outs[0][:, ch * w:(ch + 1) * w],
                                   outs[1][:, ch * w:(ch + 1) * w]], axis=0)
            ms_ref[qcols, base + ch * LANES:base + (ch + 1) * LANES] = (
                o_t.T * g[:, ch * LANES:(ch + 1) * LANES]).astype(bf)
        state[u] = None

    t = SWA_OUT_SUBTILE
    units_per_tile = (t // w) * SWA_KV_HEADS

    d_model = x_ref.shape[1]
    grp = MXU_COLS
    out_state = {}

    def out_group(i, gc):
        rows = slice(i * t, (i + 1) * t)
        cols = slice(gc * grp, (gc + 1) * grp)
        y = (x_ref[rows, cols] + _dot(mm_ref[rows, :], w_ref[0:MLA_WIDTH, cols])
             + _dot(ms_ref[rows, :], w_ref[MLA_WIDTH:, cols]))
        out_state.setdefault(i, []).append(y)

    def out_norm(i):
        rows = slice(i * t, (i + 1) * t)
        ys = out_state.pop(i)
        ssq = _tree(jnp.add, [jnp.sum(y * y, axis=-1, keepdims=True) for y in ys])
        scale = lax.rsqrt(ssq / d_model + NORM_EPS)
        for gc, y in enumerate(ys):
            cols = slice(gc * grp, (gc + 1) * grp)
            o_ref[rows, cols] = y * scale * fg_ref[:, cols]

    nu = len(units)
    lag = SWA_STORE_LAG
    gsz = SWA_UNITS_PER_STEP
    pending = []
    tasks_per_step = SWA_OUT_TASKS_PER_STEP

    def retire(count):
        for _ in range(min(count, len(pending))):
            pending.pop(0)()

    for step in range(nu // gsz + lag):
        for u in range(step * gsz, (step + 1) * gsz):
            if u < nu:
                stage_scores(u, 0)
                stage_scores(u, 1)
        retire(tasks_per_step // 2)
        for u in range((step - 1) * gsz, step * gsz):
            if 0 <= u < nu:
                stage_values(u, 0)
                stage_values(u, 1)
        retire(tasks_per_step - tasks_per_step // 2)
        for u in range((step - lag) * gsz, (step - lag + 1) * gsz):
            if 0 <= u < nu:
                stage_store(u)
                if (u + 1) % units_per_tile == 0:
                    i = u // units_per_tile
                    pending += [functools.partial(out_group, i, gc) for gc in range(d_model // grp)]
                    pending.append(functools.partial(out_norm, i))
    for task in pending:
        task()


def _swa_out_call(sinks, qst, ks4, vst, gs, mm, x2, w_o, fg, batch, seq):
    n, d = x2.shape
    r = SWA_OUT_ROWS
    w = SWA_WINDOW
    cps = seq // r
    bpc = r // w
    bps = seq // w
    v_rows = SWA_KV_HEADS * SWA_D
    ones_rows = 16
    row = lambda b, c: (b * cps + c, 0)
    halo = lambda b, c: (b * bps + jnp.maximum(c * bpc - 1, 0), 0)
    feat = lambda b, c: (b, c)
    feat_halo = lambda b, c: (b, jnp.maximum(c * bpc - 1, 0))
    const = lambda b, c: (0, 0)
    wide = pl.BlockSpec((r, SWA_WIDTH), row)
    return pl.pallas_call(
        _swa_out_kernel,
        grid=(batch, cps),
        in_specs=[
            pl.BlockSpec(memory_space=pltpu.SMEM),
            pl.BlockSpec((SWA_WIDTH, r), feat),
            wide,
            pl.BlockSpec((w, SWA_WIDTH), halo),
            pl.BlockSpec((v_rows, r), feat),
            pl.BlockSpec((v_rows, w), feat_halo),
            wide,
            pl.BlockSpec((r, MLA_WIDTH), row),
            pl.BlockSpec((r, d), row),
            pl.BlockSpec(w_o.shape, const),
            pl.BlockSpec((1, d), const),
        ],
        out_specs=pl.BlockSpec((r, d), row),
        out_shape=jax.ShapeDtypeStruct((n, d), jnp.float32),
        scratch_shapes=[pltpu.VMEM((r, SWA_WIDTH), jnp.bfloat16),
                        pltpu.VMEM((r + w, SWA_WIDTH), jnp.bfloat16),
                        pltpu.VMEM(w_o.shape, jnp.bfloat16),
                        pltpu.VMEM((SWA_KV_HEADS, SWA_D + ones_rows, r + w), jnp.bfloat16)],
        compiler_params=pltpu.CompilerParams(
            dimension_semantics=("arbitrary", "arbitrary"),
            vmem_limit_bytes=V7X_VMEM_LIMIT_BYTES),
        name="swa_out",
    )(sinks, qst, ks4, ks4, vst, vst, gs, mm, x2, w_o, fg)


def _pair_rope_pieces(src, dst, n_heads, dim):
    half = dim // 2
    out = []
    for pair in range(n_heads // 2):
        for which in range(2):
            head = src + (2 * pair + which) * dim
            chunk = dst + pair * 2 * dim
            out.append((head, chunk + which * half, half))
            out.append((head + half, chunk + 2 * half + which * half, half))
    return out


def _w_in_pieces():
    pieces, qs_pieces, src, dst = [], [], 0, 0
    for name, width in (("c_q", MLA_Q_RANK), ("c_kv", MLA_KV_RANK), ("k_rope", MLA_ROPE),
                        ("g_mla", MLA_WIDTH), ("q_s", SWA_WIDTH),
                        ("k_s", SWA_KV_HEADS * SWA_D), ("v_s", SWA_KV_HEADS * SWA_D),
                        ("g_swa", SWA_WIDTH)):
        if name == "k_rope":
            half = MLA_ROPE // 2
            pieces += [(src, dst, half), (src, dst + half, half),
                       (src + half, dst + 2 * half, half), (src + half, dst + 3 * half, half)]
            dst += 2 * MLA_ROPE
        elif name == "q_s":
            qs_pieces += _pair_rope_pieces(src, 0, SWA_Q_HEADS, SWA_D)
        elif name == "k_s":
            pieces += _pair_rope_pieces(src, dst, SWA_KV_HEADS, SWA_D)
            dst += width
        else:
            pieces.append((src, dst, width))
            dst += width
        src += width
    assert dst == PROJ_COLS
    return pieces, qs_pieces


def _gather_cols(src_ref, pieces, n_dst_cols):
    rows, n_src = src_ref.shape
    lane = lax.broadcasted_iota(jnp.int32, (rows, LANES), 1)
    loaded, rolled = {}, {}

    def load(k):
        if k not in loaded:
            width = min(LANES, n_src - k * LANES)
            v = src_ref[:, k * LANES:k * LANES + width]
            if width < LANES:
                v = jnp.concatenate([v, jnp.zeros((rows, LANES - width), v.dtype)], axis=1)
            loaded[k] = v
        return loaded[k]

    def shifted(k, shift):
        if shift == 0:
            return load(k)
        if (k, shift) not in rolled:
            rolled[(k, shift)] = pltpu.roll(load(k), shift, 1)
        return rolled[(k, shift)]

    chunks = []
    for d in range(n_dst_cols // LANES):
        acc = None
        for s0, d0, w in pieces:
            lo, hi = max(d0, d * LANES), min(d0 + w, (d + 1) * LANES)
            while lo < hi:
                s_abs = s0 + lo - d0
                k, a = divmod(s_abs, LANES)
                n = min(hi - lo, LANES - a)
                a_dst = lo - d * LANES
                val = shifted(k, (a_dst - a) % LANES)
                if n == LANES:
                    acc = val
                else:
                    mask = (lane >= a_dst) & (lane < a_dst + n)
                    acc = jnp.where(mask, val, 0.0 if acc is None else acc)
                lo += n
        chunks.append(acc)
    return chunks


def _gather_rows(src_ref, pieces, d):
    parts = []
    for s0, d0, w in pieces:
        lo, hi = max(d0, d * LANES), min(d0 + w, (d + 1) * LANES)
        if lo < hi:
            parts.append((lo, src_ref[s0 + lo - d0:s0 + hi - d0, :]))
    parts.sort(key=lambda t: t[0])
    assert sum(p.shape[0] for _, p in parts) == LANES
    return jnp.concatenate([p for _, p in parts], axis=0) if len(parts) > 1 else parts[0][1]


def _prep_latent_weights(wq_ref, wkv_ref, wqt_ref, wvt_ref):
    q_pieces = ([(hd * MLA_QK, hd * MLA_NOPE, MLA_NOPE) for hd in range(MLA_HEADS)]
                + [(hd * MLA_QK + MLA_NOPE, MLA_HEADS * MLA_NOPE + hd * MLA_ROPE, MLA_ROPE)
                   for hd in range(MLA_HEADS)])
    wq = jnp.concatenate(_gather_cols(wq_ref, q_pieces, MLA_HEADS * MLA_QK), axis=1)
    nope_w = MLA_HEADS * MLA_NOPE
    per_head = MLA_NOPE + MLA_V
    for hd in range(MLA_HEADS):
        wk = wkv_ref[:, hd * per_head:hd * per_head + MLA_NOPE]
        absorbed_t = lax.dot_general(wk, wq[:, hd * MLA_NOPE:(hd + 1) * MLA_NOPE],
                                     (((1,), (1,)), ((), ())), precision=lax.Precision.HIGHEST,
                                     preferred_element_type=jnp.float32)
        wqt_ref[hd * MLA_KV_RANK:(hd + 1) * MLA_KV_RANK, :] = absorbed_t.astype(wqt_ref.dtype)
        wvt_ref[hd * MLA_V:(hd + 1) * MLA_V, :] = (
            wkv_ref[:, hd * per_head + MLA_NOPE:(hd + 1) * per_head].T.astype(wvt_ref.dtype))
    rope_rows = MLA_HEADS * MLA_KV_RANK
    wqt_ref[rope_rows:, :] = wq[:, nope_w:].T.astype(wqt_ref.dtype)


def kernel(x, ln_mix, w_in, q_a_norm, w_q_up, kv_a_norm, w_kv_up, attn_sinks, w_out, final_norm):
    batch, seq, d = x.shape
    depth = ln_mix.shape[0]
    assert depth == 1, "final norm is fused into the single layer's output kernel"
    assert seq % MLA_BLOCK == 0 and seq % PROJ_ROWS == 0 and seq % SWA_OUT_ROWS == 0
    x2 = x.reshape(batch * seq, d)
    qt, km, vt, gm, qst, ks4, vst, gs = _proj_call(
        x2, ln_mix[0].reshape(1, -1), w_in[0].T, q_a_norm[0].reshape(1, -1), w_q_up[0],
        kv_a_norm[0].reshape(1, -1), w_kv_up[0], seq)
    mm = _mla_call(qt, km, vt, gm, batch, seq)
    out = _swa_out_call(attn_sinks[0], qst, ks4, vst, gs, mm, x2, w_out[0],
                        final_norm.reshape(1, -1), batch, seq)
    return out.reshape(batch, seq, d)
```
